```python
import jax, jax.numpy as jnp
from jax import lax
import numpy as np

D_MODEL = 1024
BATCH = 32
SEQ = 256
DEPTH = 4
DEC_BATCH = 4
DEC_SEQ = 2048
PAST_LEN = 512

GRID_W = 64
N_MIXERS = 3
N_MLA = (DEPTH + 2) // 3
N_CONV = (DEPTH + 1) // 3
N_RET = DEPTH // 3

MLA_HEADS = 8
MLA_NOPE = 128
MLA_ROPE = 64
MLA_V = 128
MLA_Q_RANK = 384
MLA_KV_RANK = 256
MLA_SCALE = (MLA_NOPE + MLA_ROPE) ** -0.5
ROPE_THETA = 10000.0
ROPE_AXIS_FREQS = MLA_ROPE // 4
Q_BLOCK = 128

CONV_WIDTH = 3

RET_HEADS = 4
RET_DK = D_MODEL // RET_HEADS
RET_DV = 2 * RET_DK
RET_CHUNK = 128

FFN_HIDDEN = -(-8 * D_MODEL // (3 * 256)) * 256
EPS = 1e-6

kernel_name = 'hybrid_diffusion_mla_conv_retention_step'


def rmsnorm(x, g):
    xf = x.astype(jnp.float32)
    y = xf * lax.rsqrt(jnp.mean(xf * xf, axis=-1, keepdims=True) + EPS)
    return (y * g.astype(jnp.float32)).astype(x.dtype)


def modulate(x, g, shift, scale):
    return rmsnorm(x, g) * (1 + scale) + shift


def axial_rope_tables(n_tokens):
    rows = n_tokens // GRID_W
    r = jnp.repeat(jnp.arange(rows, dtype=jnp.float32), GRID_W)
    col = jnp.tile(jnp.arange(GRID_W, dtype=jnp.float32), rows)
    inv = ROPE_THETA ** (-jnp.arange(ROPE_AXIS_FREQS, dtype=jnp.float32) / ROPE_AXIS_FREQS)
    ang = jnp.stack([r[:, None] * inv, col[:, None] * inv], axis=1)
    return jnp.cos(ang), jnp.sin(ang)


def apply_axial_rope(x, cos, sin):
    xr = x.reshape(x.shape[:-1] + (2, 2, ROPE_AXIS_FREQS))
    x1, x2 = xr[..., 0, :], xr[..., 1, :]
    cos = cos.astype(x.dtype)
    sin = sin.astype(x.dtype)
    out = jnp.stack([x1 * cos - x2 * sin, x2 * cos + x1 * sin], axis=-2)
    return out.reshape(x.shape)


def mla_project(h, w_a, q_norm_g, kv_norm_g, w_q_b):
    B, S, _ = h.shape
    a = h @ w_a
    q_a, ckv, k_pe = jnp.split(a, [MLA_Q_RANK, MLA_Q_RANK + MLA_KV_RANK], axis=-1)
    q = (rmsnorm(q_a, q_norm_g) @ w_q_b).reshape(B, S, MLA_HEADS, MLA_NOPE + MLA_ROPE)
    return q[..., :MLA_NOPE], q[..., MLA_NOPE:], rmsnorm(ckv, kv_norm_g), k_pe


def mla_expand(ckv, w_kv_b):
    B, S, _ = ckv.shape
    kv = (ckv @ w_kv_b).reshape(B, S, MLA_HEADS, MLA_NOPE + MLA_V)
    return kv[..., :MLA_NOPE], kv[..., MLA_NOPE:]


def mla_attend(q_nope, q_pe, k_nope, k_pe, v, w_o):
    B, Sq = q_nope.shape[:2]
    nb = Sq // Q_BLOCK

    def to_blocks(t):
        return t.reshape((B, nb, Q_BLOCK) + t.shape[2:]).swapaxes(0, 1)

    def one_block(args):
        qn, qp = args
        s = (jnp.einsum('bqhd,bkhd->bhqk', qn, k_nope)
             + jnp.einsum('bqhr,bkr->bhqk', qp, k_pe))
        p = jax.nn.softmax(s.astype(jnp.float32) * MLA_SCALE, axis=-1).astype(v.dtype)
        return jnp.einsum('bhqk,bkhd->bqhd', p, v)

    o = lax.map(one_block, (to_blocks(q_nope), to_blocks(q_pe)))
    return o.swapaxes(0, 1).reshape(B, Sq, MLA_HEADS * MLA_V) @ w_o


def short_conv_mixer(h, w_in, conv_w, w_out):
    b_gate, c_gate, u = jnp.split(h @ w_in, 3, axis=-1)
    z = c_gate * u
    S = z.shape[1]
    zp = jnp.pad(z, ((0, 0), (1, 1), (0, 0)))
    conv = zp[:, :S] * conv_w[0] + zp[:, 1:S + 1] * conv_w[1] + zp[:, 2:] * conv_w[2]
    return (b_gate * conv) @ w_out


def retention_scan(q, k, v, log_gamma, s0):
    B, S, H, _ = q.shape
    n = S // RET_CHUNK
    idx = jnp.arange(RET_CHUNK, dtype=jnp.float32)
    diff = idx[:, None] - idx[None, :]
    intra = jnp.where(diff[None] >= 0,
                      jnp.exp(jnp.maximum(diff, 0.0)[None] * log_gamma[:, None, None]), 0.0)
    q_decay = jnp.exp((idx[:, None] + 1.0) * log_gamma[None])
    k_decay = jnp.exp((RET_CHUNK - 1.0 - idx)[:, None] * log_gamma[None])
    chunk_decay = jnp.exp(RET_CHUNK * log_gamma)

    def chunks(t):
        return t.reshape((B, n, RET_CHUNK) + t.shape[2:]).swapaxes(0, 1)

    def step(state, qkv):
        qc, kc, vc = qkv
        scores = jnp.einsum('bihd,bjhd->bhij', qc, kc) * intra
        inner = jnp.einsum('bhij,bjhe->bihe', scores, vc)
        cross = jnp.einsum('bihd,bhde->bihe', qc, state) * q_decay[None, :, :, None]
        new_state = (state * chunk_decay[None, :, None, None]
                     + jnp.einsum('bjhd,bjhe->bhde', kc * k_decay[None, :, :, None], vc))
        return new_state, inner + cross

    s_final, o = lax.scan(step, s0, (chunks(q), chunks(k), chunks(v)))
    return o.swapaxes(0, 1).reshape(B, S, H, v.shape[-1]), s_final


def retention_mixer(h, w_in, log_rate, gn_g, w_out, s0_fwd, s0_bwd):
    B, S, _ = h.shape
    f32 = jnp.float32
    hk, hv = RET_HEADS * RET_DK, RET_HEADS * RET_DV
    q, k, v, g = jnp.split(h @ w_in, [hk, 2 * hk, 2 * hk + hv], axis=-1)
    q = q.reshape(B, S, RET_HEADS, RET_DK).astype(f32)
    k = k.reshape(B, S, RET_HEADS, RET_DK).astype(f32) * (RET_DK ** -0.5)
    v = v.reshape(B, S, RET_HEADS, RET_DV).astype(f32)
    log_gamma = -jnp.exp(log_rate.astype(f32))
    o_f, s_f = retention_scan(q, k, v, log_gamma[0], s0_fwd.astype(f32))
    o_b, s_b = retention_scan(q[:, ::-1], k[:, ::-1], v[:, ::-1], log_gamma[1], s0_bwd.astype(f32))
    o = o_f + o_b[:, ::-1]
    mu = jnp.mean(o, axis=-1, keepdims=True)
    var = jnp.mean(jnp.square(o - mu), axis=-1, keepdims=True)
    o = ((o - mu) * lax.rsqrt(var + EPS)).reshape(B, S, hv) * gn_g.astype(f32)
    y = jax.nn.silu(g) * o.astype(h.dtype)
    return y @ w_out, s_f.astype(h.dtype), s_b.astype(h.dtype)


def swiglu(h, w_in, w_out):
    a, b = jnp.split(h @ w_in, 2, axis=-1)
    return (jax.nn.silu(a) * b) @ w_out


def setup_inputs(seed: int = 0) -> dict:
    key = jax.random.key(seed)
    ks = jax.random.split(key, 32)
    f32 = jnp.float32

    def nrm(k, shape, scale):
        return jax.random.normal(k, shape, f32) * scale

    def gain(k, shape):
        return 1.0 + 0.05 * jax.random.normal(k, shape, f32)

    D = D_MODEL
    hk, hv = RET_HEADS * RET_DK, RET_HEADS * RET_DV
    base_rate = jnp.log(-jnp.log1p(-(2.0 ** (-5.0 - jnp.arange(RET_HEADS, dtype=f32)))))
    ret_log_rate = base_rate[None, None, :] + 0.1 * jax.random.normal(ks[25], (N_RET, 2, RET_HEADS), f32)
    return {
        'x_prompt': nrm(ks[0], (BATCH, SEQ, D), 1.0),
        'x_sample': nrm(ks[1], (DEC_BATCH, DEC_SEQ, D), 1.0),
        'c': nrm(ks[2], (DEC_BATCH, D), 1.0),
        'c_ctx': nrm(ks[3], (D,), 1.0),
        'cache_mla_ckv': nrm(ks[4], (DEC_BATCH, N_MLA, PAST_LEN, MLA_KV_RANK), 1.0),
        'cache_mla_kpe': nrm(ks[5], (DEC_BATCH, N_MLA, PAST_LEN, MLA_ROPE), 1.0),
        'state_ret': nrm(ks[6], (DEC_BATCH, N_RET, 2, RET_HEADS, RET_DK, RET_DV), 0.5),
        'ada_w': nrm(ks[7], (DEPTH, D, 6 * D), 0.5 * D ** -0.5),
        'ada_b': nrm(ks[8], (DEPTH, 6 * D), 0.02),
        'norm_mix_g': gain(ks[9], (DEPTH, D)),
        'norm_ffn_g': gain(ks[10], (DEPTH, D)),
        'mla_w_a': nrm(ks[11], (N_MLA, D, MLA_Q_RANK + MLA_KV_RANK + MLA_ROPE), D ** -0.5),
        'mla_q_norm_g': gain(ks[12], (N_MLA, MLA_Q_RANK)),
        'mla_kv_norm_g': gain(ks[13], (N_MLA, MLA_KV_RANK)),
        'mla_w_q_b': nrm(ks[14], (N_MLA, MLA_Q_RANK, MLA_HEADS * (MLA_NOPE + MLA_ROPE)), MLA_Q_RANK ** -0.5),
        'mla_w_kv_b': nrm(ks[15], (N_MLA, MLA_KV_RANK, MLA_HEADS * (MLA_NOPE + MLA_V)), MLA_KV_RANK ** -0.5),
        'mla_w_o': nrm(ks[16], (N_MLA, MLA_HEADS * MLA_V, D), (MLA_HEADS * MLA_V) ** -0.5),
        'conv_w_in': nrm(ks[17], (N_CONV, D, 3 * D), D ** -0.5),
        'conv_w': nrm(ks[18], (N_CONV, CONV_WIDTH, D), CONV_WIDTH ** -0.5),
        'conv_w_out': nrm(ks[19], (N_CONV, D, D), D ** -0.5),
        'ret_w_in': nrm(ks[20], (N_RET, D, 2 * hk + 2 * hv), D ** -0.5),
        'ret_log_rate': ret_log_rate,
        'ret_gn_g': gain(ks[21], (N_RET, hv)),
        'ret_w_out': nrm(ks[22], (N_RET, hv, D), hv ** -0.5),
        'ffn_w_in': nrm(ks[23], (DEPTH, D, 2 * FFN_HIDDEN), D ** -0.5),
        'ffn_w_out': nrm(ks[24], (DEPTH, FFN_HIDDEN, D), FFN_HIDDEN ** -0.5),
        'final_norm_g': gain(ks[26], (D,)),
    }


def reference(x_prompt, x_sample, c, c_ctx, cache_mla_ckv, cache_mla_kpe, state_ret,
              ada_w, ada_b, norm_mix_g, norm_ffn_g,
              mla_w_a, mla_q_norm_g, mla_kv_norm_g, mla_w_q_b, mla_w_kv_b, mla_w_o,
              conv_w_in, conv_w, conv_w_out,
              ret_w_in, ret_log_rate, ret_gn_g, ret_w_out,
              ffn_w_in, ffn_w_out, final_norm_g):
    cos, sin = axial_rope_tables(x_sample.shape[1])
    xc, xs = x_prompt, x_sample
    new_ckv, new_kpe, new_ret = [], [], []
    for i in range(DEPTH):
        kind, j = i % N_MIXERS, i // N_MIXERS
        mod_c = jnp.split((jax.nn.silu(c_ctx) @ ada_w[i] + ada_b[i])[None, None, :], 6, axis=-1)
        mod_s = jnp.split((jax.nn.silu(c) @ ada_w[i] + ada_b[i])[:, None, :], 6, axis=-1)
        hc = modulate(xc, norm_mix_g[i], mod_c[0], mod_c[1])
        hs = modulate(xs, norm_mix_g[i], mod_s[0], mod_s[1])
        if kind == 0:
            qn, qp, ckv, kpe = mla_project(hc, mla_w_a[j], mla_q_norm_g[j], mla_kv_norm_g[j], mla_w_q_b[j])
            kn, vv = mla_expand(ckv, mla_w_kv_b[j])
            oc = mla_attend(qn, qp, kn, kpe, vv, mla_w_o[j])
            new_ckv.append(ckv)
            new_kpe.append(kpe)
            qn_s, qp_s, ckv_s, kpe_s = mla_project(hs, mla_w_a[j], mla_q_norm_g[j], mla_kv_norm_g[j], mla_w_q_b[j])
            qp_s = apply_axial_rope(qp_s, cos[:, None], sin[:, None])
            kpe_s = apply_axial_rope(kpe_s, cos, sin)
            kn_s, v_s = mla_expand(ckv_s, mla_w_kv_b[j])
            kn_p, v_p = mla_expand(cache_mla_ckv[:, j], mla_w_kv_b[j])
            os_ = mla_attend(qn_s, qp_s,
                             jnp.concatenate([kn_p, kn_s], axis=1),
                             jnp.concatenate([cache_mla_kpe[:, j], kpe_s], axis=1),
                             jnp.concatenate([v_p, v_s], axis=1), mla_w_o[j])
        elif kind == 1:
            oc = short_conv_mixer(hc, conv_w_in[j], conv_w[j], conv_w_out[j])
            os_ = short_conv_mixer(hs, conv_w_in[j], conv_w[j], conv_w_out[j])
        else:
            zeros = jnp.zeros((xc.shape[0], RET_HEADS, RET_DK, RET_DV), xc.dtype)
            oc, s_f, s_b = retention_mixer(hc, ret_w_in[j], ret_log_rate[j], ret_gn_g[j], ret_w_out[j], zeros, zeros)
            new_ret.append(jnp.stack([s_f, s_b], axis=1))
            os_, _, _ = retention_mixer(hs, ret_w_in[j], ret_log_rate[j], ret_gn_g[j], ret_w_out[j],
                                        state_ret[:, j, 0], state_ret[:, j, 1])
        xc = xc + mod_c[2] * oc
        xs = xs + mod_s[2] * os_
        hc = modulate(xc, norm_ffn_g[i], mod_c[3], mod_c[4])
        hs = modulate(xs, norm_ffn_g[i], mod_s[3], mod_s[4])
        xc = xc + mod_c[5] * swiglu(hc, ffn_w_in[i], ffn_w_out[i])
        xs = xs + mod_s[5] * swiglu(hs, ffn_w_in[i], ffn_w_out[i])
    y_prompt = rmsnorm(xc, final_norm_g)
    y_sample = rmsnorm(xs, final_norm_g)
    return (y_prompt, y_sample, jnp.stack(new_ckv, axis=1), jnp.stack(new_kpe, axis=1), jnp.stack(new_ret, axis=1))
```

```python
import functools

import jax
import jax.numpy as jnp
from jax import lax
from jax.experimental import pallas as pl
from jax.experimental.pallas import tpu as pltpu

F32 = jnp.float32
BF16 = jnp.bfloat16

N_MIXERS = 3
MLA_HEADS = 8
MLA_NOPE = 128
MLA_ROPE = 64
MLA_V = 128
MLA_Q_RANK = 384
MLA_KV_RANK = 256
MLA_SCALE = (MLA_NOPE + MLA_ROPE) ** -0.5
ROPE_THETA = 10000.0
ROPE_AXIS_FREQS = MLA_ROPE // 4
GRID_W = 64
RET_HEADS = 4
EPS = 1e-6

LANES = 128
MLA_HEAD_PAD = 2 * LANES
MOD_ROWS = 8
VMEM_LIMIT = 56 * 1024 * 1024


def _cparams(sem):
    return pltpu.CompilerParams(dimension_semantics=sem, vmem_limit_bytes=VMEM_LIMIT)


def _resident(shape, index_map):
    return pl.BlockSpec(shape, index_map, pipeline_mode=pl.Buffered(1))


def _rms(x, g):
    return x * lax.rsqrt(jnp.mean(x * x, axis=-1, keepdims=True) + EPS) * g


def _modulate(x, g, shift, scale):
    return _rms(x, g) * (1.0 + scale) + shift


def _silu(x):
    return x * jax.nn.sigmoid(x)


def _dot(a, b):
    return jnp.dot(a, b, preferred_element_type=F32)


def _dot_nt(a, b):
    return lax.dot_general(a, b, (((1,), (1,)), ((), ())), preferred_element_type=F32)


def _ada_kernel(c_ref, w_ref, b_ref, o_ref):
    a = _silu(c_ref[...]).astype(BF16)
    o_ref[0] = _dot(a, w_ref[0].astype(BF16)) + b_ref[0]


def _ada_call(cvec, ada_w, ada_b):
    depth, d, n = ada_w.shape
    tn = d
    return pl.pallas_call(
        _ada_kernel,
        grid=(depth, n // tn),
        in_specs=[
            pl.BlockSpec((MOD_ROWS, d), lambda l, j: (0, 0)),
            pl.BlockSpec((1, d, tn), lambda l, j: (l, 0, j)),
            pl.BlockSpec((1, 1, tn), lambda l, j: (l, 0, j)),
        ],
        out_specs=pl.BlockSpec((1, MOD_ROWS, tn), lambda l, j: (l, 0, j)),
        out_shape=jax.ShapeDtypeStruct((depth, MOD_ROWS, n), F32),
        compiler_params=_cparams(("arbitrary", "arbitrary")),
        name="ada",
    )(cvec, ada_w, ada_b.reshape(depth, 1, n))


class _Rows:
    def __init__(self, n_ctx, dec_batch, dec_seq, tm):
        assert n_ctx % tm == 0 and dec_seq % tm == 0
        self.tm = tm
        self.ctx_tiles = n_ctx // tm
        self.seq_tiles = dec_seq // tm
        self.n_tiles = self.ctx_tiles + dec_batch * self.seq_tiles

    def group(self, i):
        lat = jnp.maximum(i - self.ctx_tiles, 0) // self.seq_tiles
        return jnp.where(i < self.ctx_tiles, 0, 1 + lat)


def _mla_proj_kernel(*refs, rope, emit_cache):
    it = iter(refs)
    x_ref, mod_ref, g_ref, wa_ref, qg_ref, kvg_ref, wq_ref, wkv_ref = (next(it) for _ in range(8))
    if rope:
        cq_ref, sq_ref, ck_ref, sk_ref = (next(it) for _ in range(4))
    q_ref, k_ref, v_ref = (next(it) for _ in range(3))
    if emit_cache:
        ckv_ref, kpe_ref = (next(it) for _ in range(2))

    h = _modulate(x_ref[...], g_ref[...], mod_ref[0, 0:1, :], mod_ref[0, 1:2, :]).astype(BF16)
    a = _dot(h, wa_ref[...])
    qa = a[:, :MLA_Q_RANK]
    ckv = a[:, MLA_Q_RANK:MLA_Q_RANK + MLA_KV_RANK]
    kpe = a[:, MLA_Q_RANK + MLA_KV_RANK:MLA_Q_RANK + MLA_KV_RANK + LANES]
    ckv_n = _rms(ckv, kvg_ref[...])
    if emit_cache:
        ckv_ref[...] = ckv_n
        kpe_ref[...] = kpe[:, :MLA_ROPE]
    if rope:
        kpe_sw = a[:, MLA_Q_RANK + MLA_KV_RANK + LANES:MLA_Q_RANK + MLA_KV_RANK + 2 * LANES]
        kpe = kpe * ck_ref[...] + kpe_sw * sk_ref[...]
    kpe = kpe.astype(BF16)

    qn = _rms(qa, qg_ref[...]).astype(BF16)
    nh = MLA_HEADS
    q_nope = _dot(qn, wq_ref[:, :nh * LANES])
    q_pe = _dot(qn, wq_ref[:, nh * LANES:2 * nh * LANES])
    if rope:
        q_sw = _dot(qn, wq_ref[:, 2 * nh * LANES:3 * nh * LANES])
        q_pe = q_pe * cq_ref[...] + q_sw * sq_ref[...]
    kv = _dot(ckv_n.astype(BF16), wkv_ref[...])
    for hd in range(nh):
        lo = hd * MLA_HEAD_PAD
        q_ref[:, lo:lo + LANES] = q_nope[:, hd * LANES:(hd + 1) * LANES].astype(BF16)
        q_ref[:, lo + LANES:lo + 2 * LANES] = q_pe[:, hd * LANES:(hd + 1) * LANES].astype(BF16)
        k_ref[:, lo:lo + LANES] = kv[:, hd * LANES:(hd + 1) * LANES].astype(BF16)
        k_ref[:, lo + LANES:lo + 2 * LANES] = kpe
    v_ref[...] = kv[:, nh * LANES:].astype(BF16)


def _mla_proj_call(x, mod, g, wts, rows, *, tile0, n_tiles, rope_tabs, emit_cache):
    tm, d = rows.tm, x.shape[1]
    wa, qg, kvg, wq, wkv = wts
    rope = rope_tabs is not None
    full = lambda arr: _resident(arr.shape, lambda i: (0,) * arr.ndim)
    in_specs = [
        pl.BlockSpec((tm, d), lambda i: (i + tile0, 0)),
        pl.BlockSpec((1, 6, d), lambda i: (rows.group(i + tile0), 0, 0)),
        full(g), full(wa), full(qg), full(kvg), full(wq), full(wkv),
    ]
    args = [x, mod, g, wa, qg, kvg, wq, wkv]
    if rope:
        for tab in rope_tabs:
            in_specs.append(pl.BlockSpec((tm, tab.shape[1]), lambda i: (i % rows.seq_tiles, 0)))
            args.append(tab)
    m = n_tiles * tm
    hp = MLA_HEADS * MLA_HEAD_PAD
    out_specs = [
        pl.BlockSpec((tm, hp), lambda i: (i, 0)),
        pl.BlockSpec((tm, hp), lambda i: (i, 0)),
        pl.BlockSpec((tm, MLA_HEADS * MLA_V), lambda i: (i, 0)),
    ]
    out_shape = [
        jax.ShapeDtypeStruct((m, hp), BF16),
        jax.ShapeDtypeStruct((m, hp), BF16),
        jax.ShapeDtypeStruct((m, MLA_HEADS * MLA_V), BF16),
    ]
    if emit_cache:
        out_specs += [pl.BlockSpec((tm, MLA_KV_RANK), lambda i: (i, 0)),
                      pl.BlockSpec((tm, MLA_ROPE), lambda i: (i, 0))]
        out_shape += [jax.ShapeDtypeStruct((m, MLA_KV_RANK), F32),
                      jax.ShapeDtypeStruct((m, MLA_ROPE), F32)]
    return pl.pallas_call(
        functools.partial(_mla_proj_kernel, rope=rope, emit_cache=emit_cache),
        grid=(n_tiles,),
        in_specs=in_specs,
        out_specs=out_specs,
        out_shape=out_shape,
        compiler_params=_cparams(("arbitrary",)),
        name="mla_proj_lat" if rope else "mla_proj_ctx",
    )(*args)


def _cache_expand_kernel(ckv_ref, kpe_ref, wkv_ref, k_ref, v_ref):
    kv = _dot(ckv_ref[...].astype(BF16), wkv_ref[...])
    kpe = kpe_ref[...].astype(BF16)
    nh = MLA_HEADS
    for hd in range(nh):
        lo = hd * MLA_HEAD_PAD
        k_ref[:, lo:lo + LANES] = kv[:, hd * LANES:(hd + 1) * LANES].astype(BF16)
        k_ref[:, lo + LANES:lo + 2 * LANES] = kpe
    v_ref[...] = kv[:, nh * LANES:].astype(BF16)


def _cache_expand_call(ckv, kpe_pad, wkv):
    m = ckv.shape[0]
    tm = min(m, 512)
    hp = MLA_HEADS * MLA_HEAD_PAD
    return pl.pallas_call(
        _cache_expand_kernel,
        grid=(m // tm,),
        in_specs=[
            pl.BlockSpec((tm, MLA_KV_RANK), lambda i: (i, 0)),
            pl.BlockSpec((tm, LANES), lambda i: (i, 0)),
            _resident(wkv.shape, lambda i: (0, 0)),
        ],
        out_specs=[pl.BlockSpec((tm, hp), lambda i: (i, 0)),
                   pl.BlockSpec((tm, MLA_HEADS * MLA_V), lambda i: (i, 0))],
        out_shape=[jax.ShapeDtypeStruct((m, hp), BF16),
                   jax.ShapeDtypeStruct((m, MLA_HEADS * MLA_V), BF16)],
        compiler_params=_cparams(("arbitrary",)),
        name="mla_cache_expand",
    )(ckv, kpe_pad, wkv)


def _attn_kernel(*refs, n_seg, nb, tq, sks):
    q_ref = refs[0]
    k_refs = refs[1:1 + n_seg]
    v_refs = refs[1 + n_seg:1 + 2 * n_seg]
    o_ref = refs[1 + 2 * n_seg]
    for b in range(nb):
        q = q_ref[b * tq:(b + 1) * tq, :]
        s = [_dot_nt(q, k_ref[b * sk:(b + 1) * sk, :]) for k_ref, sk in zip(k_refs, sks)]
        m = s[0].max(axis=-1, keepdims=True)
        for sj in s[1:]:
            m = jnp.maximum(m, sj.max(axis=-1, keepdims=True))
        l = None
        acc = None
        for sj, v_ref, sk in zip(s, v_refs, sks):
            p = jnp.exp((sj - m) * MLA_SCALE)
            lj = p.sum(axis=-1, keepdims=True)
            oj = _dot(p.astype(BF16), v_ref[b * sk:(b + 1) * sk, :])
            l = lj if l is None else l + lj
            acc = oj if acc is None else acc + oj
        o_ref[b * tq:(b + 1) * tq, :] = (acc / l).astype(o_ref.dtype)


def _attn_call(q, ks, vs, *, n_batch, seq, sks, nb, tq, name):
    n_seg = len(ks)
    tiles = seq // tq
    assert nb == 1 or tiles == 1
    in_specs = [pl.BlockSpec((nb * tq, MLA_HEAD_PAD), lambda b, h, t: (b * tiles + t, h))]
    in_specs += [pl.BlockSpec((nb * sk, MLA_HEAD_PAD), lambda b, h, t: (b, h)) for sk in sks]
    in_specs += [pl.BlockSpec((nb * sk, MLA_V), lambda b, h, t: (b, h)) for sk in sks]
    return pl.pallas_call(
        functools.partial(_attn_kernel, n_seg=n_seg, nb=nb, tq=tq, sks=tuple(sks)),
        grid=(n_batch // nb, MLA_HEADS, tiles),
        in_specs=in_specs,
        out_specs=pl.BlockSpec((nb * tq, MLA_V), lambda b, h, t: (b * tiles + t, h)),
        out_shape=jax.ShapeDtypeStruct((n_batch * seq, MLA_HEADS * MLA_V), BF16),
        compiler_params=_cparams(("arbitrary", "arbitrary", "arbitrary")),
        name=name,
    )(q, *ks, *vs)


def _out_res_kernel(yc_ref, yl_ref, w_ref, x_ref, mod_ref, o_ref, *, ctx_tiles):
    def emit(y_ref):
        o_ref[...] = x_ref[...] + mod_ref[0, 2:3, :] * _dot(y_ref[...], w_ref[...])

    is_ctx = pl.program_id(0) < ctx_tiles
    pl.when(is_ctx)(lambda: emit(yc_ref))
    pl.when(jnp.logical_not(is_ctx))(lambda: emit(yl_ref))


def _out_res_call(y_ctx, y_lat, w, x, mod, rows, name):
    tm, d = rows.tm, x.shape[1]
    k = w.shape[0]
    ct = rows.ctx_tiles
    return pl.pallas_call(
        functools.partial(_out_res_kernel, ctx_tiles=ct),
        grid=(rows.n_tiles,),
        in_specs=[
            pl.BlockSpec((tm, k), lambda i: (jnp.minimum(i, ct - 1), 0)),
            pl.BlockSpec((tm, k), lambda i: (jnp.maximum(i - ct, 0), 0)),
            _resident(w.shape, lambda i: (0, 0)),
            pl.BlockSpec((tm, d), lambda i: (i, 0)),
            pl.BlockSpec((1, 6, d), lambda i: (rows.group(i), 0, 0)),
        ],
        out_specs=pl.BlockSpec((tm, d), lambda i: (i, 0)),
        out_shape=jax.ShapeDtypeStruct(x.shape, F32),
        compiler_params=_cparams(("arbitrary",)),
        name=name,
    )(y_ctx, y_lat, w, x, mod)


def _ffn_kernel(x_ref, mod_ref, g_ref, win_ref, wout_ref, fg_ref, o_ref, acc_ref, *, th, final_norm):
    x = x_ref[...]
    h = _modulate(x, g_ref[...], mod_ref[0, 3:4, :], mod_ref[0, 4:5, :]).astype(BF16)
    hidden = wout_ref.shape[0]
    for c in range(hidden // th):
        a = _dot(h, win_ref[:, c * th:(c + 1) * th])
        b = _dot(h, win_ref[:, hidden + c * th:hidden + (c + 1) * th])
        part = _dot((_silu(a) * b).astype(BF16), wout_ref[c * th:(c + 1) * th, :])
        if c == 0:
            acc_ref[...] = part
        else:
            acc_ref[...] += part
    y = x + mod_ref[0, 5:6, :] * acc_ref[...]
    if final_norm:
        y = _rms(y, fg_ref[...])
    o_ref[...] = y


def _ffn_call(x, mod, g, w_in, w_out, final_g, rows, final_norm):
    tm, d = rows.tm, x.shape[1]
    return pl.pallas_call(
        functools.partial(_ffn_kernel, th=2 * LANES, final_norm=final_norm),
        grid=(rows.n_tiles,),
        in_specs=[
            pl.BlockSpec((tm, d), lambda i: (i, 0)),
            pl.BlockSpec((1, 6, d), lambda i: (rows.group(i), 0, 0)),
            _resident(g.shape, lambda i: (0, 0)),
            _resident(w_in.shape, lambda i: (0, 0)),
            _resident(w_out.shape, lambda i: (0, 0)),
            _resident(final_g.shape, lambda i: (0, 0)),
        ],
        out_specs=pl.BlockSpec((tm, d), lambda i: (i, 0)),
        out_shape=jax.ShapeDtypeStruct(x.shape, F32),
        scratch_shapes=[pltpu.VMEM((tm, d), F32)],
        compiler_params=_cparams(("arbitrary",)),
        name="ffn_final" if final_norm else "ffn",
    )(x, mod, g, w_in, w_out, final_g)


def _conv_kernel(x_ref, mod_ref, g_ref, win_ref, cw_ref, wout_ref, o_ref, h_ref, *,
                 tn, ctx_tiles, ctx_seq):
    tm, d = x_ref.shape
    x = x_ref[...]
    h_ref[...] = _modulate(x, g_ref[...], mod_ref[0, 0:1, :], mod_ref[0, 1:2, :]).astype(BF16)
    seq = jnp.where(pl.program_id(0) < ctx_tiles, ctx_seq, tm)
    pos = lax.broadcasted_iota(jnp.int32, (tm, 1), 0) & (seq - 1)
    has_prev = pos != 0
    has_next = pos != seq - 1
    for c in range(d // tn):
        sl = slice(c * tn, (c + 1) * tn)
        h = h_ref[...]
        bg = _dot(h, win_ref[:, c * tn:(c + 1) * tn])
        cg = _dot(h, win_ref[:, d + c * tn:d + (c + 1) * tn])
        u = _dot(h, win_ref[:, 2 * d + c * tn:2 * d + (c + 1) * tn])
        z = cg * u
        z_prev = jnp.where(has_prev, pltpu.roll(z, 1, 0), 0.0)
        z_next = jnp.where(has_next, pltpu.roll(z, tm - 1, 0), 0.0)
        conv = z_prev * cw_ref[0:1, sl] + z * cw_ref[1:2, sl] + z_next * cw_ref[2:3, sl]
        part = _dot((bg * conv).astype(BF16), wout_ref[sl, :])
        if c == 0:
            o_ref[...] = part
        else:
            o_ref[...] += part
    o_ref[...] = x_ref[...] + mod_ref[0, 2:3, :] * o_ref[...]


def _conv_call(x, mod, g, w_in, conv_w, w_out, rows, ctx_seq):
    tm, d = rows.tm, x.shape[1]
    return pl.pallas_call(
        functools.partial(_conv_kernel, tn=2 * LANES, ctx_tiles=rows.ctx_tiles, ctx_seq=ctx_seq),
        grid=(rows.n_tiles,),
        in_specs=[
            pl.BlockSpec((tm, d), lambda i: (i, 0)),
            pl.BlockSpec((1, 6, d), lambda i: (rows.group(i), 0, 0)),
            _resident(g.shape, lambda i: (0, 0)),
            _resident(w_in.shape, lambda i: (0, 0)),
            _resident(conv_w.shape, lambda i: (0, 0)),
            _resident(w_out.shape, lambda i: (0, 0)),
        ],
        out_specs=pl.BlockSpec((tm, d), lambda i: (i, 0)),
        out_shape=jax.ShapeDtypeStruct(x.shape, F32),
        scratch_shapes=[pltpu.VMEM((tm, d), BF16)],
        compiler_params=_cparams(("arbitrary",)),
        name="conv_mixer",
    )(x, mod, g, w_in, conv_w, w_out)


def _ret_proj_kernel(x_ref, mod_ref, g_ref, w_ref, qkv_ref, gate_ref, *, tn):
    h = _modulate(x_ref[...], g_ref[...], mod_ref[0, 0:1, :], mod_ref[0, 1:2, :]).astype(BF16)
    n_qkv = qkv_ref.shape[1]
    for c in range(n_qkv // tn):
        qkv_ref[:, c * tn:(c + 1) * tn] = _dot(h, w_ref[:, c * tn:(c + 1) * tn]).astype(BF16)
    for c in range(gate_ref.shape[1] // tn):
        gate_ref[:, c * tn:(c + 1) * tn] = _dot(h, w_ref[:, n_qkv + c * tn:n_qkv + (c + 1) * tn])


def _ret_proj_call(x, mod, g, w, rows, n_gate):
    tm, d = rows.tm, x.shape[1]
    n_qkv = w.shape[1] - n_gate
    m = x.shape[0]
    return pl.pallas_call(
        functools.partial(_ret_proj_kernel, tn=4 * LANES),
        grid=(rows.n_tiles,),
        in_specs=[
            pl.BlockSpec((tm, d), lambda i: (i, 0)),
            pl.BlockSpec((1, 6, d), lambda i: (rows.group(i), 0, 0)),
            _resident(g.shape, lambda i: (0, 0)),
            _resident(w.shape, lambda i: (0, 0)),
        ],
        out_specs=[pl.BlockSpec((tm, n_qkv), lambda i: (i, 0)),
                   pl.BlockSpec((tm, n_gate), lambda i: (i, 0))],
        out_shape=[jax.ShapeDtypeStruct((m, n_qkv), BF16),
                   jax.ShapeDtypeStruct((m, n_gate), F32)],
        compiler_params=_cparams(("arbitrary",)),
        name="ret_proj",
    )(x, mod, g, w)


def _ret_scan_kernel(*refs, chunk, has_init, emit_state):
    it = iter(refs)
    lr_ref, q_ref, k_ref, v_ref, gate_ref, gn_ref = (next(it) for _ in range(6))
    s0_ref = next(it) if has_init else None
    y_ref = next(it)
    sout_ref = next(it) if emit_state else None
    o_ref, st_ref = next(it), next(it)

    seq, dk = q_ref.shape
    n_chunks = seq // chunk
    k_scale = dk ** -0.5
    log_gamma = -jnp.exp(lr_ref[0])
    row = lax.broadcasted_iota(jnp.int32, (chunk, chunk), 0).astype(F32)
    col = lax.broadcasted_iota(jnp.int32, (chunk, chunk), 1).astype(F32)
    ridx = lax.broadcasted_iota(jnp.int32, (chunk, 1), 0).astype(F32)

    for direction in range(2):
        lg = log_gamma[direction:direction + 1, :]
        if direction == 0:
            dist = row - col
            q_pow = ridx + 1.0
            k_pow = chunk - 1.0 - ridx
        else:
            dist = col - row
            q_pow = chunk - ridx
            k_pow = ridx
        intra = jnp.where(dist >= 0, jnp.exp(jnp.maximum(dist, 0.0) * lg), 0.0)
        q_decay = jnp.exp(q_pow * lg)
        k_decay = jnp.exp(k_pow * lg) * k_scale
        chunk_decay = jnp.exp(chunk * lg)

        if has_init:
            st_ref[...] = s0_ref[0, direction, 0]
        else:
            st_ref[...] = jnp.zeros_like(st_ref)

        def body(n, carry, direction=direction, intra=intra, q_decay=q_decay,
                 k_decay=k_decay, chunk_decay=chunk_decay):
            c = n if direction == 0 else n_chunks - 1 - n
            rs = pl.ds(pl.multiple_of(c * chunk, chunk), chunk)
            qc, kc, vc = q_ref[rs, :], k_ref[rs, :], v_ref[rs, :]
            scores = _dot_nt(qc, kc) * k_scale * intra
            inner = _dot(scores.astype(BF16), vc)
            state = st_ref[...]
            cross = _dot(qc, state.astype(BF16)) * q_decay
            kd = (kc.astype(F32) * k_decay).T.astype(BF16)
            st_ref[...] = state * chunk_decay + _dot(kd, vc)
            if direction == 0:
                o_ref[rs, :] = inner + cross
            else:
                o_ref[rs, :] += inner + cross
            return carry

        lax.fori_loop(0, n_chunks, body, 0)
        if emit_state:
            sout_ref[0, 0, direction, 0] = st_ref[...]

    o = o_ref[...]
    mu = jnp.mean(o, axis=-1, keepdims=True)
    var = jnp.mean(jnp.square(o - mu), axis=-1, keepdims=True)
    on = (o - mu) * lax.rsqrt(var + EPS) * gn_ref[...]
    y_ref[...] = (_silu(gate_ref[...]) * on).astype(y_ref.dtype)


def _ret_scan_call(log_rate, qkv, gate, gn_g, s0, *, bsz, seq, row0, emit_state, name):
    n_gate = gate.shape[1]
    nh = RET_HEADS
    dv = n_gate // nh
    dk = (qkv.shape[1] - n_gate) // (2 * nh)
    chunk = min(seq, 2 * LANES)
    has_init = s0 is not None
    assert row0 % seq == 0
    b0 = row0 // seq
    in_specs = [
        pl.BlockSpec((1, 2, 1), lambda b, h: (h, 0, 0)),
        pl.BlockSpec((seq, dk), lambda b, h: (b0 + b, h)),
        pl.BlockSpec((seq, dk), lambda b, h: (b0 + b, nh + h)),
        pl.BlockSpec((seq, dv), lambda b, h: (b0 + b, (2 * nh * dk) // dv + h)),
        pl.BlockSpec((seq, dv), lambda b, h: (b0 + b, h)),
        pl.BlockSpec((1, dv), lambda b, h: (0, h)),
    ]
    args = [log_rate, qkv, qkv, qkv, gate, gn_g]
    if has_init:
        in_specs.append(pl.BlockSpec((1, 2, 1, dk, dv), lambda b, h: (b, 0, h, 0, 0)))
        args.append(s0)
    out_specs = [pl.BlockSpec((seq, dv), lambda b, h: (b, h))]
    out_shape = [jax.ShapeDtypeStruct((bsz * seq, n_gate), BF16)]
    if emit_state:
        out_specs.append(pl.BlockSpec((1, 1, 2, 1, dk, dv), lambda b, h: (b, 0, 0, h, 0, 0)))
        out_shape.append(jax.ShapeDtypeStruct((bsz, 1, 2, nh, dk, dv), F32))
    return pl.pallas_call(
        functools.partial(_ret_scan_kernel, chunk=chunk, has_init=has_init, emit_state=emit_state),
        grid=(bsz, nh),
        in_specs=in_specs,
        out_specs=out_specs,
        out_shape=out_shape,
        scratch_shapes=[pltpu.VMEM((seq, dv), F32), pltpu.VMEM((dk, dv), F32)],
        compiler_params=_cparams(("arbitrary", "arbitrary")),
        name=name,
    )(*args)


def _rope_swap_index():
    f = ROPE_AXIS_FREQS
    idx = jnp.arange(MLA_ROPE)
    return jnp.where((idx // f) % 2 == 0, idx + f, idx - f)


def _mla_weights(w_a, q_norm_g, kv_norm_g, w_q_b, w_kv_b):
    d = w_a.shape[0]
    swap = _rope_swap_index()
    nq = MLA_Q_RANK + MLA_KV_RANK
    zpad = jnp.zeros((d, LANES - MLA_ROPE), w_a.dtype)
    w_kpe = w_a[:, nq:]
    wa = jnp.concatenate([w_a[:, :nq], w_kpe, zpad, w_kpe[:, swap], zpad], axis=1).astype(BF16)
    wq = w_q_b.reshape(MLA_Q_RANK, MLA_HEADS, MLA_NOPE + MLA_ROPE)
    wq_nope = wq[:, :, :MLA_NOPE].reshape(MLA_Q_RANK, MLA_HEADS * MLA_NOPE)
    wq_pe = wq[:, :, MLA_NOPE:]
    pad = ((0, 0), (0, 0), (0, LANES - MLA_ROPE))
    wq_pe_pad = jnp.pad(wq_pe, pad).reshape(MLA_Q_RANK, MLA_HEADS * LANES)
    wq_sw_pad = jnp.pad(wq_pe[:, :, swap], pad).reshape(MLA_Q_RANK, MLA_HEADS * LANES)
    wq_all = jnp.concatenate([wq_nope, wq_pe_pad, wq_sw_pad], axis=1).astype(BF16)
    wkv = w_kv_b.reshape(MLA_KV_RANK, MLA_HEADS, MLA_NOPE + MLA_V)
    wkv_all = jnp.concatenate(
        [wkv[:, :, :MLA_NOPE].reshape(MLA_KV_RANK, -1), wkv[:, :, MLA_NOPE:].reshape(MLA_KV_RANK, -1)],
        axis=1).astype(BF16)
    return wa, q_norm_g[None, :], kv_norm_g[None, :], wq_all, wkv_all


def _rope_tables(n_tokens):
    f = ROPE_AXIS_FREQS
    rows = n_tokens // GRID_W
    r = jnp.repeat(jnp.arange(rows, dtype=F32), GRID_W)
    col = jnp.tile(jnp.arange(GRID_W, dtype=F32), rows)
    inv = ROPE_THETA ** (-jnp.arange(f, dtype=F32) / f)
    ang_r, ang_c = r[:, None] * inv, col[:, None] * inv
    cos = jnp.concatenate([jnp.cos(ang_r)] * 2 + [jnp.cos(ang_c)] * 2, axis=1)
    sin = jnp.concatenate([-jnp.sin(ang_r), jnp.sin(ang_r), -jnp.sin(ang_c), jnp.sin(ang_c)], axis=1)
    zpad = jnp.zeros((n_tokens, LANES - MLA_ROPE), F32)
    ck = jnp.concatenate([cos, zpad], axis=1)
    sk = jnp.concatenate([sin, zpad], axis=1)
    return jnp.tile(ck, (1, MLA_HEADS)), jnp.tile(sk, (1, MLA_HEADS)), ck, sk


def kernel(x_prompt, x_sample, c, c_ctx, cache_mla_ckv, cache_mla_kpe, state_ret, ada_w, ada_b, norm_mix_g, norm_ffn_g, mla_w_a, mla_q_norm_g, mla_kv_norm_g, mla_w_q_b, mla_w_kv_b, mla_w_o, conv_w_in, conv_w, conv_w_out, ret_w_in, ret_log_rate, ret_gn_g, ret_w_out, ffn_w_in, ffn_w_out, final_norm_g):
    batch, seq, d = x_prompt.shape
    dec_batch, dec_seq, _ = x_sample.shape
    depth = ada_w.shape[0]
    n_ctx = batch * seq
    n_lat = dec_batch * dec_seq
    past = cache_mla_ckv.shape[2]
    assert 1 + dec_batch <= MOD_ROWS

    x = jnp.concatenate([x_prompt.reshape(n_ctx, d), x_sample.reshape(n_lat, d)], axis=0)
    cvec = jnp.zeros((MOD_ROWS, d), F32).at[0].set(c_ctx).at[1:1 + dec_batch].set(c)
    mod_all = _ada_call(cvec, ada_w, ada_b).reshape(depth, MOD_ROWS, 6, d)

    rows_s = _Rows(n_ctx, dec_batch, dec_seq, min(512, dec_seq))
    rows_l = _Rows(n_ctx, dec_batch, dec_seq, min(1024, dec_seq))
    rows_seq = _Rows(n_ctx, dec_batch, dec_seq, dec_seq)
    rope_tabs = _rope_tables(dec_seq)
    final_g = final_norm_g[None, :]

    new_ckv, new_kpe, new_ret = [], [], []
    for i in range(depth):
        kind, j = i % N_MIXERS, i // N_MIXERS
        mod = mod_all[i]
        g_mix = norm_mix_g[i][None, :]
        if kind == 0:
            wts = _mla_weights(mla_w_a[j], mla_q_norm_g[j], mla_kv_norm_g[j], mla_w_q_b[j], mla_w_kv_b[j])
            qc, kc, vc, ckv_c, kpe_c = _mla_proj_call(
                x, mod, g_mix, wts, rows_s, tile0=0, n_tiles=rows_s.ctx_tiles,
                rope_tabs=None, emit_cache=True)
            ql, kl, vl = _mla_proj_call(
                x, mod, g_mix, wts, rows_s, tile0=rows_s.ctx_tiles,
                n_tiles=rows_s.n_tiles - rows_s.ctx_tiles, rope_tabs=rope_tabs, emit_cache=False)
            new_ckv.append(ckv_c.reshape(batch, seq, MLA_KV_RANK))
            new_kpe.append(kpe_c.reshape(batch, seq, MLA_ROPE))
            kpe_pad = jnp.pad(cache_mla_kpe[:, j], ((0, 0), (0, 0), (0, LANES - MLA_ROPE)))
            kp, vp = _cache_expand_call(
                cache_mla_ckv[:, j].reshape(dec_batch * past, MLA_KV_RANK),
                kpe_pad.reshape(dec_batch * past, LANES), wts[4])
            hp, hv = MLA_HEADS * MLA_HEAD_PAD, MLA_HEADS * MLA_V
            o_c = _attn_call(qc, [kc], [vc], n_batch=batch, seq=seq, sks=[seq],
                             nb=min(4, batch), tq=seq, name="attn_ctx")
            o_l = _attn_call(ql, [kp, kl], [vp, vl], n_batch=dec_batch, seq=dec_seq,
                             sks=[past, dec_seq], nb=1, tq=min(512, dec_seq), name="attn_lat")
            x = _out_res_call(o_c, o_l, mla_w_o[j].astype(BF16), x, mod, rows_l, "mla_out")
        elif kind == 1:
            x = _conv_call(x, mod, g_mix, conv_w_in[j].astype(BF16), conv_w[j],
                           conv_w_out[j].astype(BF16), rows_seq, seq)
        else:
            n_gate = ret_w_out.shape[1]
            qkv, gate = _ret_proj_call(x, mod, g_mix, ret_w_in[j].astype(BF16), rows_s, n_gate)
            lr = ret_log_rate[j].T[:, :, None]
            gn = ret_gn_g[j][None, :]
            y_c, st = _ret_scan_call(lr, qkv, gate, gn, None, bsz=batch, seq=seq, row0=0,
                                     emit_state=True, name="ret_scan_ctx")
            (y_l,) = _ret_scan_call(lr, qkv, gate, gn, state_ret[:, j], bsz=dec_batch, seq=dec_seq,
                                    row0=n_ctx, emit_state=False, name="ret_scan_lat")
            new_ret.append(st)
            x = _out_res_call(y_c, y_l, ret_w_out[j].astype(BF16), x, mod, rows_l, "ret_out")
        x = _ffn_call(x, mod, norm_ffn_g[i][None, :], ffn_w_in[i].astype(BF16),
                      ffn_w_out[i].astype(BF16), final_g, rows_l, final_norm=(i == depth - 1))

    y_prompt = x[:n_ctx].reshape(batch, seq, d)
    y_sample = x[n_ctx:].reshape(dec_batch, dec_seq, d)
    return (y_prompt, y_sample, jnp.stack(new_ckv, axis=1), jnp.stack(new_kpe, axis=1),
            jnp.concatenate(new_ret, axis=1))
```

```python
import functools

import jax
import jax.numpy as jnp
from jax import lax
from jax.experimental import pallas as pl
from jax.experimental.pallas import tpu as pltpu

F32 = jnp.float32
BF16 = jnp.bfloat16

N_MIXERS = 3
MLA_HEADS = 8
MLA_NOPE = 128
MLA_ROPE = 64
MLA_V = 128
MLA_Q_RANK = 384
MLA_KV_RANK = 256
MLA_SCALE = (MLA_NOPE + MLA_ROPE) ** -0.5
MLA_Q_SCALE = MLA_SCALE * 1.4426950408889634
ROPE_THETA = 10000.0
ROPE_AXIS_FREQS = MLA_ROPE // 4
GRID_W = 64
RET_HEADS = 4
EPS = 1e-6

LANES = 128
MLA_HEAD_PAD = 2 * LANES
MOD_ROWS = 8
VMEM_LIMIT = 56 * 1024 * 1024


def _cparams(sem):
    return pltpu.CompilerParams(dimension_semantics=sem, vmem_limit_bytes=VMEM_LIMIT)


def _resident(shape, index_map):
    return pl.BlockSpec(shape, index_map, pipeline_mode=pl.Buffered(1))


def _rms(x, g):
    return x * lax.rsqrt(jnp.mean(x * x, axis=-1, keepdims=True) + EPS) * g


def _modulate(x, g, shift, scale):
    return _rms(x, g) * (1.0 + scale) + shift


def _silu(x):
    return x * jax.nn.sigmoid(x)


def _dot(a, b):
    return jnp.dot(a, b, preferred_element_type=F32)


def _dot_nt(a, b):
    return lax.dot_general(a, b, (((1,), (1,)), ((), ())), preferred_element_type=F32)


def _ada_kernel(c_ref, w_ref, b_ref, o_ref):
    a = _silu(c_ref[...]).astype(BF16)
    o_ref[0] = _dot(a, w_ref[0].astype(BF16)) + b_ref[0]


def _ada_call(cvec, ada_w, ada_b):
    depth, d, n = ada_w.shape
    tn = d
    return pl.pallas_call(
        _ada_kernel,
        grid=(depth, n // tn),
        in_specs=[
            pl.BlockSpec((MOD_ROWS, d), lambda l, j: (0, 0)),
            pl.BlockSpec((1, d, tn), lambda l, j: (l, 0, j)),
            pl.BlockSpec((1, 1, tn), lambda l, j: (l, 0, j)),
        ],
        out_specs=pl.BlockSpec((1, MOD_ROWS, tn), lambda l, j: (l, 0, j)),
        out_shape=jax.ShapeDtypeStruct((depth, MOD_ROWS, n), F32),
        compiler_params=_cparams(("arbitrary", "arbitrary")),
        name="ada",
    )(cvec, ada_w, ada_b.reshape(depth, 1, n))


class _Rows:
    def __init__(self, n_ctx, dec_batch, dec_seq, tm):
        assert n_ctx % tm == 0 and dec_seq % tm == 0
        self.tm = tm
        self.ctx_tiles = n_ctx // tm
        self.seq_tiles = dec_seq // tm
        self.n_tiles = self.ctx_tiles + dec_batch * self.seq_tiles

    def group(self, i):
        lat = jnp.maximum(i - self.ctx_tiles, 0) // self.seq_tiles
        return jnp.where(i < self.ctx_tiles, 0, 1 + lat)


def _mla_proj_kernel(*refs, rope, emit_cache):
    it = iter(refs)
    x_ref, mod_ref, g_ref, wa_ref, qg_ref, kvg_ref, wq_ref, wkv_ref = (next(it) for _ in range(8))
    if rope:
        cq_ref, sq_ref, ck_ref, sk_ref = (next(it) for _ in range(4))
    q_ref, k_ref, v_ref = (next(it) for _ in range(3))
    if emit_cache:
        ckv_ref, kpe_ref = (next(it) for _ in range(2))

    h = _modulate(x_ref[...], g_ref[...], mod_ref[0, 0:1, :], mod_ref[0, 1:2, :]).astype(BF16)
    a = _dot(h, wa_ref[...])
    qa = a[:, :MLA_Q_RANK]
    ckv = a[:, MLA_Q_RANK:MLA_Q_RANK + MLA_KV_RANK]
    kpe = a[:, MLA_Q_RANK + MLA_KV_RANK:MLA_Q_RANK + MLA_KV_RANK + LANES]
    ckv_n = _rms(ckv, kvg_ref[...])
    if emit_cache:
        ckv_ref[...] = ckv_n
        kpe_ref[...] = kpe[:, :MLA_ROPE]
    if rope:
        kpe_sw = a[:, MLA_Q_RANK + MLA_KV_RANK + LANES:MLA_Q_RANK + MLA_KV_RANK + 2 * LANES]
        kpe = kpe * ck_ref[...] + kpe_sw * sk_ref[...]
    kpe = kpe.astype(BF16)

    qn = _rms(qa, qg_ref[...]).astype(BF16)
    nh = MLA_HEADS
    q_nope = _dot(qn, wq_ref[:, :nh * LANES])
    q_pe = _dot(qn, wq_ref[:, nh * LANES:2 * nh * LANES])
    if rope:
        q_sw = _dot(qn, wq_ref[:, 2 * nh * LANES:3 * nh * LANES])
        q_pe = q_pe * cq_ref[...] + q_sw * sq_ref[...]
    q_nope = q_nope * MLA_Q_SCALE
    q_pe = q_pe * MLA_Q_SCALE
    kv = _dot(ckv_n.astype(BF16), wkv_ref[...])
    for hd in range(nh):
        lo = hd * MLA_HEAD_PAD
        q_ref[:, lo:lo + LANES] = q_nope[:, hd * LANES:(hd + 1) * LANES].astype(BF16)
        q_ref[:, lo + LANES:lo + 2 * LANES] = q_pe[:, hd * LANES:(hd + 1) * LANES].astype(BF16)
        k_ref[:, lo:lo + LANES] = kv[:, hd * LANES:(hd + 1) * LANES].astype(BF16)
        k_ref[:, lo + LANES:lo + 2 * LANES] = kpe
    v_ref[...] = kv[:, nh * LANES:].astype(BF16)


def _mla_proj_call(x, mod, g, wts, rows, *, tile0, n_tiles, rope_tabs, emit_cache):
    tm, d = rows.tm, x.shape[1]
    wa, qg, kvg, wq, wkv = wts
    rope = rope_tabs is not None
    full = lambda arr: _resident(arr.shape, lambda i: (0,) * arr.ndim)
    in_specs = [
        pl.BlockSpec((tm, d), lambda i: (i + tile0, 0)),
        pl.BlockSpec((1, 6, d), lambda i: (rows.group(i + tile0), 0, 0)),
        full(g), full(wa), full(qg), full(kvg), full(wq), full(wkv),
    ]
    args = [x, mod, g, wa, qg, kvg, wq, wkv]
    if rope:
        for tab in rope_tabs:
            in_specs.append(pl.BlockSpec((tm, tab.shape[1]), lambda i: (i % rows.seq_tiles, 0)))
            args.append(tab)
    m = n_tiles * tm
    hp = MLA_HEADS * MLA_HEAD_PAD
    out_specs = [
        pl.BlockSpec((tm, hp), lambda i: (i, 0)),
        pl.BlockSpec((tm, hp), lambda i: (i, 0)),
        pl.BlockSpec((tm, MLA_HEADS * MLA_V), lambda i: (i, 0)),
    ]
    out_shape = [
        jax.ShapeDtypeStruct((m, hp), BF16),
        jax.ShapeDtypeStruct((m, hp), BF16),
        jax.ShapeDtypeStruct((m, MLA_HEADS * MLA_V), BF16),
    ]
    if emit_cache:
        out_specs += [pl.BlockSpec((tm, MLA_KV_RANK), lambda i: (i, 0)),
                      pl.BlockSpec((tm, MLA_ROPE), lambda i: (i, 0))]
        out_shape += [jax.ShapeDtypeStruct((m, MLA_KV_RANK), F32),
                      jax.ShapeDtypeStruct((m, MLA_ROPE), F32)]
    return pl.pallas_call(
        functools.partial(_mla_proj_kernel, rope=rope, emit_cache=emit_cache),
        grid=(n_tiles,),
        in_specs=in_specs,
        out_specs=out_specs,
        out_shape=out_shape,
        compiler_params=_cparams(("arbitrary",)),
        name="mla_proj_lat" if rope else "mla_proj_ctx",
    )(*args)


def _cache_expand_kernel(ckv_ref, kpe_ref, wkv_ref, k_ref, v_ref):
    kv = _dot(ckv_ref[...].astype(BF16), wkv_ref[...])
    kpe = kpe_ref[...].astype(BF16)
    nh = MLA_HEADS
    for hd in range(nh):
        lo = hd * MLA_HEAD_PAD
        k_ref[:, lo:lo + LANES] = kv[:, hd * LANES:(hd + 1) * LANES].astype(BF16)
        k_ref[:, lo + LANES:lo + 2 * LANES] = kpe
    v_ref[...] = kv[:, nh * LANES:].astype(BF16)


def _cache_expand_call(ckv, kpe_pad, wkv):
    m = ckv.shape[0]
    tm = min(m, 512)
    hp = MLA_HEADS * MLA_HEAD_PAD
    return pl.pallas_call(
        _cache_expand_kernel,
        grid=(m // tm,),
        in_specs=[
            pl.BlockSpec((tm, MLA_KV_RANK), lambda i: (i, 0)),
            pl.BlockSpec((tm, LANES), lambda i: (i, 0)),
            _resident(wkv.shape, lambda i: (0, 0)),
        ],
        out_specs=[pl.BlockSpec((tm, hp), lambda i: (i, 0)),
                   pl.BlockSpec((tm, MLA_HEADS * MLA_V), lambda i: (i, 0))],
        out_shape=[jax.ShapeDtypeStruct((m, hp), BF16),
                   jax.ShapeDtypeStruct((m, MLA_HEADS * MLA_V), BF16)],
        compiler_params=_cparams(("arbitrary",)),
        name="mla_cache_expand",
    )(ckv, kpe_pad, wkv)


def _attn_kernel(*refs, n_seg, nb, tq, sks):
    q_ref = refs[0]
    k_refs = refs[1:1 + n_seg]
    v_refs = refs[1 + n_seg:1 + 2 * n_seg]
    o_ref = refs[1 + 2 * n_seg]
    s_ref = refs[2 + 2 * n_seg]
    problems = [(b, h) for b in range(nb) for h in range(MLA_HEADS)]
    seg_cols = [sum(sks[:i]) for i in range(n_seg)]

    def scores(idx):
        b, h = problems[idx]
        q = q_ref[b * tq:(b + 1) * tq, h * MLA_HEAD_PAD:(h + 1) * MLA_HEAD_PAD]
        m_lane = None
        for k_ref, sk, c0 in zip(k_refs, sks, seg_cols):
            s = _dot_nt(q, k_ref[b * sk:(b + 1) * sk, h * MLA_HEAD_PAD:(h + 1) * MLA_HEAD_PAD])
            s_ref[idx % 2, :, c0:c0 + sk] = s
            for c in range(sk // LANES):
                piece = s[:, c * LANES:(c + 1) * LANES]
                m_lane = piece if m_lane is None else jnp.maximum(m_lane, piece)
        return m_lane.max(axis=-1, keepdims=True)

    def weighted_values(idx, m):
        b, h = problems[idx]
        acc = None
        for v_ref, sk, c0 in zip(v_refs, sks, seg_cols):
            p = jnp.exp2(s_ref[idx % 2, :, c0:c0 + sk] - m).astype(BF16)
            v = v_ref[b * sk:(b + 1) * sk, h * MLA_V:(h + 1) * MLA_V]
            part = _dot(p, jnp.concatenate([v, jnp.ones_like(v)], axis=1))
            acc = part if acc is None else acc + part
        o = acc[:, :MLA_V] / acc[:, MLA_V:]
        o_ref[b * tq:(b + 1) * tq, h * MLA_V:(h + 1) * MLA_V] = o.astype(o_ref.dtype)

    m = scores(0)
    for idx in range(len(problems)):
        m_next = scores(idx + 1) if idx + 1 < len(problems) else None
        weighted_values(idx, m)
        m = m_next


def _attn_call(q, ks, vs, *, n_batch, seq, sks, nb, tq, name):
    n_seg = len(ks)
    tiles = seq // tq
    assert nb == 1 or tiles == 1
    hp, hv = MLA_HEADS * MLA_HEAD_PAD, MLA_HEADS * MLA_V
    in_specs = [pl.BlockSpec((nb * tq, hp), lambda b, t: (b * tiles + t, 0))]
    in_specs += [pl.BlockSpec((nb * sk, hp), lambda b, t: (b, 0)) for sk in sks]
    in_specs += [pl.BlockSpec((nb * sk, hv), lambda b, t: (b, 0)) for sk in sks]
    return pl.pallas_call(
        functools.partial(_attn_kernel, n_seg=n_seg, nb=nb, tq=tq, sks=tuple(sks)),
        grid=(n_batch // nb, tiles),
        in_specs=in_specs,
        out_specs=pl.BlockSpec((nb * tq, hv), lambda b, t: (b * tiles + t, 0)),
        out_shape=jax.ShapeDtypeStruct((n_batch * seq, hv), BF16),
        scratch_shapes=[pltpu.VMEM((2, tq, sum(sks)), F32)],
        compiler_params=_cparams(("arbitrary", "arbitrary")),
        name=name,
    )(q, *ks, *vs)


def _out_res_kernel(yc_ref, yl_ref, w_ref, x_ref, mod_ref, o_ref, *, ctx_tiles):
    def emit(y_ref):
        o_ref[...] = x_ref[...] + mod_ref[0, 2:3, :] * _dot(y_ref[...], w_ref[...])

    is_ctx = pl.program_id(0) < ctx_tiles
    pl.when(is_ctx)(lambda: emit(yc_ref))
    pl.when(jnp.logical_not(is_ctx))(lambda: emit(yl_ref))


def _out_res_call(y_ctx, y_lat, w, x, mod, rows, name):
    tm, d = rows.tm, x.shape[1]
    k = w.shape[0]
    ct = rows.ctx_tiles
    return pl.pallas_call(
        functools.partial(_out_res_kernel, ctx_tiles=ct),
        grid=(rows.n_tiles,),
        in_specs=[
            pl.BlockSpec((tm, k), lambda i: (jnp.minimum(i, ct - 1), 0)),
            pl.BlockSpec((tm, k), lambda i: (jnp.maximum(i - ct, 0), 0)),
            _resident(w.shape, lambda i: (0, 0)),
            pl.BlockSpec((tm, d), lambda i: (i, 0)),
            pl.BlockSpec((1, 6, d), lambda i: (rows.group(i), 0, 0)),
        ],
        out_specs=pl.BlockSpec((tm, d), lambda i: (i, 0)),
        out_shape=jax.ShapeDtypeStruct(x.shape, F32),
        compiler_params=_cparams(("arbitrary",)),
        name=name,
    )(y_ctx, y_lat, w, x, mod)


def _ffn_kernel(x_ref, mod_ref, g_ref, win_ref, wout_ref, fg_ref, o_ref, acc_ref, *, th, final_norm):
    x = x_ref[...]
    h = _modulate(x, g_ref[...], mod_ref[0, 3:4, :], mod_ref[0, 4:5, :]).astype(BF16)
    hidden = wout_ref.shape[0]
    for c in range(hidden // th):
        a = _dot(h, win_ref[:, c * th:(c + 1) * th])
        b = _dot(h, win_ref[:, hidden + c * th:hidden + (c + 1) * th])
        part = _dot((_silu(a) * b).astype(BF16), wout_ref[c * th:(c + 1) * th, :])
        if c == 0:
            acc_ref[...] = part
        else:
            acc_ref[...] += part
    y = x + mod_ref[0, 5:6, :] * acc_ref[...]
    if final_norm:
        y = _rms(y, fg_ref[...])
    o_ref[...] = y


def _ffn_call(x, mod, g, w_in, w_out, final_g, rows, final_norm):
    tm, d = rows.tm, x.shape[1]
    return pl.pallas_call(
        functools.partial(_ffn_kernel, th=2 * LANES, final_norm=final_norm),
        grid=(rows.n_tiles,),
        in_specs=[
            pl.BlockSpec((tm, d), lambda i: (i, 0)),
            pl.BlockSpec((1, 6, d), lambda i: (rows.group(i), 0, 0)),
            _resident(g.shape, lambda i: (0, 0)),
            _resident(w_in.shape, lambda i: (0, 0)),
            _resident(w_out.shape, lambda i: (0, 0)),
            _resident(final_g.shape, lambda i: (0, 0)),
        ],
        out_specs=pl.BlockSpec((tm, d), lambda i: (i, 0)),
        out_shape=jax.ShapeDtypeStruct(x.shape, F32),
        scratch_shapes=[pltpu.VMEM((tm, d), F32)],
        compiler_params=_cparams(("arbitrary",)),
        name="ffn_final" if final_norm else "ffn",
    )(x, mod, g, w_in, w_out, final_g)


def _conv_kernel(x_ref, mod_ref, g_ref, win_ref, cw_ref, wout_ref, o_ref, h_ref, *,
                 tn, ctx_tiles, ctx_seq):
    tm, d = x_ref.shape
    x = x_ref[...]
    h_ref[...] = _modulate(x, g_ref[...], mod_ref[0, 0:1, :], mod_ref[0, 1:2, :]).astype(BF16)
    seq = jnp.where(pl.program_id(0) < ctx_tiles, ctx_seq, tm)
    pos = lax.broadcasted_iota(jnp.int32, (tm, 1), 0) & (seq - 1)
    has_prev = pos != 0
    has_next = pos != seq - 1
    for c in range(d // tn):
        sl = slice(c * tn, (c + 1) * tn)
        h = h_ref[...]
        bg = _dot(h, win_ref[:, c * tn:(c + 1) * tn])
        cg = _dot(h, win_ref[:, d + c * tn:d + (c + 1) * tn])
        u = _dot(h, win_ref[:, 2 * d + c * tn:2 * d + (c + 1) * tn])
        z = cg * u
        z_prev = jnp.where(has_prev, pltpu.roll(z, 1, 0), 0.0)
        z_next = jnp.where(has_next, pltpu.roll(z, tm - 1, 0), 0.0)
        conv = z_prev * cw_ref[0:1, sl] + z * cw_ref[1:2, sl] + z_next * cw_ref[2:3, sl]
        part = _dot((bg * conv).astype(BF16), wout_ref[sl, :])
        if c == 0:
            o_ref[...] = part
        else:
            o_ref[...] += part
    o_ref[...] = x_ref[...] + mod_ref[0, 2:3, :] * o_ref[...]


def _conv_call(x, mod, g, w_in, conv_w, w_out, rows, ctx_seq):
    tm, d = rows.tm, x.shape[1]
    return pl.pallas_call(
        functools.partial(_conv_kernel, tn=2 * LANES, ctx_tiles=rows.ctx_tiles, ctx_seq=ctx_seq),
        grid=(rows.n_tiles,),
        in_specs=[
            pl.BlockSpec((tm, d), lambda i: (i, 0)),
            pl.BlockSpec((1, 6, d), lambda i: (rows.group(i), 0, 0)),
            _resident(g.shape, lambda i: (0, 0)),
            _resident(w_in.shape, lambda i: (0, 0)),
            _resident(conv_w.shape, lambda i: (0, 0)),
            _resident(w_out.shape, lambda i: (0, 0)),
        ],
        out_specs=pl.BlockSpec((tm, d), lambda i: (i, 0)),
        out_shape=jax.ShapeDtypeStruct(x.shape, F32),
        scratch_shapes=[pltpu.VMEM((tm, d), BF16)],
        compiler_params=_cparams(("arbitrary",)),
        name="conv_mixer",
    )(x, mod, g, w_in, conv_w, w_out)


def _ret_proj_kernel(x_ref, mod_ref, g_ref, w_ref, qkv_ref, gate_ref, *, tn):
    h = _modulate(x_ref[...], g_ref[...], mod_ref[0, 0:1, :], mod_ref[0, 1:2, :]).astype(BF16)
    n_qkv = qkv_ref.shape[1]
    for c in range(n_qkv // tn):
        qkv_ref[:, c * tn:(c + 1) * tn] = _dot(h, w_ref[:, c * tn:(c + 1) * tn]).astype(BF16)
    for c in range(gate_ref.shape[1] // tn):
        gate = _dot(h, w_ref[:, n_qkv + c * tn:n_qkv + (c + 1) * tn])
        gate_ref[:, c * tn:(c + 1) * tn] = _silu(gate).astype(gate_ref.dtype)


def _ret_proj_call(x, mod, g, w, rows, n_gate):
    tm, d = rows.tm, x.shape[1]
    n_qkv = w.shape[1] - n_gate
    m = x.shape[0]
    return pl.pallas_call(
        functools.partial(_ret_proj_kernel, tn=4 * LANES),
        grid=(rows.n_tiles,),
        in_specs=[
            pl.BlockSpec((tm, d), lambda i: (i, 0)),
            pl.BlockSpec((1, 6, d), lambda i: (rows.group(i), 0, 0)),
            _resident(g.shape, lambda i: (0, 0)),
            _resident(w.shape, lambda i: (0, 0)),
        ],
        out_specs=[pl.BlockSpec((tm, n_qkv), lambda i: (i, 0)),
                   pl.BlockSpec((tm, n_gate), lambda i: (i, 0))],
        out_shape=[jax.ShapeDtypeStruct((m, n_qkv), BF16),
                   jax.ShapeDtypeStruct((m, n_gate), BF16)],
        compiler_params=_cparams(("arbitrary",)),
        name="ret_proj",
    )(x, mod, g, w)


def _ret_scan_kernel(*refs, chunk, hps, dk, dv, has_init, emit_state):
    it = iter(refs)
    lr_ref, q_ref, k_ref, v_ref, gate_ref, gn_ref = (next(it) for _ in range(6))
    s0_ref = next(it) if has_init else None
    y_ref = next(it)
    sout_ref = next(it) if emit_state else None
    o_ref, st_ref = next(it), next(it)

    seq = q_ref.shape[0]
    n_chunks = seq // chunk
    k_scale = dk ** -0.5
    row = lax.broadcasted_iota(jnp.int32, (chunk, chunk), 0).astype(F32)
    col = lax.broadcasted_iota(jnp.int32, (chunk, chunk), 1).astype(F32)
    ridx = lax.broadcasted_iota(jnp.int32, (chunk, 1), 0).astype(F32)
    dist = row - col

    for hd in range(hps):
        qs = slice(hd * dk, (hd + 1) * dk)
        vs = slice(hd * dv, (hd + 1) * dv)
        log_gamma = -jnp.exp(lr_ref[hd])
        lg_f, lg_b = log_gamma[0:1, :], log_gamma[1:2, :]
        mask = jnp.where(dist > 0, jnp.exp(jnp.maximum(dist, 0.0) * lg_f),
                         jnp.where(dist < 0, jnp.exp(jnp.maximum(-dist, 0.0) * lg_b), 2.0)) * k_scale
        q_decay = (jnp.exp((ridx + 1.0) * lg_f), jnp.exp((chunk - ridx) * lg_b))
        k_decay = (jnp.exp((chunk - 1.0 - ridx) * lg_f) * k_scale, jnp.exp(ridx * lg_b) * k_scale)
        chunk_decay = (jnp.exp(chunk * lg_f), jnp.exp(chunk * lg_b))

        for direction in range(2):
            order = range(n_chunks) if direction == 0 else range(n_chunks - 1, -1, -1)
            if has_init:
                st_ref[hd] = s0_ref[0, direction, hd]
            for step, c in enumerate(order):
                rs = slice(c * chunk, (c + 1) * chunk)
                qc, kc, vc = q_ref[rs, qs], k_ref[rs, qs], v_ref[rs, vs]
                have_state = has_init or step > 0
                if direction == 0:
                    scores = _dot_nt(qc, kc) * mask
                    o_ref[rs, vs] = _dot(scores.astype(BF16), vc)
                if have_state:
                    o_ref[rs, vs] += _dot(qc, st_ref[hd].astype(BF16)) * q_decay[direction]
                kd = (kc.astype(F32) * k_decay[direction]).T.astype(BF16)
                update = _dot(kd, vc)
                if have_state:
                    st_ref[hd] = st_ref[hd] * chunk_decay[direction] + update
                else:
                    st_ref[hd] = update
            if emit_state:
                sout_ref[0, 0, direction, hd] = st_ref[hd]

        o = o_ref[:, vs]
        mu = jnp.mean(o, axis=-1, keepdims=True)
        var = jnp.mean(jnp.square(o - mu), axis=-1, keepdims=True)
        on = (o - mu) * lax.rsqrt(var + EPS) * gn_ref[:, vs]
        y_ref[:, vs] = (gate_ref[:, vs].astype(F32) * on).astype(y_ref.dtype)


def _ret_scan_call(log_rate, qkv, gate, gn_g, s0, *, bsz, seq, row0, hps, emit_state, name):
    n_gate = gate.shape[1]
    nh = RET_HEADS
    dv = n_gate // nh
    dk = (qkv.shape[1] - n_gate) // (2 * nh)
    chunk = min(seq, 2 * LANES)
    has_init = s0 is not None
    assert row0 % seq == 0 and nh % hps == 0
    b0 = row0 // seq
    hg = nh // hps
    in_specs = [
        pl.BlockSpec((hps, 2, 1), lambda b, h: (h, 0, 0)),
        pl.BlockSpec((seq, hps * dk), lambda b, h: (b0 + b, h)),
        pl.BlockSpec((seq, hps * dk), lambda b, h: (b0 + b, hg + h)),
        pl.BlockSpec((seq, hps * dv), lambda b, h: (b0 + b, (2 * nh * dk) // (hps * dv) + h)),
        pl.BlockSpec((seq, hps * dv), lambda b, h: (b0 + b, h)),
        pl.BlockSpec((1, hps * dv), lambda b, h: (0, h)),
    ]
    args = [log_rate, qkv, qkv, qkv, gate, gn_g]
    if has_init:
        in_specs.append(pl.BlockSpec((1, 2, hps, dk, dv), lambda b, h: (b, 0, h, 0, 0)))
        args.append(s0)
    out_specs = [pl.BlockSpec((seq, hps * dv), lambda b, h: (b, h))]
    out_shape = [jax.ShapeDtypeStruct((bsz * seq, n_gate), BF16)]
    if emit_state:
        out_specs.append(pl.BlockSpec((1, 1, 2, hps, dk, dv), lambda b, h: (b, 0, 0, h, 0, 0)))
        out_shape.append(jax.ShapeDtypeStruct((bsz, 1, 2, nh, dk, dv), F32))
    return pl.pallas_call(
        functools.partial(_ret_scan_kernel, chunk=chunk, hps=hps, dk=dk, dv=dv,
                          has_init=has_init, emit_state=emit_state),
        grid=(bsz, hg),
        in_specs=in_specs,
        out_specs=out_specs,
        out_shape=out_shape,
        scratch_shapes=[pltpu.VMEM((seq, hps * dv), F32), pltpu.VMEM((hps, dk, dv), F32)],
        compiler_params=_cparams(("arbitrary", "arbitrary")),
        name=name,
    )(*args)


def _rope_swap_index():
    f = ROPE_AXIS_FREQS
    idx = jnp.arange(MLA_ROPE)
    return jnp.where((idx // f) % 2 == 0, idx + f, idx - f)


def _mla_weights(w_a, q_norm_g, kv_norm_g, w_q_b, w_kv_b):
    d = w_a.shape[0]
    swap = _rope_swap_index()
    nq = MLA_Q_RANK + MLA_KV_RANK
    zpad = jnp.zeros((d, LANES - MLA_ROPE), w_a.dtype)
    w_kpe = w_a[:, nq:]
    wa = jnp.concatenate([w_a[:, :nq], w_kpe, zpad, w_kpe[:, swap], zpad], axis=1).astype(BF16)
    wq = w_q_b.reshape(MLA_Q_RANK, MLA_HEADS, MLA_NOPE + MLA_ROPE)
    wq_nope = wq[:, :, :MLA_NOPE].reshape(MLA_Q_RANK, MLA_HEADS * MLA_NOPE)
    wq_pe = wq[:, :, MLA_NOPE:]
    pad = ((0, 0), (0, 0), (0, LANES - MLA_ROPE))
    wq_pe_pad = jnp.pad(wq_pe, pad).reshape(MLA_Q_RANK, MLA_HEADS * LANES)
    wq_sw_pad = jnp.pad(wq_pe[:, :, swap], pad).reshape(MLA_Q_RANK, MLA_HEADS * LANES)
    wq_all = jnp.concatenate([wq_nope, wq_pe_pad, wq_sw_pad], axis=1).astype(BF16)
    wkv = w_kv_b.reshape(MLA_KV_RANK, MLA_HEADS, MLA_NOPE + MLA_V)
    wkv_all = jnp.concatenate(
        [wkv[:, :, :MLA_NOPE].reshape(MLA_KV_RANK, -1), wkv[:, :, MLA_NOPE:].reshape(MLA_KV_RANK, -1)],
        axis=1).astype(BF16)
    return wa, q_norm_g[None, :], kv_norm_g[None, :], wq_all, wkv_all


def _rope_tables(n_tokens):
    f = ROPE_AXIS_FREQS
    rows = n_tokens // GRID_W
    r = jnp.repeat(jnp.arange(rows, dtype=F32), GRID_W)
    col = jnp.tile(jnp.arange(GRID_W, dtype=F32), rows)
    inv = ROPE_THETA ** (-jnp.arange(f, dtype=F32) / f)
    ang_r, ang_c = r[:, None] * inv, col[:, None] * inv
    cos = jnp.concatenate([jnp.cos(ang_r)] * 2 + [jnp.cos(ang_c)] * 2, axis=1)
    sin = jnp.concatenate([-jnp.sin(ang_r), jnp.sin(ang_r), -jnp.sin(ang_c), jnp.sin(ang_c)], axis=1)
    zpad = jnp.zeros((n_tokens, LANES - MLA_ROPE), F32)
    ck = jnp.concatenate([cos, zpad], axis=1)
    sk = jnp.concatenate([sin, zpad], axis=1)
    return jnp.tile(ck, (1, MLA_HEADS)), jnp.tile(sk, (1, MLA_HEADS)), ck, sk


def kernel(x_prompt, x_sample, c, c_ctx, cache_mla_ckv, cache_mla_kpe, state_ret, ada_w, ada_b, norm_mix_g, norm_ffn_g, mla_w_a, mla_q_norm_g, mla_kv_norm_g, mla_w_q_b, mla_w_kv_b, mla_w_o, conv_w_in, conv_w, conv_w_out, ret_w_in, ret_log_rate, ret_gn_g, ret_w_out, ffn_w_in, ffn_w_out, final_norm_g):
    batch, seq, d = x_prompt.shape
    dec_batch, dec_seq, _ = x_sample.shape
    depth = ada_w.shape[0]
    n_ctx = batch * seq
    n_lat = dec_batch * dec_seq
    past = cache_mla_ckv.shape[2]
    assert 1 + dec_batch <= MOD_ROWS

    x = jnp.concatenate([x_prompt.reshape(n_ctx, d), x_sample.reshape(n_lat, d)], axis=0)
    cvec = jnp.zeros((MOD_ROWS, d), F32).at[0].set(c_ctx).at[1:1 + dec_batch].set(c)
    mod_all = _ada_call(cvec, ada_w, ada_b).reshape(depth, MOD_ROWS, 6, d)

    rows_s = _Rows(n_ctx, dec_batch, dec_seq, min(512, dec_seq))
    rows_l = _Rows(n_ctx, dec_batch, dec_seq, min(1024, dec_seq))
    rows_seq = _Rows(n_ctx, dec_batch, dec_seq, dec_seq)
    rope_tabs = _rope_tables(dec_seq)
    final_g = final_norm_g[None, :]

    new_ckv, new_kpe, new_ret = [], [], []
    for i in range(depth):
        kind, j = i % N_MIXERS, i // N_MIXERS
        mod = mod_all[i]
        g_mix = norm_mix_g[i][None, :]
        if kind == 0:
            wts = _mla_weights(mla_w_a[j], mla_q_norm_g[j], mla_kv_norm_g[j], mla_w_q_b[j], mla_w_kv_b[j])
            qc, kc, vc, ckv_c, kpe_c = _mla_proj_call(
                x, mod, g_mix, wts, rows_s, tile0=0, n_tiles=rows_s.ctx_tiles,
                rope_tabs=None, emit_cache=True)
            ql, kl, vl = _mla_proj_call(
                x, mod, g_mix, wts, rows_s, tile0=rows_s.ctx_tiles,
                n_tiles=rows_s.n_tiles - rows_s.ctx_tiles, rope_tabs=rope_tabs, emit_cache=False)
            new_ckv.append(ckv_c.reshape(batch, seq, MLA_KV_RANK))
            new_kpe.append(kpe_c.reshape(batch, seq, MLA_ROPE))
            kpe_pad = jnp.pad(cache_mla_kpe[:, j], ((0, 0), (0, 0), (0, LANES - MLA_ROPE)))
            kp, vp = _cache_expand_call(
                cache_mla_ckv[:, j].reshape(dec_batch * past, MLA_KV_RANK),
                kpe_pad.reshape(dec_batch * past, LANES), wts[4])
            o_c = _attn_call(qc, [kc], [vc], n_batch=batch, seq=seq, sks=[seq],
                             nb=min(4, batch), tq=seq, name="attn_ctx")
            o_l = _attn_call(ql, [kp, kl], [vp, vl], n_batch=dec_batch, seq=dec_seq,
                             sks=[past, dec_seq], nb=1, tq=min(256, dec_seq), name="attn_lat")
            x = _out_res_call(o_c, o_l, mla_w_o[j].astype(BF16), x, mod, rows_l, "mla_out")
        elif kind == 1:
            x = _conv_call(x, mod, g_mix, conv_w_in[j].astype(BF16), conv_w[j],
                           conv_w_out[j].astype(BF16), rows_seq, seq)
        else:
            n_gate = ret_w_out.shape[1]
            qkv, gate = _ret_proj_call(x, mod, g_mix, ret_w_in[j].astype(BF16), rows_s, n_gate)
            lr = ret_log_rate[j].T[:, :, None]
            gn = ret_gn_g[j][None, :]
            y_c, st = _ret_scan_call(lr, qkv, gate, gn, None, bsz=batch, seq=seq, row0=0,
                                     hps=RET_HEADS, emit_state=True, name="ret_scan_ctx")
            (y_l,) = _ret_scan_call(lr, qkv, gate, gn, state_ret[:, j], bsz=dec_batch, seq=dec_seq,
                                    row0=n_ctx, hps=1, emit_state=False, name="ret_scan_lat")
            new_ret.append(st)
            x = _out_res_call(y_c, y_l, ret_w_out[j].astype(BF16), x, mod, rows_l, "ret_out")
        x = _ffn_call(x, mod, norm_ffn_g[i][None, :], ffn_w_in[i].astype(BF16),
                      ffn_w_out[i].astype(BF16), final_g, rows_l, final_norm=(i == depth - 1))

    y_prompt = x[:n_ctx].reshape(batch, seq, d)
    y_sample = x[n_ctx:].reshape(dec_batch, dec_seq, d)
    return (y_prompt, y_sample, jnp.stack(new_ckv, axis=1), jnp.stack(new_kpe, axis=1),
            jnp.concatenate(new_ret, axis=1))
```

```python
import functools

import jax
import jax.numpy as jnp
from jax import lax
from jax.experimental import pallas as pl
from jax.experimental.pallas import tpu as pltpu

F32 = jnp.float32
BF16 = jnp.bfloat16

N_MIXERS = 3
MLA_HEADS = 8
MLA_NOPE = 128
MLA_ROPE = 64
MLA_V = 128
MLA_Q_RANK = 384
MLA_KV_RANK = 256
MLA_SCALE = (MLA_NOPE + MLA_ROPE) ** -0.5
MLA_Q_SCALE = MLA_SCALE * 1.4426950408889634
ROPE_THETA = 10000.0
ROPE_AXIS_FREQS = MLA_ROPE // 4
GRID_W = 64
RET_HEADS = 4
EPS = 1e-6

LANES = 128
MLA_HEAD_PAD = 2 * LANES
MOD_ROWS = 8
VMEM_LIMIT = 56 * 1024 * 1024


def _cparams(sem):
    return pltpu.CompilerParams(dimension_semantics=sem, vmem_limit_bytes=VMEM_LIMIT)


def _resident(shape, index_map):
    return pl.BlockSpec(shape, index_map, pipeline_mode=pl.Buffered(1))


def _rms(x, g):
    return x * lax.rsqrt(jnp.mean(x * x, axis=-1, keepdims=True) + EPS) * g


def _modulate(x, g, shift, scale):
    return _rms(x, g) * (1.0 + scale) + shift


def _silu(x):
    return x * jax.nn.sigmoid(x)


def _dot(a, b):
    return jnp.dot(a, b, preferred_element_type=F32)


def _dot_nt(a, b):
    return lax.dot_general(a, b, (((1,), (1,)), ((), ())), preferred_element_type=F32)


def _ada_kernel(c_ref, w_ref, b_ref, o_ref):
    a = _silu(c_ref[...]).astype(BF16)
    o_ref[0] = _dot(a, w_ref[0].astype(BF16)) + b_ref[0]


def _ada_call(cvec, ada_w, ada_b):
    depth, d, n = ada_w.shape
    tn = d
    return pl.pallas_call(
        _ada_kernel,
        grid=(depth, n // tn),
        in_specs=[
            pl.BlockSpec((MOD_ROWS, d), lambda l, j: (0, 0)),
            pl.BlockSpec((1, d, tn), lambda l, j: (l, 0, j)),
            pl.BlockSpec((1, 1, tn), lambda l, j: (l, 0, j)),
        ],
        out_specs=pl.BlockSpec((1, MOD_ROWS, tn), lambda l, j: (l, 0, j)),
        out_shape=jax.ShapeDtypeStruct((depth, MOD_ROWS, n), F32),
        compiler_params=_cparams(("arbitrary", "arbitrary")),
        name="ada",
    )(cvec, ada_w, ada_b.reshape(depth, 1, n))


class _Rows:
    def __init__(self, n_ctx, dec_batch, dec_seq, tm):
        assert n_ctx % tm == 0 and dec_seq % tm == 0
        self.tm = tm
        self.ctx_tiles = n_ctx // tm
        self.seq_tiles = dec_seq // tm
        self.n_tiles = self.ctx_tiles + dec_batch * self.seq_tiles

    def group(self, i):
        lat = jnp.maximum(i - self.ctx_tiles, 0) // self.seq_tiles
        return jnp.where(i < self.ctx_tiles, 0, 1 + lat)


def _mla_proj_kernel(*refs, rope, emit_cache):
    it = iter(refs)
    x_ref, mod_ref, g_ref, wa_ref, qg_ref, kvg_ref, wq_ref, wkv_ref = (next(it) for _ in range(8))
    if rope:
        ck_ref, sk_ref = (next(it) for _ in range(2))
    q_ref, k_ref, v_ref = (next(it) for _ in range(3))
    if emit_cache:
        ckv_ref, kpe_ref = (next(it) for _ in range(2))

    h = _modulate(x_ref[...], g_ref[...], mod_ref[0, 0:1, :], mod_ref[0, 1:2, :]).astype(BF16)
    a = _dot(h, wa_ref[...])
    qa = a[:, :MLA_Q_RANK]
    ckv = a[:, MLA_Q_RANK:MLA_Q_RANK + MLA_KV_RANK]
    kpe = a[:, MLA_Q_RANK + MLA_KV_RANK:MLA_Q_RANK + MLA_KV_RANK + LANES]
    ckv_n = _rms(ckv, kvg_ref[...])
    if emit_cache:
        ckv_ref[...] = ckv_n
        kpe_ref[...] = kpe[:, :MLA_ROPE]
    if rope:
        kpe_sw = a[:, MLA_Q_RANK + MLA_KV_RANK + LANES:MLA_Q_RANK + MLA_KV_RANK + 2 * LANES]
        kpe = kpe * ck_ref[...] + kpe_sw * sk_ref[...]
    kpe = kpe.astype(BF16)

    qn = _rms(qa, qg_ref[...]).astype(BF16)
    nh = MLA_HEADS
    q_nope = _dot(qn, wq_ref[:, :nh * LANES])
    q_pe = _dot(qn, wq_ref[:, nh * LANES:2 * nh * LANES])
    if rope:
        q_sw = _dot(qn, wq_ref[:, 2 * nh * LANES:3 * nh * LANES])
        cos, sin = ck_ref[...] * MLA_Q_SCALE, sk_ref[...] * MLA_Q_SCALE
    kv = _dot(ckv_n.astype(BF16), wkv_ref[...])
    for hd in range(nh):
        lo = hd * MLA_HEAD_PAD
        hs = slice(hd * LANES, (hd + 1) * LANES)
        if rope:
            q_pe_h = q_pe[:, hs] * cos + q_sw[:, hs] * sin
        else:
            q_pe_h = q_pe[:, hs] * MLA_Q_SCALE
        q_ref[:, lo:lo + LANES] = (q_nope[:, hs] * MLA_Q_SCALE).astype(BF16)
        q_ref[:, lo + LANES:lo + 2 * LANES] = q_pe_h.astype(BF16)
        k_ref[:, lo:lo + LANES] = kv[:, hd * LANES:(hd + 1) * LANES].astype(BF16)
        k_ref[:, lo + LANES:lo + 2 * LANES] = kpe
    v_ref[...] = kv[:, nh * LANES:].astype(BF16)


def _mla_proj_call(x, x_tile0, mod, g, wts, rows, *, tile0, n_tiles, rope_tabs, emit_cache):
    tm, d = rows.tm, x.shape[1]
    wa, qg, kvg, wq, wkv = wts
    rope = rope_tabs is not None
    full = lambda arr: _resident(arr.shape, lambda i: (0,) * arr.ndim)
    in_specs = [
        pl.BlockSpec((tm, d), lambda i: (i + x_tile0, 0)),
        pl.BlockSpec((1, 6, d), lambda i: (rows.group(i + tile0), 0, 0)),
        full(g), full(wa), full(qg), full(kvg), full(wq), full(wkv),
    ]
    args = [x, mod, g, wa, qg, kvg, wq, wkv]
    if rope:
        for tab in rope_tabs:
            in_specs.append(pl.BlockSpec((tm, tab.shape[1]), lambda i: (i % rows.seq_tiles, 0)))
            args.append(tab)
    m = n_tiles * tm
    hp = MLA_HEADS * MLA_HEAD_PAD
    out_specs = [
        pl.BlockSpec((tm, hp), lambda i: (i, 0)),
        pl.BlockSpec((tm, hp), lambda i: (i, 0)),
        pl.BlockSpec((tm, MLA_HEADS * MLA_V), lambda i: (i, 0)),
    ]
    out_shape = [
        jax.ShapeDtypeStruct((m, hp), BF16),
        jax.ShapeDtypeStruct((m, hp), BF16),
        jax.ShapeDtypeStruct((m, MLA_HEADS * MLA_V), BF16),
    ]
    if emit_cache:
        out_specs += [pl.BlockSpec((tm, MLA_KV_RANK), lambda i: (i, 0)),
                      pl.BlockSpec((tm, MLA_ROPE), lambda i: (i, 0))]
        out_shape += [jax.ShapeDtypeStruct((m, MLA_KV_RANK), F32),
                      jax.ShapeDtypeStruct((m, MLA_ROPE), F32)]
    return pl.pallas_call(
        functools.partial(_mla_proj_kernel, rope=rope, emit_cache=emit_cache),
        grid=(n_tiles,),
        in_specs=in_specs,
        out_specs=out_specs,
        out_shape=out_shape,
        compiler_params=_cparams(("arbitrary",)),
        name="mla_proj_lat" if rope else "mla_proj_ctx",
    )(*args)


def _cache_expand_kernel(ckv_ref, kpe_ref, wkv_ref, k_ref, v_ref):
    kv = _dot(ckv_ref[...].astype(BF16), wkv_ref[...])
    kpe = kpe_ref[...].astype(BF16)
    nh = MLA_HEADS
    for hd in range(nh):
        lo = hd * MLA_HEAD_PAD
        k_ref[:, lo:lo + LANES] = kv[:, hd * LANES:(hd + 1) * LANES].astype(BF16)
        k_ref[:, lo + LANES:lo + 2 * LANES] = kpe
    v_ref[...] = kv[:, nh * LANES:].astype(BF16)


def _cache_expand_call(ckv, kpe_pad, wkv):
    m = ckv.shape[0]
    tm = min(m, 512)
    hp = MLA_HEADS * MLA_HEAD_PAD
    return pl.pallas_call(
        _cache_expand_kernel,
        grid=(m // tm,),
        in_specs=[
            pl.BlockSpec((tm, MLA_KV_RANK), lambda i: (i, 0)),
            pl.BlockSpec((tm, LANES), lambda i: (i, 0)),
            _resident(wkv.shape, lambda i: (0, 0)),
        ],
        out_specs=[pl.BlockSpec((tm, hp), lambda i: (i, 0)),
                   pl.BlockSpec((tm, MLA_HEADS * MLA_V), lambda i: (i, 0))],
        out_shape=[jax.ShapeDtypeStruct((m, hp), BF16),
                   jax.ShapeDtypeStruct((m, MLA_HEADS * MLA_V), BF16)],
        compiler_params=_cparams(("arbitrary",)),
        name="mla_cache_expand",
    )(ckv, kpe_pad, wkv)


def _attn_kernel(*refs, n_seg, nb, tq, sks):
    q_ref = refs[0]
    k_refs = refs[1:1 + n_seg]
    v_refs = refs[1 + n_seg:1 + 2 * n_seg]
    o_ref = refs[1 + 2 * n_seg]
    s_ref = refs[2 + 2 * n_seg]
    problems = [(b, h) for b in range(nb) for h in range(MLA_HEADS)]
    seg_cols = [sum(sks[:i]) for i in range(n_seg)]

    def scores(idx):
        b, h = problems[idx]
        q = q_ref[b * tq:(b + 1) * tq, h * MLA_HEAD_PAD:(h + 1) * MLA_HEAD_PAD]
        m_lane = None
        for k_ref, sk, c0 in zip(k_refs, sks, seg_cols):
            s = _dot_nt(q, k_ref[b * sk:(b + 1) * sk, h * MLA_HEAD_PAD:(h + 1) * MLA_HEAD_PAD])
            s_ref[idx % 2, :, c0:c0 + sk] = s
            for c in range(sk // LANES):
                piece = s[:, c * LANES:(c + 1) * LANES]
                m_lane = piece if m_lane is None else jnp.maximum(m_lane, piece)
        return m_lane.max(axis=-1, keepdims=True)

    def weighted_values(idx, m):
        b, h = problems[idx]
        acc = None
        for v_ref, sk, c0 in zip(v_refs, sks, seg_cols):
            p = jnp.exp2(s_ref[idx % 2, :, c0:c0 + sk] - m).astype(BF16)
            v = v_ref[b * sk:(b + 1) * sk, h * MLA_V:(h + 1) * MLA_V]
            part = _dot(p, jnp.concatenate([v, jnp.ones_like(v)], axis=1))
            acc = part if acc is None else acc + part
        o = acc[:, :MLA_V] / acc[:, MLA_V:]
        o_ref[b * tq:(b + 1) * tq, h * MLA_V:(h + 1) * MLA_V] = o.astype(o_ref.dtype)

    m = scores(0)
    for idx in range(len(problems)):
        m_next = scores(idx + 1) if idx + 1 < len(problems) else None
        weighted_values(idx, m)
        m = m_next


def _attn_call(q, ks, vs, *, n_batch, seq, sks, nb, tq, name):
    n_seg = len(ks)
    tiles = seq // tq
    assert nb == 1 or tiles == 1
    hp, hv = MLA_HEADS * MLA_HEAD_PAD, MLA_HEADS * MLA_V
    in_specs = [pl.BlockSpec((nb * tq, hp), lambda b, t: (b * tiles + t, 0))]
    in_specs += [pl.BlockSpec((nb * sk, hp), lambda b, t: (b, 0)) for sk in sks]
    in_specs += [pl.BlockSpec((nb * sk, hv), lambda b, t: (b, 0)) for sk in sks]
    return pl.pallas_call(
        functools.partial(_attn_kernel, n_seg=n_seg, nb=nb, tq=tq, sks=tuple(sks)),
        grid=(n_batch // nb, tiles),
        in_specs=in_specs,
        out_specs=pl.BlockSpec((nb * tq, hv), lambda b, t: (b * tiles + t, 0)),
        out_shape=jax.ShapeDtypeStruct((n_batch * seq, hv), BF16),
        scratch_shapes=[pltpu.VMEM((2, tq, sum(sks)), F32)],
        compiler_params=_cparams(("arbitrary", "arbitrary")),
        name=name,
    )(q, *ks, *vs)


def _split_specs(rows, width):
    ct = rows.ctx_tiles
    return [pl.BlockSpec((rows.tm, width), lambda i: (jnp.minimum(i, ct - 1), 0)),
            pl.BlockSpec((rows.tm, width), lambda i: (jnp.maximum(i - ct, 0), 0))]


def _out_res_kernel(*refs, ctx_tiles, split_x):
    it = iter(refs)
    yc_ref, yl_ref, w_ref = next(it), next(it), next(it)
    xc_ref = next(it)
    xl_ref = next(it) if split_x else xc_ref
    mod_ref, o_ref = next(it), next(it)

    def emit(y_ref, x_ref):
        o_ref[...] = x_ref[...] + mod_ref[0, 2:3, :] * _dot(y_ref[...], w_ref[0])

    is_ctx = pl.program_id(0) < ctx_tiles
    pl.when(is_ctx)(lambda: emit(yc_ref, xc_ref))
    pl.when(jnp.logical_not(is_ctx))(lambda: emit(yl_ref, xl_ref))


def _out_res_call(y_ctx, y_lat, w_all, layer, x, mod, rows, name):
    split_x = isinstance(x, tuple)
    tm = rows.tm
    _, k, d = w_all.shape
    x_specs = _split_specs(rows, d) if split_x else [pl.BlockSpec((tm, d), lambda i: (i, 0))]
    xs = list(x) if split_x else [x]
    return pl.pallas_call(
        functools.partial(_out_res_kernel, ctx_tiles=rows.ctx_tiles, split_x=split_x),
        grid=(rows.n_tiles,),
        in_specs=_split_specs(rows, k) + [_resident((1, k, d), lambda i: (layer, 0, 0))] + x_specs + [
            pl.BlockSpec((1, 6, d), lambda i: (rows.group(i), 0, 0)),
        ],
        out_specs=pl.BlockSpec((tm, d), lambda i: (i, 0)),
        out_shape=jax.ShapeDtypeStruct((rows.n_tiles * tm, d), F32),
        compiler_params=_cparams(("arbitrary",)),
        name=name,
    )(y_ctx, y_lat, w_all, *xs, mod)


def _ffn_kernel(*refs, th, final, ctx_tiles):
    x_ref, mod_ref, g_ref, win_ref, wout_ref, fg_ref = refs[:6]
    out_refs, acc_ref, act_ref = refs[6:-2], refs[-2], refs[-1]
    x = x_ref[...]
    h = _modulate(x, g_ref[...], mod_ref[0, 3:4, :], mod_ref[0, 4:5, :]).astype(BF16)
    hidden = wout_ref.shape[1]
    for c in range(hidden // th):
        a = _dot(h, win_ref[0, :, c * th:(c + 1) * th])
        b = _dot(h, win_ref[0, :, hidden + c * th:hidden + (c + 1) * th])
        act_ref[:, c * th:(c + 1) * th] = (_silu(a) * b).astype(BF16)
    y = x + mod_ref[0, 5:6, :] * _dot(act_ref[...], wout_ref[0])
    if not final:
        out_refs[0][...] = y
    else:
        acc_ref[...] = _rms(y, fg_ref[...])
        is_ctx = pl.program_id(0) < ctx_tiles

        @pl.when(is_ctx)
        def _():
            out_refs[0][...] = acc_ref[...]

        @pl.when(jnp.logical_not(is_ctx))
        def _():
            out_refs[1][...] = acc_ref[...]


def _ffn_call(x, mod, g, w_in_all, w_out_all, layer, final_g, rows, final):
    tm, d = rows.tm, x.shape[1]
    if final:
        out_specs = _split_specs(rows, d)
        out_shape = [jax.ShapeDtypeStruct((rows.ctx_tiles * tm, d), F32),
                     jax.ShapeDtypeStruct(((rows.n_tiles - rows.ctx_tiles) * tm, d), F32)]
    else:
        out_specs = pl.BlockSpec((tm, d), lambda i: (i, 0))
        out_shape = jax.ShapeDtypeStruct(x.shape, F32)
    return pl.pallas_call(
        functools.partial(_ffn_kernel, th=2 * LANES, final=final, ctx_tiles=rows.ctx_tiles),
        grid=(rows.n_tiles,),
        in_specs=[
            pl.BlockSpec((tm, d), lambda i: (i, 0)),
            pl.BlockSpec((1, 6, d), lambda i: (rows.group(i), 0, 0)),
            _resident(g.shape, lambda i: (0, 0)),
            _resident((1,) + w_in_all.shape[1:], lambda i: (layer, 0, 0)),
            _resident((1,) + w_out_all.shape[1:], lambda i: (layer, 0, 0)),
            _resident(final_g.shape, lambda i: (0, 0)),
        ],
        out_specs=out_specs,
        out_shape=out_shape,
        scratch_shapes=[pltpu.VMEM((tm, d), F32), pltpu.VMEM((tm, w_out_all.shape[1]), BF16)],
        compiler_params=_cparams(("arbitrary",)),
        name="ffn_final" if final else "ffn",
    )(x, mod, g, w_in_all, w_out_all, final_g)


def _conv_kernel(x_ref, mod_ref, g_ref, win_ref, cw_ref, wout_ref, o_ref, h_ref, *,
                 tn, ctx_tiles, ctx_seq):
    tm, d = x_ref.shape
    x = x_ref[...]
    h_ref[...] = _modulate(x, g_ref[...], mod_ref[0, 0:1, :], mod_ref[0, 1:2, :]).astype(BF16)
    seq = jnp.where(pl.program_id(0) < ctx_tiles, ctx_seq, tm)
    pos = lax.broadcasted_iota(jnp.int32, (tm, 1), 0) & (seq - 1)
    has_prev = pos != 0
    has_next = pos != seq - 1
    for c in range(d // tn):
        sl = slice(c * tn, (c + 1) * tn)
        h = h_ref[...]
        bg = _dot(h, win_ref[:, c * tn:(c + 1) * tn])
        cg = _dot(h, win_ref[:, d + c * tn:d + (c + 1) * tn])
        u = _dot(h, win_ref[:, 2 * d + c * tn:2 * d + (c + 1) * tn])
        z = cg * u
        z_prev = jnp.where(has_prev, pltpu.roll(z, 1, 0), 0.0)
        z_next = jnp.where(has_next, pltpu.roll(z, tm - 1, 0), 0.0)
        conv = z_prev * cw_ref[0:1, sl] + z * cw_ref[1:2, sl] + z_next * cw_ref[2:3, sl]
        part = _dot((bg * conv).astype(BF16), wout_ref[sl, :])
        if c == 0:
            o_ref[...] = part
        else:
            o_ref[...] += part
    o_ref[...] = x_ref[...] + mod_ref[0, 2:3, :] * o_ref[...]


def _conv_call(x, mod, g, w_in, conv_w, w_out, rows, ctx_seq):
    tm, d = rows.tm, x.shape[1]
    return pl.pallas_call(
        functools.partial(_conv_kernel, tn=2 * LANES, ctx_tiles=rows.ctx_tiles, ctx_seq=ctx_seq),
        grid=(rows.n_tiles,),
        in_specs=[
            pl.BlockSpec((tm, d), lambda i: (i, 0)),
            pl.BlockSpec((1, 6, d), lambda i: (rows.group(i), 0, 0)),
            _resident(g.shape, lambda i: (0, 0)),
            _resident(w_in.shape, lambda i: (0, 0)),
            _resident(conv_w.shape, lambda i: (0, 0)),
            _resident(w_out.shape, lambda i: (0, 0)),
        ],
        out_specs=pl.BlockSpec((tm, d), lambda i: (i, 0)),
        out_shape=jax.ShapeDtypeStruct(x.shape, F32),
        scratch_shapes=[pltpu.VMEM((tm, d), BF16)],
        compiler_params=_cparams(("arbitrary",)),
        name="conv_mixer",
    )(x, mod, g, w_in, conv_w, w_out)


def _ret_proj_kernel(x_ref, mod_ref, g_ref, w_ref, qkv_ref, gate_ref, *, tn):
    h = _modulate(x_ref[...], g_ref[...], mod_ref[0, 0:1, :], mod_ref[0, 1:2, :]).astype(BF16)
    n_qkv = qkv_ref.shape[1]
    for c in range(n_qkv // tn):
        qkv_ref[:, c * tn:(c + 1) * tn] = _dot(h, w_ref[:, c * tn:(c + 1) * tn]).astype(BF16)
    for c in range(gate_ref.shape[1] // tn):
        gate = _dot(h, w_ref[:, n_qkv + c * tn:n_qkv + (c + 1) * tn])
        gate_ref[:, c * tn:(c + 1) * tn] = _silu(gate).astype(gate_ref.dtype)


def _ret_proj_call(x, mod, g, w, rows, n_gate):
    tm, d = rows.tm, x.shape[1]
    n_qkv = w.shape[1] - n_gate
    m = x.shape[0]
    return pl.pallas_call(
        functools.partial(_ret_proj_kernel, tn=4 * LANES),
        grid=(rows.n_tiles,),
        in_specs=[
            pl.BlockSpec((tm, d), lambda i: (i, 0)),
            pl.BlockSpec((1, 6, d), lambda i: (rows.group(i), 0, 0)),
            _resident(g.shape, lambda i: (0, 0)),
            _resident(w.shape, lambda i: (0, 0)),
        ],
        out_specs=[pl.BlockSpec((tm, n_qkv), lambda i: (i, 0)),
                   pl.BlockSpec((tm, n_gate), lambda i: (i, 0))],
        out_shape=[jax.ShapeDtypeStruct((m, n_qkv), BF16),
                   jax.ShapeDtypeStruct((m, n_gate), BF16)],
        compiler_params=_cparams(("arbitrary",)),
        name="ret_proj",
    )(x, mod, g, w)


def _ret_scan_kernel(*refs, chunk, hps, dk, dv, has_init, emit_state):
    it = iter(refs)
    lr_ref, q_ref, k_ref, v_ref, gate_ref, gn_ref = (next(it) for _ in range(6))
    s0_ref = next(it) if has_init else None
    y_ref = next(it)
    sout_ref = next(it) if emit_state else None
    o_ref, st_ref = next(it), next(it)

    seq = q_ref.shape[0]
    n_chunks = seq // chunk
    k_scale = dk ** -0.5
    row = lax.broadcasted_iota(jnp.int32, (chunk, chunk), 0).astype(F32)
    col = lax.broadcasted_iota(jnp.int32, (chunk, chunk), 1).astype(F32)
    ridx = lax.broadcasted_iota(jnp.int32, (chunk, 1), 0).astype(F32)
    dist = row - col

    for hd in range(hps):
        qs = slice(hd * dk, (hd + 1) * dk)
        vs = slice(hd * dv, (hd + 1) * dv)
        log_gamma = -jnp.exp(lr_ref[hd])
        lg_f, lg_b = log_gamma[0:1, :], log_gamma[1:2, :]
        mask = jnp.where(dist > 0, jnp.exp(jnp.maximum(dist, 0.0) * lg_f),
                         jnp.where(dist < 0, jnp.exp(jnp.maximum(-dist, 0.0) * lg_b), 2.0)) * k_scale
        q_decay = (jnp.exp((ridx + 1.0) * lg_f), jnp.exp((chunk - ridx) * lg_b))
        k_decay = (jnp.exp((chunk - 1.0 - ridx) * lg_f) * k_scale, jnp.exp(ridx * lg_b) * k_scale)
        chunk_decay = (jnp.exp(chunk * lg_f), jnp.exp(chunk * lg_b))

        for direction in range(2):
            order = range(n_chunks) if direction == 0 else range(n_chunks - 1, -1, -1)
            if has_init:
                st_ref[hd] = s0_ref[0, direction, hd]
            for step, c in enumerate(order):
                rs = slice(c * chunk, (c + 1) * chunk)
                qc, kc, vc = q_ref[rs, qs], k_ref[rs, qs], v_ref[rs, vs]
                have_state = has_init or step > 0
                if direction == 0:
                    scores = _dot_nt(qc, kc) * mask
                    o_ref[rs, vs] = _dot(scores.astype(BF16), vc)
                if have_state:
                    o_ref[rs, vs] += _dot(qc, st_ref[hd].astype(BF16)) * q_decay[direction]
                kd = (kc.astype(F32) * k_decay[direction]).T.astype(BF16)
                update = _dot(kd, vc)
                if have_state:
                    st_ref[hd] = st_ref[hd] * chunk_decay[direction] + update
                else:
                    st_ref[hd] = update
            if emit_state:
                sout_ref[0, 0, direction, hd] = st_ref[hd]

        o = o_ref[:, vs]
        mu = jnp.mean(o, axis=-1, keepdims=True)
        var = jnp.mean(jnp.square(o - mu), axis=-1, keepdims=True)
        on = (o - mu) * lax.rsqrt(var + EPS) * gn_ref[:, vs]
        y_ref[:, vs] = (gate_ref[:, vs].astype(F32) * on).astype(y_ref.dtype)


def _ret_scan_call(log_rate, qkv, gate, gn_g, s0, *, bsz, seq, row0, hps, emit_state, name):
    n_gate = gate.shape[1]
    nh = RET_HEADS
    dv = n_gate // nh
    dk = (qkv.shape[1] - n_gate) // (2 * nh)
    chunk = min(seq, 2 * LANES)
    has_init = s0 is not None
    assert row0 % seq == 0 and nh % hps == 0
    b0 = row0 // seq
    hg = nh // hps
    in_specs = [
        pl.BlockSpec((hps, 2, 1), lambda b, h: (h, 0, 0)),
        pl.BlockSpec((seq, hps * dk), lambda b, h: (b0 + b, h)),
        pl.BlockSpec((seq, hps * dk), lambda b, h: (b0 + b, hg + h)),
        pl.BlockSpec((seq, hps * dv), lambda b, h: (b0 + b, (2 * nh * dk) // (hps * dv) + h)),
        pl.BlockSpec((seq, hps * dv), lambda b, h: (b0 + b, h)),
        pl.BlockSpec((1, hps * dv), lambda b, h: (0, h)),
    ]
    args = [log_rate, qkv, qkv, qkv, gate, gn_g]
    if has_init:
        in_specs.append(pl.BlockSpec((1, 2, hps, dk, dv), lambda b, h: (b, 0, h, 0, 0)))
        args.append(s0)
    out_specs = [pl.BlockSpec((seq, hps * dv), lambda b, h: (b, h))]
    out_shape = [jax.ShapeDtypeStruct((bsz * seq, n_gate), BF16)]
    if emit_state:
        out_specs.append(pl.BlockSpec((1, 1, 2, hps, dk, dv), lambda b, h: (b, 0, 0, h, 0, 0)))
        out_shape.append(jax.ShapeDtypeStruct((bsz, 1, 2, nh, dk, dv), F32))
    return pl.pallas_call(
        functools.partial(_ret_scan_kernel, chunk=chunk, hps=hps, dk=dk, dv=dv,
                          has_init=has_init, emit_state=emit_state),
        grid=(bsz, hg),
        in_specs=in_specs,
        out_specs=out_specs,
        out_shape=out_shape,
        scratch_shapes=[pltpu.VMEM((seq, hps * dv), F32), pltpu.VMEM((hps, dk, dv), F32)],
        compiler_params=_cparams(("arbitrary", "arbitrary")),
        name=name,
    )(*args)


def _rope_swap_index():
    f = ROPE_AXIS_FREQS
    idx = jnp.arange(MLA_ROPE)
    return jnp.where((idx // f) % 2 == 0, idx + f, idx - f)


def _mla_weights(w_a, q_norm_g, kv_norm_g, w_q_b, w_kv_b):
    d = w_a.shape[0]
    swap = _rope_swap_index()
    nq = MLA_Q_RANK + MLA_KV_RANK
    zpad = jnp.zeros((d, LANES - MLA_ROPE), w_a.dtype)
    w_kpe = w_a[:, nq:]
    wa = jnp.concatenate([w_a[:, :nq], w_kpe, zpad, w_kpe[:, swap], zpad], axis=1).astype(BF16)
    wq = w_q_b.reshape(MLA_Q_RANK, MLA_HEADS, MLA_NOPE + MLA_ROPE)
    wq_nope = wq[:, :, :MLA_NOPE].reshape(MLA_Q_RANK, MLA_HEADS * MLA_NOPE)
    wq_pe = wq[:, :, MLA_NOPE:]
    pad = ((0, 0), (0, 0), (0, LANES - MLA_ROPE))
    wq_pe_pad = jnp.pad(wq_pe, pad).reshape(MLA_Q_RANK, MLA_HEADS * LANES)
    wq_sw_pad = jnp.pad(wq_pe[:, :, swap], pad).reshape(MLA_Q_RANK, MLA_HEADS * LANES)
    wq_all = jnp.concatenate([wq_nope, wq_pe_pad, wq_sw_pad], axis=1).astype(BF16)
    wkv = w_kv_b.reshape(MLA_KV_RANK, MLA_HEADS, MLA_NOPE + MLA_V)
    wkv_all = jnp.concatenate(
        [wkv[:, :, :MLA_NOPE].reshape(MLA_KV_RANK, -1), wkv[:, :, MLA_NOPE:].reshape(MLA_KV_RANK, -1)],
        axis=1).astype(BF16)
    return wa, q_norm_g[None, :], kv_norm_g[None, :], wq_all, wkv_all


def _rope_tables(n_tokens):
    f = ROPE_AXIS_FREQS
    rows = n_tokens // GRID_W
    r = jnp.repeat(jnp.arange(rows, dtype=F32), GRID_W)
    col = jnp.tile(jnp.arange(GRID_W, dtype=F32), rows)
    inv = ROPE_THETA ** (-jnp.arange(f, dtype=F32) / f)
    ang_r, ang_c = r[:, None] * inv, col[:, None] * inv
    cos = jnp.concatenate([jnp.cos(ang_r)] * 2 + [jnp.cos(ang_c)] * 2, axis=1)
    sin = jnp.concatenate([-jnp.sin(ang_r), jnp.sin(ang_r), -jnp.sin(ang_c), jnp.sin(ang_c)], axis=1)
    zpad = jnp.zeros((n_tokens, LANES - MLA_ROPE), F32)
    return jnp.concatenate([cos, zpad], axis=1), jnp.concatenate([sin, zpad], axis=1)


def kernel(x_prompt, x_sample, c, c_ctx, cache_mla_ckv, cache_mla_kpe, state_ret, ada_w, ada_b, norm_mix_g, norm_ffn_g, mla_w_a, mla_q_norm_g, mla_kv_norm_g, mla_w_q_b, mla_w_kv_b, mla_w_o, conv_w_in, conv_w, conv_w_out, ret_w_in, ret_log_rate, ret_gn_g, ret_w_out, ffn_w_in, ffn_w_out, final_norm_g):
    batch, seq, d = x_prompt.shape
    dec_batch, dec_seq, _ = x_sample.shape
    depth = ada_w.shape[0]
    n_ctx = batch * seq
    n_lat = dec_batch * dec_seq
    past = cache_mla_ckv.shape[2]
    assert 1 + dec_batch <= MOD_ROWS

    cvec = jnp.zeros((MOD_ROWS, d), F32).at[0].set(c_ctx).at[1:1 + dec_batch].set(c)
    mod_all = _ada_call(cvec, ada_w, ada_b).reshape(depth, MOD_ROWS, 6, d)

    rows_s = _Rows(n_ctx, dec_batch, dec_seq, min(512, dec_seq))
    rows_l = _Rows(n_ctx, dec_batch, dec_seq, min(1024, dec_seq))
    rows_seq = _Rows(n_ctx, dec_batch, dec_seq, dec_seq)
    rope_tabs = _rope_tables(dec_seq)
    final_g = final_norm_g[None, :]
    ffn_w_in_bf, ffn_w_out_bf = ffn_w_in.astype(BF16), ffn_w_out.astype(BF16)
    mla_w_o_bf, ret_w_out_bf = mla_w_o.astype(BF16), ret_w_out.astype(BF16)

    x = (x_prompt.reshape(n_ctx, d), x_sample.reshape(n_lat, d))
    new_ckv, new_kpe, new_ret = [], [], []
    for i in range(depth):
        kind, j = i % N_MIXERS, i // N_MIXERS
        mod = mod_all[i]
        g_mix = norm_mix_g[i][None, :]
        if kind == 0:
            wts = _mla_weights(mla_w_a[j], mla_q_norm_g[j], mla_kv_norm_g[j], mla_w_q_b[j], mla_w_kv_b[j])
            split = isinstance(x, tuple)
            qc, kc, vc, ckv_c, kpe_c = _mla_proj_call(
                x[0] if split else x, 0, mod, g_mix, wts, rows_s, tile0=0, n_tiles=rows_s.ctx_tiles,
                rope_tabs=None, emit_cache=True)
            ql, kl, vl = _mla_proj_call(
                x[1] if split else x, 0 if split else rows_s.ctx_tiles, mod, g_mix, wts, rows_s,
                tile0=rows_s.ctx_tiles, n_tiles=rows_s.n_tiles - rows_s.ctx_tiles,
                rope_tabs=rope_tabs, emit_cache=False)
            new_ckv.append(ckv_c.reshape(batch, seq, MLA_KV_RANK))
            new_kpe.append(kpe_c.reshape(batch, seq, MLA_ROPE))
            kpe_pad = jnp.pad(cache_mla_kpe[:, j], ((0, 0), (0, 0), (0, LANES - MLA_ROPE)))
            kp, vp = _cache_expand_call(
                cache_mla_ckv[:, j].reshape(dec_batch * past, MLA_KV_RANK),
                kpe_pad.reshape(dec_batch * past, LANES), wts[4])
            o_c = _attn_call(qc, [kc], [vc], n_batch=batch, seq=seq, sks=[seq],
                             nb=min(4, batch), tq=seq, name="attn_ctx")
            o_l = _attn_call(ql, [kp, kl], [vp, vl], n_batch=dec_batch, seq=dec_seq,
                             sks=[past, dec_seq], nb=1, tq=min(256, dec_seq), name="attn_lat")
            x = _out_res_call(o_c, o_l, mla_w_o_bf, j, x, mod, rows_l, "mla_out")
        elif kind == 1:
            assert not isinstance(x, tuple)
            x = _conv_call(x, mod, g_mix, conv_w_in[j].astype(BF16), conv_w[j],
                           conv_w_out[j].astype(BF16), rows_seq, seq)
        else:
            n_gate = ret_w_out.shape[1]
            qkv, gate = _ret_proj_call(x, mod, g_mix, ret_w_in[j].astype(BF16), rows_s, n_gate)
            lr = ret_log_rate[j].T[:, :, None]
            gn = ret_gn_g[j][None, :]
            y_c, st = _ret_scan_call(lr, qkv, gate, gn, None, bsz=batch, seq=seq, row0=0,
                                     hps=RET_HEADS, emit_state=True, name="ret_scan_ctx")
            (y_l,) = _ret_scan_call(lr, qkv, gate, gn, state_ret[:, j], bsz=dec_batch, seq=dec_seq,
                                    row0=n_ctx, hps=1, emit_state=False, name="ret_scan_lat")
            new_ret.append(st)
            x = _out_res_call(y_c, y_l, ret_w_out_bf, j, x, mod, rows_l, "ret_out")
        x = _ffn_call(x, mod, norm_ffn_g[i][None, :], ffn_w_in_bf, ffn_w_out_bf, i, final_g, rows_l,
                      final=(i == depth - 1))

    y_prompt = x[0].reshape(batch, seq, d)
    y_sample = x[1].reshape(dec_batch, dec_seq, d)
    return (y_prompt, y_sample, jnp.stack(new_ckv, axis=1), jnp.stack(new_kpe, axis=1),
            jnp.concatenate(new_ret, axis=1))
```

```python
import functools
import math

import jax
import jax.numpy as jnp
import numpy as np
from jax import lax
from jax.experimental import pallas as pl
from jax.experimental.pallas import tpu as pltpu

F32 = jnp.float32
BF16 = jnp.bfloat16

N_MIXERS = 3
MLA_HEADS = 8
MLA_NOPE = 128
MLA_ROPE = 64
MLA_V = 128
MLA_Q_RANK = 384
MLA_KV_RANK = 256
MLA_SCALE = (MLA_NOPE + MLA_ROPE) ** -0.5
MLA_Q_SCALE = MLA_SCALE * 1.4426950408889634
ROPE_THETA = 10000.0
ROPE_AXIS_FREQS = MLA_ROPE // 4
GRID_W = 64
RET_HEADS = 4
EPS = 1e-6

LANES = 128
MLA_HEAD_PAD = 2 * LANES
MOD_ROWS = 8
VMEM_LIMIT = 56 * 1024 * 1024


def _cparams(sem):
    return pltpu.CompilerParams(dimension_semantics=sem, vmem_limit_bytes=VMEM_LIMIT)


def _resident(shape, index_map):
    return pl.BlockSpec(shape, index_map, pipeline_mode=pl.Buffered(1))


def _rms(x, g):
    return x * lax.rsqrt(jnp.mean(x * x, axis=-1, keepdims=True) + EPS) * g


def _modulate(x, g, shift, scale):
    return _rms(x, g) * (1.0 + scale) + shift


def _silu(x):
    return x * jax.nn.sigmoid(x)


def _dot(a, b):
    return jnp.dot(a, b, preferred_element_type=F32)


def _dot_nt(a, b):
    return lax.dot_general(a, b, (((1,), (1,)), ((), ())), preferred_element_type=F32)


MAX_CAST_SLABS = 16


def _cast_specs(jobs, n_steps, step_of):
    n_slabs = math.gcd(n_steps, MAX_CAST_SLABS)
    per = n_steps // n_slabs

    def slab_of_step(*g):
        return step_of(*g) // per

    in_specs, out_specs, out_shape = [], [], []
    for w, layer in jobs:
        _, k, n = w.shape
        blk = (1, k // n_slabs, n)
        in_specs.append(pl.BlockSpec(blk, lambda *g, layer=layer: (layer, slab_of_step(*g), 0)))
        out_specs.append(pl.BlockSpec(blk, lambda *g: (0, slab_of_step(*g), 0)))
        out_shape.append(jax.ShapeDtypeStruct((1, k, n), BF16))
    return in_specs, out_specs, out_shape


def _run_casts(in_refs, out_refs):
    for src, dst in zip(in_refs, out_refs):
        dst[...] = src[...].astype(BF16)


def _ada_kernel(c_ref, w_ref, b_ref, o_ref):
    a = _silu(c_ref[...]).astype(BF16)
    o_ref[0] = _dot(a, w_ref[0].astype(BF16)) + b_ref[0]


def _ada_call(cvec, ada_w, ada_b):
    depth, d, n = ada_w.shape
    tn = d
    return pl.pallas_call(
        _ada_kernel,
        grid=(depth, n // tn),
        in_specs=[
            pl.BlockSpec((MOD_ROWS, d), lambda l, j: (0, 0)),
            pl.BlockSpec((1, d, tn), lambda l, j: (l, 0, j)),
            pl.BlockSpec((1, 1, tn), lambda l, j: (l, 0, j)),
        ],
        out_specs=pl.BlockSpec((1, MOD_ROWS, tn), lambda l, j: (l, 0, j)),
        out_shape=jax.ShapeDtypeStruct((depth, MOD_ROWS, n), F32),
        compiler_params=_cparams(("arbitrary", "arbitrary")),
        name="ada",
    )(cvec, ada_w, ada_b.reshape(depth, 1, n))


class _Rows:
    def __init__(self, n_ctx, dec_batch, dec_seq, tm):
        assert n_ctx % tm == 0 and dec_seq % tm == 0
        self.tm = tm
        self.ctx_tiles = n_ctx // tm
        self.seq_tiles = dec_seq // tm
        self.n_tiles = self.ctx_tiles + dec_batch * self.seq_tiles

    def group(self, i):
        lat = jnp.maximum(i - self.ctx_tiles, 0) // self.seq_tiles
        return jnp.where(i < self.ctx_tiles, 0, 1 + lat)


def _mla_proj_kernel(*refs, rope, emit_cache):
    it = iter(refs)
    x_ref, mod_ref, g_ref, wa_ref, qg_ref, kvg_ref, wq_ref, wkv_ref = (next(it) for _ in range(8))
    if rope:
        ck_ref, sk_ref = (next(it) for _ in range(2))
    q_ref, k_ref, v_ref = (next(it) for _ in range(3))
    if emit_cache:
        ckv_ref, kpe_ref = (next(it) for _ in range(2))

    h = _modulate(x_ref[...], g_ref[...], mod_ref[0, 0:1, :], mod_ref[0, 1:2, :]).astype(BF16)
    a = _dot(h, wa_ref[...])
    qa = a[:, :MLA_Q_RANK]
    ckv = a[:, MLA_Q_RANK:MLA_Q_RANK + MLA_KV_RANK]
    kpe = a[:, MLA_Q_RANK + MLA_KV_RANK:MLA_Q_RANK + MLA_KV_RANK + LANES]
    ckv_n = _rms(ckv, kvg_ref[...])
    if emit_cache:
        ckv_ref[...] = ckv_n
        kpe_ref[...] = kpe[:, :MLA_ROPE]
    if rope:
        kpe_sw = a[:, MLA_Q_RANK + MLA_KV_RANK + LANES:MLA_Q_RANK + MLA_KV_RANK + 2 * LANES]
        kpe = kpe * ck_ref[...] + kpe_sw * sk_ref[...]
    kpe = kpe.astype(BF16)

    qn = _rms(qa, qg_ref[...]).astype(BF16)
    nh = MLA_HEADS
    q_nope = _dot(qn, wq_ref[:, :nh * LANES])
    q_pe = _dot(qn, wq_ref[:, nh * LANES:2 * nh * LANES])
    if rope:
        q_sw = _dot(qn, wq_ref[:, 2 * nh * LANES:3 * nh * LANES])
        cos, sin = ck_ref[...] * MLA_Q_SCALE, sk_ref[...] * MLA_Q_SCALE
    kv = _dot(ckv_n.astype(BF16), wkv_ref[...])
    for hd in range(nh):
        lo = hd * MLA_HEAD_PAD
        hs = slice(hd * LANES, (hd + 1) * LANES)
        if rope:
            q_pe_h = q_pe[:, hs] * cos + q_sw[:, hs] * sin
        else:
            q_pe_h = q_pe[:, hs] * MLA_Q_SCALE
        q_ref[:, lo:lo + LANES] = (q_nope[:, hs] * MLA_Q_SCALE).astype(BF16)
        q_ref[:, lo + LANES:lo + 2 * LANES] = q_pe_h.astype(BF16)
        k_ref[:, lo:lo + LANES] = kv[:, hd * LANES:(hd + 1) * LANES].astype(BF16)
        k_ref[:, lo + LANES:lo + 2 * LANES] = kpe
    v_ref[...] = kv[:, nh * LANES:].astype(BF16)


def _mla_proj_call(x, x_tile0, mod, g, wts, rows, *, tile0, n_tiles, rope_tabs, emit_cache):
    tm, d = rows.tm, x.shape[1]
    wa, qg, kvg, wq, wkv = wts
    rope = rope_tabs is not None
    full = lambda arr: _resident(arr.shape, lambda i: (0,) * arr.ndim)
    in_specs = [
        pl.BlockSpec((tm, d), lambda i: (i + x_tile0, 0)),
        pl.BlockSpec((1, 6, d), lambda i: (rows.group(i + tile0), 0, 0)),
        full(g), full(wa), full(qg), full(kvg), full(wq), full(wkv),
    ]
    args = [x, mod, g, wa, qg, kvg, wq, wkv]
    if rope:
        for tab in rope_tabs:
            in_specs.append(pl.BlockSpec((tm, tab.shape[1]), lambda i: (i % rows.seq_tiles, 0)))
            args.append(tab)
    m = n_tiles * tm
    hp = MLA_HEADS * MLA_HEAD_PAD
    out_specs = [
        pl.BlockSpec((tm, hp), lambda i: (i, 0)),
        pl.BlockSpec((tm, hp), lambda i: (i, 0)),
        pl.BlockSpec((tm, MLA_HEADS * MLA_V), lambda i: (i, 0)),
    ]
    out_shape = [
        jax.ShapeDtypeStruct((m, hp), BF16),
        jax.ShapeDtypeStruct((m, hp), BF16),
        jax.ShapeDtypeStruct((m, MLA_HEADS * MLA_V), BF16),
    ]
    if emit_cache:
        out_specs += [pl.BlockSpec((tm, MLA_KV_RANK), lambda i: (i, 0)),
                      pl.BlockSpec((tm, MLA_ROPE), lambda i: (i, 0))]
        out_shape += [jax.ShapeDtypeStruct((m, MLA_KV_RANK), F32),
                      jax.ShapeDtypeStruct((m, MLA_ROPE), F32)]
    return pl.pallas_call(
        functools.partial(_mla_proj_kernel, rope=rope, emit_cache=emit_cache),
        grid=(n_tiles,),
        in_specs=in_specs,
        out_specs=out_specs,
        out_shape=out_shape,
        compiler_params=_cparams(("arbitrary",)),
        name="mla_proj_lat" if rope else "mla_proj_ctx",
    )(*args)


def _cache_expand_kernel(ckv_ref, kpe_ref, wkv_ref, k_ref, v_ref):
    kv = _dot(ckv_ref[...].astype(BF16), wkv_ref[...])
    kpe = kpe_ref[...].astype(BF16)
    nh = MLA_HEADS
    for hd in range(nh):
        lo = hd * MLA_HEAD_PAD
        k_ref[:, lo:lo + LANES] = kv[:, hd * LANES:(hd + 1) * LANES].astype(BF16)
        k_ref[:, lo + LANES:lo + 2 * LANES] = kpe
    v_ref[...] = kv[:, nh * LANES:].astype(BF16)


def _cache_expand_call(ckv, kpe_pad, wkv):
    m = ckv.shape[0]
    tm = min(m, 512)
    hp = MLA_HEADS * MLA_HEAD_PAD
    return pl.pallas_call(
        _cache_expand_kernel,
        grid=(m // tm,),
        in_specs=[
            pl.BlockSpec((tm, MLA_KV_RANK), lambda i: (i, 0)),
            pl.BlockSpec((tm, LANES), lambda i: (i, 0)),
            _resident(wkv.shape, lambda i: (0, 0)),
        ],
        out_specs=[pl.BlockSpec((tm, hp), lambda i: (i, 0)),
                   pl.BlockSpec((tm, MLA_HEADS * MLA_V), lambda i: (i, 0))],
        out_shape=[jax.ShapeDtypeStruct((m, hp), BF16),
                   jax.ShapeDtypeStruct((m, MLA_HEADS * MLA_V), BF16)],
        compiler_params=_cparams(("arbitrary",)),
        name="mla_cache_expand",
    )(ckv, kpe_pad, wkv)


def _attn_kernel(*refs, n_seg, nb, tq, sks, n_cast):
    q_ref = refs[0]
    k_refs = refs[1:1 + n_seg]
    v_refs = refs[1 + n_seg:1 + 2 * n_seg]
    n_in = 1 + 2 * n_seg + n_cast
    o_ref = refs[n_in]
    s_ref = refs[n_in + 1 + n_cast]
    _run_casts(refs[1 + 2 * n_seg:n_in], refs[n_in + 1:n_in + 1 + n_cast])
    problems = [(b, h) for b in range(nb) for h in range(MLA_HEADS)]
    seg_cols = [sum(sks[:i]) for i in range(n_seg)]

    def scores(idx):
        b, h = problems[idx]
        q = q_ref[b * tq:(b + 1) * tq, h * MLA_HEAD_PAD:(h + 1) * MLA_HEAD_PAD]
        m_lane = None
        for k_ref, sk, c0 in zip(k_refs, sks, seg_cols):
            s = _dot_nt(q, k_ref[b * sk:(b + 1) * sk, h * MLA_HEAD_PAD:(h + 1) * MLA_HEAD_PAD])
            s_ref[idx % 2, :, c0:c0 + sk] = s
            for c in range(sk // LANES):
                piece = s[:, c * LANES:(c + 1) * LANES]
                m_lane = piece if m_lane is None else jnp.maximum(m_lane, piece)
        return m_lane.max(axis=-1, keepdims=True)

    def weighted_values(idx, m):
        b, h = problems[idx]
        acc = None
        for v_ref, sk, c0 in zip(v_refs, sks, seg_cols):
            p = jnp.exp2(s_ref[idx % 2, :, c0:c0 + sk] - m).astype(BF16)
            v = v_ref[b * sk:(b + 1) * sk, h * MLA_V:(h + 1) * MLA_V]
            part = _dot(p, jnp.concatenate([v, jnp.ones_like(v)], axis=1))
            acc = part if acc is None else acc + part
        o = acc[:, :MLA_V] / acc[:, MLA_V:]
        o_ref[b * tq:(b + 1) * tq, h * MLA_V:(h + 1) * MLA_V] = o.astype(o_ref.dtype)

    m = scores(0)
    for idx in range(len(problems)):
        m_next = scores(idx + 1) if idx + 1 < len(problems) else None
        weighted_values(idx, m)
        m = m_next


def _attn_call(q, ks, vs, *, n_batch, seq, sks, nb, tq, name, cast_jobs=()):
    n_seg = len(ks)
    tiles = seq // tq
    assert nb == 1 or tiles == 1
    hp, hv = MLA_HEADS * MLA_HEAD_PAD, MLA_HEADS * MLA_V
    in_specs = [pl.BlockSpec((nb * tq, hp), lambda b, t: (b * tiles + t, 0))]
    in_specs += [pl.BlockSpec((nb * sk, hp), lambda b, t: (b, 0)) for sk in sks]
    in_specs += [pl.BlockSpec((nb * sk, hv), lambda b, t: (b, 0)) for sk in sks]
    c_in, c_out, c_shape = _cast_specs(cast_jobs, (n_batch // nb) * tiles, lambda b, t: b * tiles + t)
    outs = pl.pallas_call(
        functools.partial(_attn_kernel, n_seg=n_seg, nb=nb, tq=tq, sks=tuple(sks), n_cast=len(cast_jobs)),
        grid=(n_batch // nb, tiles),
        in_specs=in_specs + c_in,
        out_specs=[pl.BlockSpec((nb * tq, hv), lambda b, t: (b * tiles + t, 0))] + c_out,
        out_shape=[jax.ShapeDtypeStruct((n_batch * seq, hv), BF16)] + c_shape,
        scratch_shapes=[pltpu.VMEM((2, tq, sum(sks)), F32)],
        compiler_params=_cparams(("arbitrary", "arbitrary")),
        name=name,
    )(q, *ks, *vs, *[w for w, _ in cast_jobs])
    return outs[0], list(outs[1:])


def _split_specs(rows, width):
    ct = rows.ctx_tiles
    return [pl.BlockSpec((rows.tm, width), lambda i: (jnp.minimum(i, ct - 1), 0)),
            pl.BlockSpec((rows.tm, width), lambda i: (jnp.maximum(i - ct, 0), 0))]


def _out_res_kernel(*refs, ctx_tiles, split_x):
    it = iter(refs)
    yc_ref, yl_ref, w_ref = next(it), next(it), next(it)
    xc_ref = next(it)
    xl_ref = next(it) if split_x else xc_ref
    mod_ref, o_ref = next(it), next(it)

    def emit(y_ref, x_ref):
        o_ref[...] = x_ref[...] + mod_ref[0, 2:3, :] * _dot(y_ref[...], w_ref[0])

    is_ctx = pl.program_id(0) < ctx_tiles
    pl.when(is_ctx)(lambda: emit(yc_ref, xc_ref))
    pl.when(jnp.logical_not(is_ctx))(lambda: emit(yl_ref, xl_ref))


def _out_res_call(y_ctx, y_lat, w_all, layer, x, mod, rows, name):
    split_x = isinstance(x, tuple)
    tm = rows.tm
    _, k, d = w_all.shape
    x_specs = _split_specs(rows, d) if split_x else [pl.BlockSpec((tm, d), lambda i: (i, 0))]
    xs = list(x) if split_x else [x]
    return pl.pallas_call(
        functools.partial(_out_res_kernel, ctx_tiles=rows.ctx_tiles, split_x=split_x),
        grid=(rows.n_tiles,),
        in_specs=_split_specs(rows, k) + [_resident((1, k, d), lambda i: (layer, 0, 0))] + x_specs + [
            pl.BlockSpec((1, 6, d), lambda i: (rows.group(i), 0, 0)),
        ],
        out_specs=pl.BlockSpec((tm, d), lambda i: (i, 0)),
        out_shape=jax.ShapeDtypeStruct((rows.n_tiles * tm, d), F32),
        compiler_params=_cparams(("arbitrary",)),
        name=name,
    )(y_ctx, y_lat, w_all, *xs, mod)


def _ffn_kernel(*refs, th, final, ctx_tiles, n_cast):
    x_ref, mod_ref, g_ref, win_ref, wout_ref, fg_ref = refs[:6]
    cast_in = refs[6:6 + n_cast]
    n_out = 2 if final else 1
    out_refs = refs[6 + n_cast:6 + n_cast + n_out]
    cast_out = refs[6 + n_cast + n_out:6 + 2 * n_cast + n_out]
    scratch = refs[6 + 2 * n_cast + n_out:]
    act_ref = scratch[0]
    _run_casts(cast_in, cast_out)
    x = x_ref[...]
    h = _modulate(x, g_ref[...], mod_ref[0, 3:4, :], mod_ref[0, 4:5, :]).astype(BF16)
    hidden = wout_ref.shape[1]
    for c in range(hidden // th):
        a = _dot(h, win_ref[0, :, c * th:(c + 1) * th])
        b = _dot(h, win_ref[0, :, hidden + c * th:hidden + (c + 1) * th])
        act_ref[:, c * th:(c + 1) * th] = (_silu(a) * b).astype(BF16)
    y = x + mod_ref[0, 5:6, :] * _dot(act_ref[...], wout_ref[0])
    if not final:
        out_refs[0][...] = y
    else:
        acc_ref = scratch[1]
        acc_ref[...] = _rms(y, fg_ref[...])
        is_ctx = pl.program_id(0) < ctx_tiles

        @pl.when(is_ctx)
        def _():
            out_refs[0][...] = acc_ref[...]

        @pl.when(jnp.logical_not(is_ctx))
        def _():
            out_refs[1][...] = acc_ref[...]


def _ffn_call(x, mod, g, w_in_all, w_out_all, layer, final_g, rows, final, cast_jobs=()):
    tm, d = rows.tm, x.shape[1]
    if final:
        out_specs = _split_specs(rows, d)
        out_shape = [jax.ShapeDtypeStruct((rows.ctx_tiles * tm, d), F32),
                     jax.ShapeDtypeStruct(((rows.n_tiles - rows.ctx_tiles) * tm, d), F32)]
        scratch = [pltpu.VMEM((tm, w_out_all.shape[1]), BF16), pltpu.VMEM((tm, d), F32)]
    else:
        out_specs = [pl.BlockSpec((tm, d), lambda i: (i, 0))]
        out_shape = [jax.ShapeDtypeStruct(x.shape, F32)]
        scratch = [pltpu.VMEM((tm, w_out_all.shape[1]), BF16)]
    c_in, c_out, c_shape = _cast_specs(cast_jobs, rows.n_tiles, lambda i: i)
    outs = pl.pallas_call(
        functools.partial(_ffn_kernel, th=2 * LANES, final=final, ctx_tiles=rows.ctx_tiles,
                          n_cast=len(cast_jobs)),
        grid=(rows.n_tiles,),
        in_specs=[
            pl.BlockSpec((tm, d), lambda i: (i, 0)),
            pl.BlockSpec((1, 6, d), lambda i: (rows.group(i), 0, 0)),
            _resident(g.shape, lambda i: (0, 0)),
            _resident((1,) + w_in_all.shape[1:], lambda i: (layer, 0, 0)),
            _resident((1,) + w_out_all.shape[1:], lambda i: (layer, 0, 0)),
            _resident(final_g.shape, lambda i: (0, 0)),
        ] + c_in,
        out_specs=out_specs + c_out,
        out_shape=out_shape + c_shape,
        scratch_shapes=scratch,
        compiler_params=_cparams(("arbitrary",)),
        name="ffn_final" if final else "ffn",
    )(x, mod, g, w_in_all, w_out_all, final_g, *[w for w, _ in cast_jobs])
    n_out = 2 if final else 1
    stream = tuple(outs[:2]) if final else outs[0]
    return stream, list(outs[n_out:])


def _conv_kernel(x_ref, mod_ref, g_ref, win_ref, cw_ref, wout_ref, o_ref, h_ref, *,
                 tn, ctx_tiles, ctx_seq):
    tm, d = x_ref.shape
    x = x_ref[...]
    h_ref[...] = _modulate(x, g_ref[...], mod_ref[0, 0:1, :], mod_ref[0, 1:2, :]).astype(BF16)
    seq = jnp.where(pl.program_id(0) < ctx_tiles, ctx_seq, tm)
    pos = lax.broadcasted_iota(jnp.int32, (tm, 1), 0) & (seq - 1)
    has_prev = pos != 0
    has_next = pos != seq - 1
    for c in range(d // tn):
        sl = slice(c * tn, (c + 1) * tn)
        h = h_ref[...]
        bg = _dot(h, win_ref[:, c * tn:(c + 1) * tn])
        cg = _dot(h, win_ref[:, d + c * tn:d + (c + 1) * tn])
        u = _dot(h, win_ref[:, 2 * d + c * tn:2 * d + (c + 1) * tn])
        z = cg * u
        z_prev = jnp.where(has_prev, pltpu.roll(z, 1, 0), 0.0)
        z_next = jnp.where(has_next, pltpu.roll(z, tm - 1, 0), 0.0)
        conv = z_prev * cw_ref[0:1, sl] + z * cw_ref[1:2, sl] + z_next * cw_ref[2:3, sl]
        part = _dot((bg * conv).astype(BF16), wout_ref[sl, :])
        if c == 0:
            o_ref[...] = part
        else:
            o_ref[...] += part
    o_ref[...] = x_ref[...] + mod_ref[0, 2:3, :] * o_ref[...]


def _conv_call(x, mod, g, w_in, conv_w, w_out, rows, ctx_seq):
    tm, d = rows.tm, x.shape[1]
    return pl.pallas_call(
        functools.partial(_conv_kernel, tn=2 * LANES, ctx_tiles=rows.ctx_tiles, ctx_seq=ctx_seq),
        grid=(rows.n_tiles,),
        in_specs=[
            pl.BlockSpec((tm, d), lambda i: (i, 0)),
            pl.BlockSpec((1, 6, d), lambda i: (rows.group(i), 0, 0)),
            _resident(g.shape, lambda i: (0, 0)),
            _resident(w_in.shape, lambda i: (0, 0)),
            _resident(conv_w.shape, lambda i: (0, 0)),
            _resident(w_out.shape, lambda i: (0, 0)),
        ],
        out_specs=pl.BlockSpec((tm, d), lambda i: (i, 0)),
        out_shape=jax.ShapeDtypeStruct(x.shape, F32),
        scratch_shapes=[pltpu.VMEM((tm, d), BF16)],
        compiler_params=_cparams(("arbitrary",)),
        name="conv_mixer",
    )(x, mod, g, w_in, conv_w, w_out)


def _ret_proj_kernel(x_ref, mod_ref, g_ref, w_ref, qkv_ref, gate_ref, *, tn):
    h = _modulate(x_ref[...], g_ref[...], mod_ref[0, 0:1, :], mod_ref[0, 1:2, :]).astype(BF16)
    n_qkv = qkv_ref.shape[1]
    for c in range(n_qkv // tn):
        qkv_ref[:, c * tn:(c + 1) * tn] = _dot(h, w_ref[:, c * tn:(c + 1) * tn]).astype(BF16)
    for c in range(gate_ref.shape[1] // tn):
        gate = _dot(h, w_ref[:, n_qkv + c * tn:n_qkv + (c + 1) * tn])
        gate_ref[:, c * tn:(c + 1) * tn] = _silu(gate).astype(gate_ref.dtype)


def _ret_proj_call(x, mod, g, w, rows, n_gate):
    tm, d = rows.tm, x.shape[1]
    n_qkv = w.shape[1] - n_gate
    m = x.shape[0]
    return pl.pallas_call(
        functools.partial(_ret_proj_kernel, tn=4 * LANES),
        grid=(rows.n_tiles,),
        in_specs=[
            pl.BlockSpec((tm, d), lambda i: (i, 0)),
            pl.BlockSpec((1, 6, d), lambda i: (rows.group(i), 0, 0)),
            _resident(g.shape, lambda i: (0, 0)),
            _resident(w.shape, lambda i: (0, 0)),
        ],
        out_specs=[pl.BlockSpec((tm, n_qkv), lambda i: (i, 0)),
                   pl.BlockSpec((tm, n_gate), lambda i: (i, 0))],
        out_shape=[jax.ShapeDtypeStruct((m, n_qkv), BF16),
                   jax.ShapeDtypeStruct((m, n_gate), BF16)],
        compiler_params=_cparams(("arbitrary",)),
        name="ret_proj",
    )(x, mod, g, w)


def _ret_scan_kernel(*refs, chunk, hps, dk, dv, has_init, emit_state):
    it = iter(refs)
    lr_ref, q_ref, k_ref, v_ref, gate_ref, gn_ref = (next(it) for _ in range(6))
    s0_ref = next(it) if has_init else None
    y_ref = next(it)
    sout_ref = next(it) if emit_state else None
    o_ref, st_ref = next(it), next(it)

    seq = q_ref.shape[0]
    n_chunks = seq // chunk
    k_scale = dk ** -0.5
    row = lax.broadcasted_iota(jnp.int32, (chunk, chunk), 0).astype(F32)
    col = lax.broadcasted_iota(jnp.int32, (chunk, chunk), 1).astype(F32)
    ridx = lax.broadcasted_iota(jnp.int32, (chunk, 1), 0).astype(F32)
    dist = row - col

    for hd in range(hps):
        qs = slice(hd * dk, (hd + 1) * dk)
        vs = slice(hd * dv, (hd + 1) * dv)
        log_gamma = -jnp.exp(lr_ref[hd])
        lg_f, lg_b = log_gamma[0:1, :], log_gamma[1:2, :]
        mask = jnp.where(dist > 0, jnp.exp(jnp.maximum(dist, 0.0) * lg_f),
                         jnp.where(dist < 0, jnp.exp(jnp.maximum(-dist, 0.0) * lg_b), 2.0)) * k_scale
        q_decay = (jnp.exp((ridx + 1.0) * lg_f), jnp.exp((chunk - ridx) * lg_b))
        k_decay = (jnp.exp((chunk - 1.0 - ridx) * lg_f) * k_scale, jnp.exp(ridx * lg_b) * k_scale)
        chunk_decay = (jnp.exp(chunk * lg_f), jnp.exp(chunk * lg_b))

        for direction in range(2):
            order = range(n_chunks) if direction == 0 else range(n_chunks - 1, -1, -1)
            if has_init:
                st_ref[hd] = s0_ref[0, direction, hd]
            for step, c in enumerate(order):
                rs = slice(c * chunk, (c + 1) * chunk)
                qc, kc, vc = q_ref[rs, qs], k_ref[rs, qs], v_ref[rs, vs]
                have_state = has_init or step > 0
                if direction == 0:
                    scores = _dot_nt(qc, kc) * mask
                    o_ref[rs, vs] = _dot(scores.astype(BF16), vc)
                if have_state:
                    o_ref[rs, vs] += _dot(qc, st_ref[hd].astype(BF16)) * q_decay[direction]
                kd = (kc.astype(F32) * k_decay[direction]).T.astype(BF16)
                update = _dot(kd, vc)
                if have_state:
                    st_ref[hd] = st_ref[hd] * chunk_decay[direction] + update
                else:
                    st_ref[hd] = update
            if emit_state:
                sout_ref[0, 0, direction, hd] = st_ref[hd]

        o = o_ref[:, vs]
        mu = jnp.mean(o, axis=-1, keepdims=True)
        var = jnp.mean(jnp.square(o - mu), axis=-1, keepdims=True)
        on = (o - mu) * lax.rsqrt(var + EPS) * gn_ref[:, vs]
        y_ref[:, vs] = (gate_ref[:, vs].astype(F32) * on).astype(y_ref.dtype)


def _ret_scan_call(log_rate, qkv, gate, gn_g, s0, *, bsz, seq, row0, hps, emit_state, name):
    n_gate = gate.shape[1]
    nh = RET_HEADS
    dv = n_gate // nh
    dk = (qkv.shape[1] - n_gate) // (2 * nh)
    chunk = min(seq, 2 * LANES)
    has_init = s0 is not None
    assert row0 % seq == 0 and nh % hps == 0
    b0 = row0 // seq
    hg = nh // hps
    in_specs = [
        pl.BlockSpec((hps, 2, 1), lambda b, h: (h, 0, 0)),
        pl.BlockSpec((seq, hps * dk), lambda b, h: (b0 + b, h)),
        pl.BlockSpec((seq, hps * dk), lambda b, h: (b0 + b, hg + h)),
        pl.BlockSpec((seq, hps * dv), lambda b, h: (b0 + b, (2 * nh * dk) // (hps * dv) + h)),
        pl.BlockSpec((seq, hps * dv), lambda b, h: (b0 + b, h)),
        pl.BlockSpec((1, hps * dv), lambda b, h: (0, h)),
    ]
    args = [log_rate, qkv, qkv, qkv, gate, gn_g]
    if has_init:
        in_specs.append(pl.BlockSpec((1, 2, hps, dk, dv), lambda b, h: (b, 0, h, 0, 0)))
        args.append(s0)
    out_specs = [pl.BlockSpec((seq, hps * dv), lambda b, h: (b, h))]
    out_shape = [jax.ShapeDtypeStruct((bsz * seq, n_gate), BF16)]
    if emit_state:
        out_specs.append(pl.BlockSpec((1, 1, 2, hps, dk, dv), lambda b, h: (b, 0, 0, h, 0, 0)))
        out_shape.append(jax.ShapeDtypeStruct((bsz, 1, 2, nh, dk, dv), F32))
    return pl.pallas_call(
        functools.partial(_ret_scan_kernel, chunk=chunk, hps=hps, dk=dk, dv=dv,
                          has_init=has_init, emit_state=emit_state),
        grid=(bsz, hg),
        in_specs=in_specs,
        out_specs=out_specs,
        out_shape=out_shape,
        scratch_shapes=[pltpu.VMEM((seq, hps * dv), F32), pltpu.VMEM((hps, dk, dv), F32)],
        compiler_params=_cparams(("arbitrary", "arbitrary")),
        name=name,
    )(*args)


def _rope_swap_index():
    f = ROPE_AXIS_FREQS
    idx = jnp.arange(MLA_ROPE)
    return jnp.where((idx // f) % 2 == 0, idx + f, idx - f)


def _mla_weights(w_a, q_norm_g, kv_norm_g, w_q_b, w_kv_b):
    d = w_a.shape[0]
    swap = _rope_swap_index()
    nq = MLA_Q_RANK + MLA_KV_RANK
    zpad = jnp.zeros((d, LANES - MLA_ROPE), w_a.dtype)
    w_kpe = w_a[:, nq:]
    wa = jnp.concatenate([w_a[:, :nq], w_kpe, zpad, w_kpe[:, swap], zpad], axis=1).astype(BF16)
    wq = w_q_b.reshape(MLA_Q_RANK, MLA_HEADS, MLA_NOPE + MLA_ROPE)
    wq_nope = wq[:, :, :MLA_NOPE].reshape(MLA_Q_RANK, MLA_HEADS * MLA_NOPE)
    wq_pe = wq[:, :, MLA_NOPE:]
    pad = ((0, 0), (0, 0), (0, LANES - MLA_ROPE))
    wq_pe_pad = jnp.pad(wq_pe, pad).reshape(MLA_Q_RANK, MLA_HEADS * LANES)
    wq_sw_pad = jnp.pad(wq_pe[:, :, swap], pad).reshape(MLA_Q_RANK, MLA_HEADS * LANES)
    wq_all = jnp.concatenate([wq_nope, wq_pe_pad, wq_sw_pad], axis=1).astype(BF16)
    wkv = w_kv_b.reshape(MLA_KV_RANK, MLA_HEADS, MLA_NOPE + MLA_V)
    wkv_all = jnp.concatenate(
        [wkv[:, :, :MLA_NOPE].reshape(MLA_KV_RANK, -1), wkv[:, :, MLA_NOPE:].reshape(MLA_KV_RANK, -1)],
        axis=1).astype(BF16)
    return wa, q_norm_g[None, :], kv_norm_g[None, :], wq_all, wkv_all


def _rope_tables(n_tokens):
    f = ROPE_AXIS_FREQS
    f32 = np.float32
    rows = n_tokens // GRID_W
    r = np.repeat(np.arange(rows, dtype=f32), GRID_W)
    col = np.tile(np.arange(GRID_W, dtype=f32), rows)
    inv = (f32(ROPE_THETA) ** (-np.arange(f, dtype=f32) / f32(f))).astype(f32)
    ang_r, ang_c = r[:, None] * inv, col[:, None] * inv
    cos = np.concatenate([np.cos(ang_r)] * 2 + [np.cos(ang_c)] * 2, axis=1)
    sin = np.concatenate([-np.sin(ang_r), np.sin(ang_r), -np.sin(ang_c), np.sin(ang_c)], axis=1)
    zpad = np.zeros((n_tokens, LANES - MLA_ROPE), f32)
    return (jnp.asarray(np.concatenate([cos, zpad], axis=1), F32),
            jnp.asarray(np.concatenate([sin, zpad], axis=1), F32))


def kernel(x_prompt, x_sample, c, c_ctx, cache_mla_ckv, cache_mla_kpe, state_ret, ada_w, ada_b, norm_mix_g, norm_ffn_g, mla_w_a, mla_q_norm_g, mla_kv_norm_g, mla_w_q_b, mla_w_kv_b, mla_w_o, conv_w_in, conv_w, conv_w_out, ret_w_in, ret_log_rate, ret_gn_g, ret_w_out, ffn_w_in, ffn_w_out, final_norm_g):
    batch, seq, d = x_prompt.shape
    dec_batch, dec_seq, _ = x_sample.shape
    depth = ada_w.shape[0]
    n_ctx = batch * seq
    n_lat = dec_batch * dec_seq
    past = cache_mla_ckv.shape[2]
    assert 1 + dec_batch <= MOD_ROWS

    cvec = jnp.zeros((MOD_ROWS, d), F32).at[0].set(c_ctx).at[1:1 + dec_batch].set(c)
    mod_all = _ada_call(cvec, ada_w, ada_b).reshape(depth, MOD_ROWS, 6, d)

    rows_s = _Rows(n_ctx, dec_batch, dec_seq, min(512, dec_seq))
    rows_l = _Rows(n_ctx, dec_batch, dec_seq, min(1024, dec_seq))
    rows_seq = _Rows(n_ctx, dec_batch, dec_seq, dec_seq)
    rope_tabs = _rope_tables(dec_seq)
    final_g = final_norm_g[None, :]
    mla_w_o_bf = mla_w_o.astype(BF16)

    def mixer_cast_jobs(layer):
        if layer >= depth:
            return []
        kind, j = layer % N_MIXERS, layer // N_MIXERS
        if kind == 1:
            return [(conv_w_in, j), (conv_w_out, j)]
        if kind == 2:
            return [(ret_w_in, j), (ret_w_out, j)]
        return []

    x = (x_prompt.reshape(n_ctx, d), x_sample.reshape(n_lat, d))
    new_ckv, new_kpe, new_ret = [], [], []
    ffn_bf = mixer_bf = None
    for i in range(depth):
        kind, j = i % N_MIXERS, i // N_MIXERS
        mod = mod_all[i]
        g_mix = norm_mix_g[i][None, :]
        if ffn_bf is None:
            assert kind == 0
        if kind == 0:
            wts = _mla_weights(mla_w_a[j], mla_q_norm_g[j], mla_kv_norm_g[j], mla_w_q_b[j], mla_w_kv_b[j])
            split = isinstance(x, tuple)
            qc, kc, vc, ckv_c, kpe_c = _mla_proj_call(
                x[0] if split else x, 0, mod, g_mix, wts, rows_s, tile0=0, n_tiles=rows_s.ctx_tiles,
                rope_tabs=None, emit_cache=True)
            ql, kl, vl = _mla_proj_call(
                x[1] if split else x, 0 if split else rows_s.ctx_tiles, mod, g_mix, wts, rows_s,
                tile0=rows_s.ctx_tiles, n_tiles=rows_s.n_tiles - rows_s.ctx_tiles,
                rope_tabs=rope_tabs, emit_cache=False)
            new_ckv.append(ckv_c.reshape(batch, seq, MLA_KV_RANK))
            new_kpe.append(kpe_c.reshape(batch, seq, MLA_ROPE))
            kpe_pad = jnp.pad(cache_mla_kpe[:, j], ((0, 0), (0, 0), (0, LANES - MLA_ROPE)))
            kp, vp = _cache_expand_call(
                cache_mla_ckv[:, j].reshape(dec_batch * past, MLA_KV_RANK),
                kpe_pad.reshape(dec_batch * past, LANES), wts[4])
            o_c, _ = _attn_call(qc, [kc], [vc], n_batch=batch, seq=seq, sks=[seq],
                                nb=min(4, batch), tq=seq, name="attn_ctx")
            jobs = [] if ffn_bf is not None else [(ffn_w_in, i), (ffn_w_out, i)]
            o_l, cast = _attn_call(ql, [kp, kl], [vp, vl], n_batch=dec_batch, seq=dec_seq,
                                   sks=[past, dec_seq], nb=1, tq=min(256, dec_seq), name="attn_lat",
                                   cast_jobs=jobs)
            if jobs:
                ffn_bf = cast
            x = _out_res_call(o_c, o_l, mla_w_o_bf, j, x, mod, rows_l, "mla_out")
        elif kind == 1:
            x = _conv_call(x, mod, g_mix, mixer_bf[0][0], conv_w[j], mixer_bf[1][0], rows_seq, seq)
        else:
            n_gate = ret_w_out.shape[1]
            qkv, gate = _ret_proj_call(x, mod, g_mix, mixer_bf[0][0], rows_s, n_gate)
            lr = ret_log_rate[j].T[:, :, None]
            gn = ret_gn_g[j][None, :]
            y_c, st = _ret_scan_call(lr, qkv, gate, gn, None, bsz=batch, seq=seq, row0=0,
                                     hps=RET_HEADS, emit_state=True, name="ret_scan_ctx")
            (y_l,) = _ret_scan_call(lr, qkv, gate, gn, state_ret[:, j], bsz=dec_batch, seq=dec_seq,
                                    row0=n_ctx, hps=1, emit_state=False, name="ret_scan_lat")
            new_ret.append(st)
            x = _out_res_call(y_c, y_l, mixer_bf[1], 0, x, mod, rows_l, "ret_out")
        last = i == depth - 1
        jobs = [] if last else [(ffn_w_in, i + 1), (ffn_w_out, i + 1)] + mixer_cast_jobs(i + 1)
        x, cast = _ffn_call(x, mod, norm_ffn_g[i][None, :], ffn_bf[0], ffn_bf[1], 0, final_g, rows_l,
                            final=last, cast_jobs=jobs)
        ffn_bf, mixer_bf = cast[:2], cast[2:]

    y_prompt = x[0].reshape(batch, seq, d)
    y_sample = x[1].reshape(dec_batch, dec_seq, d)
    return (y_prompt, y_sample, jnp.stack(new_ckv, axis=1), jnp.stack(new_kpe, axis=1),
            jnp.concatenate(new_ret, axis=1))
```

```python
import functools
import math

import jax
import jax.numpy as jnp
import numpy as np
from jax import lax
from jax.experimental import pallas as pl
from jax.experimental.pallas import tpu as pltpu

F32 = jnp.float32
BF16 = jnp.bfloat16

N_MIXERS = 3
MLA_HEADS = 8
MLA_NOPE = 128
MLA_ROPE = 64
MLA_V = 128
MLA_Q_RANK = 384
MLA_KV_RANK = 256
MLA_SCALE = (MLA_NOPE + MLA_ROPE) ** -0.5
MLA_Q_SCALE = MLA_SCALE * 1.4426950408889634
ROPE_THETA = 10000.0
ROPE_AXIS_FREQS = MLA_ROPE // 4
GRID_W = 64
RET_HEADS = 4
EPS = 1e-6

LANES = 128
MLA_HEAD_PAD = 2 * LANES
MOD_ROWS = 8
VMEM_LIMIT = 56 * 1024 * 1024


def _cparams(sem):
    return pltpu.CompilerParams(dimension_semantics=sem, vmem_limit_bytes=VMEM_LIMIT)


def _resident(shape, index_map):
    return pl.BlockSpec(shape, index_map, pipeline_mode=pl.Buffered(1))


def _rms(x, g):
    return x * lax.rsqrt(jnp.mean(x * x, axis=-1, keepdims=True) + EPS) * g


def _modulate(x, g, shift, scale):
    return _rms(x, g) * (1.0 + scale) + shift


def _silu(x):
    return x * jax.nn.sigmoid(x)


def _dot(a, b):
    return jnp.dot(a, b, preferred_element_type=F32)


def _dot_nt(a, b):
    return lax.dot_general(a, b, (((1,), (1,)), ((), ())), preferred_element_type=F32)


MAX_CAST_SLABS = 16


def _cast_specs(jobs, n_steps, step_of):
    n_slabs = math.gcd(n_steps, MAX_CAST_SLABS)
    per = n_steps // n_slabs

    def slab_of_step(*g):
        return step_of(*g) // per

    in_specs, out_specs, out_shape = [], [], []
    for w, layer in jobs:
        _, k, n = w.shape
        blk = (1, k // n_slabs, n)
        in_specs.append(pl.BlockSpec(blk, lambda *g, layer=layer: (layer, slab_of_step(*g), 0)))
        out_specs.append(pl.BlockSpec(blk, lambda *g: (0, slab_of_step(*g), 0)))
        out_shape.append(jax.ShapeDtypeStruct((1, k, n), BF16))
    return in_specs, out_specs, out_shape


def _run_casts(in_refs, out_refs):
    for src, dst in zip(in_refs, out_refs):
        dst[...] = src[...].astype(BF16)


def _ada_kernel(c_ref, w_ref, b_ref, o_ref):
    a = _silu(c_ref[...]).astype(BF16)
    o_ref[0] = _dot(a, w_ref[0].astype(BF16)) + b_ref[0]


def _ada_call(cvec, ada_w, ada_b):
    depth, d, n = ada_w.shape
    tn = d
    return pl.pallas_call(
        _ada_kernel,
        grid=(depth, n // tn),
        in_specs=[
            pl.BlockSpec((MOD_ROWS, d), lambda l, j: (0, 0)),
            pl.BlockSpec((1, d, tn), lambda l, j: (l, 0, j)),
            pl.BlockSpec((1, 1, tn), lambda l, j: (l, 0, j)),
        ],
        out_specs=pl.BlockSpec((1, MOD_ROWS, tn), lambda l, j: (l, 0, j)),
        out_shape=jax.ShapeDtypeStruct((depth, MOD_ROWS, n), F32),
        compiler_params=_cparams(("arbitrary", "arbitrary")),
        name="ada",
    )(cvec, ada_w, ada_b.reshape(depth, 1, n))


class _Rows:
    def __init__(self, n_ctx, dec_batch, dec_seq, tm):
        assert n_ctx % tm == 0 and dec_seq % tm == 0
        self.tm = tm
        self.ctx_tiles = n_ctx // tm
        self.seq_tiles = dec_seq // tm
        self.n_tiles = self.ctx_tiles + dec_batch * self.seq_tiles

    def group(self, i):
        lat = jnp.maximum(i - self.ctx_tiles, 0) // self.seq_tiles
        return jnp.where(i < self.ctx_tiles, 0, 1 + lat)


def _mla_proj_kernel(*refs, rope, emit_cache):
    it = iter(refs)
    x_ref, mod_ref, g_ref, wa_ref, qg_ref, kvg_ref, wq_ref, wkv_ref = (next(it) for _ in range(8))
    if rope:
        ck_ref, sk_ref = (next(it) for _ in range(2))
    q_ref, k_ref, v_ref = (next(it) for _ in range(3))
    if emit_cache:
        ckv_ref, kpe_ref = (next(it) for _ in range(2))

    h = _modulate(x_ref[...], g_ref[...], mod_ref[0, 0:1, :], mod_ref[0, 1:2, :]).astype(BF16)
    a = _dot(h, wa_ref[...])
    qa = a[:, :MLA_Q_RANK]
    ckv = a[:, MLA_Q_RANK:MLA_Q_RANK + MLA_KV_RANK]
    kpe = a[:, MLA_Q_RANK + MLA_KV_RANK:MLA_Q_RANK + MLA_KV_RANK + LANES]
    ckv_n = _rms(ckv, kvg_ref[...])
    if emit_cache:
        ckv_ref[...] = ckv_n
        kpe_ref[...] = kpe[:, :MLA_ROPE]
    if rope:
        kpe_sw = a[:, MLA_Q_RANK + MLA_KV_RANK + LANES:MLA_Q_RANK + MLA_KV_RANK + 2 * LANES]
        kpe = kpe * ck_ref[...] + kpe_sw * sk_ref[...]
    kpe = kpe.astype(BF16)

    qn = _rms(qa, qg_ref[...]).astype(BF16)
    nh = MLA_HEADS
    q_nope = _dot(qn, wq_ref[:, :nh * LANES])
    q_pe = _dot(qn, wq_ref[:, nh * LANES:2 * nh * LANES])
    if rope:
        q_sw = _dot(qn, wq_ref[:, 2 * nh * LANES:3 * nh * LANES])
        cos, sin = ck_ref[...] * MLA_Q_SCALE, sk_ref[...] * MLA_Q_SCALE
    kv = _dot(ckv_n.astype(BF16), wkv_ref[...])
    for hd in range(nh):
        lo = hd * MLA_HEAD_PAD
        hs = slice(hd * LANES, (hd + 1) * LANES)
        if rope:
            q_pe_h = q_pe[:, hs] * cos + q_sw[:, hs] * sin
        else:
            q_pe_h = q_pe[:, hs] * MLA_Q_SCALE
        q_ref[:, lo:lo + LANES] = (q_nope[:, hs] * MLA_Q_SCALE).astype(BF16)
        q_ref[:, lo + LANES:lo + 2 * LANES] = q_pe_h.astype(BF16)
        k_ref[:, lo:lo + LANES] = kv[:, hd * LANES:(hd + 1) * LANES].astype(BF16)
        k_ref[:, lo + LANES:lo + 2 * LANES] = kpe
    v_ref[...] = kv[:, nh * LANES:].astype(BF16)


def _mla_proj_call(x, x_tile0, mod, g, wts, rows, *, tile0, n_tiles, rope_tabs, emit_cache):
    tm, d = rows.tm, x.shape[1]
    wa, qg, kvg, wq, wkv = wts
    rope = rope_tabs is not None
    full = lambda arr: _resident(arr.shape, lambda i: (0,) * arr.ndim)
    in_specs = [
        pl.BlockSpec((tm, d), lambda i: (i + x_tile0, 0)),
        pl.BlockSpec((1, 6, d), lambda i: (rows.group(i + tile0), 0, 0)),
        full(g), full(wa), full(qg), full(kvg), full(wq), full(wkv),
    ]
    args = [x, mod, g, wa, qg, kvg, wq, wkv]
    if rope:
        for tab in rope_tabs:
            in_specs.append(pl.BlockSpec((tm, tab.shape[1]), lambda i: (i % rows.seq_tiles, 0)))
            args.append(tab)
    m = n_tiles * tm
    hp = MLA_HEADS * MLA_HEAD_PAD
    out_specs = [
        pl.BlockSpec((tm, hp), lambda i: (i, 0)),
        pl.BlockSpec((tm, hp), lambda i: (i, 0)),
        pl.BlockSpec((tm, MLA_HEADS * MLA_V), lambda i: (i, 0)),
    ]
    out_shape = [
        jax.ShapeDtypeStruct((m, hp), BF16),
        jax.ShapeDtypeStruct((m, hp), BF16),
        jax.ShapeDtypeStruct((m, MLA_HEADS * MLA_V), BF16),
    ]
    if emit_cache:
        out_specs += [pl.BlockSpec((tm, MLA_KV_RANK), lambda i: (i, 0)),
                      pl.BlockSpec((tm, MLA_ROPE), lambda i: (i, 0))]
        out_shape += [jax.ShapeDtypeStruct((m, MLA_KV_RANK), F32),
                      jax.ShapeDtypeStruct((m, MLA_ROPE), F32)]
    return pl.pallas_call(
        functools.partial(_mla_proj_kernel, rope=rope, emit_cache=emit_cache),
        grid=(n_tiles,),
        in_specs=in_specs,
        out_specs=out_specs,
        out_shape=out_shape,
        compiler_params=_cparams(("arbitrary",)),
        name="mla_proj_lat" if rope else "mla_proj_ctx",
    )(*args)


def _cache_expand_kernel(ckv_ref, kpe_ref, wkv_ref, k_ref, v_ref):
    kv = _dot(ckv_ref[...].astype(BF16), wkv_ref[...])
    kpe = kpe_ref[...].astype(BF16)
    nh = MLA_HEADS
    for hd in range(nh):
        lo = hd * MLA_HEAD_PAD
        k_ref[:, lo:lo + LANES] = kv[:, hd * LANES:(hd + 1) * LANES].astype(BF16)
        k_ref[:, lo + LANES:lo + 2 * LANES] = kpe
    v_ref[...] = kv[:, nh * LANES:].astype(BF16)


def _cache_expand_call(ckv, kpe_pad, wkv):
    m = ckv.shape[0]
    tm = min(m, 512)
    hp = MLA_HEADS * MLA_HEAD_PAD
    return pl.pallas_call(
        _cache_expand_kernel,
        grid=(m // tm,),
        in_specs=[
            pl.BlockSpec((tm, MLA_KV_RANK), lambda i: (i, 0)),
            pl.BlockSpec((tm, LANES), lambda i: (i, 0)),
            _resident(wkv.shape, lambda i: (0, 0)),
        ],
        out_specs=[pl.BlockSpec((tm, hp), lambda i: (i, 0)),
                   pl.BlockSpec((tm, MLA_HEADS * MLA_V), lambda i: (i, 0))],
        out_shape=[jax.ShapeDtypeStruct((m, hp), BF16),
                   jax.ShapeDtypeStruct((m, MLA_HEADS * MLA_V), BF16)],
        compiler_params=_cparams(("arbitrary",)),
        name="mla_cache_expand",
    )(ckv, kpe_pad, wkv)


def _attn_kernel(*refs, n_seg, nb, tq, sks, n_cast):
    q_ref = refs[0]
    k_refs = refs[1:1 + n_seg]
    v_refs = refs[1 + n_seg:1 + 2 * n_seg]
    n_in = 1 + 2 * n_seg + n_cast
    o_ref = refs[n_in]
    s_ref = refs[n_in + 1 + n_cast]
    _run_casts(refs[1 + 2 * n_seg:n_in], refs[n_in + 1:n_in + 1 + n_cast])
    problems = [(b, h) for b in range(nb) for h in range(MLA_HEADS)]
    seg_cols = [sum(sks[:i]) for i in range(n_seg)]

    def scores(idx):
        b, h = problems[idx]
        q = q_ref[b * tq:(b + 1) * tq, h * MLA_HEAD_PAD:(h + 1) * MLA_HEAD_PAD]
        m_lane = None
        for k_ref, sk, c0 in zip(k_refs, sks, seg_cols):
            s = _dot_nt(q, k_ref[b * sk:(b + 1) * sk, h * MLA_HEAD_PAD:(h + 1) * MLA_HEAD_PAD])
            s_ref[idx % 2, :, c0:c0 + sk] = s
            for c in range(sk // LANES):
                piece = s[:, c * LANES:(c + 1) * LANES]
                m_lane = piece if m_lane is None else jnp.maximum(m_lane, piece)
        return m_lane.max(axis=-1, keepdims=True)

    def weighted_values(idx, m):
        b, h = problems[idx]
        acc = None
        for v_ref, sk, c0 in zip(v_refs, sks, seg_cols):
            p = jnp.exp2(s_ref[idx % 2, :, c0:c0 + sk] - m).astype(BF16)
            v = v_ref[b * sk:(b + 1) * sk, h * MLA_V:(h + 1) * MLA_V]
            part = _dot(p, jnp.concatenate([v, jnp.ones_like(v)], axis=1))
            acc = part if acc is None else acc + part
        o = acc[:, :MLA_V] / acc[:, MLA_V:]
        o_ref[b * tq:(b + 1) * tq, h * MLA_V:(h + 1) * MLA_V] = o.astype(o_ref.dtype)

    m = scores(0)
    for idx in range(len(problems)):
        m_next = scores(idx + 1) if idx + 1 < len(problems) else None
        weighted_values(idx, m)
        m = m_next


def _attn_call(q, ks, vs, *, n_batch, seq, sks, nb, tq, name, cast_jobs=()):
    n_seg = len(ks)
    tiles = seq // tq
    assert nb == 1 or tiles == 1
    hp, hv = MLA_HEADS * MLA_HEAD_PAD, MLA_HEADS * MLA_V
    in_specs = [pl.BlockSpec((nb * tq, hp), lambda b, t: (b * tiles + t, 0))]
    in_specs += [pl.BlockSpec((nb * sk, hp), lambda b, t: (b, 0)) for sk in sks]
    in_specs += [pl.BlockSpec((nb * sk, hv), lambda b, t: (b, 0)) for sk in sks]
    c_in, c_out, c_shape = _cast_specs(cast_jobs, (n_batch // nb) * tiles, lambda b, t: b * tiles + t)
    outs = pl.pallas_call(
        functools.partial(_attn_kernel, n_seg=n_seg, nb=nb, tq=tq, sks=tuple(sks), n_cast=len(cast_jobs)),
        grid=(n_batch // nb, tiles),
        in_specs=in_specs + c_in,
        out_specs=[pl.BlockSpec((nb * tq, hv), lambda b, t: (b * tiles + t, 0))] + c_out,
        out_shape=[jax.ShapeDtypeStruct((n_batch * seq, hv), BF16)] + c_shape,
        scratch_shapes=[pltpu.VMEM((2, tq, sum(sks)), F32)],
        compiler_params=_cparams(("arbitrary", "arbitrary")),
        name=name,
    )(q, *ks, *vs, *[w for w, _ in cast_jobs])
    return outs[0], list(outs[1:])


def _split_specs(rows, width):
    ct = rows.ctx_tiles
    return [pl.BlockSpec((rows.tm, width), lambda i: (jnp.minimum(i, ct - 1), 0)),
            pl.BlockSpec((rows.tm, width), lambda i: (jnp.maximum(i - ct, 0), 0))]


def _out_res_kernel(*refs, ctx_tiles, split_x):
    it = iter(refs)
    yc_ref, yl_ref, w_ref = next(it), next(it), next(it)
    xc_ref = next(it)
    xl_ref = next(it) if split_x else xc_ref
    mod_ref, o_ref = next(it), next(it)

    def emit(y_ref, x_ref):
        o_ref[...] = x_ref[...] + mod_ref[0, 2:3, :] * _dot(y_ref[...], w_ref[0])

    is_ctx = pl.program_id(0) < ctx_tiles
    pl.when(is_ctx)(lambda: emit(yc_ref, xc_ref))
    pl.when(jnp.logical_not(is_ctx))(lambda: emit(yl_ref, xl_ref))


def _out_res_call(y_ctx, y_lat, w_all, layer, x, mod, rows, name):
    split_x = isinstance(x, tuple)
    tm = rows.tm
    _, k, d = w_all.shape
    x_specs = _split_specs(rows, d) if split_x else [pl.BlockSpec((tm, d), lambda i: (i, 0))]
    xs = list(x) if split_x else [x]
    return pl.pallas_call(
        functools.partial(_out_res_kernel, ctx_tiles=rows.ctx_tiles, split_x=split_x),
        grid=(rows.n_tiles,),
        in_specs=_split_specs(rows, k) + [_resident((1, k, d), lambda i: (layer, 0, 0))] + x_specs + [
            pl.BlockSpec((1, 6, d), lambda i: (rows.group(i), 0, 0)),
        ],
        out_specs=pl.BlockSpec((tm, d), lambda i: (i, 0)),
        out_shape=jax.ShapeDtypeStruct((rows.n_tiles * tm, d), F32),
        compiler_params=_cparams(("arbitrary",)),
        name=name,
    )(y_ctx, y_lat, w_all, *xs, mod)


def _ffn_kernel(*refs, th, final, ctx_tiles, n_cast):
    x_ref, mod_ref, g_ref, win_ref, wout_ref, fg_ref = refs[:6]
    cast_in = refs[6:6 + n_cast]
    n_out = 2 if final else 1
    out_refs = refs[6 + n_cast:6 + n_cast + n_out]
    cast_out = refs[6 + n_cast + n_out:6 + 2 * n_cast + n_out]
    scratch = refs[6 + 2 * n_cast + n_out:]
    act_ref = scratch[0]
    _run_casts(cast_in, cast_out)
    x = x_ref[...]
    h = _modulate(x, g_ref[...], mod_ref[0, 3:4, :], mod_ref[0, 4:5, :]).astype(BF16)
    hidden = wout_ref.shape[1]
    for c in range(hidden // th):
        a = _dot(h, win_ref[0, :, c * th:(c + 1) * th])
        b = _dot(h, win_ref[0, :, hidden + c * th:hidden + (c + 1) * th])
        act_ref[:, c * th:(c + 1) * th] = (_silu(a) * b).astype(BF16)
    y = x + mod_ref[0, 5:6, :] * _dot(act_ref[...], wout_ref[0])
    if not final:
        out_refs[0][...] = y
    else:
        acc_ref = scratch[1]
        acc_ref[...] = _rms(y, fg_ref[...])
        is_ctx = pl.program_id(0) < ctx_tiles

        @pl.when(is_ctx)
        def _():
            out_refs[0][...] = acc_ref[...]

        @pl.when(jnp.logical_not(is_ctx))
        def _():
            out_refs[1][...] = acc_ref[...]


def _ffn_call(x, mod, g, w_in_all, w_out_all, layer, final_g, rows, final, cast_jobs=()):
    tm, d = rows.tm, x.shape[1]
    if final:
        out_specs = _split_specs(rows, d)
        out_shape = [jax.ShapeDtypeStruct((rows.ctx_tiles * tm, d), F32),
                     jax.ShapeDtypeStruct(((rows.n_tiles - rows.ctx_tiles) * tm, d), F32)]
        scratch = [pltpu.VMEM((tm, w_out_all.shape[1]), BF16), pltpu.VMEM((tm, d), F32)]
    else:
        out_specs = [pl.BlockSpec((tm, d), lambda i: (i, 0))]
        out_shape = [jax.ShapeDtypeStruct(x.shape, F32)]
        scratch = [pltpu.VMEM((tm, w_out_all.shape[1]), BF16)]
    c_in, c_out, c_shape = _cast_specs(cast_jobs, rows.n_tiles, lambda i: i)
    outs = pl.pallas_call(
        functools.partial(_ffn_kernel, th=2 * LANES, final=final, ctx_tiles=rows.ctx_tiles,
                          n_cast=len(cast_jobs)),
        grid=(rows.n_tiles,),
        in_specs=[
            pl.BlockSpec((tm, d), lambda i: (i, 0)),
            pl.BlockSpec((1, 6, d), lambda i: (rows.group(i), 0, 0)),
            _resident(g.shape, lambda i: (0, 0)),
            _resident((1,) + w_in_all.shape[1:], lambda i: (layer, 0, 0)),
            _resident((1,) + w_out_all.shape[1:], lambda i: (layer, 0, 0)),
            _resident(final_g.shape, lambda i: (0, 0)),
        ] + c_in,
        out_specs=out_specs + c_out,
        out_shape=out_shape + c_shape,
        scratch_shapes=scratch,
        compiler_params=_cparams(("arbitrary",)),
        name="ffn_final" if final else "ffn",
    )(x, mod, g, w_in_all, w_out_all, final_g, *[w for w, _ in cast_jobs])
    n_out = 2 if final else 1
    stream = tuple(outs[:2]) if final else outs[0]
    return stream, list(outs[n_out:])


CONV_HALO = 16


def _conv_kernel(x_ref, xp_ref, xn_ref, mod_ref, g_ref, win_ref, cw_ref, wout_ref, o_ref,
                 h_ref, z_ref, act_ref, *, tn, ctx_tiles, ctx_seq, lat_seq):
    tm, d = x_ref.shape
    hl = CONV_HALO
    shift, scale = mod_ref[0, 0:1, :], mod_ref[0, 1:2, :]
    h_ref[0:hl, :] = _modulate(xp_ref[...], g_ref[...], shift, scale).astype(BF16)
    h_ref[hl:hl + tm, :] = _modulate(x_ref[...], g_ref[...], shift, scale).astype(BF16)
    h_ref[hl + tm:, :] = _modulate(xn_ref[...], g_ref[...], shift, scale).astype(BF16)
    i = pl.program_id(0)
    is_ctx = i < ctx_tiles
    row = lax.broadcasted_iota(jnp.int32, (tm, 1), 0)
    lat_row0 = (jnp.maximum(i - ctx_tiles, 0) % (lat_seq // tm)) * tm
    pos = jnp.where(is_ctx, row & (ctx_seq - 1), row + lat_row0)
    seq = jnp.where(is_ctx, ctx_seq, lat_seq)
    has_prev = pos != 0
    has_next = pos != seq - 1
    for c in range(d // tn):
        sl = slice(c * tn, (c + 1) * tn)
        h = h_ref[...]
        cg = _dot(h, win_ref[:, d + c * tn:d + (c + 1) * tn])
        u = _dot(h, win_ref[:, 2 * d + c * tn:2 * d + (c + 1) * tn])
        z_ref[...] = cg * u
        bg = _dot(h_ref[hl:hl + tm, :], win_ref[:, sl])
        z_prev = jnp.where(has_prev, z_ref[hl - 1:hl - 1 + tm, :], 0.0)
        z_next = jnp.where(has_next, z_ref[hl + 1:hl + 1 + tm, :], 0.0)
        conv = z_prev * cw_ref[0:1, sl] + z_ref[hl:hl + tm, :] * cw_ref[1:2, sl] + z_next * cw_ref[2:3, sl]
        act_ref[:, sl] = (bg * conv).astype(BF16)
    o_ref[...] = x_ref[...] + mod_ref[0, 2:3, :] * _dot(act_ref[...], wout_ref[...])


def _conv_call(x, mod, g, w_in, conv_w, w_out, rows, ctx_seq, lat_seq):
    tm, d = rows.tm, x.shape[1]
    hl = CONV_HALO
    assert tm % ctx_seq == 0 and ctx_seq & (ctx_seq - 1) == 0 and lat_seq % tm == 0 and tm % hl == 0
    last_halo = x.shape[0] // hl - 1
    return pl.pallas_call(
        functools.partial(_conv_kernel, tn=2 * LANES, ctx_tiles=rows.ctx_tiles, ctx_seq=ctx_seq,
                          lat_seq=lat_seq),
        grid=(rows.n_tiles,),
        in_specs=[
            pl.BlockSpec((tm, d), lambda i: (i, 0)),
            pl.BlockSpec((hl, d), lambda i: (jnp.maximum(i * (tm // hl) - 1, 0), 0)),
            pl.BlockSpec((hl, d), lambda i: (jnp.minimum((i + 1) * (tm // hl), last_halo), 0)),
            pl.BlockSpec((1, 6, d), lambda i: (rows.group(i), 0, 0)),
            _resident(g.shape, lambda i: (0, 0)),
            _resident(w_in.shape, lambda i: (0, 0)),
            _resident(conv_w.shape, lambda i: (0, 0)),
            _resident(w_out.shape, lambda i: (0, 0)),
        ],
        out_specs=pl.BlockSpec((tm, d), lambda i: (i, 0)),
        out_shape=jax.ShapeDtypeStruct(x.shape, F32),
        scratch_shapes=[pltpu.VMEM((tm + 2 * hl, d), BF16), pltpu.VMEM((tm + 2 * hl, 2 * LANES), F32),
                        pltpu.VMEM((tm, d), BF16)],
        compiler_params=_cparams(("arbitrary",)),
        name="conv_mixer",
    )(x, x, x, mod, g, w_in, conv_w, w_out)


def _ret_proj_kernel(x_ref, mod_ref, g_ref, w_ref, qkv_ref, gate_ref, *, tn):
    h = _modulate(x_ref[...], g_ref[...], mod_ref[0, 0:1, :], mod_ref[0, 1:2, :]).astype(BF16)
    n_qkv = qkv_ref.shape[1]
    for c in range(n_qkv // tn):
        qkv_ref[:, c * tn:(c + 1) * tn] = _dot(h, w_ref[:, c * tn:(c + 1) * tn]).astype(BF16)
    for c in range(gate_ref.shape[1] // tn):
        gate = _dot(h, w_ref[:, n_qkv + c * tn:n_qkv + (c + 1) * tn])
        gate_ref[:, c * tn:(c + 1) * tn] = _silu(gate).astype(gate_ref.dtype)


def _ret_proj_call(x, mod, g, w, rows, n_gate):
    tm, d = rows.tm, x.shape[1]
    n_qkv = w.shape[1] - n_gate
    m = x.shape[0]
    return pl.pallas_call(
        functools.partial(_ret_proj_kernel, tn=4 * LANES),
        grid=(rows.n_tiles,),
        in_specs=[
            pl.BlockSpec((tm, d), lambda i: (i, 0)),
            pl.BlockSpec((1, 6, d), lambda i: (rows.group(i), 0, 0)),
            _resident(g.shape, lambda i: (0, 0)),
            _resident(w.shape, lambda i: (0, 0)),
        ],
        out_specs=[pl.BlockSpec((tm, n_qkv), lambda i: (i, 0)),
                   pl.BlockSpec((tm, n_gate), lambda i: (i, 0))],
        out_shape=[jax.ShapeDtypeStruct((m, n_qkv), BF16),
                   jax.ShapeDtypeStruct((m, n_gate), BF16)],
        compiler_params=_cparams(("arbitrary",)),
        name="ret_proj",
    )(x, mod, g, w)


def _ret_scan_kernel(*refs, chunk, hps, dk, dv, has_init, emit_state):
    it = iter(refs)
    lr_ref, q_ref, k_ref, v_ref, gate_ref, gn_ref = (next(it) for _ in range(6))
    s0_ref = next(it) if has_init else None
    y_ref = next(it)
    sout_ref = next(it) if emit_state else None
    o_ref, st_ref = next(it), next(it)

    seq = q_ref.shape[0]
    n_chunks = seq // chunk
    k_scale = dk ** -0.5
    row = lax.broadcasted_iota(jnp.int32, (chunk, chunk), 0).astype(F32)
    col = lax.broadcasted_iota(jnp.int32, (chunk, chunk), 1).astype(F32)
    ridx = lax.broadcasted_iota(jnp.int32, (chunk, 1), 0).astype(F32)
    dist = row - col

    for hd in range(hps):
        qs = slice(hd * dk, (hd + 1) * dk)
        vs = slice(hd * dv, (hd + 1) * dv)
        log_gamma = -jnp.exp(lr_ref[hd])
        lg_f, lg_b = log_gamma[0:1, :], log_gamma[1:2, :]
        mask = jnp.where(dist > 0, jnp.exp(jnp.maximum(dist, 0.0) * lg_f),
                         jnp.where(dist < 0, jnp.exp(jnp.maximum(-dist, 0.0) * lg_b), 2.0)) * k_scale
        q_decay = (jnp.exp((ridx + 1.0) * lg_f), jnp.exp((chunk - ridx) * lg_b))
        k_decay = (jnp.exp((chunk - 1.0 - ridx) * lg_f) * k_scale, jnp.exp(ridx * lg_b) * k_scale)
        chunk_decay = (jnp.exp(chunk * lg_f), jnp.exp(chunk * lg_b))

        for direction in range(2):
            order = range(n_chunks) if direction == 0 else range(n_chunks - 1, -1, -1)
            if has_init:
                st_ref[hd] = s0_ref[0, direction, hd]
            for step, c in enumerate(order):
                rs = slice(c * chunk, (c + 1) * chunk)
                qc, kc, vc = q_ref[rs, qs], k_ref[rs, qs], v_ref[rs, vs]
                have_state = has_init or step > 0
                if direction == 0:
                    scores = _dot_nt(qc, kc) * mask
                    o_ref[rs, vs] = _dot(scores.astype(BF16), vc)
                if have_state:
                    o_ref[rs, vs] += _dot(qc, st_ref[hd].astype(BF16)) * q_decay[direction]
                kd = (kc.astype(F32) * k_decay[direction]).T.astype(BF16)
                update = _dot(kd, vc)
                if have_state:
                    st_ref[hd] = st_ref[hd] * chunk_decay[direction] + update
                else:
                    st_ref[hd] = update
            if emit_state:
                sout_ref[0, 0, direction, hd] = st_ref[hd]

        o = o_ref[:, vs]
        mu = jnp.mean(o, axis=-1, keepdims=True)
        var = jnp.mean(jnp.square(o - mu), axis=-1, keepdims=True)
        on = (o - mu) * lax.rsqrt(var + EPS) * gn_ref[:, vs]
        y_ref[:, vs] = (gate_ref[:, vs].astype(F32) * on).astype(y_ref.dtype)


def _ret_scan_call(log_rate, qkv, gate, gn_g, s0, *, bsz, seq, row0, hps, emit_state, name):
    n_gate = gate.shape[1]
    nh = RET_HEADS
    dv = n_gate // nh
    dk = (qkv.shape[1] - n_gate) // (2 * nh)
    chunk = min(seq, 2 * LANES)
    has_init = s0 is not None
    assert row0 % seq == 0 and nh % hps == 0
    b0 = row0 // seq
    hg = nh // hps
    in_specs = [
        pl.BlockSpec((hps, 2, 1), lambda b, h: (h, 0, 0)),
        pl.BlockSpec((seq, hps * dk), lambda b, h: (b0 + b, h)),
        pl.BlockSpec((seq, hps * dk), lambda b, h: (b0 + b, hg + h)),
        pl.BlockSpec((seq, hps * dv), lambda b, h: (b0 + b, (2 * nh * dk) // (hps * dv) + h)),
        pl.BlockSpec((seq, hps * dv), lambda b, h: (b0 + b, h)),
        pl.BlockSpec((1, hps * dv), lambda b, h: (0, h)),
    ]
    args = [log_rate, qkv, qkv, qkv, gate, gn_g]
    if has_init:
        in_specs.append(pl.BlockSpec((1, 2, hps, dk, dv), lambda b, h: (b, 0, h, 0, 0)))
        args.append(s0)
    out_specs = [pl.BlockSpec((seq, hps * dv), lambda b, h: (b, h))]
    out_shape = [jax.ShapeDtypeStruct((bsz * seq, n_gate), BF16)]
    if emit_state:
        out_specs.append(pl.BlockSpec((1, 1, 2, hps, dk, dv), lambda b, h: (b, 0, 0, h, 0, 0)))
        out_shape.append(jax.ShapeDtypeStruct((bsz, 1, 2, nh, dk, dv), F32))
    return pl.pallas_call(
        functools.partial(_ret_scan_kernel, chunk=chunk, hps=hps, dk=dk, dv=dv,
                          has_init=has_init, emit_state=emit_state),
        grid=(bsz, hg),
        in_specs=in_specs,
        out_specs=out_specs,
        out_shape=out_shape,
        scratch_shapes=[pltpu.VMEM((seq, hps * dv), F32), pltpu.VMEM((hps, dk, dv), F32)],
        compiler_params=_cparams(("arbitrary", "arbitrary")),
        name=name,
    )(*args)


def _rope_swap_index():
    f = ROPE_AXIS_FREQS
    idx = jnp.arange(MLA_ROPE)
    return jnp.where((idx // f) % 2 == 0, idx + f, idx - f)


def _mla_weights(w_a, q_norm_g, kv_norm_g, w_q_b, w_kv_b):
    d = w_a.shape[0]
    swap = _rope_swap_index()
    nq = MLA_Q_RANK + MLA_KV_RANK
    zpad = jnp.zeros((d, LANES - MLA_ROPE), w_a.dtype)
    w_kpe = w_a[:, nq:]
    wa = jnp.concatenate([w_a[:, :nq], w_kpe, zpad, w_kpe[:, swap], zpad], axis=1).astype(BF16)
    wq = w_q_b.reshape(MLA_Q_RANK, MLA_HEADS, MLA_NOPE + MLA_ROPE)
    wq_nope = wq[:, :, :MLA_NOPE].reshape(MLA_Q_RANK, MLA_HEADS * MLA_NOPE)
    wq_pe = wq[:, :, MLA_NOPE:]
    pad = ((0, 0), (0, 0), (0, LANES - MLA_ROPE))
    wq_pe_pad = jnp.pad(wq_pe, pad).reshape(MLA_Q_RANK, MLA_HEADS * LANES)
    wq_sw_pad = jnp.pad(wq_pe[:, :, swap], pad).reshape(MLA_Q_RANK, MLA_HEADS * LANES)
    wq_all = jnp.concatenate([wq_nope, wq_pe_pad, wq_sw_pad], axis=1).astype(BF16)
    wkv = w_kv_b.reshape(MLA_KV_RANK, MLA_HEADS, MLA_NOPE + MLA_V)
    wkv_all = jnp.concatenate(
        [wkv[:, :, :MLA_NOPE].reshape(MLA_KV_RANK, -1), wkv[:, :, MLA_NOPE:].reshape(MLA_KV_RANK, -1)],
        axis=1).astype(BF16)
    return wa, q_norm_g[None, :], kv_norm_g[None, :], wq_all, wkv_all


def _rope_tables(n_tokens):
    f = ROPE_AXIS_FREQS
    f32 = np.float32
    rows = n_tokens // GRID_W
    r = np.repeat(np.arange(rows, dtype=f32), GRID_W)
    col = np.tile(np.arange(GRID_W, dtype=f32), rows)
    inv = (f32(ROPE_THETA) ** (-np.arange(f, dtype=f32) / f32(f))).astype(f32)
    ang_r, ang_c = r[:, None] * inv, col[:, None] * inv
    cos = np.concatenate([np.cos(ang_r)] * 2 + [np.cos(ang_c)] * 2, axis=1)
    sin = np.concatenate([-np.sin(ang_r), np.sin(ang_r), -np.sin(ang_c), np.sin(ang_c)], axis=1)
    zpad = np.zeros((n_tokens, LANES - MLA_ROPE), f32)
    return (jnp.asarray(np.concatenate([cos, zpad], axis=1), F32),
            jnp.asarray(np.concatenate([sin, zpad], axis=1), F32))


def kernel(x_prompt, x_sample, c, c_ctx, cache_mla_ckv, cache_mla_kpe, state_ret, ada_w, ada_b, norm_mix_g, norm_ffn_g, mla_w_a, mla_q_norm_g, mla_kv_norm_g, mla_w_q_b, mla_w_kv_b, mla_w_o, conv_w_in, conv_w, conv_w_out, ret_w_in, ret_log_rate, ret_gn_g, ret_w_out, ffn_w_in, ffn_w_out, final_norm_g):
    batch, seq, d = x_prompt.shape
    dec_batch, dec_seq, _ = x_sample.shape
    depth = ada_w.shape[0]
    n_ctx = batch * seq
    n_lat = dec_batch * dec_seq
    past = cache_mla_ckv.shape[2]
    assert 1 + dec_batch <= MOD_ROWS

    cvec = jnp.zeros((MOD_ROWS, d), F32).at[0].set(c_ctx).at[1:1 + dec_batch].set(c)
    mod_all = _ada_call(cvec, ada_w, ada_b).reshape(depth, MOD_ROWS, 6, d)

    rows_s = _Rows(n_ctx, dec_batch, dec_seq, min(512, dec_seq))
    rows_l = _Rows(n_ctx, dec_batch, dec_seq, min(1024, dec_seq))
    rope_tabs = _rope_tables(dec_seq)
    final_g = final_norm_g[None, :]
    mla_w_o_bf = mla_w_o.astype(BF16)

    def mixer_cast_jobs(layer):
        if layer >= depth:
            return []
        kind, j = layer % N_MIXERS, layer // N_MIXERS
        if kind == 1:
            return [(conv_w_in, j), (conv_w_out, j)]
        if kind == 2:
            return [(ret_w_in, j), (ret_w_out, j)]
        return []

    x = (x_prompt.reshape(n_ctx, d), x_sample.reshape(n_lat, d))
    new_ckv, new_kpe, new_ret = [], [], []
    ffn_bf = mixer_bf = None
    for i in range(depth):
        kind, j = i % N_MIXERS, i // N_MIXERS
        mod = mod_all[i]
        g_mix = norm_mix_g[i][None, :]
        if ffn_bf is None:
            assert kind == 0
        if kind == 0:
            wts = _mla_weights(mla_w_a[j], mla_q_norm_g[j], mla_kv_norm_g[j], mla_w_q_b[j], mla_w_kv_b[j])
            split = isinstance(x, tuple)
            qc, kc, vc, ckv_c, kpe_c = _mla_proj_call(
                x[0] if split else x, 0, mod, g_mix, wts, rows_s, tile0=0, n_tiles=rows_s.ctx_tiles,
                rope_tabs=None, emit_cache=True)
            ql, kl, vl = _mla_proj_call(
                x[1] if split else x, 0 if split else rows_s.ctx_tiles, mod, g_mix, wts, rows_s,
                tile0=rows_s.ctx_tiles, n_tiles=rows_s.n_tiles - rows_s.ctx_tiles,
                rope_tabs=rope_tabs, emit_cache=False)
            new_ckv.append(ckv_c.reshape(batch, seq, MLA_KV_RANK))
            new_kpe.append(kpe_c.reshape(batch, seq, MLA_ROPE))
            kpe_pad = jnp.pad(cache_mla_kpe[:, j], ((0, 0), (0, 0), (0, LANES - MLA_ROPE)))
            kp, vp = _cache_expand_call(
                cache_mla_ckv[:, j].reshape(dec_batch * past, MLA_KV_RANK),
                kpe_pad.reshape(dec_batch * past, LANES), wts[4])
            o_c, _ = _attn_call(qc, [kc], [vc], n_batch=batch, seq=seq, sks=[seq],
                                nb=min(4, batch), tq=seq, name="attn_ctx")
            jobs = [] if ffn_bf is not None else [(ffn_w_in, i), (ffn_w_out, i)]
            o_l, cast = _attn_call(ql, [kp, kl], [vp, vl], n_batch=dec_batch, seq=dec_seq,
                                   sks=[past, dec_seq], nb=1, tq=min(512, dec_seq), name="attn_lat",
                                   cast_jobs=jobs)
            if jobs:
                ffn_bf = cast
            x = _out_res_call(o_c, o_l, mla_w_o_bf, j, x, mod, rows_l, "mla_out")
        elif kind == 1:
            x = _conv_call(x, mod, g_mix, mixer_bf[0][0], conv_w[j], mixer_bf[1][0], rows_l, seq, dec_seq)
        else:
            n_gate = ret_w_out.shape[1]
            qkv, gate = _ret_proj_call(x, mod, g_mix, mixer_bf[0][0], rows_l, n_gate)
            lr = ret_log_rate[j].T[:, :, None]
            gn = ret_gn_g[j][None, :]
            y_c, st = _ret_scan_call(lr, qkv, gate, gn, None, bsz=batch, seq=seq, row0=0,
                                     hps=RET_HEADS, emit_state=True, name="ret_scan_ctx")
            (y_l,) = _ret_scan_call(lr, qkv, gate, gn, state_ret[:, j], bsz=dec_batch, seq=dec_seq,
                                    row0=n_ctx, hps=1, emit_state=False, name="ret_scan_lat")
            new_ret.append(st)
            x = _out_res_call(y_c, y_l, mixer_bf[1], 0, x, mod, rows_l, "ret_out")
        last = i == depth - 1
        jobs = [] if last else [(ffn_w_in, i + 1), (ffn_w_out, i + 1)] + mixer_cast_jobs(i + 1)
        x, cast = _ffn_call(x, mod, norm_ffn_g[i][None, :], ffn_bf[0], ffn_bf[1], 0, final_g, rows_l,
                            final=last, cast_jobs=jobs)
        ffn_bf, mixer_bf = cast[:2], cast[2:]

    y_prompt = x[0].reshape(batch, seq, d)
    y_sample = x[1].reshape(dec_batch, dec_seq, d)
    return (y_prompt, y_sample, jnp.stack(new_ckv, axis=1), jnp.stack(new_kpe, axis=1),
            jnp.concatenate(new_ret, axis=1))
```

```python
import functools
import math

import jax
import jax.numpy as jnp
import numpy as np
from jax import lax
from jax.experimental import pallas as pl
from jax.experimental.pallas import tpu as pltpu

F32 = jnp.float32
BF16 = jnp.bfloat16

N_MIXERS = 3
MLA_HEADS = 8
MLA_NOPE = 128
MLA_ROPE = 64
MLA_V = 128
MLA_Q_RANK = 384
MLA_KV_RANK = 256
MLA_SCALE = (MLA_NOPE + MLA_ROPE) ** -0.5
MLA_Q_SCALE = MLA_SCALE * 1.4426950408889634
ROPE_THETA = 10000.0
ROPE_AXIS_FREQS = MLA_ROPE // 4
GRID_W = 64
RET_HEADS = 4
EPS = 1e-6

LANES = 128
MLA_HEAD_PAD = 2 * LANES
MOD_ROWS = 8
VMEM_LIMIT = 56 * 1024 * 1024


def _cparams(sem):
    return pltpu.CompilerParams(dimension_semantics=sem, vmem_limit_bytes=VMEM_LIMIT)


def _resident(shape, index_map):
    return pl.BlockSpec(shape, index_map, pipeline_mode=pl.Buffered(1))


def _rms(x, g):
    return x * lax.rsqrt(jnp.mean(x * x, axis=-1, keepdims=True) + EPS) * g


def _modulate(x, g, shift, scale):
    return _rms(x, g) * (1.0 + scale) + shift


def _silu(x):
    return x * jax.nn.sigmoid(x)


def _dot(a, b):
    return jnp.dot(a, b, preferred_element_type=F32)


def _dot_nt(a, b):
    return lax.dot_general(a, b, (((1,), (1,)), ((), ())), preferred_element_type=F32)


MAX_CAST_SLABS = 16


def _cast_specs(jobs, n_steps, step_of):
    n_slabs = math.gcd(n_steps, MAX_CAST_SLABS)
    per = n_steps // n_slabs

    def slab_of_step(*g):
        return step_of(*g) // per

    in_specs, out_specs, out_shape = [], [], []
    for w, layer in jobs:
        _, k, n = w.shape
        blk = (1, k // n_slabs, n)
        in_specs.append(pl.BlockSpec(blk, lambda *g, layer=layer: (layer, slab_of_step(*g), 0)))
        out_specs.append(pl.BlockSpec(blk, lambda *g: (0, slab_of_step(*g), 0)))
        out_shape.append(jax.ShapeDtypeStruct((1, k, n), BF16))
    return in_specs, out_specs, out_shape


def _run_casts(in_refs, out_refs):
    for src, dst in zip(in_refs, out_refs):
        dst[...] = src[...].astype(BF16)


def _ada_kernel(c_ref, w_ref, b_ref, o_ref):
    a = _silu(c_ref[...]).astype(BF16)
    o_ref[0] = _dot(a, w_ref[0].astype(BF16)) + b_ref[0]


def _ada_call(cvec, ada_w, ada_b):
    depth, d, n = ada_w.shape
    tn = d
    return pl.pallas_call(
        _ada_kernel,
        grid=(depth, n // tn),
        in_specs=[
            pl.BlockSpec((MOD_ROWS, d), lambda l, j: (0, 0)),
            pl.BlockSpec((1, d, tn), lambda l, j: (l, 0, j)),
            pl.BlockSpec((1, 1, tn), lambda l, j: (l, 0, j)),
        ],
        out_specs=pl.BlockSpec((1, MOD_ROWS, tn), lambda l, j: (l, 0, j)),
        out_shape=jax.ShapeDtypeStruct((depth, MOD_ROWS, n), F32),
        compiler_params=_cparams(("arbitrary", "arbitrary")),
        name="ada",
    )(cvec, ada_w, ada_b.reshape(depth, 1, n))


class _Rows:
    def __init__(self, n_ctx, dec_batch, dec_seq, tm):
        assert n_ctx % tm == 0 and dec_seq % tm == 0
        self.tm = tm
        self.ctx_tiles = n_ctx // tm
        self.seq_tiles = dec_seq // tm
        self.n_tiles = self.ctx_tiles + dec_batch * self.seq_tiles

    def group(self, i):
        lat = jnp.maximum(i - self.ctx_tiles, 0) // self.seq_tiles
        return jnp.where(i < self.ctx_tiles, 0, 1 + lat)


def _mla_proj_kernel(*refs, rope, emit_cache):
    it = iter(refs)
    x_ref, mod_ref, g_ref, wa_ref, qg_ref, kvg_ref, wq_ref, wkv_ref = (next(it) for _ in range(8))
    if rope:
        ck_ref, sk_ref = (next(it) for _ in range(2))
    q_ref, k_ref, v_ref = (next(it) for _ in range(3))
    if emit_cache:
        ckv_ref, kpe_ref = (next(it) for _ in range(2))

    h = _modulate(x_ref[...], g_ref[...], mod_ref[0, 0:1, :], mod_ref[0, 1:2, :]).astype(BF16)
    a = _dot(h, wa_ref[...])
    qa = a[:, :MLA_Q_RANK]
    ckv = a[:, MLA_Q_RANK:MLA_Q_RANK + MLA_KV_RANK]
    kpe = a[:, MLA_Q_RANK + MLA_KV_RANK:MLA_Q_RANK + MLA_KV_RANK + LANES]
    ckv_n = _rms(ckv, kvg_ref[...])
    if emit_cache:
        ckv_ref[...] = ckv_n
        kpe_ref[...] = kpe[:, :MLA_ROPE]
    if rope:
        kpe_sw = a[:, MLA_Q_RANK + MLA_KV_RANK + LANES:MLA_Q_RANK + MLA_KV_RANK + 2 * LANES]
        kpe = kpe * ck_ref[...] + kpe_sw * sk_ref[...]
    kpe = kpe.astype(BF16)

    qn = _rms(qa, qg_ref[...]).astype(BF16)
    nh = MLA_HEADS
    q_nope = _dot(qn, wq_ref[:, :nh * LANES])
    q_pe = _dot(qn, wq_ref[:, nh * LANES:2 * nh * LANES])
    if rope:
        q_sw = _dot(qn, wq_ref[:, 2 * nh * LANES:3 * nh * LANES])
        cos, sin = ck_ref[...] * MLA_Q_SCALE, sk_ref[...] * MLA_Q_SCALE
    kv = _dot(ckv_n.astype(BF16), wkv_ref[...])
    for hd in range(nh):
        lo = hd * MLA_HEAD_PAD
        hs = slice(hd * LANES, (hd + 1) * LANES)
        if rope:
            q_pe_h = q_pe[:, hs] * cos + q_sw[:, hs] * sin
        else:
            q_pe_h = q_pe[:, hs] * MLA_Q_SCALE
        q_ref[:, lo:lo + LANES] = (q_nope[:, hs] * MLA_Q_SCALE).astype(BF16)
        q_ref[:, lo + LANES:lo + 2 * LANES] = q_pe_h.astype(BF16)
        k_ref[:, lo:lo + LANES] = kv[:, hd * LANES:(hd + 1) * LANES].astype(BF16)
        k_ref[:, lo + LANES:lo + 2 * LANES] = kpe
    v_ref[...] = kv[:, nh * LANES:].astype(BF16)


def _mla_proj_call(x, x_tile0, mod, g, wts, rows, *, tile0, n_tiles, rope_tabs, emit_cache):
    tm, d = rows.tm, x.shape[1]
    wa, qg, kvg, wq, wkv = wts
    rope = rope_tabs is not None
    full = lambda arr: _resident(arr.shape, lambda i: (0,) * arr.ndim)
    in_specs = [
        pl.BlockSpec((tm, d), lambda i: (i + x_tile0, 0)),
        pl.BlockSpec((1, 6, d), lambda i: (rows.group(i + tile0), 0, 0)),
        full(g), full(wa), full(qg), full(kvg), full(wq), full(wkv),
    ]
    args = [x, mod, g, wa, qg, kvg, wq, wkv]
    if rope:
        for tab in rope_tabs:
            in_specs.append(pl.BlockSpec((tm, tab.shape[1]), lambda i: (i % rows.seq_tiles, 0)))
            args.append(tab)
    m = n_tiles * tm
    hp = MLA_HEADS * MLA_HEAD_PAD
    out_specs = [
        pl.BlockSpec((tm, hp), lambda i: (i, 0)),
        pl.BlockSpec((tm, hp), lambda i: (i, 0)),
        pl.BlockSpec((tm, MLA_HEADS * MLA_V), lambda i: (i, 0)),
    ]
    out_shape = [
        jax.ShapeDtypeStruct((m, hp), BF16),
        jax.ShapeDtypeStruct((m, hp), BF16),
        jax.ShapeDtypeStruct((m, MLA_HEADS * MLA_V), BF16),
    ]
    if emit_cache:
        out_specs += [pl.BlockSpec((tm, MLA_KV_RANK), lambda i: (i, 0)),
                      pl.BlockSpec((tm, MLA_ROPE), lambda i: (i, 0))]
        out_shape += [jax.ShapeDtypeStruct((m, MLA_KV_RANK), F32),
                      jax.ShapeDtypeStruct((m, MLA_ROPE), F32)]
    return pl.pallas_call(
        functools.partial(_mla_proj_kernel, rope=rope, emit_cache=emit_cache),
        grid=(n_tiles,),
        in_specs=in_specs,
        out_specs=out_specs,
        out_shape=out_shape,
        compiler_params=_cparams(("arbitrary",)),
        name="mla_proj_lat" if rope else "mla_proj_ctx",
    )(*args)


def _cache_expand_kernel(ckv_ref, kpe_ref, wkv_ref, k_ref, v_ref):
    kv = _dot(ckv_ref[...].astype(BF16), wkv_ref[...])
    kpe = kpe_ref[...].astype(BF16)
    nh = MLA_HEADS
    for hd in range(nh):
        lo = hd * MLA_HEAD_PAD
        k_ref[:, lo:lo + LANES] = kv[:, hd * LANES:(hd + 1) * LANES].astype(BF16)
        k_ref[:, lo + LANES:lo + 2 * LANES] = kpe
    v_ref[...] = kv[:, nh * LANES:].astype(BF16)


def _cache_expand_call(ckv, kpe_pad, wkv):
    m = ckv.shape[0]
    tm = min(m, 512)
    hp = MLA_HEADS * MLA_HEAD_PAD
    return pl.pallas_call(
        _cache_expand_kernel,
        grid=(m // tm,),
        in_specs=[
            pl.BlockSpec((tm, MLA_KV_RANK), lambda i: (i, 0)),
            pl.BlockSpec((tm, LANES), lambda i: (i, 0)),
            _resident(wkv.shape, lambda i: (0, 0)),
        ],
        out_specs=[pl.BlockSpec((tm, hp), lambda i: (i, 0)),
                   pl.BlockSpec((tm, MLA_HEADS * MLA_V), lambda i: (i, 0))],
        out_shape=[jax.ShapeDtypeStruct((m, hp), BF16),
                   jax.ShapeDtypeStruct((m, MLA_HEADS * MLA_V), BF16)],
        compiler_params=_cparams(("arbitrary",)),
        name="mla_cache_expand",
    )(ckv, kpe_pad, wkv)


def _attn_kernel(*refs, n_seg, nb, tq, sks, n_cast):
    q_ref = refs[0]
    k_refs = refs[1:1 + n_seg]
    v_refs = refs[1 + n_seg:1 + 2 * n_seg]
    n_in = 1 + 2 * n_seg + n_cast
    o_ref = refs[n_in]
    s_ref = refs[n_in + 1 + n_cast]
    _run_casts(refs[1 + 2 * n_seg:n_in], refs[n_in + 1:n_in + 1 + n_cast])
    problems = [(b, h) for b in range(nb) for h in range(MLA_HEADS)]
    seg_cols = [sum(sks[:i]) for i in range(n_seg)]

    def scores(idx):
        b, h = problems[idx]
        q = q_ref[b * tq:(b + 1) * tq, h * MLA_HEAD_PAD:(h + 1) * MLA_HEAD_PAD]
        m_lane = None
        for k_ref, sk, c0 in zip(k_refs, sks, seg_cols):
            s = _dot_nt(q, k_ref[b * sk:(b + 1) * sk, h * MLA_HEAD_PAD:(h + 1) * MLA_HEAD_PAD])
            s_ref[idx % 2, :, c0:c0 + sk] = s
            for c in range(sk // LANES):
                piece = s[:, c * LANES:(c + 1) * LANES]
                m_lane = piece if m_lane is None else jnp.maximum(m_lane, piece)
        return m_lane.max(axis=-1, keepdims=True)

    def weighted_values(idx, m):
        b, h = problems[idx]
        acc = None
        for v_ref, sk, c0 in zip(v_refs, sks, seg_cols):
            p = jnp.exp2(s_ref[idx % 2, :, c0:c0 + sk] - m).astype(BF16)
            v = v_ref[b * sk:(b + 1) * sk, h * MLA_V:(h + 1) * MLA_V]
            part = _dot(p, jnp.concatenate([v, jnp.ones_like(v)], axis=1))
            acc = part if acc is None else acc + part
        o = acc[:, :MLA_V] / acc[:, MLA_V:]
        o_ref[b * tq:(b + 1) * tq, h * MLA_V:(h + 1) * MLA_V] = o.astype(o_ref.dtype)

    m = scores(0)
    for idx in range(len(problems)):
        m_next = scores(idx + 1) if idx + 1 < len(problems) else None
        weighted_values(idx, m)
        m = m_next


def _attn_call(q, ks, vs, *, n_batch, seq, sks, nb, tq, name, cast_jobs=()):
    n_seg = len(ks)
    tiles = seq // tq
    assert nb == 1 or tiles == 1
    hp, hv = MLA_HEADS * MLA_HEAD_PAD, MLA_HEADS * MLA_V
    in_specs = [pl.BlockSpec((nb * tq, hp), lambda b, t: (b * tiles + t, 0))]
    in_specs += [pl.BlockSpec((nb * sk, hp), lambda b, t: (b, 0)) for sk in sks]
    in_specs += [pl.BlockSpec((nb * sk, hv), lambda b, t: (b, 0)) for sk in sks]
    c_in, c_out, c_shape = _cast_specs(cast_jobs, (n_batch // nb) * tiles, lambda b, t: b * tiles + t)
    outs = pl.pallas_call(
        functools.partial(_attn_kernel, n_seg=n_seg, nb=nb, tq=tq, sks=tuple(sks), n_cast=len(cast_jobs)),
        grid=(n_batch // nb, tiles),
        in_specs=in_specs + c_in,
        out_specs=[pl.BlockSpec((nb * tq, hv), lambda b, t: (b * tiles + t, 0))] + c_out,
        out_shape=[jax.ShapeDtypeStruct((n_batch * seq, hv), BF16)] + c_shape,
        scratch_shapes=[pltpu.VMEM((2, tq, sum(sks)), F32)],
        compiler_params=_cparams(("arbitrary", "arbitrary")),
        name=name,
    )(q, *ks, *vs, *[w for w, _ in cast_jobs])
    return outs[0], list(outs[1:])


def _split_specs(rows, width):
    ct = rows.ctx_tiles
    return [pl.BlockSpec((rows.tm, width), lambda i: (jnp.minimum(i, ct - 1), 0)),
            pl.BlockSpec((rows.tm, width), lambda i: (jnp.maximum(i - ct, 0), 0))]


def _ffn_kernel(*refs, th, final, ctx_tiles, n_cast, n_x, mixed):
    it = iter(refs)
    x_refs = [next(it) for _ in range(n_x)]
    if mixed:
        yc_ref, yl_ref, wo_ref = next(it), next(it), next(it)
    mod_ref, g_ref, win_ref, wout_ref, fg_ref = (next(it) for _ in range(5))
    cast_in = [next(it) for _ in range(n_cast)]
    out_refs = [next(it) for _ in range(2 if final else 1)]
    cast_out = [next(it) for _ in range(n_cast)]
    scratch = list(it)
    act_ref = scratch[0]
    _run_casts(cast_in, cast_out)
    is_ctx = pl.program_id(0) < ctx_tiles
    if mixed:
        x1_ref = scratch[1]

        def mix(y_ref, x_ref):
            x1_ref[...] = x_ref[...] + mod_ref[0, 2:3, :] * _dot(y_ref[...], wo_ref[0])

        pl.when(is_ctx)(lambda: mix(yc_ref, x_refs[0]))
        pl.when(jnp.logical_not(is_ctx))(lambda: mix(yl_ref, x_refs[-1]))
        x = x1_ref[...]
    else:
        x = x_refs[0][...]
    h = _modulate(x, g_ref[...], mod_ref[0, 3:4, :], mod_ref[0, 4:5, :]).astype(BF16)
    hidden = wout_ref.shape[1]
    for c in range(hidden // th):
        a = _dot(h, win_ref[0, :, c * th:(c + 1) * th])
        b = _dot(h, win_ref[0, :, hidden + c * th:hidden + (c + 1) * th])
        act_ref[:, c * th:(c + 1) * th] = (_silu(a) * b).astype(BF16)
    y = x + mod_ref[0, 5:6, :] * _dot(act_ref[...], wout_ref[0])
    if not final:
        out_refs[0][...] = y
    else:
        acc_ref = scratch[-1]
        acc_ref[...] = _rms(y, fg_ref[...])

        @pl.when(is_ctx)
        def _():
            out_refs[0][...] = acc_ref[...]

        @pl.when(jnp.logical_not(is_ctx))
        def _():
            out_refs[1][...] = acc_ref[...]


def _ffn_call(x, mod, g, w_in_all, w_out_all, layer, final_g, rows, final, cast_jobs=(), mix=None):
    tm, d = rows.tm, w_in_all.shape[1]
    hidden = w_out_all.shape[1]
    split_x = isinstance(x, tuple)
    assert not split_x or mix is not None
    xs = list(x) if split_x else [x]
    x_specs = _split_specs(rows, d) if split_x else [pl.BlockSpec((tm, d), lambda i: (i, 0))]
    mix_args, mix_specs = [], []
    scratch = [pltpu.VMEM((tm, hidden), BF16)]
    if mix is not None:
        y_ctx, y_lat, w_o_all, w_o_layer = mix
        k = w_o_all.shape[1]
        mix_args = [y_ctx, y_lat, w_o_all]
        mix_specs = _split_specs(rows, k) + [_resident((1, k, d), lambda i: (w_o_layer, 0, 0))]
        scratch.append(pltpu.VMEM((tm, d), F32))
    if final:
        out_specs = _split_specs(rows, d)
        out_shape = [jax.ShapeDtypeStruct((rows.ctx_tiles * tm, d), F32),
                     jax.ShapeDtypeStruct(((rows.n_tiles - rows.ctx_tiles) * tm, d), F32)]
        scratch.append(pltpu.VMEM((tm, d), F32))
    else:
        out_specs = [pl.BlockSpec((tm, d), lambda i: (i, 0))]
        out_shape = [jax.ShapeDtypeStruct((rows.n_tiles * tm, d), F32)]
    c_in, c_out, c_shape = _cast_specs(cast_jobs, rows.n_tiles, lambda i: i)
    outs = pl.pallas_call(
        functools.partial(_ffn_kernel, th=2 * LANES, final=final, ctx_tiles=rows.ctx_tiles,
                          n_cast=len(cast_jobs), n_x=len(xs), mixed=mix is not None),
        grid=(rows.n_tiles,),
        in_specs=x_specs + mix_specs + [
            pl.BlockSpec((1, 6, d), lambda i: (rows.group(i), 0, 0)),
            _resident(g.shape, lambda i: (0, 0)),
            _resident((1, d, 2 * hidden), lambda i: (layer, 0, 0)),
            _resident((1, hidden, d), lambda i: (layer, 0, 0)),
            _resident(final_g.shape, lambda i: (0, 0)),
        ] + c_in,
        out_specs=out_specs + c_out,
        out_shape=out_shape + c_shape,
        scratch_shapes=scratch,
        compiler_params=_cparams(("arbitrary",)),
        name=("ffn_final" if final else "ffn") + ("_mix" if mix is not None else ""),
    )(*xs, *mix_args, mod, g, w_in_all, w_out_all, final_g, *[w for w, _ in cast_jobs])
    n_out = 2 if final else 1
    stream = tuple(outs[:2]) if final else outs[0]
    return stream, list(outs[n_out:])


CONV_HALO = 16


def _conv_kernel(x_ref, xp_ref, xn_ref, mod_ref, g_ref, win_ref, cw_ref, wout_ref, o_ref,
                 h_ref, z_ref, act_ref, *, tn, ctx_tiles, ctx_seq, lat_seq):
    tm, d = x_ref.shape
    hl = CONV_HALO
    shift, scale = mod_ref[0, 0:1, :], mod_ref[0, 1:2, :]
    h_ref[0:hl, :] = _modulate(xp_ref[...], g_ref[...], shift, scale).astype(BF16)
    h_ref[hl:hl + tm, :] = _modulate(x_ref[...], g_ref[...], shift, scale).astype(BF16)
    h_ref[hl + tm:, :] = _modulate(xn_ref[...], g_ref[...], shift, scale).astype(BF16)
    i = pl.program_id(0)
    is_ctx = i < ctx_tiles
    row = lax.broadcasted_iota(jnp.int32, (tm, 1), 0)
    lat_row0 = (jnp.maximum(i - ctx_tiles, 0) % (lat_seq // tm)) * tm
    pos = jnp.where(is_ctx, row & (ctx_seq - 1), row + lat_row0)
    seq = jnp.where(is_ctx, ctx_seq, lat_seq)
    has_prev = pos != 0
    has_next = pos != seq - 1
    for c in range(d // tn):
        sl = slice(c * tn, (c + 1) * tn)
        h = h_ref[...]
        cg = _dot(h, win_ref[:, d + c * tn:d + (c + 1) * tn])
        u = _dot(h, win_ref[:, 2 * d + c * tn:2 * d + (c + 1) * tn])
        z_ref[...] = cg * u
        bg = _dot(h_ref[hl:hl + tm, :], win_ref[:, sl])
        z_prev = jnp.where(has_prev, z_ref[hl - 1:hl - 1 + tm, :], 0.0)
        z_next = jnp.where(has_next, z_ref[hl + 1:hl + 1 + tm, :], 0.0)
        conv = z_prev * cw_ref[0:1, sl] + z_ref[hl:hl + tm, :] * cw_ref[1:2, sl] + z_next * cw_ref[2:3, sl]
        act_ref[:, sl] = (bg * conv).astype(BF16)
    o_ref[...] = x_ref[...] + mod_ref[0, 2:3, :] * _dot(act_ref[...], wout_ref[...])


def _conv_call(x, mod, g, w_in, conv_w, w_out, rows, ctx_seq, lat_seq):
    tm, d = rows.tm, x.shape[1]
    hl = CONV_HALO
    assert tm % ctx_seq == 0 and ctx_seq & (ctx_seq - 1) == 0 and lat_seq % tm == 0 and tm % hl == 0
    last_halo = x.shape[0] // hl - 1
    return pl.pallas_call(
        functools.partial(_conv_kernel, tn=2 * LANES, ctx_tiles=rows.ctx_tiles, ctx_seq=ctx_seq,
                          lat_seq=lat_seq),
        grid=(rows.n_tiles,),
        in_specs=[
            pl.BlockSpec((tm, d), lambda i: (i, 0)),
            pl.BlockSpec((hl, d), lambda i: (jnp.maximum(i * (tm // hl) - 1, 0), 0)),
            pl.BlockSpec((hl, d), lambda i: (jnp.minimum((i + 1) * (tm // hl), last_halo), 0)),
            pl.BlockSpec((1, 6, d), lambda i: (rows.group(i), 0, 0)),
            _resident(g.shape, lambda i: (0, 0)),
            _resident(w_in.shape, lambda i: (0, 0)),
            _resident(conv_w.shape, lambda i: (0, 0)),
            _resident(w_out.shape, lambda i: (0, 0)),
        ],
        out_specs=pl.BlockSpec((tm, d), lambda i: (i, 0)),
        out_shape=jax.ShapeDtypeStruct(x.shape, F32),
        scratch_shapes=[pltpu.VMEM((tm + 2 * hl, d), BF16), pltpu.VMEM((tm + 2 * hl, 2 * LANES), F32),
                        pltpu.VMEM((tm, d), BF16)],
        compiler_params=_cparams(("arbitrary",)),
        name="conv_mixer",
    )(x, x, x, mod, g, w_in, conv_w, w_out)


def _ret_proj_kernel(x_ref, mod_ref, g_ref, w_ref, qkv_ref, gate_ref, *, tn):
    h = _modulate(x_ref[...], g_ref[...], mod_ref[0, 0:1, :], mod_ref[0, 1:2, :]).astype(BF16)
    n_qkv = qkv_ref.shape[1]
    for c in range(n_qkv // tn):
        qkv_ref[:, c * tn:(c + 1) * tn] = _dot(h, w_ref[:, c * tn:(c + 1) * tn]).astype(BF16)
    for c in range(gate_ref.shape[1] // tn):
        gate = _dot(h, w_ref[:, n_qkv + c * tn:n_qkv + (c + 1) * tn])
        gate_ref[:, c * tn:(c + 1) * tn] = _silu(gate).astype(gate_ref.dtype)


def _ret_proj_call(x, mod, g, w, rows, n_gate):
    tm, d = rows.tm, x.shape[1]
    n_qkv = w.shape[1] - n_gate
    m = x.shape[0]
    return pl.pallas_call(
        functools.partial(_ret_proj_kernel, tn=4 * LANES),
        grid=(rows.n_tiles,),
        in_specs=[
            pl.BlockSpec((tm, d), lambda i: (i, 0)),
            pl.BlockSpec((1, 6, d), lambda i: (rows.group(i), 0, 0)),
            _resident(g.shape, lambda i: (0, 0)),
            _resident(w.shape, lambda i: (0, 0)),
        ],
        out_specs=[pl.BlockSpec((tm, n_qkv), lambda i: (i, 0)),
                   pl.BlockSpec((tm, n_gate), lambda i: (i, 0))],
        out_shape=[jax.ShapeDtypeStruct((m, n_qkv), BF16),
                   jax.ShapeDtypeStruct((m, n_gate), BF16)],
        compiler_params=_cparams(("arbitrary",)),
        name="ret_proj",
    )(x, mod, g, w)


def _ret_scan_kernel(*refs, chunk, hps, dk, dv, has_init, emit_state):
    it = iter(refs)
    lr_ref, q_ref, k_ref, v_ref, gate_ref, gn_ref = (next(it) for _ in range(6))
    s0_ref = next(it) if has_init else None
    y_ref = next(it)
    sout_ref = next(it) if emit_state else None
    o_ref, st_ref = next(it), next(it)

    seq = q_ref.shape[0]
    n_chunks = seq // chunk
    k_scale = dk ** -0.5
    row = lax.broadcasted_iota(jnp.int32, (chunk, chunk), 0).astype(F32)
    col = lax.broadcasted_iota(jnp.int32, (chunk, chunk), 1).astype(F32)
    ridx = lax.broadcasted_iota(jnp.int32, (chunk, 1), 0).astype(F32)
    dist = row - col

    for hd in range(hps):
        qs = slice(hd * dk, (hd + 1) * dk)
        vs = slice(hd * dv, (hd + 1) * dv)
        log_gamma = -jnp.exp(lr_ref[hd])
        lg_f, lg_b = log_gamma[0:1, :], log_gamma[1:2, :]
        mask = jnp.where(dist > 0, jnp.exp(jnp.maximum(dist, 0.0) * lg_f),
                         jnp.where(dist < 0, jnp.exp(jnp.maximum(-dist, 0.0) * lg_b), 2.0)) * k_scale
        q_decay = (jnp.exp((ridx + 1.0) * lg_f), jnp.exp((chunk - ridx) * lg_b))
        k_decay = (jnp.exp((chunk - 1.0 - ridx) * lg_f) * k_scale, jnp.exp(ridx * lg_b) * k_scale)
        chunk_decay = (jnp.exp(chunk * lg_f), jnp.exp(chunk * lg_b))

        for direction in range(2):
            order = range(n_chunks) if direction == 0 else range(n_chunks - 1, -1, -1)
            if has_init:
                st_ref[hd] = s0_ref[0, direction, hd]
            for step, c in enumerate(order):
                rs = slice(c * chunk, (c + 1) * chunk)
                qc, kc, vc = q_ref[rs, qs], k_ref[rs, qs], v_ref[rs, vs]
                have_state = has_init or step > 0
                if direction == 0:
                    scores = _dot_nt(qc, kc) * mask
                    o_ref[rs, vs] = _dot(scores.astype(BF16), vc)
                if have_state:
                    o_ref[rs, vs] += _dot(qc, st_ref[hd].astype(BF16)) * q_decay[direction]
                kd = (kc.astype(F32) * k_decay[direction]).T.astype(BF16)
                update = _dot(kd, vc)
                if have_state:
                    st_ref[hd] = st_ref[hd] * chunk_decay[direction] + update
                else:
                    st_ref[hd] = update
            if emit_state:
                sout_ref[0, 0, direction, hd] = st_ref[hd]

        o = o_ref[:, vs]
        mu = jnp.mean(o, axis=-1, keepdims=True)
        var = jnp.mean(jnp.square(o - mu), axis=-1, keepdims=True)
        on = (o - mu) * lax.rsqrt(var + EPS) * gn_ref[:, vs]
        y_ref[:, vs] = (gate_ref[:, vs].astype(F32) * on).astype(y_ref.dtype)


def _ret_scan_call(log_rate, qkv, gate, gn_g, s0, *, bsz, seq, row0, hps, emit_state, name):
    n_gate = gate.shape[1]
    nh = RET_HEADS
    dv = n_gate // nh
    dk = (qkv.shape[1] - n_gate) // (2 * nh)
    chunk = min(seq, 2 * LANES)
    has_init = s0 is not None
    assert row0 % seq == 0 and nh % hps == 0
    b0 = row0 // seq
    hg = nh // hps
    in_specs = [
        pl.BlockSpec((hps, 2, 1), lambda b, h: (h, 0, 0)),
        pl.BlockSpec((seq, hps * dk), lambda b, h: (b0 + b, h)),
        pl.BlockSpec((seq, hps * dk), lambda b, h: (b0 + b, hg + h)),
        pl.BlockSpec((seq, hps * dv), lambda b, h: (b0 + b, (2 * nh * dk) // (hps * dv) + h)),
        pl.BlockSpec((seq, hps * dv), lambda b, h: (b0 + b, h)),
        pl.BlockSpec((1, hps * dv), lambda b, h: (0, h)),
    ]
    args = [log_rate, qkv, qkv, qkv, gate, gn_g]
    if has_init:
        in_specs.append(pl.BlockSpec((1, 2, hps, dk, dv), lambda b, h: (b, 0, h, 0, 0)))
        args.append(s0)
    out_specs = [pl.BlockSpec((seq, hps * dv), lambda b, h: (b, h))]
    out_shape = [jax.ShapeDtypeStruct((bsz * seq, n_gate), BF16)]
    if emit_state:
        out_specs.append(pl.BlockSpec((1, 1, 2, hps, dk, dv), lambda b, h: (b, 0, 0, h, 0, 0)))
        out_shape.append(jax.ShapeDtypeStruct((bsz, 1, 2, nh, dk, dv), F32))
    return pl.pallas_call(
        functools.partial(_ret_scan_kernel, chunk=chunk, hps=hps, dk=dk, dv=dv,
                          has_init=has_init, emit_state=emit_state),
        grid=(bsz, hg),
        in_specs=in_specs,
        out_specs=out_specs,
        out_shape=out_shape,
        scratch_shapes=[pltpu.VMEM((seq, hps * dv), F32), pltpu.VMEM((hps, dk, dv), F32)],
        compiler_params=_cparams(("arbitrary", "arbitrary")),
        name=name,
    )(*args)


def _rope_swap_index():
    f = ROPE_AXIS_FREQS
    idx = jnp.arange(MLA_ROPE)
    return jnp.where((idx // f) % 2 == 0, idx + f, idx - f)


def _mla_weights(w_a, q_norm_g, kv_norm_g, w_q_b, w_kv_b):
    d = w_a.shape[0]
    swap = _rope_swap_index()
    nq = MLA_Q_RANK + MLA_KV_RANK
    zpad = jnp.zeros((d, LANES - MLA_ROPE), w_a.dtype)
    w_kpe = w_a[:, nq:]
    wa = jnp.concatenate([w_a[:, :nq], w_kpe, zpad, w_kpe[:, swap], zpad], axis=1).astype(BF16)
    wq = w_q_b.reshape(MLA_Q_RANK, MLA_HEADS, MLA_NOPE + MLA_ROPE)
    wq_nope = wq[:, :, :MLA_NOPE].reshape(MLA_Q_RANK, MLA_HEADS * MLA_NOPE)
    wq_pe = wq[:, :, MLA_NOPE:]
    pad = ((0, 0), (0, 0), (0, LANES - MLA_ROPE))
    wq_pe_pad = jnp.pad(wq_pe, pad).reshape(MLA_Q_RANK, MLA_HEADS * LANES)
    wq_sw_pad = jnp.pad(wq_pe[:, :, swap], pad).reshape(MLA_Q_RANK, MLA_HEADS * LANES)
    wq_all = jnp.concatenate([wq_nope, wq_pe_pad, wq_sw_pad], axis=1).astype(BF16)
    wkv = w_kv_b.reshape(MLA_KV_RANK, MLA_HEADS, MLA_NOPE + MLA_V)
    wkv_all = jnp.concatenate(
        [wkv[:, :, :MLA_NOPE].reshape(MLA_KV_RANK, -1), wkv[:, :, MLA_NOPE:].reshape(MLA_KV_RANK, -1)],
        axis=1).astype(BF16)
    return wa, q_norm_g[None, :], kv_norm_g[None, :], wq_all, wkv_all


def _rope_tables(n_tokens):
    f = ROPE_AXIS_FREQS
    f32 = np.float32
    rows = n_tokens // GRID_W
    r = np.repeat(np.arange(rows, dtype=f32), GRID_W)
    col = np.tile(np.arange(GRID_W, dtype=f32), rows)
    inv = (f32(ROPE_THETA) ** (-np.arange(f, dtype=f32) / f32(f))).astype(f32)
    ang_r, ang_c = r[:, None] * inv, col[:, None] * inv
    cos = np.concatenate([np.cos(ang_r)] * 2 + [np.cos(ang_c)] * 2, axis=1)
    sin = np.concatenate([-np.sin(ang_r), np.sin(ang_r), -np.sin(ang_c), np.sin(ang_c)], axis=1)
    zpad = np.zeros((n_tokens, LANES - MLA_ROPE), f32)
    return (jnp.asarray(np.concatenate([cos, zpad], axis=1), F32),
            jnp.asarray(np.concatenate([sin, zpad], axis=1), F32))


def kernel(x_prompt, x_sample, c, c_ctx, cache_mla_ckv, cache_mla_kpe, state_ret, ada_w, ada_b, norm_mix_g, norm_ffn_g, mla_w_a, mla_q_norm_g, mla_kv_norm_g, mla_w_q_b, mla_w_kv_b, mla_w_o, conv_w_in, conv_w, conv_w_out, ret_w_in, ret_log_rate, ret_gn_g, ret_w_out, ffn_w_in, ffn_w_out, final_norm_g):
    batch, seq, d = x_prompt.shape
    dec_batch, dec_seq, _ = x_sample.shape
    depth = ada_w.shape[0]
    n_ctx = batch * seq
    n_lat = dec_batch * dec_seq
    past = cache_mla_ckv.shape[2]
    assert 1 + dec_batch <= MOD_ROWS

    cvec = jnp.zeros((MOD_ROWS, d), F32).at[0].set(c_ctx).at[1:1 + dec_batch].set(c)
    mod_all = _ada_call(cvec, ada_w, ada_b).reshape(depth, MOD_ROWS, 6, d)

    rows_s = _Rows(n_ctx, dec_batch, dec_seq, min(512, dec_seq))
    rows_l = _Rows(n_ctx, dec_batch, dec_seq, min(1024, dec_seq))
    rope_tabs = _rope_tables(dec_seq)
    final_g = final_norm_g[None, :]
    mla_w_o_bf = mla_w_o.astype(BF16)

    def mixer_cast_jobs(layer):
        if layer >= depth:
            return []
        kind, j = layer % N_MIXERS, layer // N_MIXERS
        if kind == 1:
            return [(conv_w_in, j), (conv_w_out, j)]
        if kind == 2:
            return [(ret_w_in, j), (ret_w_out, j)]
        return []

    x = (x_prompt.reshape(n_ctx, d), x_sample.reshape(n_lat, d))
    new_ckv, new_kpe, new_ret = [], [], []
    ffn_bf = mixer_bf = None
    for i in range(depth):
        kind, j = i % N_MIXERS, i // N_MIXERS
        mod = mod_all[i]
        g_mix = norm_mix_g[i][None, :]
        if ffn_bf is None:
            assert kind == 0
        if kind == 0:
            wts = _mla_weights(mla_w_a[j], mla_q_norm_g[j], mla_kv_norm_g[j], mla_w_q_b[j], mla_w_kv_b[j])
            split = isinstance(x, tuple)
            qc, kc, vc, ckv_c, kpe_c = _mla_proj_call(
                x[0] if split else x, 0, mod, g_mix, wts, rows_s, tile0=0, n_tiles=rows_s.ctx_tiles,
                rope_tabs=None, emit_cache=True)
            ql, kl, vl = _mla_proj_call(
                x[1] if split else x, 0 if split else rows_s.ctx_tiles, mod, g_mix, wts, rows_s,
                tile0=rows_s.ctx_tiles, n_tiles=rows_s.n_tiles - rows_s.ctx_tiles,
                rope_tabs=rope_tabs, emit_cache=False)
            new_ckv.append(ckv_c.reshape(batch, seq, MLA_KV_RANK))
            new_kpe.append(kpe_c.reshape(batch, seq, MLA_ROPE))
            kpe_pad = jnp.pad(cache_mla_kpe[:, j], ((0, 0), (0, 0), (0, LANES - MLA_ROPE)))
            kp, vp = _cache_expand_call(
                cache_mla_ckv[:, j].reshape(dec_batch * past, MLA_KV_RANK),
                kpe_pad.reshape(dec_batch * past, LANES), wts[4])
            o_c, _ = _attn_call(qc, [kc], [vc], n_batch=batch, seq=seq, sks=[seq],
                                nb=min(4, batch), tq=seq, name="attn_ctx")
            jobs = [] if ffn_bf is not None else [(ffn_w_in, i), (ffn_w_out, i)]
            o_l, cast = _attn_call(ql, [kp, kl], [vp, vl], n_batch=dec_batch, seq=dec_seq,
                                   sks=[past, dec_seq], nb=1, tq=min(512, dec_seq), name="attn_lat",
                                   cast_jobs=jobs)
            if jobs:
                ffn_bf = cast
            mix = (o_c, o_l, mla_w_o_bf, j)
        elif kind == 1:
            mix = None
            x = _conv_call(x, mod, g_mix, mixer_bf[0][0], conv_w[j], mixer_bf[1][0], rows_l, seq, dec_seq)
        else:
            n_gate = ret_w_out.shape[1]
            qkv, gate = _ret_proj_call(x, mod, g_mix, mixer_bf[0][0], rows_l, n_gate)
            lr = ret_log_rate[j].T[:, :, None]
            gn = ret_gn_g[j][None, :]
            y_c, st = _ret_scan_call(lr, qkv, gate, gn, None, bsz=batch, seq=seq, row0=0,
                                     hps=RET_HEADS, emit_state=True, name="ret_scan_ctx")
            (y_l,) = _ret_scan_call(lr, qkv, gate, gn, state_ret[:, j], bsz=dec_batch, seq=dec_seq,
                                    row0=n_ctx, hps=1, emit_state=False, name="ret_scan_lat")
            new_ret.append(st)
            mix = (y_c, y_l, mixer_bf[1], 0)
        last = i == depth - 1
        jobs = [] if last else [(ffn_w_in, i + 1), (ffn_w_out, i + 1)] + mixer_cast_jobs(i + 1)
        x, cast = _ffn_call(x, mod, norm_ffn_g[i][None, :], ffn_bf[0], ffn_bf[1], 0, final_g,
                            rows_l if mix is None else rows_s, final=last, cast_jobs=jobs, mix=mix)
        ffn_bf, mixer_bf = cast[:2], cast[2:]

    y_prompt = x[0].reshape(batch, seq, d)
    y_sample = x[1].reshape(dec_batch, dec_seq, d)
    return (y_prompt, y_sample, jnp.stack(new_ckv, axis=1), jnp.stack(new_kpe, axis=1),
            jnp.concatenate(new_ret, axis=1))
```

```python
import functools
import math

import jax
import jax.numpy as jnp
import numpy as np
from jax import lax
from jax.experimental import pallas as pl
from jax.experimental.pallas import tpu as pltpu

F32 = jnp.float32
BF16 = jnp.bfloat16

N_MIXERS = 3
MLA_HEADS = 8
MLA_NOPE = 128
MLA_ROPE = 64
MLA_V = 128
MLA_Q_RANK = 384
MLA_KV_RANK = 256
MLA_SCALE = (MLA_NOPE + MLA_ROPE) ** -0.5
MLA_Q_SCALE = MLA_SCALE * 1.4426950408889634
ROPE_THETA = 10000.0
ROPE_AXIS_FREQS = MLA_ROPE // 4
GRID_W = 64
RET_HEADS = 4
EPS = 1e-6

LANES = 128
MLA_HEAD_PAD = 2 * LANES
MOD_ROWS = 8
VMEM_LIMIT = 56 * 1024 * 1024


def _cparams(sem):
    return pltpu.CompilerParams(dimension_semantics=sem, vmem_limit_bytes=VMEM_LIMIT)


def _resident(shape, index_map):
    return pl.BlockSpec(shape, index_map, pipeline_mode=pl.Buffered(1))


def _rms(x, g):
    return x * lax.rsqrt(jnp.mean(x * x, axis=-1, keepdims=True) + EPS) * g


def _modulate(x, g, shift, scale):
    return _rms(x, g) * (1.0 + scale) + shift


def _silu(x):
    return x * jax.nn.sigmoid(x)


def _dot(a, b):
    return jnp.dot(a, b, preferred_element_type=F32)


def _dot_nt(a, b):
    return lax.dot_general(a, b, (((1,), (1,)), ((), ())), preferred_element_type=F32)


MAX_CAST_SLABS = 16


def _cast_specs(jobs, n_steps, step_of):
    n_slabs = math.gcd(n_steps, MAX_CAST_SLABS)
    per = n_steps // n_slabs

    def slab_of_step(*g):
        return step_of(*g) // per

    in_specs, out_specs, out_shape = [], [], []
    for w, layer in jobs:
        _, k, n = w.shape
        blk = (1, k // n_slabs, n)
        in_specs.append(pl.BlockSpec(blk, lambda *g, layer=layer: (layer, slab_of_step(*g), 0)))
        out_specs.append(pl.BlockSpec(blk, lambda *g: (0, slab_of_step(*g), 0)))
        out_shape.append(jax.ShapeDtypeStruct((1, k, n), BF16))
    return in_specs, out_specs, out_shape


def _run_casts(in_refs, out_refs):
    for src, dst in zip(in_refs, out_refs):
        dst[...] = src[...].astype(BF16)


def _ada_kernel(c_ref, w_ref, b_ref, o_ref):
    a = _silu(c_ref[...]).astype(BF16)
    o_ref[0] = _dot(a, w_ref[0].astype(BF16)) + b_ref[0]


def _ada_call(cvec, ada_w, ada_b):
    depth, d, n = ada_w.shape
    tn = d
    return pl.pallas_call(
        _ada_kernel,
        grid=(depth, n // tn),
        in_specs=[
            pl.BlockSpec((MOD_ROWS, d), lambda l, j: (0, 0)),
            pl.BlockSpec((1, d, tn), lambda l, j: (l, 0, j)),
            pl.BlockSpec((1, 1, tn), lambda l, j: (l, 0, j)),
        ],
        out_specs=pl.BlockSpec((1, MOD_ROWS, tn), lambda l, j: (l, 0, j)),
        out_shape=jax.ShapeDtypeStruct((depth, MOD_ROWS, n), F32),
        compiler_params=_cparams(("arbitrary", "arbitrary")),
        name="ada",
    )(cvec, ada_w, ada_b.reshape(depth, 1, n))


class _Rows:
    def __init__(self, n_ctx, dec_batch, dec_seq, tm):
        assert n_ctx % tm == 0 and dec_seq % tm == 0
        self.tm = tm
        self.ctx_tiles = n_ctx // tm
        self.seq_tiles = dec_seq // tm
        self.n_tiles = self.ctx_tiles + dec_batch * self.seq_tiles

    def group(self, i):
        lat = jnp.maximum(i - self.ctx_tiles, 0) // self.seq_tiles
        return jnp.where(i < self.ctx_tiles, 0, 1 + lat)


def _mla_project(x_ref, mod_ref, g_ref, wa_ref, qg_ref, kvg_ref, wq_ref, wkv_ref, rope_refs,
                 q_ref, k_ref, v_ref, cache_refs):
    h = _modulate(x_ref[...], g_ref[...], mod_ref[0, 0:1, :], mod_ref[0, 1:2, :]).astype(BF16)
    a = _dot(h, wa_ref[...])
    nq = MLA_Q_RANK + MLA_KV_RANK
    ckv_n = _rms(a[:, MLA_Q_RANK:nq], kvg_ref[...])
    kpe = a[:, nq:nq + LANES]
    if cache_refs is not None:
        cache_refs[0][...] = ckv_n
        cache_refs[1][...] = kpe[:, :MLA_ROPE]
    if rope_refs is not None:
        cos, sin = rope_refs[0][...], rope_refs[1][...]
        kpe = kpe * cos + a[:, nq + LANES:nq + 2 * LANES] * sin
    kpe = kpe.astype(BF16)

    qn = _rms(a[:, :MLA_Q_RANK], qg_ref[...]).astype(BF16)
    nh = MLA_HEADS
    n_nope, n_pe = nh * MLA_NOPE, nh * MLA_ROPE
    q_nope = _dot(qn, wq_ref[:, :n_nope])
    q_pe = _dot(qn, wq_ref[:, n_nope:n_nope + n_pe])
    if rope_refs is not None:
        q_sw = _dot(qn, wq_ref[:, n_nope + n_pe:n_nope + 2 * n_pe])
        cos, sin = cos * MLA_Q_SCALE, sin * MLA_Q_SCALE
    kv = _dot(ckv_n.astype(BF16), wkv_ref[...])
    low_half = lax.broadcasted_iota(jnp.int32, (1, LANES), 1) < MLA_ROPE
    for pair in range(nh // 2):
        ps = slice(pair * LANES, (pair + 1) * LANES)
        if rope_refs is not None:
            pe = q_pe[:, ps] * cos + q_sw[:, ps] * sin
        else:
            pe = q_pe[:, ps] * MLA_Q_SCALE
        for hd, keep in ((2 * pair, low_half), (2 * pair + 1, jnp.logical_not(low_half))):
            lo = hd * MLA_HEAD_PAD
            hs = slice(hd * LANES, (hd + 1) * LANES)
            q_ref[:, lo:lo + LANES] = (q_nope[:, hs] * MLA_Q_SCALE).astype(BF16)
            q_ref[:, lo + LANES:lo + 2 * LANES] = jnp.where(keep, pe, 0.0).astype(BF16)
            k_ref[:, lo:lo + LANES] = kv[:, hs].astype(BF16)
            k_ref[:, lo + LANES:lo + 2 * LANES] = kpe
    v_ref[...] = kv[:, nh * LANES:].astype(BF16)


def _mla_proj_kernel(x_ref, mod_ref, g_ref, wa_ref, qg_ref, kvg_ref, wq_ref, wkv_ref, ck_ref, sk_ref,
                     q_ref, k_ref, v_ref):
    _mla_project(x_ref, mod_ref, g_ref, wa_ref, qg_ref, kvg_ref, wq_ref, wkv_ref, (ck_ref, sk_ref),
                 q_ref, k_ref, v_ref, None)


def _mla_proj_call(x, x_tile0, mod, g, wts, rows, *, tile0, n_tiles, rope_tabs):
    tm, d = rows.tm, x.shape[1]
    wa, qg, kvg, wq, wkv = wts
    full = lambda arr: _resident(arr.shape, lambda i: (0,) * arr.ndim)
    in_specs = [
        pl.BlockSpec((tm, d), lambda i: (i + x_tile0, 0)),
        pl.BlockSpec((1, 6, d), lambda i: (rows.group(i + tile0), 0, 0)),
        full(g), full(wa), full(qg), full(kvg), full(wq), full(wkv),
    ]
    in_specs += [pl.BlockSpec((tm, LANES), lambda i: (i % rows.seq_tiles, 0)) for _ in rope_tabs]
    m = n_tiles * tm
    hp, hv = MLA_HEADS * MLA_HEAD_PAD, MLA_HEADS * MLA_V
    return pl.pallas_call(
        _mla_proj_kernel,
        grid=(n_tiles,),
        in_specs=in_specs,
        out_specs=[pl.BlockSpec((tm, hp), lambda i: (i, 0)), pl.BlockSpec((tm, hp), lambda i: (i, 0)),
                   pl.BlockSpec((tm, hv), lambda i: (i, 0))],
        out_shape=[jax.ShapeDtypeStruct((m, hp), BF16), jax.ShapeDtypeStruct((m, hp), BF16),
                   jax.ShapeDtypeStruct((m, hv), BF16)],
        compiler_params=_cparams(("arbitrary",)),
        name="mla_proj_lat",
    )(x, mod, g, wa, qg, kvg, wq, wkv, *rope_tabs)


def _mla_ctx_kernel(x_ref, mod_ref, g_ref, wa_ref, qg_ref, kvg_ref, wq_ref, wkv_ref,
                    o_ref, ckv_ref, kpe_ref, q_s, k_s, v_s, s_ref, *, seq):
    _mla_project(x_ref, mod_ref, g_ref, wa_ref, qg_ref, kvg_ref, wq_ref, wkv_ref, None,
                 q_s, k_s, v_s, (ckv_ref, kpe_ref))
    _attend(q_s, [k_s], [v_s], o_ref, s_ref, nb=x_ref.shape[0] // seq, tq=seq, sks=(seq,))


def _mla_ctx_call(x, mod, g, wts, rows, seq):
    tm, d = rows.tm, x.shape[1]
    assert tm % seq == 0
    wa, qg, kvg, wq, wkv = wts
    full = lambda arr: _resident(arr.shape, lambda i: (0,) * arr.ndim)
    m = rows.ctx_tiles * tm
    hp, hv = MLA_HEADS * MLA_HEAD_PAD, MLA_HEADS * MLA_V
    return pl.pallas_call(
        functools.partial(_mla_ctx_kernel, seq=seq),
        grid=(rows.ctx_tiles,),
        in_specs=[
            pl.BlockSpec((tm, d), lambda i: (i, 0)),
            pl.BlockSpec((1, 6, d), lambda i: (0, 0, 0)),
            full(g), full(wa), full(qg), full(kvg), full(wq), full(wkv),
        ],
        out_specs=[pl.BlockSpec((tm, hv), lambda i: (i, 0)),
                   pl.BlockSpec((tm, MLA_KV_RANK), lambda i: (i, 0)),
                   pl.BlockSpec((tm, MLA_ROPE), lambda i: (i, 0))],
        out_shape=[jax.ShapeDtypeStruct((m, hv), BF16),
                   jax.ShapeDtypeStruct((m, MLA_KV_RANK), F32),
                   jax.ShapeDtypeStruct((m, MLA_ROPE), F32)],
        scratch_shapes=[pltpu.VMEM((tm, hp), BF16), pltpu.VMEM((tm, hp), BF16), pltpu.VMEM((tm, hv), BF16),
                        pltpu.VMEM((2, seq, seq), F32)],
        compiler_params=_cparams(("arbitrary",)),
        name="mla_ctx",
    )(x, mod, g, wa, qg, kvg, wq, wkv)


def _cache_expand_kernel(ckv_ref, kpe_ref, wkv_ref, k_ref, v_ref):
    kv = _dot(ckv_ref[...].astype(BF16), wkv_ref[...])
    kpe = kpe_ref[...].astype(BF16)
    nh = MLA_HEADS
    for hd in range(nh):
        lo = hd * MLA_HEAD_PAD
        k_ref[:, lo:lo + LANES] = kv[:, hd * LANES:(hd + 1) * LANES].astype(BF16)
        k_ref[:, lo + LANES:lo + 2 * LANES] = kpe
    v_ref[...] = kv[:, nh * LANES:].astype(BF16)


def _cache_expand_call(ckv, kpe_pad, wkv):
    m = ckv.shape[0]
    tm = min(m, 512)
    hp = MLA_HEADS * MLA_HEAD_PAD
    return pl.pallas_call(
        _cache_expand_kernel,
        grid=(m // tm,),
        in_specs=[
            pl.BlockSpec((tm, MLA_KV_RANK), lambda i: (i, 0)),
            pl.BlockSpec((tm, LANES), lambda i: (i, 0)),
            _resident(wkv.shape, lambda i: (0, 0)),
        ],
        out_specs=[pl.BlockSpec((tm, hp), lambda i: (i, 0)),
                   pl.BlockSpec((tm, MLA_HEADS * MLA_V), lambda i: (i, 0))],
        out_shape=[jax.ShapeDtypeStruct((m, hp), BF16),
                   jax.ShapeDtypeStruct((m, MLA_HEADS * MLA_V), BF16)],
        compiler_params=_cparams(("arbitrary",)),
        name="mla_cache_expand",
    )(ckv, kpe_pad, wkv)


def _attend(q_ref, k_refs, v_refs, o_ref, s_ref, *, nb, tq, sks):
    problems = [(b, h) for b in range(nb) for h in range(MLA_HEADS)]
    seg_cols = [sum(sks[:i]) for i in range(len(sks))]

    def scores(idx):
        b, h = problems[idx]
        q = q_ref[b * tq:(b + 1) * tq, h * MLA_HEAD_PAD:(h + 1) * MLA_HEAD_PAD]
        m_lane = None
        for k_ref, sk, c0 in zip(k_refs, sks, seg_cols):
            s = _dot_nt(q, k_ref[b * sk:(b + 1) * sk, h * MLA_HEAD_PAD:(h + 1) * MLA_HEAD_PAD])
            s_ref[idx % 2, :, c0:c0 + sk] = s
            for c in range(sk // LANES):
                piece = s[:, c * LANES:(c + 1) * LANES]
                m_lane = piece if m_lane is None else jnp.maximum(m_lane, piece)
        return m_lane.max(axis=-1, keepdims=True)

    def weighted_values(idx, m):
        b, h = problems[idx]
        acc = None
        for v_ref, sk, c0 in zip(v_refs, sks, seg_cols):
            p = jnp.exp2(s_ref[idx % 2, :, c0:c0 + sk] - m).astype(BF16)
            v = v_ref[b * sk:(b + 1) * sk, h * MLA_V:(h + 1) * MLA_V]
            part = _dot(p, jnp.concatenate([v, jnp.ones_like(v)], axis=1))
            acc = part if acc is None else acc + part
        o = acc[:, :MLA_V] / acc[:, MLA_V:]
        o_ref[b * tq:(b + 1) * tq, h * MLA_V:(h + 1) * MLA_V] = o.astype(o_ref.dtype)

    m = scores(0)
    for idx in range(len(problems)):
        m_next = scores(idx + 1) if idx + 1 < len(problems) else None
        weighted_values(idx, m)
        m = m_next


def _attn_kernel(*refs, n_seg, nb, tq, sks, n_cast):
    n_in = 1 + 2 * n_seg + n_cast
    _run_casts(refs[1 + 2 * n_seg:n_in], refs[n_in + 1:n_in + 1 + n_cast])
    _attend(refs[0], refs[1:1 + n_seg], refs[1 + n_seg:1 + 2 * n_seg], refs[n_in],
            refs[n_in + 1 + n_cast], nb=nb, tq=tq, sks=sks)


def _attn_call(q, ks, vs, *, n_batch, seq, sks, nb, tq, name, cast_jobs=()):
    n_seg = len(ks)
    tiles = seq // tq
    assert nb == 1 or tiles == 1
    hp, hv = MLA_HEADS * MLA_HEAD_PAD, MLA_HEADS * MLA_V
    in_specs = [pl.BlockSpec((nb * tq, hp), lambda b, t: (b * tiles + t, 0))]
    in_specs += [pl.BlockSpec((nb * sk, hp), lambda b, t: (b, 0)) for sk in sks]
    in_specs += [pl.BlockSpec((nb * sk, hv), lambda b, t: (b, 0)) for sk in sks]
    c_in, c_out, c_shape = _cast_specs(cast_jobs, (n_batch // nb) * tiles, lambda b, t: b * tiles + t)
    outs = pl.pallas_call(
        functools.partial(_attn_kernel, n_seg=n_seg, nb=nb, tq=tq, sks=tuple(sks), n_cast=len(cast_jobs)),
        grid=(n_batch // nb, tiles),
        in_specs=in_specs + c_in,
        out_specs=[pl.BlockSpec((nb * tq, hv), lambda b, t: (b * tiles + t, 0))] + c_out,
        out_shape=[jax.ShapeDtypeStruct((n_batch * seq, hv), BF16)] + c_shape,
        scratch_shapes=[pltpu.VMEM((2, tq, sum(sks)), F32)],
        compiler_params=_cparams(("arbitrary", "arbitrary")),
        name=name,
    )(q, *ks, *vs, *[w for w, _ in cast_jobs])
    return outs[0], list(outs[1:])


def _split_specs(rows, width):
    ct = rows.ctx_tiles
    return [pl.BlockSpec((rows.tm, width), lambda i: (jnp.minimum(i, ct - 1), 0)),
            pl.BlockSpec((rows.tm, width), lambda i: (jnp.maximum(i - ct, 0), 0))]


def _ffn_kernel(*refs, th, final, ctx_tiles, n_cast, n_x, mixed):
    it = iter(refs)
    x_refs = [next(it) for _ in range(n_x)]
    if mixed:
        yc_ref, yl_ref, wo_ref = next(it), next(it), next(it)
    mod_ref, g_ref, win_ref, wout_ref, fg_ref = (next(it) for _ in range(5))
    cast_in = [next(it) for _ in range(n_cast)]
    out_refs = [next(it) for _ in range(2 if final else 1)]
    cast_out = [next(it) for _ in range(n_cast)]
    scratch = list(it)
    act_ref = scratch[0]
    _run_casts(cast_in, cast_out)
    is_ctx = pl.program_id(0) < ctx_tiles
    if mixed:
        x1_ref = scratch[1]

        def mix(y_ref, x_ref):
            x1_ref[...] = x_ref[...] + mod_ref[0, 2:3, :] * _dot(y_ref[...], wo_ref[0])

        pl.when(is_ctx)(lambda: mix(yc_ref, x_refs[0]))
        pl.when(jnp.logical_not(is_ctx))(lambda: mix(yl_ref, x_refs[-1]))
        x = x1_ref[...]
    else:
        x = x_refs[0][...]
    h = _modulate(x, g_ref[...], mod_ref[0, 3:4, :], mod_ref[0, 4:5, :]).astype(BF16)
    hidden = wout_ref.shape[1]
    for c in range(hidden // th):
        a = _dot(h, win_ref[0, :, c * th:(c + 1) * th])
        b = _dot(h, win_ref[0, :, hidden + c * th:hidden + (c + 1) * th])
        act_ref[:, c * th:(c + 1) * th] = (_silu(a) * b).astype(BF16)
    y = x + mod_ref[0, 5:6, :] * _dot(act_ref[...], wout_ref[0])
    if not final:
        out_refs[0][...] = y
    else:
        acc_ref = scratch[-1]
        acc_ref[...] = _rms(y, fg_ref[...])

        @pl.when(is_ctx)
        def _():
            out_refs[0][...] = acc_ref[...]

        @pl.when(jnp.logical_not(is_ctx))
        def _():
            out_refs[1][...] = acc_ref[...]


def _ffn_call(x, mod, g, w_in_all, w_out_all, layer, final_g, rows, final, cast_jobs=(), mix=None):
    tm, d = rows.tm, w_in_all.shape[1]
    hidden = w_out_all.shape[1]
    split_x = isinstance(x, tuple)
    assert not split_x or mix is not None
    xs = list(x) if split_x else [x]
    x_specs = _split_specs(rows, d) if split_x else [pl.BlockSpec((tm, d), lambda i: (i, 0))]
    mix_args, mix_specs = [], []
    scratch = [pltpu.VMEM((tm, hidden), BF16)]
    if mix is not None:
        y_ctx, y_lat, w_o_all, w_o_layer = mix
        k = w_o_all.shape[1]
        mix_args = [y_ctx, y_lat, w_o_all]
        mix_specs = _split_specs(rows, k) + [_resident((1, k, d), lambda i: (w_o_layer, 0, 0))]
        scratch.append(pltpu.VMEM((tm, d), F32))
    if final:
        out_specs = _split_specs(rows, d)
        out_shape = [jax.ShapeDtypeStruct((rows.ctx_tiles * tm, d), F32),
                     jax.ShapeDtypeStruct(((rows.n_tiles - rows.ctx_tiles) * tm, d), F32)]
        scratch.append(pltpu.VMEM((tm, d), F32))
    else:
        out_specs = [pl.BlockSpec((tm, d), lambda i: (i, 0))]
        out_shape = [jax.ShapeDtypeStruct((rows.n_tiles * tm, d), F32)]
    c_in, c_out, c_shape = _cast_specs(cast_jobs, rows.n_tiles, lambda i: i)
    outs = pl.pallas_call(
        functools.partial(_ffn_kernel, th=2 * LANES, final=final, ctx_tiles=rows.ctx_tiles,
                          n_cast=len(cast_jobs), n_x=len(xs), mixed=mix is not None),
        grid=(rows.n_tiles,),
        in_specs=x_specs + mix_specs + [
            pl.BlockSpec((1, 6, d), lambda i: (rows.group(i), 0, 0)),
            _resident(g.shape, lambda i: (0, 0)),
            _resident((1, d, 2 * hidden), lambda i: (layer, 0, 0)),
            _resident((1, hidden, d), lambda i: (layer, 0, 0)),
            _resident(final_g.shape, lambda i: (0, 0)),
        ] + c_in,
        out_specs=out_specs + c_out,
        out_shape=out_shape + c_shape,
        scratch_shapes=scratch,
        compiler_params=_cparams(("arbitrary",)),
        name=("ffn_final" if final else "ffn") + ("_mix" if mix is not None else ""),
    )(*xs, *mix_args, mod, g, w_in_all, w_out_all, final_g, *[w for w, _ in cast_jobs])
    n_out = 2 if final else 1
    stream = tuple(outs[:2]) if final else outs[0]
    return stream, list(outs[n_out:])


CONV_HALO = 16


def _conv_kernel(x_ref, xp_ref, xn_ref, mod_ref, g_ref, win_ref, cw_ref, wout_ref, o_ref,
                 h_ref, z_ref, act_ref, *, tn, ctx_tiles, ctx_seq, lat_seq):
    tm, d = x_ref.shape
    hl = CONV_HALO
    shift, scale = mod_ref[0, 0:1, :], mod_ref[0, 1:2, :]
    h_ref[0:hl, :] = _modulate(xp_ref[...], g_ref[...], shift, scale).astype(BF16)
    h_ref[hl:hl + tm, :] = _modulate(x_ref[...], g_ref[...], shift, scale).astype(BF16)
    h_ref[hl + tm:, :] = _modulate(xn_ref[...], g_ref[...], shift, scale).astype(BF16)
    i = pl.program_id(0)
    is_ctx = i < ctx_tiles
    row = lax.broadcasted_iota(jnp.int32, (tm, 1), 0)
    lat_row0 = (jnp.maximum(i - ctx_tiles, 0) % (lat_seq // tm)) * tm
    pos = jnp.where(is_ctx, row & (ctx_seq - 1), row + lat_row0)
    seq = jnp.where(is_ctx, ctx_seq, lat_seq)
    has_prev = pos != 0
    has_next = pos != seq - 1
    for c in range(d // tn):
        sl = slice(c * tn, (c + 1) * tn)
        h = h_ref[...]
        cg = _dot(h, win_ref[:, d + c * tn:d + (c + 1) * tn])
        u = _dot(h, win_ref[:, 2 * d + c * tn:2 * d + (c + 1) * tn])
        z_ref[...] = cg * u
        bg = _dot(h_ref[hl:hl + tm, :], win_ref[:, sl])
        z_prev = jnp.where(has_prev, z_ref[hl - 1:hl - 1 + tm, :], 0.0)
        z_next = jnp.where(has_next, z_ref[hl + 1:hl + 1 + tm, :], 0.0)
        conv = z_prev * cw_ref[0:1, sl] + z_ref[hl:hl + tm, :] * cw_ref[1:2, sl] + z_next * cw_ref[2:3, sl]
        act_ref[:, sl] = (bg * conv).astype(BF16)
    o_ref[...] = x_ref[...] + mod_ref[0, 2:3, :] * _dot(act_ref[...], wout_ref[...])


def _conv_call(x, mod, g, w_in, conv_w, w_out, rows, ctx_seq, lat_seq):
    tm, d = rows.tm, x.shape[1]
    hl = CONV_HALO
    assert tm % ctx_seq == 0 and ctx_seq & (ctx_seq - 1) == 0 and lat_seq % tm == 0 and tm % hl == 0
    last_halo = x.shape[0] // hl - 1
    return pl.pallas_call(
        functools.partial(_conv_kernel, tn=2 * LANES, ctx_tiles=rows.ctx_tiles, ctx_seq=ctx_seq,
                          lat_seq=lat_seq),
        grid=(rows.n_tiles,),
        in_specs=[
            pl.BlockSpec((tm, d), lambda i: (i, 0)),
            pl.BlockSpec((hl, d), lambda i: (jnp.maximum(i * (tm // hl) - 1, 0), 0)),
            pl.BlockSpec((hl, d), lambda i: (jnp.minimum((i + 1) * (tm // hl), last_halo), 0)),
            pl.BlockSpec((1, 6, d), lambda i: (rows.group(i), 0, 0)),
            _resident(g.shape, lambda i: (0, 0)),
            _resident(w_in.shape, lambda i: (0, 0)),
            _resident(conv_w.shape, lambda i: (0, 0)),
            _resident(w_out.shape, lambda i: (0, 0)),
        ],
        out_specs=pl.BlockSpec((tm, d), lambda i: (i, 0)),
        out_shape=jax.ShapeDtypeStruct(x.shape, F32),
        scratch_shapes=[pltpu.VMEM((tm + 2 * hl, d), BF16), pltpu.VMEM((tm + 2 * hl, 2 * LANES), F32),
                        pltpu.VMEM((tm, d), BF16)],
        compiler_params=_cparams(("arbitrary",)),
        name="conv_mixer",
    )(x, x, x, mod, g, w_in, conv_w, w_out)


def _ret_proj_kernel(x_ref, mod_ref, g_ref, w_ref, qkv_ref, gate_ref, *, tn):
    h = _modulate(x_ref[...], g_ref[...], mod_ref[0, 0:1, :], mod_ref[0, 1:2, :]).astype(BF16)
    n_qkv = qkv_ref.shape[1]
    for c in range(n_qkv // tn):
        qkv_ref[:, c * tn:(c + 1) * tn] = _dot(h, w_ref[:, c * tn:(c + 1) * tn]).astype(BF16)
    for c in range(gate_ref.shape[1] // tn):
        gate = _dot(h, w_ref[:, n_qkv + c * tn:n_qkv + (c + 1) * tn])
        gate_ref[:, c * tn:(c + 1) * tn] = _silu(gate).astype(gate_ref.dtype)


def _ret_proj_call(x, mod, g, w, rows, n_gate):
    tm, d = rows.tm, x.shape[1]
    n_qkv = w.shape[1] - n_gate
    m = x.shape[0]
    return pl.pallas_call(
        functools.partial(_ret_proj_kernel, tn=4 * LANES),
        grid=(rows.n_tiles,),
        in_specs=[
            pl.BlockSpec((tm, d), lambda i: (i, 0)),
            pl.BlockSpec((1, 6, d), lambda i: (rows.group(i), 0, 0)),
            _resident(g.shape, lambda i: (0, 0)),
            _resident(w.shape, lambda i: (0, 0)),
        ],
        out_specs=[pl.BlockSpec((tm, n_qkv), lambda i: (i, 0)),
                   pl.BlockSpec((tm, n_gate), lambda i: (i, 0))],
        out_shape=[jax.ShapeDtypeStruct((m, n_qkv), BF16),
                   jax.ShapeDtypeStruct((m, n_gate), BF16)],
        compiler_params=_cparams(("arbitrary",)),
        name="ret_proj",
    )(x, mod, g, w)


def _ret_scan_kernel(*refs, chunk, hps, dk, dv, has_init, emit_state):
    it = iter(refs)
    lr_ref, q_ref, k_ref, v_ref, gate_ref, gn_ref = (next(it) for _ in range(6))
    s0_ref = next(it) if has_init else None
    y_ref = next(it)
    sout_ref = next(it) if emit_state else None
    o_ref, st_ref = next(it), next(it)

    seq = q_ref.shape[0]
    n_chunks = seq // chunk
    k_scale = dk ** -0.5
    row = lax.broadcasted_iota(jnp.int32, (chunk, chunk), 0).astype(F32)
    col = lax.broadcasted_iota(jnp.int32, (chunk, chunk), 1).astype(F32)
    ridx = lax.broadcasted_iota(jnp.int32, (chunk, 1), 0).astype(F32)
    dist = row - col

    for hd in range(hps):
        qs = slice(hd * dk, (hd + 1) * dk)
        vs = slice(hd * dv, (hd + 1) * dv)
        log_gamma = -jnp.exp(lr_ref[hd])
        lg_f, lg_b = log_gamma[0:1, :], log_gamma[1:2, :]
        mask = jnp.where(dist > 0, jnp.exp(jnp.maximum(dist, 0.0) * lg_f),
                         jnp.where(dist < 0, jnp.exp(jnp.maximum(-dist, 0.0) * lg_b), 2.0)) * k_scale
        q_decay = (jnp.exp((ridx + 1.0) * lg_f), jnp.exp((chunk - ridx) * lg_b))
        k_decay = (jnp.exp((chunk - 1.0 - ridx) * lg_f) * k_scale, jnp.exp(ridx * lg_b) * k_scale)
        chunk_decay = (jnp.exp(chunk * lg_f), jnp.exp(chunk * lg_b))

        for direction in range(2):
            order = range(n_chunks) if direction == 0 else range(n_chunks - 1, -1, -1)
            if has_init:
                st_ref[hd] = s0_ref[0, direction, hd]
            for step, c in enumerate(order):
                rs = slice(c * chunk, (c + 1) * chunk)
                qc, kc, vc = q_ref[rs, qs], k_ref[rs, qs], v_ref[rs, vs]
                have_state = has_init or step > 0
                if direction == 0:
                    scores = _dot_nt(qc, kc) * mask
                    o_ref[rs, vs] = _dot(scores.astype(BF16), vc)
                if have_state:
                    o_ref[rs, vs] += _dot(qc, st_ref[hd].astype(BF16)) * q_decay[direction]
                kd = (kc.astype(F32) * k_decay[direction]).T.astype(BF16)
                update = _dot(kd, vc)
                if have_state:
                    st_ref[hd] = st_ref[hd] * chunk_decay[direction] + update
                else:
                    st_ref[hd] = update
            if emit_state:
                sout_ref[0, 0, direction, hd] = st_ref[hd]

        o = o_ref[:, vs]
        mu = jnp.mean(o, axis=-1, keepdims=True)
        var = jnp.mean(jnp.square(o - mu), axis=-1, keepdims=True)
        on = (o - mu) * lax.rsqrt(var + EPS) * gn_ref[:, vs]
        y_ref[:, vs] = (gate_ref[:, vs].astype(F32) * on).astype(y_ref.dtype)


def _ret_scan_call(log_rate, qkv, gate, gn_g, s0, *, bsz, seq, row0, hps, emit_state, name):
    n_gate = gate.shape[1]
    nh = RET_HEADS
    dv = n_gate // nh
    dk = (qkv.shape[1] - n_gate) // (2 * nh)
    chunk = min(seq, 2 * LANES)
    has_init = s0 is not None
    assert row0 % seq == 0 and nh % hps == 0
    b0 = row0 // seq
    hg = nh // hps
    in_specs = [
        pl.BlockSpec((hps, 2, 1), lambda b, h: (h, 0, 0)),
        pl.BlockSpec((seq, hps * dk), lambda b, h: (b0 + b, h)),
        pl.BlockSpec((seq, hps * dk), lambda b, h: (b0 + b, hg + h)),
        pl.BlockSpec((seq, hps * dv), lambda b, h: (b0 + b, (2 * nh * dk) // (hps * dv) + h)),
        pl.BlockSpec((seq, hps * dv), lambda b, h: (b0 + b, h)),
        pl.BlockSpec((1, hps * dv), lambda b, h: (0, h)),
    ]
    args = [log_rate, qkv, qkv, qkv, gate, gn_g]
    if has_init:
        in_specs.append(pl.BlockSpec((1, 2, hps, dk, dv), lambda b, h: (b, 0, h, 0, 0)))
        args.append(s0)
    out_specs = [pl.BlockSpec((seq, hps * dv), lambda b, h: (b, h))]
    out_shape = [jax.ShapeDtypeStruct((bsz * seq, n_gate), BF16)]
    if emit_state:
        out_specs.append(pl.BlockSpec((1, 1, 2, hps, dk, dv), lambda b, h: (b, 0, 0, h, 0, 0)))
        out_shape.append(jax.ShapeDtypeStruct((bsz, 1, 2, nh, dk, dv), F32))
    return pl.pallas_call(
        functools.partial(_ret_scan_kernel, chunk=chunk, hps=hps, dk=dk, dv=dv,
                          has_init=has_init, emit_state=emit_state),
        grid=(bsz, hg),
        in_specs=in_specs,
        out_specs=out_specs,
        out_shape=out_shape,
        scratch_shapes=[pltpu.VMEM((seq, hps * dv), F32), pltpu.VMEM((hps, dk, dv), F32)],
        compiler_params=_cparams(("arbitrary", "arbitrary")),
        name=name,
    )(*args)


def _rope_swap_index():
    f = ROPE_AXIS_FREQS
    idx = jnp.arange(MLA_ROPE)
    return jnp.where((idx // f) % 2 == 0, idx + f, idx - f)


def _mla_weights(w_a, q_norm_g, kv_norm_g, w_q_b, w_kv_b):
    swap = _rope_swap_index()
    nq = MLA_Q_RANK + MLA_KV_RANK
    w_kpe = w_a[:, nq:]
    w_kpe_sw = w_kpe[:, swap]
    wa = jnp.concatenate([w_a[:, :nq], w_kpe, w_kpe, w_kpe_sw, w_kpe_sw], axis=1).astype(BF16)
    wq = w_q_b.reshape(MLA_Q_RANK, MLA_HEADS, MLA_NOPE + MLA_ROPE)
    wq_nope = wq[:, :, :MLA_NOPE].reshape(MLA_Q_RANK, MLA_HEADS * MLA_NOPE)
    wq_pe = wq[:, :, MLA_NOPE:]
    wq_all = jnp.concatenate(
        [wq_nope, wq_pe.reshape(MLA_Q_RANK, -1), wq_pe[:, :, swap].reshape(MLA_Q_RANK, -1)],
        axis=1).astype(BF16)
    wkv = w_kv_b.reshape(MLA_KV_RANK, MLA_HEADS, MLA_NOPE + MLA_V)
    wkv_all = jnp.concatenate(
        [wkv[:, :, :MLA_NOPE].reshape(MLA_KV_RANK, -1), wkv[:, :, MLA_NOPE:].reshape(MLA_KV_RANK, -1)],
        axis=1).astype(BF16)
    return wa, q_norm_g[None, :], kv_norm_g[None, :], wq_all, wkv_all


def _rope_tables(n_tokens):
    f = ROPE_AXIS_FREQS
    f32 = np.float32
    rows = n_tokens // GRID_W
    r = np.repeat(np.arange(rows, dtype=f32), GRID_W)
    col = np.tile(np.arange(GRID_W, dtype=f32), rows)
    inv = (f32(ROPE_THETA) ** (-np.arange(f, dtype=f32) / f32(f))).astype(f32)
    ang_r, ang_c = r[:, None] * inv, col[:, None] * inv
    cos = np.concatenate([np.cos(ang_r)] * 2 + [np.cos(ang_c)] * 2, axis=1)
    sin = np.concatenate([-np.sin(ang_r), np.sin(ang_r), -np.sin(ang_c), np.sin(ang_c)], axis=1)
    reps = LANES // MLA_ROPE
    return (jnp.asarray(np.concatenate([cos] * reps, axis=1), F32),
            jnp.asarray(np.concatenate([sin] * reps, axis=1), F32))


def kernel(x_prompt, x_sample, c, c_ctx, cache_mla_ckv, cache_mla_kpe, state_ret, ada_w, ada_b, norm_mix_g, norm_ffn_g, mla_w_a, mla_q_norm_g, mla_kv_norm_g, mla_w_q_b, mla_w_kv_b, mla_w_o, conv_w_in, conv_w, conv_w_out, ret_w_in, ret_log_rate, ret_gn_g, ret_w_out, ffn_w_in, ffn_w_out, final_norm_g):
    batch, seq, d = x_prompt.shape
    dec_batch, dec_seq, _ = x_sample.shape
    depth = ada_w.shape[0]
    n_ctx = batch * seq
    n_lat = dec_batch * dec_seq
    past = cache_mla_ckv.shape[2]
    assert 1 + dec_batch <= MOD_ROWS

    cvec = jnp.zeros((MOD_ROWS, d), F32).at[0].set(c_ctx).at[1:1 + dec_batch].set(c)
    mod_all = _ada_call(cvec, ada_w, ada_b).reshape(depth, MOD_ROWS, 6, d)

    rows_s = _Rows(n_ctx, dec_batch, dec_seq, min(512, dec_seq))
    rows_l = _Rows(n_ctx, dec_batch, dec_seq, min(1024, dec_seq))
    rope_tabs = _rope_tables(dec_seq)
    final_g = final_norm_g[None, :]
    mla_w_o_bf = mla_w_o.astype(BF16)

    def mixer_cast_jobs(layer):
        if layer >= depth:
            return []
        kind, j = layer % N_MIXERS, layer // N_MIXERS
        if kind == 1:
            return [(conv_w_in, j), (conv_w_out, j)]
        if kind == 2:
            return [(ret_w_in, j), (ret_w_out, j)]
        return []

    x = (x_prompt.reshape(n_ctx, d), x_sample.reshape(n_lat, d))
    new_ckv, new_kpe, new_ret = [], [], []
    ffn_bf = mixer_bf = None
    for i in range(depth):
        kind, j = i % N_MIXERS, i // N_MIXERS
        mod = mod_all[i]
        g_mix = norm_mix_g[i][None, :]
        if ffn_bf is None:
            assert kind == 0
        if kind == 0:
            wts = _mla_weights(mla_w_a[j], mla_q_norm_g[j], mla_kv_norm_g[j], mla_w_q_b[j], mla_w_kv_b[j])
            split = isinstance(x, tuple)
            o_c, ckv_c, kpe_c = _mla_ctx_call(x[0] if split else x, mod, g_mix, wts, rows_l, seq)
            ql, kl, vl = _mla_proj_call(
                x[1] if split else x, 0 if split else rows_l.ctx_tiles, mod, g_mix, wts, rows_l,
                tile0=rows_l.ctx_tiles, n_tiles=rows_l.n_tiles - rows_l.ctx_tiles, rope_tabs=rope_tabs)
            new_ckv.append(ckv_c.reshape(batch, seq, MLA_KV_RANK))
            new_kpe.append(kpe_c.reshape(batch, seq, MLA_ROPE))
            kpe_rep = jnp.concatenate([cache_mla_kpe[:, j]] * (LANES // MLA_ROPE), axis=-1)
            kp, vp = _cache_expand_call(
                cache_mla_ckv[:, j].reshape(dec_batch * past, MLA_KV_RANK),
                kpe_rep.reshape(dec_batch * past, LANES), wts[4])
            jobs =[] if ffn_bf is not None else [(ffn_w_in, i), (ffn_w_out, i)]
            o_l, cast = _attn_call(ql, [kp, kl], [vp, vl], n_batch=dec_batch, seq=dec_seq,
                                   sks=[past, dec_seq], nb=1, tq=min(512, dec_seq), name="attn_lat",
                                   cast_jobs=jobs)
            if jobs:
                ffn_bf = cast
            mix = (o_c, o_l, mla_w_o_bf, j)
        elif kind == 1:
            mix = None
            x = _conv_call(x, mod, g_mix, mixer_bf[0][0], conv_w[j], mixer_bf[1][0], rows_l, seq, dec_seq)
        else:
            n_gate = ret_w_out.shape[1]
            qkv, gate = _ret_proj_call(x, mod, g_mix, mixer_bf[0][0], rows_l, n_gate)
            lr = ret_log_rate[j].T[:, :, None]
            gn = ret_gn_g[j][None, :]
            y_c, st = _ret_scan_call(lr, qkv, gate, gn, None, bsz=batch, seq=seq, row0=0,
                                     hps=RET_HEADS, emit_state=True, name="ret_scan_ctx")
            (y_l,) = _ret_scan_call(lr, qkv, gate, gn, state_ret[:, j], bsz=dec_batch, seq=dec_seq,
                                    row0=n_ctx, hps=1, emit_state=False, name="ret_scan_lat")
            new_ret.append(st)
            mix = (y_c, y_l, mixer_bf[1], 0)
        last = i == depth - 1
        jobs = [] if last else [(ffn_w_in, i + 1), (ffn_w_out, i + 1)] + mixer_cast_jobs(i + 1)
        x, cast = _ffn_call(x, mod, norm_ffn_g[i][None, :], ffn_bf[0], ffn_bf[1], 0, final_g,
                            rows_l if mix is None else rows_s, final=last, cast_jobs=jobs, mix=mix)
        ffn_bf, mixer_bf = cast[:2], cast[2:]

    y_prompt = x[0].reshape(batch, seq, d)
    y_sample = x[1].reshape(dec_batch, dec_seq, d)
    return (y_prompt, y_sample, jnp.stack(new_ckv, axis=1), jnp.stack(new_kpe, axis=1),
            jnp.concatenate(new_ret, axis=1))
```

```python
import functools
import math

import jax
import jax.numpy as jnp
import numpy as np
from jax import lax
from jax.experimental import pallas as pl
from jax.experimental.pallas import tpu as pltpu

F32 = jnp.float32
BF16 = jnp.bfloat16

N_MIXERS = 3
MLA_HEADS = 8
MLA_NOPE = 128
MLA_ROPE = 64
MLA_V = 128
MLA_Q_RANK = 384
MLA_KV_RANK = 256
MLA_SCALE = (MLA_NOPE + MLA_ROPE) ** -0.5
MLA_Q_SCALE = MLA_SCALE * 1.4426950408889634
ROPE_THETA = 10000.0
ROPE_AXIS_FREQS = MLA_ROPE // 4
GRID_W = 64
RET_HEADS = 4
EPS = 1e-6

LANES = 128
MLA_HEAD_PAD = 2 * LANES
MOD_ROWS = 8
VMEM_LIMIT = 56 * 1024 * 1024


def _cparams(sem):
    return pltpu.CompilerParams(dimension_semantics=sem, vmem_limit_bytes=VMEM_LIMIT)


def _resident(shape, index_map):
    return pl.BlockSpec(shape, index_map, pipeline_mode=pl.Buffered(1))


def _rms(x, g):
    return x * lax.rsqrt(jnp.mean(x * x, axis=-1, keepdims=True) + EPS) * g


def _modulate(x, g, shift, scale):
    return _rms(x, g) * (1.0 + scale) + shift


def _silu(x):
    return x * jax.nn.sigmoid(x)


def _dot(a, b):
    return jnp.dot(a, b, preferred_element_type=F32)


def _dot_nt(a, b):
    return lax.dot_general(a, b, (((1,), (1,)), ((), ())), preferred_element_type=F32)


MAX_CAST_SLABS = 16


def _cast_specs(jobs, n_steps, step_of):
    n_slabs = math.gcd(n_steps, MAX_CAST_SLABS)
    per = n_steps // n_slabs

    def slab_of_step(*g):
        return step_of(*g) // per

    in_specs, out_specs, out_shape = [], [], []
    for w, layer in jobs:
        _, k, n = w.shape
        blk = (1, k // n_slabs, n)
        in_specs.append(pl.BlockSpec(blk, lambda *g, layer=layer: (layer, slab_of_step(*g), 0)))
        out_specs.append(pl.BlockSpec(blk, lambda *g: (0, slab_of_step(*g), 0)))
        out_shape.append(jax.ShapeDtypeStruct((1, k, n), BF16))
    return in_specs, out_specs, out_shape


def _run_casts(in_refs, out_refs):
    for src, dst in zip(in_refs, out_refs):
        dst[...] = src[...].astype(BF16)


def _ada_kernel(c_ref, w_ref, b_ref, o_ref):
    a = _silu(c_ref[...]).astype(BF16)
    o_ref[0] = _dot(a, w_ref[0].astype(BF16)) + b_ref[0]


def _ada_call(cvec, ada_w, ada_b):
    depth, d, n = ada_w.shape
    tn = d
    return pl.pallas_call(
        _ada_kernel,
        grid=(depth, n // tn),
        in_specs=[
            pl.BlockSpec((MOD_ROWS, d), lambda l, j: (0, 0)),
            pl.BlockSpec((1, d, tn), lambda l, j: (l, 0, j)),
            pl.BlockSpec((1, 1, tn), lambda l, j: (l, 0, j)),
        ],
        out_specs=pl.BlockSpec((1, MOD_ROWS, tn), lambda l, j: (l, 0, j)),
        out_shape=jax.ShapeDtypeStruct((depth, MOD_ROWS, n), F32),
        compiler_params=_cparams(("arbitrary", "arbitrary")),
        name="ada",
    )(cvec, ada_w, ada_b.reshape(depth, 1, n))


class _Rows:
    def __init__(self, n_ctx, dec_batch, dec_seq, tm):
        assert n_ctx % tm == 0 and dec_seq % tm == 0
        self.tm = tm
        self.ctx_tiles = n_ctx // tm
        self.seq_tiles = dec_seq // tm
        self.n_tiles = self.ctx_tiles + dec_batch * self.seq_tiles

    def group(self, i):
        lat = jnp.maximum(i - self.ctx_tiles, 0) // self.seq_tiles
        return jnp.where(i < self.ctx_tiles, 0, 1 + lat)


def _mla_project(x_ref, mod_ref, g_ref, wa_ref, qg_ref, kvg_ref, wq_ref, wkv_ref, rope_refs,
                 q_ref, k_ref, v_ref, cache_refs):
    h = _modulate(x_ref[...], g_ref[...], mod_ref[0, 0:1, :], mod_ref[0, 1:2, :]).astype(BF16)
    a = _dot(h, wa_ref[...])
    nq = MLA_Q_RANK + MLA_KV_RANK
    ckv_n = _rms(a[:, MLA_Q_RANK:nq], kvg_ref[...])
    kpe = a[:, nq:nq + LANES]
    if cache_refs is not None:
        cache_refs[0][...] = ckv_n
        cache_refs[1][...] = kpe[:, :MLA_ROPE]
    if rope_refs is not None:
        cos, sin = rope_refs[0][...], rope_refs[1][...]
        kpe = kpe * cos + a[:, nq + LANES:nq + 2 * LANES] * sin
    kpe = kpe.astype(BF16)

    qn = _rms(a[:, :MLA_Q_RANK], qg_ref[...]).astype(BF16)
    nh = MLA_HEADS
    n_nope, n_pe = nh * MLA_NOPE, nh * MLA_ROPE
    q_nope = _dot(qn, wq_ref[:, :n_nope])
    q_pe = _dot(qn, wq_ref[:, n_nope:n_nope + n_pe])
    if rope_refs is not None:
        q_sw = _dot(qn, wq_ref[:, n_nope + n_pe:n_nope + 2 * n_pe])
        cos, sin = cos * MLA_Q_SCALE, sin * MLA_Q_SCALE
    kv = _dot(ckv_n.astype(BF16), wkv_ref[...])
    low_half = lax.broadcasted_iota(jnp.int32, (1, LANES), 1) < MLA_ROPE
    for pair in range(nh // 2):
        ps = slice(pair * LANES, (pair + 1) * LANES)
        if rope_refs is not None:
            pe = q_pe[:, ps] * cos + q_sw[:, ps] * sin
        else:
            pe = q_pe[:, ps] * MLA_Q_SCALE
        for hd, keep in ((2 * pair, low_half), (2 * pair + 1, jnp.logical_not(low_half))):
            lo = hd * MLA_HEAD_PAD
            hs = slice(hd * LANES, (hd + 1) * LANES)
            q_ref[:, lo:lo + LANES] = (q_nope[:, hs] * MLA_Q_SCALE).astype(BF16)
            q_ref[:, lo + LANES:lo + 2 * LANES] = jnp.where(keep, pe, 0.0).astype(BF16)
            k_ref[:, lo:lo + LANES] = kv[:, hs].astype(BF16)
            k_ref[:, lo + LANES:lo + 2 * LANES] = kpe
    v_ref[...] = kv[:, nh * LANES:].astype(BF16)


def _mla_proj_kernel(x_ref, mod_ref, g_ref, wa_ref, qg_ref, kvg_ref, wq_ref, wkv_ref, ck_ref, sk_ref,
                     q_ref, k_ref, v_ref):
    _mla_project(x_ref, mod_ref, g_ref, wa_ref, qg_ref, kvg_ref, wq_ref, wkv_ref, (ck_ref, sk_ref),
                 q_ref, k_ref, v_ref, None)


def _mla_proj_call(x, x_tile0, mod, g, wts, rows, *, tile0, n_tiles, rope_tabs):
    tm, d = rows.tm, x.shape[1]
    wa, qg, kvg, wq, wkv = wts
    full = lambda arr: _resident(arr.shape, lambda i: (0,) * arr.ndim)
    in_specs = [
        pl.BlockSpec((tm, d), lambda i: (i + x_tile0, 0)),
        pl.BlockSpec((1, 6, d), lambda i: (rows.group(i + tile0), 0, 0)),
        full(g), full(wa), full(qg), full(kvg), full(wq), full(wkv),
    ]
    in_specs += [pl.BlockSpec((tm, LANES), lambda i: (i % rows.seq_tiles, 0)) for _ in rope_tabs]
    m = n_tiles * tm
    hp, hv = MLA_HEADS * MLA_HEAD_PAD, MLA_HEADS * MLA_V
    return pl.pallas_call(
        _mla_proj_kernel,
        grid=(n_tiles,),
        in_specs=in_specs,
        out_specs=[pl.BlockSpec((tm, hp), lambda i: (i, 0)), pl.BlockSpec((tm, hp), lambda i: (i, 0)),
                   pl.BlockSpec((tm, hv), lambda i: (i, 0))],
        out_shape=[jax.ShapeDtypeStruct((m, hp), BF16), jax.ShapeDtypeStruct((m, hp), BF16),
                   jax.ShapeDtypeStruct((m, hv), BF16)],
        compiler_params=_cparams(("arbitrary",)),
        name="mla_proj_lat",
    )(x, mod, g, wa, qg, kvg, wq, wkv, *rope_tabs)


def _mla_ctx_kernel(x_ref, mod_ref, g_ref, wa_ref, qg_ref, kvg_ref, wq_ref, wkv_ref,
                    o_ref, ckv_ref, kpe_ref, q_s, k_s, v_s, s_ref, *, seq):
    _mla_project(x_ref, mod_ref, g_ref, wa_ref, qg_ref, kvg_ref, wq_ref, wkv_ref, None,
                 q_s, k_s, v_s, (ckv_ref, kpe_ref))
    _attend(q_s, [k_s], [v_s], o_ref, s_ref, nb=x_ref.shape[0] // seq, tq=seq, sks=(seq,))


def _mla_ctx_call(x, mod, g, wts, rows, seq):
    tm, d = rows.tm, x.shape[1]
    assert tm % seq == 0
    wa, qg, kvg, wq, wkv = wts
    full = lambda arr: _resident(arr.shape, lambda i: (0,) * arr.ndim)
    m = rows.ctx_tiles * tm
    hp, hv = MLA_HEADS * MLA_HEAD_PAD, MLA_HEADS * MLA_V
    return pl.pallas_call(
        functools.partial(_mla_ctx_kernel, seq=seq),
        grid=(rows.ctx_tiles,),
        in_specs=[
            pl.BlockSpec((tm, d), lambda i: (i, 0)),
            pl.BlockSpec((1, 6, d), lambda i: (0, 0, 0)),
            full(g), full(wa), full(qg), full(kvg), full(wq), full(wkv),
        ],
        out_specs=[pl.BlockSpec((tm, hv), lambda i: (i, 0)),
                   pl.BlockSpec((tm, MLA_KV_RANK), lambda i: (i, 0)),
                   pl.BlockSpec((tm, MLA_ROPE), lambda i: (i, 0))],
        out_shape=[jax.ShapeDtypeStruct((m, hv), BF16),
                   jax.ShapeDtypeStruct((m, MLA_KV_RANK), F32),
                   jax.ShapeDtypeStruct((m, MLA_ROPE), F32)],
        scratch_shapes=[pltpu.VMEM((tm, hp), BF16), pltpu.VMEM((tm, hp), BF16), pltpu.VMEM((tm, hv), BF16),
                        pltpu.VMEM((2, seq, seq), F32)],
        compiler_params=_cparams(("arbitrary",)),
        name="mla_ctx",
    )(x, mod, g, wa, qg, kvg, wq, wkv)


def _cache_expand_kernel(ckv_ref, kpe_ref, wkv_ref, k_ref, v_ref):
    kv = _dot(ckv_ref[...].astype(BF16), wkv_ref[...])
    kpe = kpe_ref[...].astype(BF16)
    nh = MLA_HEADS
    for hd in range(nh):
        lo = hd * MLA_HEAD_PAD
        k_ref[:, lo:lo + LANES] = kv[:, hd * LANES:(hd + 1) * LANES].astype(BF16)
        k_ref[:, lo + LANES:lo + 2 * LANES] = kpe
    v_ref[...] = kv[:, nh * LANES:].astype(BF16)


def _cache_expand_call(ckv, kpe_pad, wkv):
    m = ckv.shape[0]
    tm = min(m, 512)
    hp = MLA_HEADS * MLA_HEAD_PAD
    return pl.pallas_call(
        _cache_expand_kernel,
        grid=(m // tm,),
        in_specs=[
            pl.BlockSpec((tm, MLA_KV_RANK), lambda i: (i, 0)),
            pl.BlockSpec((tm, LANES), lambda i: (i, 0)),
            _resident(wkv.shape, lambda i: (0, 0)),
        ],
        out_specs=[pl.BlockSpec((tm, hp), lambda i: (i, 0)),
                   pl.BlockSpec((tm, MLA_HEADS * MLA_V), lambda i: (i, 0))],
        out_shape=[jax.ShapeDtypeStruct((m, hp), BF16),
                   jax.ShapeDtypeStruct((m, MLA_HEADS * MLA_V), BF16)],
        compiler_params=_cparams(("arbitrary",)),
        name="mla_cache_expand",
    )(ckv, kpe_pad, wkv)


def _attend(q_ref, k_refs, v_refs, o_ref, s_ref, *, nb, tq, sks):
    problems = [(b, h) for b in range(nb) for h in range(MLA_HEADS)]
    seg_cols = [sum(sks[:i]) for i in range(len(sks))]

    def scores(idx):
        b, h = problems[idx]
        q = q_ref[b * tq:(b + 1) * tq, h * MLA_HEAD_PAD:(h + 1) * MLA_HEAD_PAD]
        m_lane = None
        for k_ref, sk, c0 in zip(k_refs, sks, seg_cols):
            s = _dot_nt(q, k_ref[b * sk:(b + 1) * sk, h * MLA_HEAD_PAD:(h + 1) * MLA_HEAD_PAD])
            s_ref[idx % 2, :, c0:c0 + sk] = s
            for c in range(sk // LANES):
                piece = s[:, c * LANES:(c + 1) * LANES]
                m_lane = piece if m_lane is None else jnp.maximum(m_lane, piece)
        return m_lane.max(axis=-1, keepdims=True)

    def weighted_values(idx, m):
        b, h = problems[idx]
        acc = None
        for v_ref, sk, c0 in zip(v_refs, sks, seg_cols):
            p = jnp.exp2(s_ref[idx % 2, :, c0:c0 + sk] - m).astype(BF16)
            v = v_ref[b * sk:(b + 1) * sk, h * MLA_V:(h + 1) * MLA_V]
            part = _dot(p, jnp.concatenate([v, jnp.ones_like(v)], axis=1))
            acc = part if acc is None else acc + part
        o = acc[:, :MLA_V] / acc[:, MLA_V:]
        o_ref[b * tq:(b + 1) * tq, h * MLA_V:(h + 1) * MLA_V] = o.astype(o_ref.dtype)

    m = scores(0)
    for idx in range(len(problems)):
        m_next = scores(idx + 1) if idx + 1 < len(problems) else None
        weighted_values(idx, m)
        m = m_next


def _attn_kernel(*refs, n_seg, nb, tq, sks, n_cast):
    n_in = 1 + 2 * n_seg + n_cast
    _run_casts(refs[1 + 2 * n_seg:n_in], refs[n_in + 1:n_in + 1 + n_cast])
    _attend(refs[0], refs[1:1 + n_seg], refs[1 + n_seg:1 + 2 * n_seg], refs[n_in],
            refs[n_in + 1 + n_cast], nb=nb, tq=tq, sks=sks)


def _attn_call(q, ks, vs, *, n_batch, seq, sks, nb, tq, name, cast_jobs=()):
    n_seg = len(ks)
    tiles = seq // tq
    assert nb == 1 or tiles == 1
    hp, hv = MLA_HEADS * MLA_HEAD_PAD, MLA_HEADS * MLA_V
    in_specs = [pl.BlockSpec((nb * tq, hp), lambda b, t: (b * tiles + t, 0))]
    in_specs += [pl.BlockSpec((nb * sk, hp), lambda b, t: (b, 0)) for sk in sks]
    in_specs += [pl.BlockSpec((nb * sk, hv), lambda b, t: (b, 0)) for sk in sks]
    c_in, c_out, c_shape = _cast_specs(cast_jobs, (n_batch // nb) * tiles, lambda b, t: b * tiles + t)
    outs = pl.pallas_call(
        functools.partial(_attn_kernel, n_seg=n_seg, nb=nb, tq=tq, sks=tuple(sks), n_cast=len(cast_jobs)),
        grid=(n_batch // nb, tiles),
        in_specs=in_specs + c_in,
        out_specs=[pl.BlockSpec((nb * tq, hv), lambda b, t: (b * tiles + t, 0))] + c_out,
        out_shape=[jax.ShapeDtypeStruct((n_batch * seq, hv), BF16)] + c_shape,
        scratch_shapes=[pltpu.VMEM((2, tq, sum(sks)), F32)],
        compiler_params=_cparams(("arbitrary", "arbitrary")),
        name=name,
    )(q, *ks, *vs, *[w for w, _ in cast_jobs])
    return outs[0], list(outs[1:])


def _split_specs(rows, width):
    ct = rows.ctx_tiles
    return [pl.BlockSpec((rows.tm, width), lambda i: (jnp.minimum(i, ct - 1), 0)),
            pl.BlockSpec((rows.tm, width), lambda i: (jnp.maximum(i - ct, 0), 0))]


def _ffn_kernel(*refs, th, final, ctx_tiles, n_cast, n_x, mixed):
    it = iter(refs)
    x_refs = [next(it) for _ in range(n_x)]
    if mixed:
        yc_ref, yl_ref, wo_ref = next(it), next(it), next(it)
    mod_ref, g_ref, win_ref, wout_ref, fg_ref = (next(it) for _ in range(5))
    cast_in = [next(it) for _ in range(n_cast)]
    out_refs = [next(it) for _ in range(2 if final else 1)]
    cast_out = [next(it) for _ in range(n_cast)]
    scratch = list(it)
    act_ref = scratch[0]
    _run_casts(cast_in, cast_out)
    is_ctx = pl.program_id(0) < ctx_tiles
    if mixed:
        x1_ref = scratch[1]

        def mix(y_ref, x_ref):
            x1_ref[...] = x_ref[...] + mod_ref[0, 2:3, :] * _dot(y_ref[...], wo_ref[0])

        pl.when(is_ctx)(lambda: mix(yc_ref, x_refs[0]))
        pl.when(jnp.logical_not(is_ctx))(lambda: mix(yl_ref, x_refs[-1]))
        x = x1_ref[...]
    else:
        x = x_refs[0][...]
    h = _modulate(x, g_ref[...], mod_ref[0, 3:4, :], mod_ref[0, 4:5, :]).astype(BF16)
    hidden = wout_ref.shape[1]
    for c in range(hidden // th):
        a = _dot(h, win_ref[0, :, c * th:(c + 1) * th])
        b = _dot(h, win_ref[0, :, hidden + c * th:hidden + (c + 1) * th])
        act_ref[:, c * th:(c + 1) * th] = (_silu(a) * b).astype(BF16)
    y = x + mod_ref[0, 5:6, :] * _dot(act_ref[...], wout_ref[0])
    if not final:
        out_refs[0][...] = y
    else:
        acc_ref = scratch[-1]
        acc_ref[...] = _rms(y, fg_ref[...])

        @pl.when(is_ctx)
        def _():
            out_refs[0][...] = acc_ref[...]

        @pl.when(jnp.logical_not(is_ctx))
        def _():
            out_refs[1][...] = acc_ref[...]


def _ffn_call(x, mod, g, w_in_all, w_out_all, layer, final_g, rows, final, cast_jobs=(), mix=None):
    tm, d = rows.tm, w_in_all.shape[1]
    hidden = w_out_all.shape[1]
    split_x = isinstance(x, tuple)
    assert not split_x or mix is not None
    xs = list(x) if split_x else [x]
    x_specs = _split_specs(rows, d) if split_x else [pl.BlockSpec((tm, d), lambda i: (i, 0))]
    mix_args, mix_specs = [], []
    scratch = [pltpu.VMEM((tm, hidden), BF16)]
    if mix is not None:
        y_ctx, y_lat, w_o_all, w_o_layer = mix
        k = w_o_all.shape[1]
        mix_args = [y_ctx, y_lat, w_o_all]
        mix_specs = _split_specs(rows, k) + [_resident((1, k, d), lambda i: (w_o_layer, 0, 0))]
        scratch.append(pltpu.VMEM((tm, d), F32))
    if final:
        out_specs = _split_specs(rows, d)
        out_shape = [jax.ShapeDtypeStruct((rows.ctx_tiles * tm, d), F32),
                     jax.ShapeDtypeStruct(((rows.n_tiles - rows.ctx_tiles) * tm, d), F32)]
        scratch.append(pltpu.VMEM((tm, d), F32))
    else:
        out_specs = [pl.BlockSpec((tm, d), lambda i: (i, 0))]
        out_shape = [jax.ShapeDtypeStruct((rows.n_tiles * tm, d), F32)]
    c_in, c_out, c_shape = _cast_specs(cast_jobs, rows.n_tiles, lambda i: i)
    outs = pl.pallas_call(
        functools.partial(_ffn_kernel, th=2 * LANES, final=final, ctx_tiles=rows.ctx_tiles,
                          n_cast=len(cast_jobs), n_x=len(xs), mixed=mix is not None),
        grid=(rows.n_tiles,),
        in_specs=x_specs + mix_specs + [
            pl.BlockSpec((1, 6, d), lambda i: (rows.group(i), 0, 0)),
            _resident(g.shape, lambda i: (0, 0)),
            _resident((1, d, 2 * hidden), lambda i: (layer, 0, 0)),
            _resident((1, hidden, d), lambda i: (layer, 0, 0)),
            _resident(final_g.shape, lambda i: (0, 0)),
        ] + c_in,
        out_specs=out_specs + c_out,
        out_shape=out_shape + c_shape,
        scratch_shapes=scratch,
        compiler_params=_cparams(("arbitrary",)),
        name=("ffn_final" if final else "ffn") + ("_mix" if mix is not None else ""),
    )(*xs, *mix_args, mod, g, w_in_all, w_out_all, final_g, *[w for w, _ in cast_jobs])
    n_out = 2 if final else 1
    stream = tuple(outs[:2]) if final else outs[0]
    return stream, list(outs[n_out:])


CONV_HALO = 16


def _conv_kernel(x_ref, xp_ref, xn_ref, mod_ref, g_ref, win_ref, cw_ref, wout_ref, o_ref,
                 h_ref, z_ref, act_ref, *, tn, ctx_tiles, ctx_seq, lat_seq):
    tm, d = x_ref.shape
    hl = CONV_HALO
    shift, scale = mod_ref[0, 0:1, :], mod_ref[0, 1:2, :]
    h_ref[0:hl, :] = _modulate(xp_ref[...], g_ref[...], shift, scale).astype(BF16)
    h_ref[hl:hl + tm, :] = _modulate(x_ref[...], g_ref[...], shift, scale).astype(BF16)
    h_ref[hl + tm:, :] = _modulate(xn_ref[...], g_ref[...], shift, scale).astype(BF16)
    i = pl.program_id(0)
    is_ctx = i < ctx_tiles
    row = lax.broadcasted_iota(jnp.int32, (tm, 1), 0)
    lat_row0 = (jnp.maximum(i - ctx_tiles, 0) % (lat_seq // tm)) * tm
    pos = jnp.where(is_ctx, row & (ctx_seq - 1), row + lat_row0)
    seq = jnp.where(is_ctx, ctx_seq, lat_seq)
    has_prev = pos != 0
    has_next = pos != seq - 1
    for c in range(d // tn):
        sl = slice(c * tn, (c + 1) * tn)
        h = h_ref[...]
        cg = _dot(h, win_ref[:, d + c * tn:d + (c + 1) * tn])
        u = _dot(h, win_ref[:, 2 * d + c * tn:2 * d + (c + 1) * tn])
        z_ref[...] = cg * u
        bg = _dot(h_ref[hl:hl + tm, :], win_ref[:, sl])
        z_prev = jnp.where(has_prev, z_ref[hl - 1:hl - 1 + tm, :], 0.0)
        z_next = jnp.where(has_next, z_ref[hl + 1:hl + 1 + tm, :], 0.0)
        conv = z_prev * cw_ref[0:1, sl] + z_ref[hl:hl + tm, :] * cw_ref[1:2, sl] + z_next * cw_ref[2:3, sl]
        act_ref[:, sl] = (bg * conv).astype(BF16)
    o_ref[...] = x_ref[...] + mod_ref[0, 2:3, :] * _dot(act_ref[...], wout_ref[...])


def _conv_call(x, mod, g, w_in, conv_w, w_out, rows, ctx_seq, lat_seq):
    tm, d = rows.tm, x.shape[1]
    hl = CONV_HALO
    assert tm % ctx_seq == 0 and ctx_seq & (ctx_seq - 1) == 0 and lat_seq % tm == 0 and tm % hl == 0
    last_halo = x.shape[0] // hl - 1
    return pl.pallas_call(
        functools.partial(_conv_kernel, tn=2 * LANES, ctx_tiles=rows.ctx_tiles, ctx_seq=ctx_seq,
                          lat_seq=lat_seq),
        grid=(rows.n_tiles,),
        in_specs=[
            pl.BlockSpec((tm, d), lambda i: (i, 0)),
            pl.BlockSpec((hl, d), lambda i: (jnp.maximum(i * (tm // hl) - 1, 0), 0)),
            pl.BlockSpec((hl, d), lambda i: (jnp.minimum((i + 1) * (tm // hl), last_halo), 0)),
            pl.BlockSpec((1, 6, d), lambda i: (rows.group(i), 0, 0)),
            _resident(g.shape, lambda i: (0, 0)),
            _resident(w_in.shape, lambda i: (0, 0)),
            _resident(conv_w.shape, lambda i: (0, 0)),
            _resident(w_out.shape, lambda i: (0, 0)),
        ],
        out_specs=pl.BlockSpec((tm, d), lambda i: (i, 0)),
        out_shape=jax.ShapeDtypeStruct(x.shape, F32),
        scratch_shapes=[pltpu.VMEM((tm + 2 * hl, d), BF16), pltpu.VMEM((tm + 2 * hl, 2 * LANES), F32),
                        pltpu.VMEM((tm, d), BF16)],
        compiler_params=_cparams(("arbitrary",)),
        name="conv_mixer",
    )(x, x, x, mod, g, w_in, conv_w, w_out)


RET_PROJ_CHUNK = 4 * LANES


def _ret_project(x_ref, mod_ref, g_ref, w_ref, qkv_ref, gate_ref):
    tn = RET_PROJ_CHUNK
    h = _modulate(x_ref[...], g_ref[...], mod_ref[0, 0:1, :], mod_ref[0, 1:2, :]).astype(BF16)
    n_qkv = qkv_ref.shape[1]
    for c in range(n_qkv // tn):
        qkv_ref[:, c * tn:(c + 1) * tn] = _dot(h, w_ref[:, c * tn:(c + 1) * tn]).astype(BF16)
    for c in range(gate_ref.shape[1] // tn):
        gate = _dot(h, w_ref[:, n_qkv + c * tn:n_qkv + (c + 1) * tn])
        gate_ref[:, c * tn:(c + 1) * tn] = _silu(gate).astype(gate_ref.dtype)


def _ret_proj_call(x, mod, g, w, rows, n_gate, *, tile0, n_tiles):
    tm, d = rows.tm, x.shape[1]
    n_qkv = w.shape[1] - n_gate
    m = n_tiles * tm
    return pl.pallas_call(
        _ret_project,
        grid=(n_tiles,),
        in_specs=[
            pl.BlockSpec((tm, d), lambda i: (i + tile0, 0)),
            pl.BlockSpec((1, 6, d), lambda i: (rows.group(i + tile0), 0, 0)),
            _resident(g.shape, lambda i: (0, 0)),
            _resident(w.shape, lambda i: (0, 0)),
        ],
        out_specs=[pl.BlockSpec((tm, n_qkv), lambda i: (i, 0)),
                   pl.BlockSpec((tm, n_gate), lambda i: (i, 0))],
        out_shape=[jax.ShapeDtypeStruct((m, n_qkv), BF16),
                   jax.ShapeDtypeStruct((m, n_gate), BF16)],
        compiler_params=_cparams(("arbitrary",)),
        name="ret_proj",
    )(x, mod, g, w)


def _ret_decays(lr_ref, hd, chunk, k_scale):
    row = lax.broadcasted_iota(jnp.int32, (chunk, chunk), 0).astype(F32)
    col = lax.broadcasted_iota(jnp.int32, (chunk, chunk), 1).astype(F32)
    ridx = lax.broadcasted_iota(jnp.int32, (chunk, 1), 0).astype(F32)
    dist = row - col
    log_gamma = -jnp.exp(lr_ref[hd])
    lg_f, lg_b = log_gamma[0:1, :], log_gamma[1:2, :]
    mask = jnp.where(dist > 0, jnp.exp(jnp.maximum(dist, 0.0) * lg_f),
                     jnp.where(dist < 0, jnp.exp(jnp.maximum(-dist, 0.0) * lg_b), 2.0)) * k_scale
    q_decay = (jnp.exp((ridx + 1.0) * lg_f), jnp.exp((chunk - ridx) * lg_b))
    k_decay = (jnp.exp((chunk - 1.0 - ridx) * lg_f) * k_scale, jnp.exp(ridx * lg_b) * k_scale)
    chunk_decay = (jnp.exp(chunk * lg_f), jnp.exp(chunk * lg_b))
    return mask, q_decay, k_decay, chunk_decay


def _ret_scan(decays, q, k, v, gate_ref, gn_ref, s0_ref, y_ref, sout_ref, o_ref, st_ref, *,
              row0, seq, chunk, hps, dk, dv):
    n_chunks = seq // chunk
    has_init = s0_ref is not None
    (q_ref, q0), (k_ref, k0), (v_ref, v0) = q, k, v
    for hd in range(hps):
        mask, q_decay, k_decay, chunk_decay = decays[hd]
        vs = slice(hd * dv, (hd + 1) * dv)
        for direction in range(2):
            order = range(n_chunks) if direction == 0 else range(n_chunks - 1, -1, -1)
            if has_init:
                st_ref[hd] = s0_ref[direction, hd]
            for step, c in enumerate(order):
                rs = slice(row0 + c * chunk, row0 + (c + 1) * chunk)
                ls = slice(c * chunk, (c + 1) * chunk)
                qc = q_ref[rs, q0 + hd * dk:q0 + (hd + 1) * dk]
                kc = k_ref[rs, k0 + hd * dk:k0 + (hd + 1) * dk]
                vc = v_ref[rs, v0 + hd * dv:v0 + (hd + 1) * dv]
                have_state = has_init or step > 0
                if direction == 0:
                    scores = _dot_nt(qc, kc) * mask
                    o_ref[ls, vs] = _dot(scores.astype(BF16), vc)
                if have_state:
                    o_ref[ls, vs] += _dot(qc, st_ref[hd].astype(BF16)) * q_decay[direction]
                kd = (kc.astype(F32) * k_decay[direction]).T.astype(BF16)
                update = _dot(kd, vc)
                if have_state:
                    st_ref[hd] = st_ref[hd] * chunk_decay[direction] + update
                else:
                    st_ref[hd] = update
            if sout_ref is not None:
                sout_ref[direction, hd] = st_ref[hd]

        o = o_ref[:, vs]
        mu = jnp.mean(o, axis=-1, keepdims=True)
        var = jnp.mean(jnp.square(o - mu), axis=-1, keepdims=True)
        on = (o - mu) * lax.rsqrt(var + EPS) * gn_ref[:, vs]
        ys = slice(row0, row0 + seq)
        y_ref[ys, vs] = (gate_ref[ys, vs].astype(F32) * on).astype(y_ref.dtype)


def _ret_scan_kernel(*refs, chunk, hps, dk, dv, has_init, emit_state):
    it = iter(refs)
    lr_ref, q_ref, k_ref, v_ref, gate_ref, gn_ref = (next(it) for _ in range(6))
    s0_ref = next(it).at[0] if has_init else None
    y_ref = next(it)
    sout_ref = next(it).at[0, 0] if emit_state else None
    o_ref, st_ref = next(it), next(it)
    decays = [_ret_decays(lr_ref, hd, chunk, dk ** -0.5) for hd in range(hps)]
    _ret_scan(decays, (q_ref, 0), (k_ref, 0), (v_ref, 0), gate_ref, gn_ref, s0_ref, y_ref, sout_ref,
              o_ref, st_ref, row0=0, seq=q_ref.shape[0], chunk=chunk, hps=hps, dk=dk, dv=dv)


def _ret_ctx_kernel(x_ref, mod_ref, g_ref, w_ref, lr_ref, gn_ref, y_ref, sout_ref,
                    qkv_s, gate_s, o_s, st_s, *, seq, dk, dv):
    nh = RET_HEADS
    _ret_project(x_ref, mod_ref, g_ref, w_ref, qkv_s, gate_s)
    decays = [_ret_decays(lr_ref, hd, seq, dk ** -0.5) for hd in range(nh)]
    for b in range(x_ref.shape[0] // seq):
        _ret_scan(decays, (qkv_s, 0), (qkv_s, nh * dk), (qkv_s, 2 * nh * dk), gate_s, gn_ref, None,
                  y_ref, sout_ref.at[b, 0], o_s, st_s, row0=b * seq, seq=seq, chunk=seq,
                  hps=nh, dk=dk, dv=dv)


def _ret_ctx_call(x, mod, g, w, log_rate, gn_g, rows, seq, n_gate):
    tm, d = rows.tm, x.shape[1]
    assert tm % seq == 0 and seq <= 2 * LANES
    nh = RET_HEADS
    n_qkv = w.shape[1] - n_gate
    dv = n_gate // nh
    dk = (n_qkv - n_gate) // (2 * nh)
    nb = tm // seq
    m = rows.ctx_tiles * tm
    return pl.pallas_call(
        functools.partial(_ret_ctx_kernel, seq=seq, dk=dk, dv=dv),
        grid=(rows.ctx_tiles,),
        in_specs=[
            pl.BlockSpec((tm, d), lambda i: (i, 0)),
            pl.BlockSpec((1, 6, d), lambda i: (0, 0, 0)),
            _resident(g.shape, lambda i: (0, 0)),
            _resident(w.shape, lambda i: (0, 0)),
            _resident(log_rate.shape, lambda i: (0, 0, 0)),
            _resident(gn_g.shape, lambda i: (0, 0)),
        ],
        out_specs=[pl.BlockSpec((tm, n_gate), lambda i: (i, 0)),
                   pl.BlockSpec((nb, 1, 2, nh, dk, dv), lambda i: (i, 0, 0, 0, 0, 0))],
        out_shape=[jax.ShapeDtypeStruct((m, n_gate), BF16),
                   jax.ShapeDtypeStruct((m // seq, 1, 2, nh, dk, dv), F32)],
        scratch_shapes=[pltpu.VMEM((tm, n_qkv), BF16), pltpu.VMEM((tm, n_gate), BF16),
                        pltpu.VMEM((seq, n_gate), F32), pltpu.VMEM((nh, dk, dv), F32)],
        compiler_params=_cparams(("arbitrary",)),
        name="ret_ctx",
    )(x, mod, g, w, log_rate, gn_g)


def _ret_scan_call(log_rate, qkv, gate, gn_g, s0, *, bsz, seq, row0, hps, emit_state, name):
    n_gate = gate.shape[1]
    nh = RET_HEADS
    dv = n_gate // nh
    dk = (qkv.shape[1] - n_gate) // (2 * nh)
    chunk = min(seq, 2 * LANES)
    has_init = s0 is not None
    assert row0 % seq == 0 and nh % hps == 0
    b0 = row0 // seq
    hg = nh // hps
    in_specs = [
        pl.BlockSpec((hps, 2, 1), lambda b, h: (h, 0, 0)),
        pl.BlockSpec((seq, hps * dk), lambda b, h: (b0 + b, h)),
        pl.BlockSpec((seq, hps * dk), lambda b, h: (b0 + b, hg + h)),
        pl.BlockSpec((seq, hps * dv), lambda b, h: (b0 + b, (2 * nh * dk) // (hps * dv) + h)),
        pl.BlockSpec((seq, hps * dv), lambda b, h: (b0 + b, h)),
        pl.BlockSpec((1, hps * dv), lambda b, h: (0, h)),
    ]
    args = [log_rate, qkv, qkv, qkv, gate, gn_g]
    if has_init:
        in_specs.append(pl.BlockSpec((1, 2, hps, dk, dv), lambda b, h: (b, 0, h, 0, 0)))
        args.append(s0)
    out_specs = [pl.BlockSpec((seq, hps * dv), lambda b, h: (b, h))]
    out_shape = [jax.ShapeDtypeStruct((bsz * seq, n_gate), BF16)]
    if emit_state:
        out_specs.append(pl.BlockSpec((1, 1, 2, hps, dk, dv), lambda b, h: (b, 0, 0, h, 0, 0)))
        out_shape.append(jax.ShapeDtypeStruct((bsz, 1, 2, nh, dk, dv), F32))
    return pl.pallas_call(
        functools.partial(_ret_scan_kernel, chunk=chunk, hps=hps, dk=dk, dv=dv,
                          has_init=has_init, emit_state=emit_state),
        grid=(bsz, hg),
        in_specs=in_specs,
        out_specs=out_specs,
        out_shape=out_shape,
        scratch_shapes=[pltpu.VMEM((seq, hps * dv), F32), pltpu.VMEM((hps, dk, dv), F32)],
        compiler_params=_cparams(("arbitrary", "arbitrary")),
        name=name,
    )(*args)


def _rope_swap_index():
    f = ROPE_AXIS_FREQS
    idx = jnp.arange(MLA_ROPE)
    return jnp.where((idx // f) % 2 == 0, idx + f, idx - f)


def _mla_weights(w_a, q_norm_g, kv_norm_g, w_q_b, w_kv_b):
    swap = _rope_swap_index()
    nq = MLA_Q_RANK + MLA_KV_RANK
    w_kpe = w_a[:, nq:]
    w_kpe_sw = w_kpe[:, swap]
    wa = jnp.concatenate([w_a[:, :nq], w_kpe, w_kpe, w_kpe_sw, w_kpe_sw], axis=1).astype(BF16)
    wq = w_q_b.reshape(MLA_Q_RANK, MLA_HEADS, MLA_NOPE + MLA_ROPE)
    wq_nope = wq[:, :, :MLA_NOPE].reshape(MLA_Q_RANK, MLA_HEADS * MLA_NOPE)
    wq_pe = wq[:, :, MLA_NOPE:]
    wq_all = jnp.concatenate(
        [wq_nope, wq_pe.reshape(MLA_Q_RANK, -1), wq_pe[:, :, swap].reshape(MLA_Q_RANK, -1)],
        axis=1).astype(BF16)
    wkv = w_kv_b.reshape(MLA_KV_RANK, MLA_HEADS, MLA_NOPE + MLA_V)
    wkv_all = jnp.concatenate(
        [wkv[:, :, :MLA_NOPE].reshape(MLA_KV_RANK, -1), wkv[:, :, MLA_NOPE:].reshape(MLA_KV_RANK, -1)],
        axis=1).astype(BF16)
    return wa, q_norm_g[None, :], kv_norm_g[None, :], wq_all, wkv_all


def _rope_tables(n_tokens):
    f = ROPE_AXIS_FREQS
    f32 = np.float32
    rows = n_tokens // GRID_W
    r = np.repeat(np.arange(rows, dtype=f32), GRID_W)
    col = np.tile(np.arange(GRID_W, dtype=f32), rows)
    inv = (f32(ROPE_THETA) ** (-np.arange(f, dtype=f32) / f32(f))).astype(f32)
    ang_r, ang_c = r[:, None] * inv, col[:, None] * inv
    cos = np.concatenate([np.cos(ang_r)] * 2 + [np.cos(ang_c)] * 2, axis=1)
    sin = np.concatenate([-np.sin(ang_r), np.sin(ang_r), -np.sin(ang_c), np.sin(ang_c)], axis=1)
    reps = LANES // MLA_ROPE
    return (jnp.asarray(np.concatenate([cos] * reps, axis=1), F32),
            jnp.asarray(np.concatenate([sin] * reps, axis=1), F32))


def kernel(x_prompt, x_sample, c, c_ctx, cache_mla_ckv, cache_mla_kpe, state_ret, ada_w, ada_b, norm_mix_g, norm_ffn_g, mla_w_a, mla_q_norm_g, mla_kv_norm_g, mla_w_q_b, mla_w_kv_b, mla_w_o, conv_w_in, conv_w, conv_w_out, ret_w_in, ret_log_rate, ret_gn_g, ret_w_out, ffn_w_in, ffn_w_out, final_norm_g):
    batch, seq, d = x_prompt.shape
    dec_batch, dec_seq, _ = x_sample.shape
    depth = ada_w.shape[0]
    n_ctx = batch * seq
    n_lat = dec_batch * dec_seq
    past = cache_mla_ckv.shape[2]
    assert 1 + dec_batch <= MOD_ROWS

    cvec = jnp.zeros((MOD_ROWS, d), F32).at[0].set(c_ctx).at[1:1 + dec_batch].set(c)
    mod_all = _ada_call(cvec, ada_w, ada_b).reshape(depth, MOD_ROWS, 6, d)

    rows_s = _Rows(n_ctx, dec_batch, dec_seq, min(512, dec_seq))
    rows_l = _Rows(n_ctx, dec_batch, dec_seq, min(1024, dec_seq))
    rope_tabs = _rope_tables(dec_seq)
    final_g = final_norm_g[None, :]
    mla_w_o_bf = mla_w_o.astype(BF16)

    def mixer_cast_jobs(layer):
        if layer >= depth:
            return []
        kind, j = layer % N_MIXERS, layer // N_MIXERS
        if kind == 1:
            return [(conv_w_in, j), (conv_w_out, j)]
        if kind == 2:
            return [(ret_w_in, j), (ret_w_out, j)]
        return []

    x = (x_prompt.reshape(n_ctx, d), x_sample.reshape(n_lat, d))
    new_ckv, new_kpe, new_ret = [], [], []
    ffn_bf = mixer_bf = None
    for i in range(depth):
        kind, j = i % N_MIXERS, i // N_MIXERS
        mod = mod_all[i]
        g_mix = norm_mix_g[i][None, :]
        if ffn_bf is None:
            assert kind == 0
        if kind == 0:
            wts = _mla_weights(mla_w_a[j], mla_q_norm_g[j], mla_kv_norm_g[j], mla_w_q_b[j], mla_w_kv_b[j])
            split = isinstance(x, tuple)
            o_c, ckv_c, kpe_c = _mla_ctx_call(x[0] if split else x, mod, g_mix, wts, rows_l, seq)
            ql, kl, vl = _mla_proj_call(
                x[1] if split else x, 0 if split else rows_l.ctx_tiles, mod, g_mix, wts, rows_l,
                tile0=rows_l.ctx_tiles, n_tiles=rows_l.n_tiles - rows_l.ctx_tiles, rope_tabs=rope_tabs)
            new_ckv.append(ckv_c.reshape(batch, seq, MLA_KV_RANK))
            new_kpe.append(kpe_c.reshape(batch, seq, MLA_ROPE))
            kpe_rep = jnp.concatenate([cache_mla_kpe[:, j]] * (LANES // MLA_ROPE), axis=-1)
            kp, vp = _cache_expand_call(
                cache_mla_ckv[:, j].reshape(dec_batch * past, MLA_KV_RANK),
                kpe_rep.reshape(dec_batch * past, LANES), wts[4])
            jobs =[] if ffn_bf is not None else [(ffn_w_in, i), (ffn_w_out, i)]
            o_l, cast = _attn_call(ql, [kp, kl], [vp, vl], n_batch=dec_batch, seq=dec_seq,
                                   sks=[past, dec_seq], nb=1, tq=min(512, dec_seq), name="attn_lat",
                                   cast_jobs=jobs)
            if jobs:
                ffn_bf = cast
            mix = (o_c, o_l, mla_w_o_bf, j)
        elif kind == 1:
            mix = None
            x = _conv_call(x, mod, g_mix, mixer_bf[0][0], conv_w[j], mixer_bf[1][0], rows_l, seq, dec_seq)
        else:
            n_gate = ret_w_out.shape[1]
            lr = ret_log_rate[j].T[:, :, None]
            gn = ret_gn_g[j][None, :]
            y_c, st = _ret_ctx_call(x, mod, g_mix, mixer_bf[0][0], lr, gn, rows_s, seq, n_gate)
            qkv, gate = _ret_proj_call(x, mod, g_mix, mixer_bf[0][0], rows_l, n_gate,
                                       tile0=rows_l.ctx_tiles, n_tiles=rows_l.n_tiles - rows_l.ctx_tiles)
            (y_l,) = _ret_scan_call(lr, qkv, gate, gn, state_ret[:, j], bsz=dec_batch, seq=dec_seq,
                                    row0=0, hps=1, emit_state=False, name="ret_scan_lat")
            new_ret.append(st)
            mix = (y_c, y_l, mixer_bf[1], 0)
        last = i == depth - 1
        jobs = [] if last else [(ffn_w_in, i + 1), (ffn_w_out, i + 1)] + mixer_cast_jobs(i + 1)
        x, cast = _ffn_call(x, mod, norm_ffn_g[i][None, :], ffn_bf[0], ffn_bf[1], 0, final_g,
                            rows_l if mix is None else rows_s, final=last, cast_jobs=jobs, mix=mix)
        ffn_bf, mixer_bf = cast[:2], cast[2:]

    y_prompt = x[0].reshape(batch, seq, d)
    y_sample = x[1].reshape(dec_batch, dec_seq, d)
    return (y_prompt, y_sample, jnp.stack(new_ckv, axis=1), jnp.stack(new_kpe, axis=1),
            jnp.concatenate(new_ret, axis=1))
```

```python
import functools
import math

import jax
import jax.numpy as jnp
import numpy as np
from jax import lax
from jax.experimental import pallas as pl
from jax.experimental.pallas import tpu as pltpu

F32 = jnp.float32
BF16 = jnp.bfloat16

N_MIXERS = 3
MLA_HEADS = 8
MLA_NOPE = 128
MLA_ROPE = 64
MLA_V = 128
MLA_Q_RANK = 384
MLA_KV_RANK = 256
MLA_SCALE = (MLA_NOPE + MLA_ROPE) ** -0.5
MLA_Q_SCALE = MLA_SCALE * 1.4426950408889634
ROPE_THETA = 10000.0
ROPE_AXIS_FREQS = MLA_ROPE // 4
GRID_W = 64
RET_HEADS = 4
EPS = 1e-6

LANES = 128
MLA_HEAD_PAD = 2 * LANES
MOD_ROWS = 8
VMEM_LIMIT = 56 * 1024 * 1024


def _cparams(sem):
    return pltpu.CompilerParams(dimension_semantics=sem, vmem_limit_bytes=VMEM_LIMIT)


def _resident(shape, index_map):
    return pl.BlockSpec(shape, index_map, pipeline_mode=pl.Buffered(1))


def _rms(x, g):
    return x * lax.rsqrt(jnp.mean(x * x, axis=-1, keepdims=True) + EPS) * g


def _modulate(x, g, shift, scale):
    return _rms(x, g) * (1.0 + scale) + shift


def _silu(x):
    return x * jax.nn.sigmoid(x)


def _dot(a, b):
    return jnp.dot(a, b, preferred_element_type=F32)


def _dot_nt(a, b):
    return lax.dot_general(a, b, (((1,), (1,)), ((), ())), preferred_element_type=F32)


MAX_CAST_SLABS = 16


def _cast_specs(jobs, n_steps, step_of):
    n_slabs = math.gcd(n_steps, MAX_CAST_SLABS)
    per = n_steps // n_slabs

    def slab_of_step(*g):
        return step_of(*g) // per

    in_specs, out_specs, out_shape = [], [], []
    for w, layer in jobs:
        _, k, n = w.shape
        blk = (1, k // n_slabs, n)
        in_specs.append(pl.BlockSpec(blk, lambda *g, layer=layer: (layer, slab_of_step(*g), 0)))
        out_specs.append(pl.BlockSpec(blk, lambda *g: (0, slab_of_step(*g), 0)))
        out_shape.append(jax.ShapeDtypeStruct((1, k, n), BF16))
    return in_specs, out_specs, out_shape


def _run_casts(in_refs, out_refs):
    for src, dst in zip(in_refs, out_refs):
        dst[...] = src[...].astype(BF16)


def _ada_kernel(c_ref, w_ref, b_ref, o_ref):
    a = _silu(c_ref[...]).astype(BF16)
    o_ref[0] = _dot(a, w_ref[0].astype(BF16)) + b_ref[0]


def _ada_call(cvec, ada_w, ada_b):
    depth, d, n = ada_w.shape
    tn = d
    return pl.pallas_call(
        _ada_kernel,
        grid=(depth, n // tn),
        in_specs=[
            pl.BlockSpec((MOD_ROWS, d), lambda l, j: (0, 0)),
            pl.BlockSpec((1, d, tn), lambda l, j: (l, 0, j)),
            pl.BlockSpec((1, 1, tn), lambda l, j: (l, 0, j)),
        ],
        out_specs=pl.BlockSpec((1, MOD_ROWS, tn), lambda l, j: (l, 0, j)),
        out_shape=jax.ShapeDtypeStruct((depth, MOD_ROWS, n), F32),
        compiler_params=_cparams(("arbitrary", "arbitrary")),
        name="ada",
    )(cvec, ada_w, ada_b.reshape(depth, 1, n))


class _Rows:
    def __init__(self, n_ctx, dec_batch, dec_seq, tm):
        assert n_ctx % tm == 0 and dec_seq % tm == 0
        self.tm = tm
        self.ctx_tiles = n_ctx // tm
        self.seq_tiles = dec_seq // tm
        self.n_tiles = self.ctx_tiles + dec_batch * self.seq_tiles

    def group(self, i):
        lat = jnp.maximum(i - self.ctx_tiles, 0) // self.seq_tiles
        return jnp.where(i < self.ctx_tiles, 0, 1 + lat)


def _mla_project(x_ref, mod_ref, g_ref, wa_ref, qg_ref, kvg_ref, wq_ref, wkv_ref, rope_refs,
                 q_ref, k_ref, v_ref, cache_refs):
    h = _modulate(x_ref[...], g_ref[...], mod_ref[0, 0:1, :], mod_ref[0, 1:2, :]).astype(BF16)
    a = _dot(h, wa_ref[...])
    nq = MLA_Q_RANK + MLA_KV_RANK
    ckv_n = _rms(a[:, MLA_Q_RANK:nq], kvg_ref[...])
    kpe = a[:, nq:nq + LANES]
    if cache_refs is not None:
        cache_refs[0][...] = ckv_n
        cache_refs[1][...] = kpe[:, :MLA_ROPE]
    if rope_refs is not None:
        cos, sin = rope_refs[0][...], rope_refs[1][...]
        kpe = kpe * cos + a[:, nq + LANES:nq + 2 * LANES] * sin
    kpe = kpe.astype(BF16)

    qn = _rms(a[:, :MLA_Q_RANK], qg_ref[...]).astype(BF16)
    nh = MLA_HEADS
    n_nope, n_pe = nh * MLA_NOPE, nh * MLA_ROPE
    q_nope = _dot(qn, wq_ref[:, :n_nope])
    q_pe = _dot(qn, wq_ref[:, n_nope:n_nope + n_pe])
    if rope_refs is not None:
        q_sw = _dot(qn, wq_ref[:, n_nope + n_pe:n_nope + 2 * n_pe])
        cos, sin = cos * MLA_Q_SCALE, sin * MLA_Q_SCALE
    kv = _dot(ckv_n.astype(BF16), wkv_ref[...])
    low_half = lax.broadcasted_iota(jnp.int32, (1, LANES), 1) < MLA_ROPE
    for pair in range(nh // 2):
        ps = slice(pair * LANES, (pair + 1) * LANES)
        if rope_refs is not None:
            pe = q_pe[:, ps] * cos + q_sw[:, ps] * sin
        else:
            pe = q_pe[:, ps] * MLA_Q_SCALE
        for hd, keep in ((2 * pair, low_half), (2 * pair + 1, jnp.logical_not(low_half))):
            lo = hd * MLA_HEAD_PAD
            hs = slice(hd * LANES, (hd + 1) * LANES)
            q_ref[:, lo:lo + LANES] = (q_nope[:, hs] * MLA_Q_SCALE).astype(BF16)
            q_ref[:, lo + LANES:lo + 2 * LANES] = jnp.where(keep, pe, 0.0).astype(BF16)
            k_ref[:, lo:lo + LANES] = kv[:, hs].astype(BF16)
            k_ref[:, lo + LANES:lo + 2 * LANES] = kpe
    v_ref[...] = kv[:, nh * LANES:].astype(BF16)


def _mla_proj_kernel(x_ref, mod_ref, g_ref, wa_ref, qg_ref, kvg_ref, wq_ref, wkv_ref, ck_ref, sk_ref,
                     q_ref, k_ref, v_ref):
    _mla_project(x_ref, mod_ref, g_ref, wa_ref, qg_ref, kvg_ref, wq_ref, wkv_ref, (ck_ref, sk_ref),
                 q_ref, k_ref, v_ref, None)


def _mla_proj_call(x, x_tile0, mod, g, wts, rows, *, tile0, n_tiles, rope_tabs):
    tm, d = rows.tm, x.shape[1]
    wa, qg, kvg, wq, wkv = wts
    full = lambda arr: _resident(arr.shape, lambda i: (0,) * arr.ndim)
    in_specs = [
        pl.BlockSpec((tm, d), lambda i: (i + x_tile0, 0)),
        pl.BlockSpec((1, 6, d), lambda i: (rows.group(i + tile0), 0, 0)),
        full(g), full(wa), full(qg), full(kvg), full(wq), full(wkv),
    ]
    in_specs += [pl.BlockSpec((tm, LANES), lambda i: (i % rows.seq_tiles, 0)) for _ in rope_tabs]
    m = n_tiles * tm
    hp, hv = MLA_HEADS * MLA_HEAD_PAD, MLA_HEADS * MLA_V
    return pl.pallas_call(
        _mla_proj_kernel,
        grid=(n_tiles,),
        in_specs=in_specs,
        out_specs=[pl.BlockSpec((tm, hp), lambda i: (i, 0)), pl.BlockSpec((tm, hp), lambda i: (i, 0)),
                   pl.BlockSpec((tm, hv), lambda i: (i, 0))],
        out_shape=[jax.ShapeDtypeStruct((m, hp), BF16), jax.ShapeDtypeStruct((m, hp), BF16),
                   jax.ShapeDtypeStruct((m, hv), BF16)],
        compiler_params=_cparams(("arbitrary",)),
        name="mla_proj_lat",
    )(x, mod, g, wa, qg, kvg, wq, wkv, *rope_tabs)


def _mla_ctx_kernel(x_ref, mod_ref, g_ref, wa_ref, qg_ref, kvg_ref, wq_ref, wkv_ref,
                    o_ref, ckv_ref, kpe_ref, q_s, k_s, v_s, s_ref, *, seq):
    _mla_project(x_ref, mod_ref, g_ref, wa_ref, qg_ref, kvg_ref, wq_ref, wkv_ref, None,
                 q_s, k_s, v_s, (ckv_ref, kpe_ref))
    _attend(q_s, [k_s], [v_s], o_ref, s_ref, nb=x_ref.shape[0] // seq, tq=seq, sks=(seq,))


def _mla_ctx_call(x, mod, g, wts, rows, seq):
    tm, d = rows.tm, x.shape[1]
    assert tm % seq == 0
    wa, qg, kvg, wq, wkv = wts
    full = lambda arr: _resident(arr.shape, lambda i: (0,) * arr.ndim)
    m = rows.ctx_tiles * tm
    hp, hv = MLA_HEADS * MLA_HEAD_PAD, MLA_HEADS * MLA_V
    return pl.pallas_call(
        functools.partial(_mla_ctx_kernel, seq=seq),
        grid=(rows.ctx_tiles,),
        in_specs=[
            pl.BlockSpec((tm, d), lambda i: (i, 0)),
            pl.BlockSpec((1, 6, d), lambda i: (0, 0, 0)),
            full(g), full(wa), full(qg), full(kvg), full(wq), full(wkv),
        ],
        out_specs=[pl.BlockSpec((tm, hv), lambda i: (i, 0)),
                   pl.BlockSpec((tm, MLA_KV_RANK), lambda i: (i, 0)),
                   pl.BlockSpec((tm, MLA_ROPE), lambda i: (i, 0))],
        out_shape=[jax.ShapeDtypeStruct((m, hv), BF16),
                   jax.ShapeDtypeStruct((m, MLA_KV_RANK), F32),
                   jax.ShapeDtypeStruct((m, MLA_ROPE), F32)],
        scratch_shapes=[pltpu.VMEM((tm, hp), BF16), pltpu.VMEM((tm, hp), BF16), pltpu.VMEM((tm, hv), BF16),
                        pltpu.VMEM((2, seq, seq), F32)],
        compiler_params=_cparams(("arbitrary",)),
        name="mla_ctx",
    )(x, mod, g, wa, qg, kvg, wq, wkv)


def _cache_expand_kernel(ckv_ref, kpe_ref, wkv_ref, k_ref, v_ref):
    kv = _dot(ckv_ref[...].astype(BF16), wkv_ref[...])
    kpe = kpe_ref[...].astype(BF16)
    nh = MLA_HEADS
    for hd in range(nh):
        lo = hd * MLA_HEAD_PAD
        k_ref[:, lo:lo + LANES] = kv[:, hd * LANES:(hd + 1) * LANES].astype(BF16)
        k_ref[:, lo + LANES:lo + 2 * LANES] = kpe
    v_ref[...] = kv[:, nh * LANES:].astype(BF16)


def _cache_expand_call(ckv, kpe_pad, wkv):
    m = ckv.shape[0]
    tm = min(m, 512)
    hp = MLA_HEADS * MLA_HEAD_PAD
    return pl.pallas_call(
        _cache_expand_kernel,
        grid=(m // tm,),
        in_specs=[
            pl.BlockSpec((tm, MLA_KV_RANK), lambda i: (i, 0)),
            pl.BlockSpec((tm, LANES), lambda i: (i, 0)),
            _resident(wkv.shape, lambda i: (0, 0)),
        ],
        out_specs=[pl.BlockSpec((tm, hp), lambda i: (i, 0)),
                   pl.BlockSpec((tm, MLA_HEADS * MLA_V), lambda i: (i, 0))],
        out_shape=[jax.ShapeDtypeStruct((m, hp), BF16),
                   jax.ShapeDtypeStruct((m, MLA_HEADS * MLA_V), BF16)],
        compiler_params=_cparams(("arbitrary",)),
        name="mla_cache_expand",
    )(ckv, kpe_pad, wkv)


def _attend(q_ref, k_refs, v_refs, o_ref, s_ref, *, nb, tq, sks):
    problems = [(b, h) for b in range(nb) for h in range(MLA_HEADS)]
    seg_cols = [sum(sks[:i]) for i in range(len(sks))]

    def scores(idx):
        b, h = problems[idx]
        q = q_ref[b * tq:(b + 1) * tq, h * MLA_HEAD_PAD:(h + 1) * MLA_HEAD_PAD]
        m_lane = None
        for k_ref, sk, c0 in zip(k_refs, sks, seg_cols):
            s = _dot_nt(q, k_ref[b * sk:(b + 1) * sk, h * MLA_HEAD_PAD:(h + 1) * MLA_HEAD_PAD])
            s_ref[idx % 2, :, c0:c0 + sk] = s
            for c in range(sk // LANES):
                piece = s[:, c * LANES:(c + 1) * LANES]
                m_lane = piece if m_lane is None else jnp.maximum(m_lane, piece)
        return m_lane.max(axis=-1, keepdims=True)

    def weighted_values(idx, m):
        b, h = problems[idx]
        acc = None
        for v_ref, sk, c0 in zip(v_refs, sks, seg_cols):
            p = jnp.exp2(s_ref[idx % 2, :, c0:c0 + sk] - m).astype(BF16)
            v = v_ref[b * sk:(b + 1) * sk, h * MLA_V:(h + 1) * MLA_V]
            part = _dot(p, jnp.concatenate([v, jnp.ones_like(v)], axis=1))
            acc = part if acc is None else acc + part
        o = acc[:, :MLA_V] / acc[:, MLA_V:]
        o_ref[b * tq:(b + 1) * tq, h * MLA_V:(h + 1) * MLA_V] = o.astype(o_ref.dtype)

    m = scores(0)
    for idx in range(len(problems)):
        m_next = scores(idx + 1) if idx + 1 < len(problems) else None
        weighted_values(idx, m)
        m = m_next


def _attn_kernel(*refs, n_seg, nb, tq, sks, n_cast):
    n_in = 1 + 2 * n_seg + n_cast
    _run_casts(refs[1 + 2 * n_seg:n_in], refs[n_in + 1:n_in + 1 + n_cast])
    _attend(refs[0], refs[1:1 + n_seg], refs[1 + n_seg:1 + 2 * n_seg], refs[n_in],
            refs[n_in + 1 + n_cast], nb=nb, tq=tq, sks=sks)


def _attn_call(q, ks, vs, *, n_batch, seq, sks, nb, tq, name, cast_jobs=()):
    n_seg = len(ks)
    tiles = seq // tq
    assert nb == 1 or tiles == 1
    hp, hv = MLA_HEADS * MLA_HEAD_PAD, MLA_HEADS * MLA_V
    in_specs = [pl.BlockSpec((nb * tq, hp), lambda b, t: (b * tiles + t, 0))]
    in_specs += [pl.BlockSpec((nb * sk, hp), lambda b, t: (b, 0)) for sk in sks]
    in_specs += [pl.BlockSpec((nb * sk, hv), lambda b, t: (b, 0)) for sk in sks]
    c_in, c_out, c_shape = _cast_specs(cast_jobs, (n_batch // nb) * tiles, lambda b, t: b * tiles + t)
    outs = pl.pallas_call(
        functools.partial(_attn_kernel, n_seg=n_seg, nb=nb, tq=tq, sks=tuple(sks), n_cast=len(cast_jobs)),
        grid=(n_batch // nb, tiles),
        in_specs=in_specs + c_in,
        out_specs=[pl.BlockSpec((nb * tq, hv), lambda b, t: (b * tiles + t, 0))] + c_out,
        out_shape=[jax.ShapeDtypeStruct((n_batch * seq, hv), BF16)] + c_shape,
        scratch_shapes=[pltpu.VMEM((2, tq, sum(sks)), F32)],
        compiler_params=_cparams(("arbitrary", "arbitrary")),
        name=name,
    )(q, *ks, *vs, *[w for w, _ in cast_jobs])
    return outs[0], list(outs[1:])


def _split_specs(rows, width):
    ct = rows.ctx_tiles
    return [pl.BlockSpec((rows.tm, width), lambda i: (jnp.minimum(i, ct - 1), 0)),
            pl.BlockSpec((rows.tm, width), lambda i: (jnp.maximum(i - ct, 0), 0))]


def _ffn_kernel(*refs, th, final, ctx_tiles, n_cast, n_x, mixed):
    it = iter(refs)
    x_refs = [next(it) for _ in range(n_x)]
    if mixed:
        yc_ref, yl_ref, wo_ref = next(it), next(it), next(it)
    mod_ref, g_ref, win_ref, wout_ref, fg_ref = (next(it) for _ in range(5))
    cast_in = [next(it) for _ in range(n_cast)]
    out_refs = [next(it) for _ in range(2 if final else 1)]
    cast_out = [next(it) for _ in range(n_cast)]
    scratch = list(it)
    act_ref = scratch[0]
    _run_casts(cast_in, cast_out)
    is_ctx = pl.program_id(0) < ctx_tiles
    if mixed:
        x1_ref = scratch[1]

        def mix(y_ref, x_ref):
            x1_ref[...] = x_ref[...] + mod_ref[0, 2:3, :] * _dot(y_ref[...], wo_ref[0])

        pl.when(is_ctx)(lambda: mix(yc_ref, x_refs[0]))
        pl.when(jnp.logical_not(is_ctx))(lambda: mix(yl_ref, x_refs[-1]))
        x = x1_ref[...]
    else:
        x = x_refs[0][...]
    h = _modulate(x, g_ref[...], mod_ref[0, 3:4, :], mod_ref[0, 4:5, :]).astype(BF16)
    hidden = wout_ref.shape[1]
    for c in range(hidden // th):
        a = _dot(h, win_ref[0, :, c * th:(c + 1) * th])
        b = _dot(h, win_ref[0, :, hidden + c * th:hidden + (c + 1) * th])
        act_ref[:, c * th:(c + 1) * th] = (_silu(a) * b).astype(BF16)
    y = x + mod_ref[0, 5:6, :] * _dot(act_ref[...], wout_ref[0])
    if not final:
        out_refs[0][...] = y
    else:
        acc_ref = scratch[-1]
        acc_ref[...] = _rms(y, fg_ref[...])

        @pl.when(is_ctx)
        def _():
            out_refs[0][...] = acc_ref[...]

        @pl.when(jnp.logical_not(is_ctx))
        def _():
            out_refs[1][...] = acc_ref[...]


def _ffn_call(x, mod, g, w_in_all, w_out_all, layer, final_g, rows, final, cast_jobs=(), mix=None):
    tm, d = rows.tm, w_in_all.shape[1]
    hidden = w_out_all.shape[1]
    split_x = isinstance(x, tuple)
    assert not split_x or mix is not None
    xs = list(x) if split_x else [x]
    x_specs = _split_specs(rows, d) if split_x else [pl.BlockSpec((tm, d), lambda i: (i, 0))]
    mix_args, mix_specs = [], []
    scratch = [pltpu.VMEM((tm, hidden), BF16)]
    if mix is not None:
        y_ctx, y_lat, w_o_all, w_o_layer = mix
        k = w_o_all.shape[1]
        mix_args = [y_ctx, y_lat, w_o_all]
        mix_specs = _split_specs(rows, k) + [_resident((1, k, d), lambda i: (w_o_layer, 0, 0))]
        scratch.append(pltpu.VMEM((tm, d), F32))
    if final:
        out_specs = _split_specs(rows, d)
        out_shape = [jax.ShapeDtypeStruct((rows.ctx_tiles * tm, d), F32),
                     jax.ShapeDtypeStruct(((rows.n_tiles - rows.ctx_tiles) * tm, d), F32)]
        scratch.append(pltpu.VMEM((tm, d), F32))
    else:
        out_specs = [pl.BlockSpec((tm, d), lambda i: (i, 0))]
        out_shape = [jax.ShapeDtypeStruct((rows.n_tiles * tm, d), F32)]
    c_in, c_out, c_shape = _cast_specs(cast_jobs, rows.n_tiles, lambda i: i)
    outs = pl.pallas_call(
        functools.partial(_ffn_kernel, th=2 * LANES, final=final, ctx_tiles=rows.ctx_tiles,
                          n_cast=len(cast_jobs), n_x=len(xs), mixed=mix is not None),
        grid=(rows.n_tiles,),
        in_specs=x_specs + mix_specs + [
            pl.BlockSpec((1, 6, d), lambda i: (rows.group(i), 0, 0)),
            _resident(g.shape, lambda i: (0, 0)),
            _resident((1, d, 2 * hidden), lambda i: (layer, 0, 0)),
            _resident((1, hidden, d), lambda i: (layer, 0, 0)),
            _resident(final_g.shape, lambda i: (0, 0)),
        ] + c_in,
        out_specs=out_specs + c_out,
        out_shape=out_shape + c_shape,
        scratch_shapes=scratch,
        compiler_params=_cparams(("arbitrary",)),
        name=("ffn_final" if final else "ffn") + ("_mix" if mix is not None else ""),
    )(*xs, *mix_args, mod, g, w_in_all, w_out_all, final_g, *[w for w, _ in cast_jobs])
    n_out = 2 if final else 1
    stream = tuple(outs[:2]) if final else outs[0]
    return stream, list(outs[n_out:])


CONV_HALO = 16


def _conv_kernel(x_ref, xp_ref, xn_ref, mod_ref, g_ref, win_ref, cw_ref, wout_ref, o_ref,
                 h_ref, z_ref, act_ref, *, tn, ctx_tiles, ctx_seq, lat_seq):
    tm, d = x_ref.shape
    hl = CONV_HALO
    shift, scale = mod_ref[0, 0:1, :], mod_ref[0, 1:2, :]
    h_ref[0:hl, :] = _modulate(xp_ref[...], g_ref[...], shift, scale).astype(BF16)
    h_ref[hl:hl + tm, :] = _modulate(x_ref[...], g_ref[...], shift, scale).astype(BF16)
    h_ref[hl + tm:, :] = _modulate(xn_ref[...], g_ref[...], shift, scale).astype(BF16)
    i = pl.program_id(0)
    is_ctx = i < ctx_tiles
    row = lax.broadcasted_iota(jnp.int32, (tm, 1), 0)
    lat_row0 = (jnp.maximum(i - ctx_tiles, 0) % (lat_seq // tm)) * tm
    pos = jnp.where(is_ctx, row & (ctx_seq - 1), row + lat_row0)
    seq = jnp.where(is_ctx, ctx_seq, lat_seq)
    has_prev = pos != 0
    has_next = pos != seq - 1
    for c in range(d // tn):
        sl = slice(c * tn, (c + 1) * tn)
        h = h_ref[...]
        cg = _dot(h, win_ref[:, d + c * tn:d + (c + 1) * tn])
        u = _dot(h, win_ref[:, 2 * d + c * tn:2 * d + (c + 1) * tn])
        z_ref[...] = cg * u
        bg = _dot(h_ref[hl:hl + tm, :], win_ref[:, sl])
        z_prev = jnp.where(has_prev, z_ref[hl - 1:hl - 1 + tm, :], 0.0)
        z_next = jnp.where(has_next, z_ref[hl + 1:hl + 1 + tm, :], 0.0)
        conv = z_prev * cw_ref[0:1, sl] + z_ref[hl:hl + tm, :] * cw_ref[1:2, sl] + z_next * cw_ref[2:3, sl]
        act_ref[:, sl] = (bg * conv).astype(BF16)
    o_ref[...] = x_ref[...] + mod_ref[0, 2:3, :] * _dot(act_ref[...], wout_ref[...])


def _conv_call(x, mod, g, w_in, conv_w, w_out, rows, ctx_seq, lat_seq):
    tm, d = rows.tm, x.shape[1]
    hl = CONV_HALO
    assert tm % ctx_seq == 0 and ctx_seq & (ctx_seq - 1) == 0 and lat_seq % tm == 0 and tm % hl == 0
    last_halo = x.shape[0] // hl - 1
    return pl.pallas_call(
        functools.partial(_conv_kernel, tn=2 * LANES, ctx_tiles=rows.ctx_tiles, ctx_seq=ctx_seq,
                          lat_seq=lat_seq),
        grid=(rows.n_tiles,),
        in_specs=[
            pl.BlockSpec((tm, d), lambda i: (i, 0)),
            pl.BlockSpec((hl, d), lambda i: (jnp.maximum(i * (tm // hl) - 1, 0), 0)),
            pl.BlockSpec((hl, d), lambda i: (jnp.minimum((i + 1) * (tm // hl), last_halo), 0)),
            pl.BlockSpec((1, 6, d), lambda i: (rows.group(i), 0, 0)),
            _resident(g.shape, lambda i: (0, 0)),
            _resident(w_in.shape, lambda i: (0, 0)),
            _resident(conv_w.shape, lambda i: (0, 0)),
            _resident(w_out.shape, lambda i: (0, 0)),
        ],
        out_specs=pl.BlockSpec((tm, d), lambda i: (i, 0)),
        out_shape=jax.ShapeDtypeStruct(x.shape, F32),
        scratch_shapes=[pltpu.VMEM((tm + 2 * hl, d), BF16), pltpu.VMEM((tm + 2 * hl, 2 * LANES), F32),
                        pltpu.VMEM((tm, d), BF16)],
        compiler_params=_cparams(("arbitrary",)),
        name="conv_mixer",
    )(x, x, x, mod, g, w_in, conv_w, w_out)


RET_PROJ_CHUNK = 4 * LANES


def _ret_project(x_ref, mod_ref, g_ref, w_ref, qkv_ref, gate_ref):
    tn = RET_PROJ_CHUNK
    h = _modulate(x_ref[...], g_ref[...], mod_ref[0, 0:1, :], mod_ref[0, 1:2, :]).astype(BF16)
    n_qkv = qkv_ref.shape[1]
    for c in range(n_qkv // tn):
        qkv_ref[:, c * tn:(c + 1) * tn] = _dot(h, w_ref[:, c * tn:(c + 1) * tn]).astype(BF16)
    for c in range(gate_ref.shape[1] // tn):
        gate = _dot(h, w_ref[:, n_qkv + c * tn:n_qkv + (c + 1) * tn])
        gate_ref[:, c * tn:(c + 1) * tn] = _silu(gate).astype(gate_ref.dtype)


def _ret_decays(lr_ref, hd, chunk, k_scale):
    row = lax.broadcasted_iota(jnp.int32, (chunk, chunk), 0).astype(F32)
    col = lax.broadcasted_iota(jnp.int32, (chunk, chunk), 1).astype(F32)
    ridx = lax.broadcasted_iota(jnp.int32, (chunk, 1), 0).astype(F32)
    dist = row - col
    log_gamma = -jnp.exp(lr_ref[hd])
    lg_f, lg_b = log_gamma[0:1, :], log_gamma[1:2, :]
    mask = jnp.where(dist > 0, jnp.exp(jnp.maximum(dist, 0.0) * lg_f),
                     jnp.where(dist < 0, jnp.exp(jnp.maximum(-dist, 0.0) * lg_b), 2.0)) * k_scale
    q_decay = (jnp.exp((ridx + 1.0) * lg_f), jnp.exp((chunk - ridx) * lg_b))
    k_decay = (jnp.exp((chunk - 1.0 - ridx) * lg_f) * k_scale, jnp.exp(ridx * lg_b) * k_scale)
    chunk_decay = (jnp.exp(chunk * lg_f), jnp.exp(chunk * lg_b))
    return mask, q_decay, k_decay, chunk_decay


def _ret_scan(decays, q, k, v, gate_ref, gn_ref, s0_ref, y_ref, sout_ref, o_ref, st_ref, *,
              row0, seq, chunk, hps, dk, dv):
    n_chunks = seq // chunk
    has_init = s0_ref is not None
    (q_ref, q0), (k_ref, k0), (v_ref, v0) = q, k, v
    for hd in range(hps):
        mask, q_decay, k_decay, chunk_decay = decays[hd]
        vs = slice(hd * dv, (hd + 1) * dv)
        for direction in range(2):
            order = range(n_chunks) if direction == 0 else range(n_chunks - 1, -1, -1)
            if has_init:
                st_ref[hd] = s0_ref[direction, hd]
            for step, c in enumerate(order):
                rs = slice(row0 + c * chunk, row0 + (c + 1) * chunk)
                ls = slice(c * chunk, (c + 1) * chunk)
                qc = q_ref[rs, q0 + hd * dk:q0 + (hd + 1) * dk]
                kc = k_ref[rs, k0 + hd * dk:k0 + (hd + 1) * dk]
                vc = v_ref[rs, v0 + hd * dv:v0 + (hd + 1) * dv]
                have_state = has_init or step > 0
                if direction == 0:
                    scores = _dot_nt(qc, kc) * mask
                    o_ref[ls, vs] = _dot(scores.astype(BF16), vc)
                if have_state:
                    o_ref[ls, vs] += _dot(qc, st_ref[hd].astype(BF16)) * q_decay[direction]
                kd = (kc.astype(F32) * k_decay[direction]).T.astype(BF16)
                update = _dot(kd, vc)
                if have_state:
                    st_ref[hd] = st_ref[hd] * chunk_decay[direction] + update
                else:
                    st_ref[hd] = update
            if sout_ref is not None:
                sout_ref[direction, hd] = st_ref[hd]

        o = o_ref[:, vs]
        mu = jnp.mean(o, axis=-1, keepdims=True)
        var = jnp.mean(jnp.square(o - mu), axis=-1, keepdims=True)
        on = (o - mu) * lax.rsqrt(var + EPS) * gn_ref[:, vs]
        ys = slice(row0, row0 + seq)
        y_ref[ys, vs] = (gate_ref[ys, vs].astype(F32) * on).astype(y_ref.dtype)


def _ret_ctx_kernel(x_ref, mod_ref, g_ref, w_ref, lr_ref, gn_ref, y_ref, sout_ref,
                    qkv_s, gate_s, o_s, st_s, *, seq, dk, dv):
    nh = RET_HEADS
    _ret_project(x_ref, mod_ref, g_ref, w_ref, qkv_s, gate_s)
    decays = [_ret_decays(lr_ref, hd, seq, dk ** -0.5) for hd in range(nh)]
    for b in range(x_ref.shape[0] // seq):
        _ret_scan(decays, (qkv_s, 0), (qkv_s, nh * dk), (qkv_s, 2 * nh * dk), gate_s, gn_ref, None,
                  y_ref, sout_ref.at[b, 0], o_s, st_s, row0=b * seq, seq=seq, chunk=seq,
                  hps=nh, dk=dk, dv=dv)


def _ret_ctx_call(x, mod, g, w, log_rate, gn_g, rows, seq, n_gate):
    tm, d = rows.tm, x.shape[1]
    assert tm % seq == 0 and seq <= 2 * LANES
    nh = RET_HEADS
    n_qkv = w.shape[1] - n_gate
    dv = n_gate // nh
    dk = (n_qkv - n_gate) // (2 * nh)
    nb = tm // seq
    m = rows.ctx_tiles * tm
    return pl.pallas_call(
        functools.partial(_ret_ctx_kernel, seq=seq, dk=dk, dv=dv),
        grid=(rows.ctx_tiles,),
        in_specs=[
            pl.BlockSpec((tm, d), lambda i: (i, 0)),
            pl.BlockSpec((1, 6, d), lambda i: (0, 0, 0)),
            _resident(g.shape, lambda i: (0, 0)),
            _resident(w.shape, lambda i: (0, 0)),
            _resident(log_rate.shape, lambda i: (0, 0, 0)),
            _resident(gn_g.shape, lambda i: (0, 0)),
        ],
        out_specs=[pl.BlockSpec((tm, n_gate), lambda i: (i, 0)),
                   pl.BlockSpec((nb, 1, 2, nh, dk, dv), lambda i: (i, 0, 0, 0, 0, 0))],
        out_shape=[jax.ShapeDtypeStruct((m, n_gate), BF16),
                   jax.ShapeDtypeStruct((m // seq, 1, 2, nh, dk, dv), F32)],
        scratch_shapes=[pltpu.VMEM((tm, n_qkv), BF16), pltpu.VMEM((tm, n_gate), BF16),
                        pltpu.VMEM((seq, n_gate), F32), pltpu.VMEM((nh, dk, dv), F32)],
        compiler_params=_cparams(("arbitrary",)),
        name="ret_ctx",
    )(x, mod, g, w, log_rate, gn_g)


RET_LAT_ROWS = 512


def _ret_lat_kernel(x_ref, mod_ref, g_ref, wq_ref, wk_ref, wv_ref, wg_ref, lr_ref, gn_ref, s0_ref,
                    y_ref, h_s, q_s, k_s, v_s, gate_s, o_s, st_s, *, chunk, dk, dv):
    seq = x_ref.shape[0]

    @pl.when(pl.program_id(1) == 0)
    def _():
        h_s[...] = _modulate(x_ref[...], g_ref[...], mod_ref[0, 0:1, :], mod_ref[0, 1:2, :]).astype(BF16)

    rc = min(RET_LAT_ROWS, seq)
    for r in range(seq // rc):
        rs = slice(r * rc, (r + 1) * rc)
        h = h_s[rs, :]
        q_s[rs, :] = _dot(h, wq_ref[...]).astype(BF16)
        k_s[rs, :] = _dot(h, wk_ref[...]).astype(BF16)
        v_s[rs, :] = _dot(h, wv_ref[...]).astype(BF16)
        gate_s[rs, :] = _silu(_dot(h, wg_ref[...])).astype(BF16)
    decays = [_ret_decays(lr_ref, 0, chunk, dk ** -0.5)]
    _ret_scan(decays, (q_s, 0), (k_s, 0), (v_s, 0), gate_s, gn_ref, s0_ref.at[0], y_ref, None,
              o_s, st_s, row0=0, seq=seq, chunk=chunk, hps=1, dk=dk, dv=dv)


def _ret_lat_call(x, mod, g, w, log_rate, gn_g, s0, *, bsz, seq, row0, n_gate):
    d = x.shape[1]
    nh = RET_HEADS
    n_qkv = w.shape[1] - n_gate
    dv = n_gate // nh
    dk = (n_qkv - n_gate) // (2 * nh)
    chunk = min(seq, 2 * LANES)
    assert row0 % seq == 0 and dv % dk == 0
    b0 = row0 // seq
    return pl.pallas_call(
        functools.partial(_ret_lat_kernel, chunk=chunk, dk=dk, dv=dv),
        grid=(bsz, nh),
        in_specs=[
            pl.BlockSpec((seq, d), lambda b, h: (b0 + b, 0)),
            pl.BlockSpec((1, 6, d), lambda b, h: (1 + b, 0, 0)),
            _resident(g.shape, lambda b, h: (0, 0)),
            pl.BlockSpec((d, dk), lambda b, h: (0, h)),
            pl.BlockSpec((d, dk), lambda b, h: (0, nh + h)),
            pl.BlockSpec((d, dv), lambda b, h: (0, (2 * nh * dk) // dv + h)),
            pl.BlockSpec((d, dv), lambda b, h: (0, n_qkv // dv + h)),
            pl.BlockSpec((1, 2, 1), lambda b, h: (h, 0, 0)),
            pl.BlockSpec((1, dv), lambda b, h: (0, h)),
            pl.BlockSpec((1, 2, 1, dk, dv), lambda b, h: (b, 0, h, 0, 0)),
        ],
        out_specs=pl.BlockSpec((seq, dv), lambda b, h: (b, h)),
        out_shape=jax.ShapeDtypeStruct((bsz * seq, n_gate), BF16),
        scratch_shapes=[pltpu.VMEM((seq, d), BF16), pltpu.VMEM((seq, dk), BF16), pltpu.VMEM((seq, dk), BF16),
                        pltpu.VMEM((seq, dv), BF16), pltpu.VMEM((seq, dv), BF16),
                        pltpu.VMEM((seq, dv), F32), pltpu.VMEM((1, dk, dv), F32)],
        compiler_params=_cparams(("arbitrary", "arbitrary")),
        name="ret_lat",
    )(x, mod, g, w, w, w, w, log_rate, gn_g, s0)


def _rope_swap_index():
    f = ROPE_AXIS_FREQS
    idx = jnp.arange(MLA_ROPE)
    return jnp.where((idx // f) % 2 == 0, idx + f, idx - f)


def _mla_weights(w_a, q_norm_g, kv_norm_g, w_q_b, w_kv_b):
    swap = _rope_swap_index()
    nq = MLA_Q_RANK + MLA_KV_RANK
    w_kpe = w_a[:, nq:]
    w_kpe_sw = w_kpe[:, swap]
    wa = jnp.concatenate([w_a[:, :nq], w_kpe, w_kpe, w_kpe_sw, w_kpe_sw], axis=1).astype(BF16)
    wq = w_q_b.reshape(MLA_Q_RANK, MLA_HEADS, MLA_NOPE + MLA_ROPE)
    wq_nope = wq[:, :, :MLA_NOPE].reshape(MLA_Q_RANK, MLA_HEADS * MLA_NOPE)
    wq_pe = wq[:, :, MLA_NOPE:]
    wq_all = jnp.concatenate(
        [wq_nope, wq_pe.reshape(MLA_Q_RANK, -1), wq_pe[:, :, swap].reshape(MLA_Q_RANK, -1)],
        axis=1).astype(BF16)
    wkv = w_kv_b.reshape(MLA_KV_RANK, MLA_HEADS, MLA_NOPE + MLA_V)
    wkv_all = jnp.concatenate(
        [wkv[:, :, :MLA_NOPE].reshape(MLA_KV_RANK, -1), wkv[:, :, MLA_NOPE:].reshape(MLA_KV_RANK, -1)],
        axis=1).astype(BF16)
    return wa, q_norm_g[None, :], kv_norm_g[None, :], wq_all, wkv_all


def _rope_tables(n_tokens):
    f = ROPE_AXIS_FREQS
    f32 = np.float32
    rows = n_tokens // GRID_W
    r = np.repeat(np.arange(rows, dtype=f32), GRID_W)
    col = np.tile(np.arange(GRID_W, dtype=f32), rows)
    inv = (f32(ROPE_THETA) ** (-np.arange(f, dtype=f32) / f32(f))).astype(f32)
    ang_r, ang_c = r[:, None] * inv, col[:, None] * inv
    cos = np.concatenate([np.cos(ang_r)] * 2 + [np.cos(ang_c)] * 2, axis=1)
    sin = np.concatenate([-np.sin(ang_r), np.sin(ang_r), -np.sin(ang_c), np.sin(ang_c)], axis=1)
    reps = LANES // MLA_ROPE
    return (jnp.asarray(np.concatenate([cos] * reps, axis=1), F32),
            jnp.asarray(np.concatenate([sin] * reps, axis=1), F32))


def kernel(x_prompt, x_sample, c, c_ctx, cache_mla_ckv, cache_mla_kpe, state_ret, ada_w, ada_b, norm_mix_g, norm_ffn_g, mla_w_a, mla_q_norm_g, mla_kv_norm_g, mla_w_q_b, mla_w_kv_b, mla_w_o, conv_w_in, conv_w, conv_w_out, ret_w_in, ret_log_rate, ret_gn_g, ret_w_out, ffn_w_in, ffn_w_out, final_norm_g):
    batch, seq, d = x_prompt.shape
    dec_batch, dec_seq, _ = x_sample.shape
    depth = ada_w.shape[0]
    n_ctx = batch * seq
    n_lat = dec_batch * dec_seq
    past = cache_mla_ckv.shape[2]
    assert 1 + dec_batch <= MOD_ROWS

    cvec = jnp.zeros((MOD_ROWS, d), F32).at[0].set(c_ctx).at[1:1 + dec_batch].set(c)
    mod_all = _ada_call(cvec, ada_w, ada_b).reshape(depth, MOD_ROWS, 6, d)

    rows_s = _Rows(n_ctx, dec_batch, dec_seq, min(512, dec_seq))
    rows_l = _Rows(n_ctx, dec_batch, dec_seq, min(1024, dec_seq))
    rope_tabs = _rope_tables(dec_seq)
    final_g = final_norm_g[None, :]
    mla_w_o_bf = mla_w_o.astype(BF16)

    def mixer_cast_jobs(layer):
        if layer >= depth:
            return []
        kind, j = layer % N_MIXERS, layer // N_MIXERS
        if kind == 1:
            return [(conv_w_in, j), (conv_w_out, j)]
        if kind == 2:
            return [(ret_w_in, j), (ret_w_out, j)]
        return []

    x = (x_prompt.reshape(n_ctx, d), x_sample.reshape(n_lat, d))
    new_ckv, new_kpe, new_ret = [], [], []
    ffn_bf = mixer_bf = None
    for i in range(depth):
        kind, j = i % N_MIXERS, i // N_MIXERS
        mod = mod_all[i]
        g_mix = norm_mix_g[i][None, :]
        if ffn_bf is None:
            assert kind == 0
        if kind == 0:
            wts = _mla_weights(mla_w_a[j], mla_q_norm_g[j], mla_kv_norm_g[j], mla_w_q_b[j], mla_w_kv_b[j])
            split = isinstance(x, tuple)
            o_c, ckv_c, kpe_c = _mla_ctx_call(x[0] if split else x, mod, g_mix, wts, rows_l, seq)
            ql, kl, vl = _mla_proj_call(
                x[1] if split else x, 0 if split else rows_l.ctx_tiles, mod, g_mix, wts, rows_l,
                tile0=rows_l.ctx_tiles, n_tiles=rows_l.n_tiles - rows_l.ctx_tiles, rope_tabs=rope_tabs)
            new_ckv.append(ckv_c.reshape(batch, seq, MLA_KV_RANK))
            new_kpe.append(kpe_c.reshape(batch, seq, MLA_ROPE))
            kpe_rep = jnp.concatenate([cache_mla_kpe[:, j]] * (LANES // MLA_ROPE), axis=-1)
            kp, vp = _cache_expand_call(
                cache_mla_ckv[:, j].reshape(dec_batch * past, MLA_KV_RANK),
                kpe_rep.reshape(dec_batch * past, LANES), wts[4])
            jobs =[] if ffn_bf is not None else [(ffn_w_in, i), (ffn_w_out, i)]
            o_l, cast = _attn_call(ql, [kp, kl], [vp, vl], n_batch=dec_batch, seq=dec_seq,
                                   sks=[past, dec_seq], nb=1, tq=min(512, dec_seq), name="attn_lat",
                                   cast_jobs=jobs)
            if jobs:
                ffn_bf = cast
            mix = (o_c, o_l, mla_w_o_bf, j)
        elif kind == 1:
            mix = None
            x = _conv_call(x, mod, g_mix, mixer_bf[0][0], conv_w[j], mixer_bf[1][0], rows_l, seq, dec_seq)
        else:
            n_gate = ret_w_out.shape[1]
            lr = ret_log_rate[j].T[:, :, None]
            gn = ret_gn_g[j][None, :]
            y_c, st = _ret_ctx_call(x, mod, g_mix, mixer_bf[0][0], lr, gn, rows_s, seq, n_gate)
            y_l = _ret_lat_call(x, mod, g_mix, mixer_bf[0][0], lr, gn, state_ret[:, j],
                                bsz=dec_batch, seq=dec_seq, row0=n_ctx, n_gate=n_gate)
            new_ret.append(st)
            mix = (y_c, y_l, mixer_bf[1], 0)
        last = i == depth - 1
        jobs = [] if last else [(ffn_w_in, i + 1), (ffn_w_out, i + 1)] + mixer_cast_jobs(i + 1)
        x, cast = _ffn_call(x, mod, norm_ffn_g[i][None, :], ffn_bf[0], ffn_bf[1], 0, final_g,
                            rows_l if mix is None else rows_s, final=last, cast_jobs=jobs, mix=mix)
        ffn_bf, mixer_bf = cast[:2], cast[2:]

    y_prompt = x[0].reshape(batch, seq, d)
    y_sample = x[1].reshape(dec_batch, dec_seq, d)
    return (y_prompt, y_sample, jnp.stack(new_ckv, axis=1), jnp.stack(new_kpe, axis=1),
            jnp.concatenate(new_ret, axis=1))
```

```python
import functools
import math

import jax
import jax.numpy as jnp
import numpy as np
from jax import lax
from jax.experimental import pallas as pl
from jax.experimental.pallas import tpu as pltpu

F32 = jnp.float32
BF16 = jnp.bfloat16

N_MIXERS = 3
MLA_HEADS = 8
MLA_NOPE = 128
MLA_ROPE = 64
MLA_V = 128
MLA_Q_RANK = 384
MLA_KV_RANK = 256
MLA_SCALE = (MLA_NOPE + MLA_ROPE) ** -0.5
MLA_Q_SCALE = MLA_SCALE * 1.4426950408889634
ROPE_THETA = 10000.0
ROPE_AXIS_FREQS = MLA_ROPE // 4
GRID_W = 64
RET_HEADS = 4
EPS = 1e-6

LANES = 128
MLA_HEAD_PAD = 2 * LANES
MOD_ROWS = 8
VMEM_LIMIT = 56 * 1024 * 1024

ROW_TILE = 1024
ROW_TILE_FUSED = 512
ATTN_Q_TILE = 512


def _cparams(sem):
    return pltpu.CompilerParams(dimension_semantics=sem, vmem_limit_bytes=VMEM_LIMIT)


def _resident(shape, index_map):
    return pl.BlockSpec(shape, index_map, pipeline_mode=pl.Buffered(1))


def _rms(x, g):
    return x * lax.rsqrt(jnp.mean(x * x, axis=-1, keepdims=True) + EPS) * g


def _modulate(x, g, shift, scale):
    return _rms(x, g) * (1.0 + scale) + shift


def _silu(x):
    return x * jax.nn.sigmoid(x)


def _dot(a, b):
    return jnp.dot(a, b, preferred_element_type=F32)


def _dot_nt(a, b):
    return lax.dot_general(a, b, (((1,), (1,)), ((), ())), preferred_element_type=F32)


MAX_CAST_SLABS = 16


def _cast_specs(jobs, n_steps, step_of):
    n_slabs = math.gcd(n_steps, MAX_CAST_SLABS)
    per = n_steps // n_slabs

    def slab_of_step(*g):
        return step_of(*g) // per

    in_specs, out_specs, out_shape = [], [], []
    for w, layer in jobs:
        _, k, n = w.shape
        blk = (1, k // n_slabs, n)
        in_specs.append(pl.BlockSpec(blk, lambda *g, layer=layer: (layer, slab_of_step(*g), 0)))
        out_specs.append(pl.BlockSpec(blk, lambda *g: (0, slab_of_step(*g), 0)))
        out_shape.append(jax.ShapeDtypeStruct((1, k, n), BF16))
    return in_specs, out_specs, out_shape


def _run_casts(in_refs, out_refs):
    for src, dst in zip(in_refs, out_refs):
        dst[...] = src[...].astype(BF16)


def _ada_kernel(c_ref, w_ref, b_ref, o_ref):
    a = _silu(c_ref[...]).astype(BF16)
    o_ref[0] = _dot(a, w_ref[0].astype(BF16)) + b_ref[0]


def _ada_call(cvec, ada_w, ada_b):
    depth, d, n = ada_w.shape
    tn = n // 2
    return pl.pallas_call(
        _ada_kernel,
        grid=(depth, n // tn),
        in_specs=[
            pl.BlockSpec((MOD_ROWS, d), lambda l, j: (0, 0)),
            pl.BlockSpec((1, d, tn), lambda l, j: (l, 0, j)),
            pl.BlockSpec((1, 1, tn), lambda l, j: (l, 0, j)),
        ],
        out_specs=pl.BlockSpec((1, MOD_ROWS, tn), lambda l, j: (l, 0, j)),
        out_shape=jax.ShapeDtypeStruct((depth, MOD_ROWS, n), F32),
        compiler_params=_cparams(("arbitrary", "arbitrary")),
        name="ada",
    )(cvec, ada_w, ada_b.reshape(depth, 1, n))


class _Rows:
    def __init__(self, n_ctx, dec_batch, dec_seq, tm):
        assert n_ctx % tm == 0 and dec_seq % tm == 0
        self.tm = tm
        self.ctx_tiles = n_ctx // tm
        self.seq_tiles = dec_seq // tm
        self.n_tiles = self.ctx_tiles + dec_batch * self.seq_tiles

    def group(self, i):
        lat = jnp.maximum(i - self.ctx_tiles, 0) // self.seq_tiles
        return jnp.where(i < self.ctx_tiles, 0, 1 + lat)


def _mla_project(x_ref, mod_ref, g_ref, wa_ref, qg_ref, kvg_ref, wq_ref, wkv_ref, rope_refs,
                 q_ref, k_ref, v_ref, cache_refs):
    h = _modulate(x_ref[...], g_ref[...], mod_ref[0, 0:1, :], mod_ref[0, 1:2, :]).astype(BF16)
    a = _dot(h, wa_ref[...])
    nq = MLA_Q_RANK + MLA_KV_RANK
    ckv_n = _rms(a[:, MLA_Q_RANK:nq], kvg_ref[...])
    kpe = a[:, nq:nq + LANES]
    if cache_refs is not None:
        cache_refs[0][...] = ckv_n
        cache_refs[1][...] = kpe[:, :MLA_ROPE]
    if rope_refs is not None:
        cos, sin = rope_refs[0][...], rope_refs[1][...]
        kpe = kpe * cos + a[:, nq + LANES:nq + 2 * LANES] * sin
    kpe = kpe.astype(BF16)

    qn = _rms(a[:, :MLA_Q_RANK], qg_ref[...]).astype(BF16)
    nh = MLA_HEADS
    n_nope, n_pe = nh * MLA_NOPE, nh * MLA_ROPE
    q_nope = _dot(qn, wq_ref[:, :n_nope])
    q_pe = _dot(qn, wq_ref[:, n_nope:n_nope + n_pe])
    if rope_refs is not None:
        q_sw = _dot(qn, wq_ref[:, n_nope + n_pe:n_nope + 2 * n_pe])
        cos, sin = cos * MLA_Q_SCALE, sin * MLA_Q_SCALE
    kv = _dot(ckv_n.astype(BF16), wkv_ref[...])
    low_half = lax.broadcasted_iota(jnp.int32, (1, LANES), 1) < MLA_ROPE
    for pair in range(nh // 2):
        ps = slice(pair * LANES, (pair + 1) * LANES)
        if rope_refs is not None:
            pe = q_pe[:, ps] * cos + q_sw[:, ps] * sin
        else:
            pe = q_pe[:, ps] * MLA_Q_SCALE
        for hd, keep in ((2 * pair, low_half), (2 * pair + 1, jnp.logical_not(low_half))):
            lo = hd * MLA_HEAD_PAD
            hs = slice(hd * LANES, (hd + 1) * LANES)
            q_ref[:, lo:lo + LANES] = (q_nope[:, hs] * MLA_Q_SCALE).astype(BF16)
            q_ref[:, lo + LANES:lo + 2 * LANES] = jnp.where(keep, pe, 0.0).astype(BF16)
            k_ref[:, lo:lo + LANES] = kv[:, hs].astype(BF16)
            k_ref[:, lo + LANES:lo + 2 * LANES] = kpe
    v_ref[...] = kv[:, nh * LANES:].astype(BF16)


def _mla_proj_kernel(x_ref, mod_ref, g_ref, wa_ref, qg_ref, kvg_ref, wq_ref, wkv_ref, ck_ref, sk_ref,
                     q_ref, k_ref, v_ref):
    _mla_project(x_ref, mod_ref, g_ref, wa_ref, qg_ref, kvg_ref, wq_ref, wkv_ref, (ck_ref, sk_ref),
                 q_ref, k_ref, v_ref, None)


def _mla_proj_call(x, x_tile0, mod, g, wts, rows, *, tile0, n_tiles, rope_tabs):
    tm, d = rows.tm, x.shape[1]
    wa, qg, kvg, wq, wkv = wts
    full = lambda arr: _resident(arr.shape, lambda i: (0,) * arr.ndim)
    in_specs = [
        pl.BlockSpec((tm, d), lambda i: (i + x_tile0, 0)),
        pl.BlockSpec((1, 6, d), lambda i: (rows.group(i + tile0), 0, 0)),
        full(g), full(wa), full(qg), full(kvg), full(wq), full(wkv),
    ]
    in_specs += [pl.BlockSpec((tm, LANES), lambda i: (i % rows.seq_tiles, 0)) for _ in rope_tabs]
    m = n_tiles * tm
    hp, hv = MLA_HEADS * MLA_HEAD_PAD, MLA_HEADS * MLA_V
    return pl.pallas_call(
        _mla_proj_kernel,
        grid=(n_tiles,),
        in_specs=in_specs,
        out_specs=[pl.BlockSpec((tm, hp), lambda i: (i, 0)), pl.BlockSpec((tm, hp), lambda i: (i, 0)),
                   pl.BlockSpec((tm, hv), lambda i: (i, 0))],
        out_shape=[jax.ShapeDtypeStruct((m, hp), BF16), jax.ShapeDtypeStruct((m, hp), BF16),
                   jax.ShapeDtypeStruct((m, hv), BF16)],
        compiler_params=_cparams(("arbitrary",)),
        name="mla_proj_lat",
    )(x, mod, g, wa, qg, kvg, wq, wkv, *rope_tabs)


def _mla_ctx_kernel(x_ref, mod_ref, g_ref, wa_ref, qg_ref, kvg_ref, wq_ref, wkv_ref,
                    o_ref, ckv_ref, kpe_ref, q_s, k_s, v_s, s_ref, *, seq):
    _mla_project(x_ref, mod_ref, g_ref, wa_ref, qg_ref, kvg_ref, wq_ref, wkv_ref, None,
                 q_s, k_s, v_s, (ckv_ref, kpe_ref))
    _attend(q_s, [k_s], [v_s], o_ref, s_ref, nb=x_ref.shape[0] // seq, tq=seq, sks=(seq,))


def _mla_ctx_call(x, mod, g, wts, rows, seq):
    tm, d = rows.tm, x.shape[1]
    assert tm % seq == 0
    wa, qg, kvg, wq, wkv = wts
    full = lambda arr: _resident(arr.shape, lambda i: (0,) * arr.ndim)
    m = rows.ctx_tiles * tm
    hp, hv = MLA_HEADS * MLA_HEAD_PAD, MLA_HEADS * MLA_V
    return pl.pallas_call(
        functools.partial(_mla_ctx_kernel, seq=seq),
        grid=(rows.ctx_tiles,),
        in_specs=[
            pl.BlockSpec((tm, d), lambda i: (i, 0)),
            pl.BlockSpec((1, 6, d), lambda i: (0, 0, 0)),
            full(g), full(wa), full(qg), full(kvg), full(wq), full(wkv),
        ],
        out_specs=[pl.BlockSpec((tm, hv), lambda i: (i, 0)),
                   pl.BlockSpec((tm, MLA_KV_RANK), lambda i: (i, 0)),
                   pl.BlockSpec((tm, MLA_ROPE), lambda i: (i, 0))],
        out_shape=[jax.ShapeDtypeStruct((m, hv), BF16),
                   jax.ShapeDtypeStruct((m, MLA_KV_RANK), F32),
                   jax.ShapeDtypeStruct((m, MLA_ROPE), F32)],
        scratch_shapes=[pltpu.VMEM((tm, hp), BF16), pltpu.VMEM((tm, hp), BF16), pltpu.VMEM((tm, hv), BF16),
                        pltpu.VMEM((2, seq, seq), F32)],
        compiler_params=_cparams(("arbitrary",)),
        name="mla_ctx",
    )(x, mod, g, wa, qg, kvg, wq, wkv)


def _cache_expand_kernel(ckv_ref, kpe_ref, wkv_ref, k_ref, v_ref):
    kv = _dot(ckv_ref[...].astype(BF16), wkv_ref[...])
    kpe = kpe_ref[...].astype(BF16)
    nh = MLA_HEADS
    for hd in range(nh):
        lo = hd * MLA_HEAD_PAD
        k_ref[:, lo:lo + LANES] = kv[:, hd * LANES:(hd + 1) * LANES].astype(BF16)
        k_ref[:, lo + LANES:lo + 2 * LANES] = kpe
    v_ref[...] = kv[:, nh * LANES:].astype(BF16)


def _cache_expand_call(ckv, kpe_pad, wkv):
    m = ckv.shape[0]
    tm = min(m, 512)
    hp = MLA_HEADS * MLA_HEAD_PAD
    return pl.pallas_call(
        _cache_expand_kernel,
        grid=(m // tm,),
        in_specs=[
            pl.BlockSpec((tm, MLA_KV_RANK), lambda i: (i, 0)),
            pl.BlockSpec((tm, LANES), lambda i: (i, 0)),
            _resident(wkv.shape, lambda i: (0, 0)),
        ],
        out_specs=[pl.BlockSpec((tm, hp), lambda i: (i, 0)),
                   pl.BlockSpec((tm, MLA_HEADS * MLA_V), lambda i: (i, 0))],
        out_shape=[jax.ShapeDtypeStruct((m, hp), BF16),
                   jax.ShapeDtypeStruct((m, MLA_HEADS * MLA_V), BF16)],
        compiler_params=_cparams(("arbitrary",)),
        name="mla_cache_expand",
    )(ckv, kpe_pad, wkv)


def _attend(q_ref, k_refs, v_refs, o_ref, s_ref, *, nb, tq, sks):
    problems = [(b, h) for b in range(nb) for h in range(MLA_HEADS)]
    seg_cols = [sum(sks[:i]) for i in range(len(sks))]

    def scores(idx):
        b, h = problems[idx]
        q = q_ref[b * tq:(b + 1) * tq, h * MLA_HEAD_PAD:(h + 1) * MLA_HEAD_PAD]
        m_lane = None
        for k_ref, sk, c0 in zip(k_refs, sks, seg_cols):
            s = _dot_nt(q, k_ref[b * sk:(b + 1) * sk, h * MLA_HEAD_PAD:(h + 1) * MLA_HEAD_PAD])
            s_ref[idx % 2, :, c0:c0 + sk] = s
            for c in range(sk // LANES):
                piece = s[:, c * LANES:(c + 1) * LANES]
                m_lane = piece if m_lane is None else jnp.maximum(m_lane, piece)
        return m_lane.max(axis=-1, keepdims=True)

    def weighted_values(idx, m):
        b, h = problems[idx]
        acc = None
        for v_ref, sk, c0 in zip(v_refs, sks, seg_cols):
            p = jnp.exp2(s_ref[idx % 2, :, c0:c0 + sk] - m).astype(BF16)
            v = v_ref[b * sk:(b + 1) * sk, h * MLA_V:(h + 1) * MLA_V]
            part = _dot(p, jnp.concatenate([v, jnp.ones_like(v)], axis=1))
            acc = part if acc is None else acc + part
        o = acc[:, :MLA_V] / acc[:, MLA_V:]
        o_ref[b * tq:(b + 1) * tq, h * MLA_V:(h + 1) * MLA_V] = o.astype(o_ref.dtype)

    m = scores(0)
    for idx in range(len(problems)):
        m_next = scores(idx + 1) if idx + 1 < len(problems) else None
        weighted_values(idx, m)
        m = m_next


def _attn_kernel(*refs, n_seg, nb, tq, sks, n_cast):
    n_in = 1 + 2 * n_seg + n_cast
    _run_casts(refs[1 + 2 * n_seg:n_in], refs[n_in + 1:n_in + 1 + n_cast])
    _attend(refs[0], refs[1:1 + n_seg], refs[1 + n_seg:1 + 2 * n_seg], refs[n_in],
            refs[n_in + 1 + n_cast], nb=nb, tq=tq, sks=sks)


def _attn_call(q, ks, vs, *, n_batch, seq, sks, nb, tq, name, cast_jobs=()):
    n_seg = len(ks)
    tiles = seq // tq
    assert nb == 1 or tiles == 1
    hp, hv = MLA_HEADS * MLA_HEAD_PAD, MLA_HEADS * MLA_V
    in_specs = [pl.BlockSpec((nb * tq, hp), lambda b, t: (b * tiles + t, 0))]
    in_specs += [pl.BlockSpec((nb * sk, hp), lambda b, t: (b, 0)) for sk in sks]
    in_specs += [pl.BlockSpec((nb * sk, hv), lambda b, t: (b, 0)) for sk in sks]
    c_in, c_out, c_shape = _cast_specs(cast_jobs, (n_batch // nb) * tiles, lambda b, t: b * tiles + t)
    outs = pl.pallas_call(
        functools.partial(_attn_kernel, n_seg=n_seg, nb=nb, tq=tq, sks=tuple(sks), n_cast=len(cast_jobs)),
        grid=(n_batch // nb, tiles),
        in_specs=in_specs + c_in,
        out_specs=[pl.BlockSpec((nb * tq, hv), lambda b, t: (b * tiles + t, 0))] + c_out,
        out_shape=[jax.ShapeDtypeStruct((n_batch * seq, hv), BF16)] + c_shape,
        scratch_shapes=[pltpu.VMEM((2, tq, sum(sks)), F32)],
        compiler_params=_cparams(("arbitrary", "arbitrary")),
        name=name,
    )(q, *ks, *vs, *[w for w, _ in cast_jobs])
    return outs[0], list(outs[1:])


def _split_specs(rows, width):
    ct = rows.ctx_tiles
    return [pl.BlockSpec((rows.tm, width), lambda i: (jnp.minimum(i, ct - 1), 0)),
            pl.BlockSpec((rows.tm, width), lambda i: (jnp.maximum(i - ct, 0), 0))]


def _ffn_kernel(*refs, th, final, ctx_tiles, n_cast, n_x, mixed):
    it = iter(refs)
    x_refs = [next(it) for _ in range(n_x)]
    if mixed:
        yc_ref, yl_ref, wo_ref = next(it), next(it), next(it)
    mod_ref, g_ref, win_ref, wout_ref, fg_ref = (next(it) for _ in range(5))
    cast_in = [next(it) for _ in range(n_cast)]
    out_refs = [next(it) for _ in range(2 if final else 1)]
    cast_out = [next(it) for _ in range(n_cast)]
    scratch = list(it)
    act_ref = scratch[0]
    _run_casts(cast_in, cast_out)
    is_ctx = pl.program_id(0) < ctx_tiles
    if mixed:
        x1_ref = scratch[1]

        def mix(y_ref, x_ref):
            x1_ref[...] = x_ref[...] + mod_ref[0, 2:3, :] * _dot(y_ref[...], wo_ref[0])

        pl.when(is_ctx)(lambda: mix(yc_ref, x_refs[0]))
        pl.when(jnp.logical_not(is_ctx))(lambda: mix(yl_ref, x_refs[-1]))
        x = x1_ref[...]
    else:
        x = x_refs[0][...]
    h = _modulate(x, g_ref[...], mod_ref[0, 3:4, :], mod_ref[0, 4:5, :]).astype(BF16)
    hidden = wout_ref.shape[1]
    for c in range(hidden // th):
        a = _dot(h, win_ref[0, :, c * th:(c + 1) * th])
        b = _dot(h, win_ref[0, :, hidden + c * th:hidden + (c + 1) * th])
        act_ref[:, c * th:(c + 1) * th] = (_silu(a) * b).astype(BF16)
    y = x + mod_ref[0, 5:6, :] * _dot(act_ref[...], wout_ref[0])
    if not final:
        out_refs[0][...] = y
    else:
        acc_ref = scratch[-1]
        acc_ref[...] = _rms(y, fg_ref[...])

        @pl.when(is_ctx)
        def _():
            out_refs[0][...] = acc_ref[...]

        @pl.when(jnp.logical_not(is_ctx))
        def _():
            out_refs[1][...] = acc_ref[...]


def _ffn_call(x, mod, g, w_in_all, w_out_all, layer, final_g, rows, final, cast_jobs=(), mix=None):
    tm, d = rows.tm, w_in_all.shape[1]
    hidden = w_out_all.shape[1]
    split_x = isinstance(x, tuple)
    assert not split_x or mix is not None
    xs = list(x) if split_x else [x]
    x_specs = _split_specs(rows, d) if split_x else [pl.BlockSpec((tm, d), lambda i: (i, 0))]
    mix_args, mix_specs = [], []
    scratch = [pltpu.VMEM((tm, hidden), BF16)]
    if mix is not None:
        y_ctx, y_lat, w_o_all, w_o_layer = mix
        k = w_o_all.shape[1]
        mix_args = [y_ctx, y_lat, w_o_all]
        mix_specs = _split_specs(rows, k) + [_resident((1, k, d), lambda i: (w_o_layer, 0, 0))]
        scratch.append(pltpu.VMEM((tm, d), F32))
    if final:
        out_specs = _split_specs(rows, d)
        out_shape = [jax.ShapeDtypeStruct((rows.ctx_tiles * tm, d), F32),
                     jax.ShapeDtypeStruct(((rows.n_tiles - rows.ctx_tiles) * tm, d), F32)]
        scratch.append(pltpu.VMEM((tm, d), F32))
    else:
        out_specs = [pl.BlockSpec((tm, d), lambda i: (i, 0))]
        out_shape = [jax.ShapeDtypeStruct((rows.n_tiles * tm, d), F32)]
    c_in, c_out, c_shape = _cast_specs(cast_jobs, rows.n_tiles, lambda i: i)
    outs = pl.pallas_call(
        functools.partial(_ffn_kernel, th=2 * LANES, final=final, ctx_tiles=rows.ctx_tiles,
                          n_cast=len(cast_jobs), n_x=len(xs), mixed=mix is not None),
        grid=(rows.n_tiles,),
        in_specs=x_specs + mix_specs + [
            pl.BlockSpec((1, 6, d), lambda i: (rows.group(i), 0, 0)),
            _resident(g.shape, lambda i: (0, 0)),
            _resident((1, d, 2 * hidden), lambda i: (layer, 0, 0)),
            _resident((1, hidden, d), lambda i: (layer, 0, 0)),
            _resident(final_g.shape, lambda i: (0, 0)),
        ] + c_in,
        out_specs=out_specs + c_out,
        out_shape=out_shape + c_shape,
        scratch_shapes=scratch,
        compiler_params=_cparams(("arbitrary",)),
        name=("ffn_final" if final else "ffn") + ("_mix" if mix is not None else ""),
    )(*xs, *mix_args, mod, g, w_in_all, w_out_all, final_g, *[w for w, _ in cast_jobs])
    n_out = 2 if final else 1
    stream = tuple(outs[:2]) if final else outs[0]
    return stream, list(outs[n_out:])


CONV_HALO = 16


def _conv_kernel(x_ref, xp_ref, xn_ref, mod_ref, g_ref, win_ref, cw_ref, wout_ref, o_ref,
                 h_ref, z_ref, act_ref, *, tn, ctx_tiles, ctx_seq, lat_seq):
    tm, d = x_ref.shape
    hl = CONV_HALO
    shift, scale = mod_ref[0, 0:1, :], mod_ref[0, 1:2, :]
    h_ref[0:hl, :] = _modulate(xp_ref[...], g_ref[...], shift, scale).astype(BF16)
    h_ref[hl:hl + tm, :] = _modulate(x_ref[...], g_ref[...], shift, scale).astype(BF16)
    h_ref[hl + tm:, :] = _modulate(xn_ref[...], g_ref[...], shift, scale).astype(BF16)
    i = pl.program_id(0)
    is_ctx = i < ctx_tiles
    row = lax.broadcasted_iota(jnp.int32, (tm, 1), 0)
    lat_row0 = (jnp.maximum(i - ctx_tiles, 0) % (lat_seq // tm)) * tm
    pos = jnp.where(is_ctx, row & (ctx_seq - 1), row + lat_row0)
    seq = jnp.where(is_ctx, ctx_seq, lat_seq)
    has_prev = pos != 0
    has_next = pos != seq - 1
    for c in range(d // tn):
        sl = slice(c * tn, (c + 1) * tn)
        h = h_ref[...]
        cg = _dot(h, win_ref[:, d + c * tn:d + (c + 1) * tn])
        u = _dot(h, win_ref[:, 2 * d + c * tn:2 * d + (c + 1) * tn])
        z_ref[...] = cg * u
        bg = _dot(h_ref[hl:hl + tm, :], win_ref[:, sl])
        z_prev = jnp.where(has_prev, z_ref[hl - 1:hl - 1 + tm, :], 0.0)
        z_next = jnp.where(has_next, z_ref[hl + 1:hl + 1 + tm, :], 0.0)
        conv = z_prev * cw_ref[0:1, sl] + z_ref[hl:hl + tm, :] * cw_ref[1:2, sl] + z_next * cw_ref[2:3, sl]
        act_ref[:, sl] = (bg * conv).astype(BF16)
    o_ref[...] = x_ref[...] + mod_ref[0, 2:3, :] * _dot(act_ref[...], wout_ref[...])


def _conv_call(x, mod, g, w_in, conv_w, w_out, rows, ctx_seq, lat_seq):
    tm, d = rows.tm, x.shape[1]
    hl = CONV_HALO
    assert tm % ctx_seq == 0 and ctx_seq & (ctx_seq - 1) == 0 and lat_seq % tm == 0 and tm % hl == 0
    last_halo = x.shape[0] // hl - 1
    return pl.pallas_call(
        functools.partial(_conv_kernel, tn=2 * LANES, ctx_tiles=rows.ctx_tiles, ctx_seq=ctx_seq,
                          lat_seq=lat_seq),
        grid=(rows.n_tiles,),
        in_specs=[
            pl.BlockSpec((tm, d), lambda i: (i, 0)),
            pl.BlockSpec((hl, d), lambda i: (jnp.maximum(i * (tm // hl) - 1, 0), 0)),
            pl.BlockSpec((hl, d), lambda i: (jnp.minimum((i + 1) * (tm // hl), last_halo), 0)),
            pl.BlockSpec((1, 6, d), lambda i: (rows.group(i), 0, 0)),
            _resident(g.shape, lambda i: (0, 0)),
            _resident(w_in.shape, lambda i: (0, 0)),
            _resident(conv_w.shape, lambda i: (0, 0)),
            _resident(w_out.shape, lambda i: (0, 0)),
        ],
        out_specs=pl.BlockSpec((tm, d), lambda i: (i, 0)),
        out_shape=jax.ShapeDtypeStruct(x.shape, F32),
        scratch_shapes=[pltpu.VMEM((tm + 2 * hl, d), BF16), pltpu.VMEM((tm + 2 * hl, 2 * LANES), F32),
                        pltpu.VMEM((tm, d), BF16)],
        compiler_params=_cparams(("arbitrary",)),
        name="conv_mixer",
    )(x, x, x, mod, g, w_in, conv_w, w_out)


RET_PROJ_CHUNK = 4 * LANES


def _ret_project(x_ref, mod_ref, g_ref, w_ref, qkv_ref, gate_ref):
    tn = RET_PROJ_CHUNK
    h = _modulate(x_ref[...], g_ref[...], mod_ref[0, 0:1, :], mod_ref[0, 1:2, :]).astype(BF16)
    n_qkv = qkv_ref.shape[1]
    for c in range(n_qkv // tn):
        qkv_ref[:, c * tn:(c + 1) * tn] = _dot(h, w_ref[:, c * tn:(c + 1) * tn]).astype(BF16)
    for c in range(gate_ref.shape[1] // tn):
        gate = _dot(h, w_ref[:, n_qkv + c * tn:n_qkv + (c + 1) * tn])
        gate_ref[:, c * tn:(c + 1) * tn] = _silu(gate).astype(gate_ref.dtype)


def _ret_decays(lr_ref, hd, chunk, k_scale):
    row = lax.broadcasted_iota(jnp.int32, (chunk, chunk), 0).astype(F32)
    col = lax.broadcasted_iota(jnp.int32, (chunk, chunk), 1).astype(F32)
    ridx = lax.broadcasted_iota(jnp.int32, (chunk, 1), 0).astype(F32)
    dist = row - col
    log_gamma = -jnp.exp(lr_ref[hd])
    lg_f, lg_b = log_gamma[0:1, :], log_gamma[1:2, :]
    mask = jnp.where(dist > 0, jnp.exp(jnp.maximum(dist, 0.0) * lg_f),
                     jnp.where(dist < 0, jnp.exp(jnp.maximum(-dist, 0.0) * lg_b), 2.0)) * k_scale
    q_decay = (jnp.exp((ridx + 1.0) * lg_f), jnp.exp((chunk - ridx) * lg_b))
    k_decay = (jnp.exp((chunk - 1.0 - ridx) * lg_f) * k_scale, jnp.exp(ridx * lg_b) * k_scale)
    chunk_decay = (jnp.exp(chunk * lg_f), jnp.exp(chunk * lg_b))
    return mask, q_decay, k_decay, chunk_decay


def _ret_scan(decays, q, k, v, gate_ref, gn_ref, s0_ref, y_ref, sout_ref, o_ref, st_ref, *,
              row0, seq, chunk, hps, dk, dv):
    n_chunks = seq // chunk
    has_init = s0_ref is not None
    (q_ref, q0), (k_ref, k0), (v_ref, v0) = q, k, v
    for hd in range(hps):
        mask, q_decay, k_decay, chunk_decay = decays[hd]
        vs = slice(hd * dv, (hd + 1) * dv)
        for direction in range(2):
            order = range(n_chunks) if direction == 0 else range(n_chunks - 1, -1, -1)
            if has_init:
                st_ref[hd] = s0_ref[direction, hd]
            for step, c in enumerate(order):
                rs = slice(row0 + c * chunk, row0 + (c + 1) * chunk)
                ls = slice(c * chunk, (c + 1) * chunk)
                qc = q_ref[rs, q0 + hd * dk:q0 + (hd + 1) * dk]
                kc = k_ref[rs, k0 + hd * dk:k0 + (hd + 1) * dk]
                vc = v_ref[rs, v0 + hd * dv:v0 + (hd + 1) * dv]
                have_state = has_init or step > 0
                if direction == 0:
                    scores = _dot_nt(qc, kc) * mask
                    o_ref[ls, vs] = _dot(scores.astype(BF16), vc)
                if have_state:
                    o_ref[ls, vs] += _dot(qc, st_ref[hd].astype(BF16)) * q_decay[direction]
                kd = (kc.astype(F32) * k_decay[direction]).T.astype(BF16)
                update = _dot(kd, vc)
                if have_state:
                    st_ref[hd] = st_ref[hd] * chunk_decay[direction] + update
                else:
                    st_ref[hd] = update
            if sout_ref is not None:
                sout_ref[direction, hd] = st_ref[hd]

        o = o_ref[:, vs]
        mu = jnp.mean(o, axis=-1, keepdims=True)
        var = jnp.mean(jnp.square(o - mu), axis=-1, keepdims=True)
        on = (o - mu) * lax.rsqrt(var + EPS) * gn_ref[:, vs]
        ys = slice(row0, row0 + seq)
        y_ref[ys, vs] = (gate_ref[ys, vs].astype(F32) * on).astype(y_ref.dtype)


def _ret_ctx_kernel(x_ref, mod_ref, g_ref, w_ref, lr_ref, gn_ref, y_ref, sout_ref,
                    qkv_s, gate_s, o_s, st_s, *, seq, dk, dv):
    nh = RET_HEADS
    _ret_project(x_ref, mod_ref, g_ref, w_ref, qkv_s, gate_s)
    decays = [_ret_decays(lr_ref, hd, seq, dk ** -0.5) for hd in range(nh)]
    for b in range(x_ref.shape[0] // seq):
        _ret_scan(decays, (qkv_s, 0), (qkv_s, nh * dk), (qkv_s, 2 * nh * dk), gate_s, gn_ref, None,
                  y_ref, sout_ref.at[b, 0], o_s, st_s, row0=b * seq, seq=seq, chunk=seq,
                  hps=nh, dk=dk, dv=dv)


def _ret_ctx_call(x, mod, g, w, log_rate, gn_g, rows, seq, n_gate):
    tm, d = rows.tm, x.shape[1]
    assert tm % seq == 0 and seq <= 2 * LANES
    nh = RET_HEADS
    n_qkv = w.shape[1] - n_gate
    dv = n_gate // nh
    dk = (n_qkv - n_gate) // (2 * nh)
    nb = tm // seq
    m = rows.ctx_tiles * tm
    return pl.pallas_call(
        functools.partial(_ret_ctx_kernel, seq=seq, dk=dk, dv=dv),
        grid=(rows.ctx_tiles,),
        in_specs=[
            pl.BlockSpec((tm, d), lambda i: (i, 0)),
            pl.BlockSpec((1, 6, d), lambda i: (0, 0, 0)),
            _resident(g.shape, lambda i: (0, 0)),
            _resident(w.shape, lambda i: (0, 0)),
            _resident(log_rate.shape, lambda i: (0, 0, 0)),
            _resident(gn_g.shape, lambda i: (0, 0)),
        ],
        out_specs=[pl.BlockSpec((tm, n_gate), lambda i: (i, 0)),
                   pl.BlockSpec((nb, 1, 2, nh, dk, dv), lambda i: (i, 0, 0, 0, 0, 0))],
        out_shape=[jax.ShapeDtypeStruct((m, n_gate), BF16),
                   jax.ShapeDtypeStruct((m // seq, 1, 2, nh, dk, dv), F32)],
        scratch_shapes=[pltpu.VMEM((tm, n_qkv), BF16), pltpu.VMEM((tm, n_gate), BF16),
                        pltpu.VMEM((seq, n_gate), F32), pltpu.VMEM((nh, dk, dv), F32)],
        compiler_params=_cparams(("arbitrary",)),
        name="ret_ctx",
    )(x, mod, g, w, log_rate, gn_g)


RET_LAT_ROWS = 512


def _ret_lat_kernel(x_ref, mod_ref, g_ref, wq_ref, wk_ref, wv_ref, wg_ref, lr_ref, gn_ref, s0_ref,
                    y_ref, h_s, q_s, k_s, v_s, gate_s, o_s, st_s, *, chunk, dk, dv):
    seq = x_ref.shape[0]

    @pl.when(pl.program_id(1) == 0)
    def _():
        h_s[...] = _modulate(x_ref[...], g_ref[...], mod_ref[0, 0:1, :], mod_ref[0, 1:2, :]).astype(BF16)

    rc = min(RET_LAT_ROWS, seq)
    for r in range(seq // rc):
        rs = slice(r * rc, (r + 1) * rc)
        h = h_s[rs, :]
        q_s[rs, :] = _dot(h, wq_ref[...]).astype(BF16)
        k_s[rs, :] = _dot(h, wk_ref[...]).astype(BF16)
        v_s[rs, :] = _dot(h, wv_ref[...]).astype(BF16)
        gate_s[rs, :] = _silu(_dot(h, wg_ref[...])).astype(BF16)
    decays = [_ret_decays(lr_ref, 0, chunk, dk ** -0.5)]
    _ret_scan(decays, (q_s, 0), (k_s, 0), (v_s, 0), gate_s, gn_ref, s0_ref.at[0], y_ref, None,
              o_s, st_s, row0=0, seq=seq, chunk=chunk, hps=1, dk=dk, dv=dv)


def _ret_lat_call(x, mod, g, w, log_rate, gn_g, s0, *, bsz, seq, row0, n_gate):
    d = x.shape[1]
    nh = RET_HEADS
    n_qkv = w.shape[1] - n_gate
    dv = n_gate // nh
    dk = (n_qkv - n_gate) // (2 * nh)
    chunk = min(seq, 2 * LANES)
    assert row0 % seq == 0 and dv % dk == 0
    b0 = row0 // seq
    return pl.pallas_call(
        functools.partial(_ret_lat_kernel, chunk=chunk, dk=dk, dv=dv),
        grid=(bsz, nh),
        in_specs=[
            pl.BlockSpec((seq, d), lambda b, h: (b0 + b, 0)),
            pl.BlockSpec((1, 6, d), lambda b, h: (1 + b, 0, 0)),
            _resident(g.shape, lambda b, h: (0, 0)),
            pl.BlockSpec((d, dk), lambda b, h: (0, h)),
            pl.BlockSpec((d, dk), lambda b, h: (0, nh + h)),
            pl.BlockSpec((d, dv), lambda b, h: (0, (2 * nh * dk) // dv + h)),
            pl.BlockSpec((d, dv), lambda b, h: (0, n_qkv // dv + h)),
            pl.BlockSpec((1, 2, 1), lambda b, h: (h, 0, 0)),
            pl.BlockSpec((1, dv), lambda b, h: (0, h)),
            pl.BlockSpec((1, 2, 1, dk, dv), lambda b, h: (b, 0, h, 0, 0)),
        ],
        out_specs=pl.BlockSpec((seq, dv), lambda b, h: (b, h)),
        out_shape=jax.ShapeDtypeStruct((bsz * seq, n_gate), BF16),
        scratch_shapes=[pltpu.VMEM((seq, d), BF16), pltpu.VMEM((seq, dk), BF16), pltpu.VMEM((seq, dk), BF16),
                        pltpu.VMEM((seq, dv), BF16), pltpu.VMEM((seq, dv), BF16),
                        pltpu.VMEM((seq, dv), F32), pltpu.VMEM((1, dk, dv), F32)],
        compiler_params=_cparams(("arbitrary", "arbitrary")),
        name="ret_lat",
    )(x, mod, g, w, w, w, w, log_rate, gn_g, s0)


def _rope_swap_index():
    f = ROPE_AXIS_FREQS
    idx = jnp.arange(MLA_ROPE)
    return jnp.where((idx // f) % 2 == 0, idx + f, idx - f)


def _mla_weights(w_a, q_norm_g, kv_norm_g, w_q_b, w_kv_b):
    swap = _rope_swap_index()
    nq = MLA_Q_RANK + MLA_KV_RANK
    w_kpe = w_a[:, nq:]
    w_kpe_sw = w_kpe[:, swap]
    wa = jnp.concatenate([w_a[:, :nq], w_kpe, w_kpe, w_kpe_sw, w_kpe_sw], axis=1).astype(BF16)
    wq = w_q_b.reshape(MLA_Q_RANK, MLA_HEADS, MLA_NOPE + MLA_ROPE)
    wq_nope = wq[:, :, :MLA_NOPE].reshape(MLA_Q_RANK, MLA_HEADS * MLA_NOPE)
    wq_pe = wq[:, :, MLA_NOPE:]
    wq_all = jnp.concatenate(
        [wq_nope, wq_pe.reshape(MLA_Q_RANK, -1), wq_pe[:, :, swap].reshape(MLA_Q_RANK, -1)],
        axis=1).astype(BF16)
    wkv = w_kv_b.reshape(MLA_KV_RANK, MLA_HEADS, MLA_NOPE + MLA_V)
    wkv_all = jnp.concatenate(
        [wkv[:, :, :MLA_NOPE].reshape(MLA_KV_RANK, -1), wkv[:, :, MLA_NOPE:].reshape(MLA_KV_RANK, -1)],
        axis=1).astype(BF16)
    return wa, q_norm_g[None, :], kv_norm_g[None, :], wq_all, wkv_all


def _rope_tables(n_tokens):
    f = ROPE_AXIS_FREQS
    f32 = np.float32
    rows = n_tokens // GRID_W
    r = np.repeat(np.arange(rows, dtype=f32), GRID_W)
    col = np.tile(np.arange(GRID_W, dtype=f32), rows)
    inv = (f32(ROPE_THETA) ** (-np.arange(f, dtype=f32) / f32(f))).astype(f32)
    ang_r, ang_c = r[:, None] * inv, col[:, None] * inv
    cos = np.concatenate([np.cos(ang_r)] * 2 + [np.cos(ang_c)] * 2, axis=1)
    sin = np.concatenate([-np.sin(ang_r), np.sin(ang_r), -np.sin(ang_c), np.sin(ang_c)], axis=1)
    reps = LANES // MLA_ROPE
    return (jnp.asarray(np.concatenate([cos] * reps, axis=1), F32),
            jnp.asarray(np.concatenate([sin] * reps, axis=1), F32))


def kernel(x_prompt, x_sample, c, c_ctx, cache_mla_ckv, cache_mla_kpe, state_ret, ada_w, ada_b, norm_mix_g, norm_ffn_g, mla_w_a, mla_q_norm_g, mla_kv_norm_g, mla_w_q_b, mla_w_kv_b, mla_w_o, conv_w_in, conv_w, conv_w_out, ret_w_in, ret_log_rate, ret_gn_g, ret_w_out, ffn_w_in, ffn_w_out, final_norm_g):
    batch, seq, d = x_prompt.shape
    dec_batch, dec_seq, _ = x_sample.shape
    depth = ada_w.shape[0]
    n_ctx = batch * seq
    n_lat = dec_batch * dec_seq
    past = cache_mla_ckv.shape[2]
    assert 1 + dec_batch <= MOD_ROWS

    cvec = jnp.zeros((MOD_ROWS, d), F32).at[0].set(c_ctx).at[1:1 + dec_batch].set(c)
    mod_all = _ada_call(cvec, ada_w, ada_b).reshape(depth, MOD_ROWS, 6, d)

    rows_s = _Rows(n_ctx, dec_batch, dec_seq, min(ROW_TILE_FUSED, dec_seq))
    rows_l = _Rows(n_ctx, dec_batch, dec_seq, min(ROW_TILE, dec_seq))
    rope_tabs = _rope_tables(dec_seq)
    final_g = final_norm_g[None, :]
    mla_w_o_bf = mla_w_o.astype(BF16)

    def mixer_cast_jobs(layer):
        if layer >= depth:
            return []
        kind, j = layer % N_MIXERS, layer // N_MIXERS
        if kind == 1:
            return [(conv_w_in, j), (conv_w_out, j)]
        if kind == 2:
            return [(ret_w_in, j), (ret_w_out, j)]
        return []

    x = (x_prompt.reshape(n_ctx, d), x_sample.reshape(n_lat, d))
    new_ckv, new_kpe, new_ret = [], [], []
    ffn_bf = mixer_bf = None
    for i in range(depth):
        kind, j = i % N_MIXERS, i // N_MIXERS
        mod = mod_all[i]
        g_mix = norm_mix_g[i][None, :]
        if ffn_bf is None:
            assert kind == 0
        if kind == 0:
            wts = _mla_weights(mla_w_a[j], mla_q_norm_g[j], mla_kv_norm_g[j], mla_w_q_b[j], mla_w_kv_b[j])
            split = isinstance(x, tuple)
            o_c, ckv_c, kpe_c = _mla_ctx_call(x[0] if split else x, mod, g_mix, wts, rows_l, seq)
            ql, kl, vl = _mla_proj_call(
                x[1] if split else x, 0 if split else rows_l.ctx_tiles, mod, g_mix, wts, rows_l,
                tile0=rows_l.ctx_tiles, n_tiles=rows_l.n_tiles - rows_l.ctx_tiles, rope_tabs=rope_tabs)
            new_ckv.append(ckv_c.reshape(batch, seq, MLA_KV_RANK))
            new_kpe.append(kpe_c.reshape(batch, seq, MLA_ROPE))
            kpe_rep = jnp.concatenate([cache_mla_kpe[:, j]] * (LANES // MLA_ROPE), axis=-1)
            kp, vp = _cache_expand_call(
                cache_mla_ckv[:, j].reshape(dec_batch * past, MLA_KV_RANK),
                kpe_rep.reshape(dec_batch * past, LANES), wts[4])
            jobs = [] if ffn_bf is not None else [(ffn_w_in, i), (ffn_w_out, i)]
            o_l, cast = _attn_call(ql, [kp, kl], [vp, vl], n_batch=dec_batch, seq=dec_seq,
                                   sks=[past, dec_seq], nb=1, tq=min(ATTN_Q_TILE, dec_seq),
                                   name="attn_lat", cast_jobs=jobs)
            if jobs:
                ffn_bf = cast
            mix = (o_c, o_l, mla_w_o_bf, j)
        elif kind == 1:
            mix = None
            x = _conv_call(x, mod, g_mix, mixer_bf[0][0], conv_w[j], mixer_bf[1][0], rows_l, seq, dec_seq)
        else:
            n_gate = ret_w_out.shape[1]
            lr = ret_log_rate[j].T[:, :, None]
            gn = ret_gn_g[j][None, :]
            y_c, st = _ret_ctx_call(x, mod, g_mix, mixer_bf[0][0], lr, gn, rows_s, seq, n_gate)
            y_l = _ret_lat_call(x, mod, g_mix, mixer_bf[0][0], lr, gn, state_ret[:, j],
                                bsz=dec_batch, seq=dec_seq, row0=n_ctx, n_gate=n_gate)
            new_ret.append(st)
            mix = (y_c, y_l, mixer_bf[1], 0)
        last = i == depth - 1
        jobs = [] if last else [(ffn_w_in, i + 1), (ffn_w_out, i + 1)] + mixer_cast_jobs(i + 1)
        x, cast = _ffn_call(x, mod, norm_ffn_g[i][None, :], ffn_bf[0], ffn_bf[1], 0, final_g,
                            rows_l if mix is None else rows_s, final=last, cast_jobs=jobs, mix=mix)
        ffn_bf, mixer_bf = cast[:2], cast[2:]

    y_prompt = x[0].reshape(batch, seq, d)
    y_sample = x[1].reshape(dec_batch, dec_seq, d)
    return (y_prompt, y_sample, jnp.stack(new_ckv, axis=1), jnp.stack(new_kpe, axis=1),
            jnp.concatenate(new_ret, axis=1))
```

```python
import functools
import math

import jax
import jax.numpy as jnp
import numpy as np
from jax import lax
from jax.experimental import pallas as pl
from jax.experimental.pallas import tpu as pltpu

F32 = jnp.float32
BF16 = jnp.bfloat16

N_MIXERS = 3
MLA_HEADS = 8
MLA_NOPE = 128
MLA_ROPE = 64
MLA_V = 128
MLA_Q_RANK = 384
MLA_KV_RANK = 256
MLA_SCALE = (MLA_NOPE + MLA_ROPE) ** -0.5
MLA_Q_SCALE = MLA_SCALE * 1.4426950408889634
ROPE_THETA = 10000.0
ROPE_AXIS_FREQS = MLA_ROPE // 4
GRID_W = 64
RET_HEADS = 4
EPS = 1e-6

LANES = 128
MLA_HEAD_PAD = 2 * LANES
MOD_ROWS = 8
VMEM_LIMIT = 56 * 1024 * 1024

ROW_TILE = 1024
ROW_TILE_FUSED = 512
ATTN_Q_TILE = 512


def _cparams(sem):
    return pltpu.CompilerParams(dimension_semantics=sem, vmem_limit_bytes=VMEM_LIMIT)


def _resident(shape, index_map):
    return pl.BlockSpec(shape, index_map, pipeline_mode=pl.Buffered(1))


def _rms(x, g):
    return x * lax.rsqrt(jnp.mean(x * x, axis=-1, keepdims=True) + EPS) * g


def _modulate(x, g, shift, scale):
    return _rms(x, g) * (1.0 + scale) + shift


def _silu(x):
    return x * jax.nn.sigmoid(x)


def _dot(a, b):
    return jnp.dot(a, b, preferred_element_type=F32)


def _dot_nt(a, b):
    return lax.dot_general(a, b, (((1,), (1,)), ((), ())), preferred_element_type=F32)


MAX_CAST_SLABS = 16


def _cast_specs(jobs, n_steps, step_of):
    n_slabs = math.gcd(n_steps, MAX_CAST_SLABS)
    per = n_steps // n_slabs

    def slab_of_step(*g):
        return step_of(*g) // per

    in_specs, out_specs, out_shape = [], [], []
    for w, layer in jobs:
        _, k, n = w.shape
        blk = (1, k // n_slabs, n)
        in_specs.append(pl.BlockSpec(blk, lambda *g, layer=layer: (layer, slab_of_step(*g), 0)))
        out_specs.append(pl.BlockSpec(blk, lambda *g: (0, slab_of_step(*g), 0)))
        out_shape.append(jax.ShapeDtypeStruct((1, k, n), BF16))
    return in_specs, out_specs, out_shape


def _run_casts(in_refs, out_refs):
    for src, dst in zip(in_refs, out_refs):
        dst[...] = src[...].astype(BF16)


def _ada_kernel(c_ref, w_ref, b_ref, o_ref):
    a = _silu(c_ref[...]).astype(BF16)
    o_ref[0] = _dot(a, w_ref[0].astype(BF16)) + b_ref[0]


def _ada_call(cvec, ada_w, ada_b):
    depth, d, n = ada_w.shape
    tn = n // 2
    return pl.pallas_call(
        _ada_kernel,
        grid=(depth, n // tn),
        in_specs=[
            pl.BlockSpec((MOD_ROWS, d), lambda l, j: (0, 0)),
            pl.BlockSpec((1, d, tn), lambda l, j: (l, 0, j)),
            pl.BlockSpec((1, 1, tn), lambda l, j: (l, 0, j)),
        ],
        out_specs=pl.BlockSpec((1, MOD_ROWS, tn), lambda l, j: (l, 0, j)),
        out_shape=jax.ShapeDtypeStruct((depth, MOD_ROWS, n), F32),
        compiler_params=_cparams(("arbitrary", "arbitrary")),
        name="ada",
    )(cvec, ada_w, ada_b.reshape(depth, 1, n))


class _Rows:
    def __init__(self, n_ctx, dec_batch, dec_seq, tm):
        assert n_ctx % tm == 0 and dec_seq % tm == 0
        self.tm = tm
        self.ctx_tiles = n_ctx // tm
        self.seq_tiles = dec_seq // tm
        self.n_tiles = self.ctx_tiles + dec_batch * self.seq_tiles

    def group(self, i):
        lat = jnp.maximum(i - self.ctx_tiles, 0) // self.seq_tiles
        return jnp.where(i < self.ctx_tiles, 0, 1 + lat)


def _mla_project(x_ref, mod_ref, g_ref, wa_ref, qg_ref, kvg_ref, wq_ref, wkv_ref, rope_refs,
                 q_ref, k_ref, v_ref, cache_refs):
    h = _modulate(x_ref[...], g_ref[...], mod_ref[0, 0:1, :], mod_ref[0, 1:2, :]).astype(BF16)
    a = _dot(h, wa_ref[0])
    nq = MLA_Q_RANK + MLA_KV_RANK
    ckv_n = _rms(a[:, MLA_Q_RANK:nq], kvg_ref[0])
    kpe = a[:, nq:nq + LANES]
    if cache_refs is not None:
        cache_refs[0][...] = ckv_n
        cache_refs[1][...] = kpe[:, :MLA_ROPE]
    if rope_refs is not None:
        cos, sin = rope_refs[0][...], rope_refs[1][...]
        kpe = kpe * cos + a[:, nq + LANES:nq + 2 * LANES] * sin
    kpe = kpe.astype(BF16)

    qn = _rms(a[:, :MLA_Q_RANK], qg_ref[0]).astype(BF16)
    nh = MLA_HEADS
    n_nope, n_pe = nh * MLA_NOPE, nh * MLA_ROPE
    q_nope = _dot(qn, wq_ref[0, :, :n_nope])
    q_pe = _dot(qn, wq_ref[0, :, n_nope:n_nope + n_pe])
    if rope_refs is not None:
        q_sw = _dot(qn, wq_ref[0, :, n_nope + n_pe:n_nope + 2 * n_pe])
        cos, sin = cos * MLA_Q_SCALE, sin * MLA_Q_SCALE
    kv = _dot(ckv_n.astype(BF16), wkv_ref[0])
    low_half = lax.broadcasted_iota(jnp.int32, (1, LANES), 1) < MLA_ROPE
    for pair in range(nh // 2):
        ps = slice(pair * LANES, (pair + 1) * LANES)
        if rope_refs is not None:
            pe = q_pe[:, ps] * cos + q_sw[:, ps] * sin
        else:
            pe = q_pe[:, ps] * MLA_Q_SCALE
        for hd, keep in ((2 * pair, low_half), (2 * pair + 1, jnp.logical_not(low_half))):
            lo = hd * MLA_HEAD_PAD
            hs = slice(hd * LANES, (hd + 1) * LANES)
            q_ref[:, lo:lo + LANES] = (q_nope[:, hs] * MLA_Q_SCALE).astype(BF16)
            q_ref[:, lo + LANES:lo + 2 * LANES] = jnp.where(keep, pe, 0.0).astype(BF16)
            k_ref[:, lo:lo + LANES] = kv[:, hs].astype(BF16)
            k_ref[:, lo + LANES:lo + 2 * LANES] = kpe
    v_ref[...] = kv[:, nh * LANES:].astype(BF16)


def _mla_proj_kernel(x_ref, mod_ref, g_ref, wa_ref, qg_ref, kvg_ref, wq_ref, wkv_ref, ck_ref, sk_ref,
                     q_ref, k_ref, v_ref):
    _mla_project(x_ref, mod_ref, g_ref, wa_ref, qg_ref, kvg_ref, wq_ref, wkv_ref, (ck_ref, sk_ref),
                 q_ref, k_ref, v_ref, None)


def _layer_specs(weights, layer):
    return [_resident((1,) + w.shape[1:], lambda *_: (layer, 0, 0)) for w in weights]


def _mla_proj_call(x, x_tile0, mod, g, wts, layer, rows, *, tile0, n_tiles, rope_tabs):
    tm, d = rows.tm, x.shape[1]
    wa, qg, kvg, wq, wkv = wts
    in_specs = [
        pl.BlockSpec((tm, d), lambda i: (i + x_tile0, 0)),
        pl.BlockSpec((1, 6, d), lambda i: (rows.group(i + tile0), 0, 0)),
        _resident(g.shape, lambda i: (0, 0)),
    ] + _layer_specs(wts, layer)
    in_specs += [pl.BlockSpec((tm, LANES), lambda i: (i % rows.seq_tiles, 0)) for _ in rope_tabs]
    m = n_tiles * tm
    hp, hv = MLA_HEADS * MLA_HEAD_PAD, MLA_HEADS * MLA_V
    return pl.pallas_call(
        _mla_proj_kernel,
        grid=(n_tiles,),
        in_specs=in_specs,
        out_specs=[pl.BlockSpec((tm, hp), lambda i: (i, 0)), pl.BlockSpec((tm, hp), lambda i: (i, 0)),
                   pl.BlockSpec((tm, hv), lambda i: (i, 0))],
        out_shape=[jax.ShapeDtypeStruct((m, hp), BF16), jax.ShapeDtypeStruct((m, hp), BF16),
                   jax.ShapeDtypeStruct((m, hv), BF16)],
        compiler_params=_cparams(("arbitrary",)),
        name="mla_proj_lat",
    )(x, mod, g, wa, qg, kvg, wq, wkv, *rope_tabs)


def _mla_ctx_kernel(x_ref, mod_ref, g_ref, wa_ref, qg_ref, kvg_ref, wq_ref, wkv_ref,
                    o_ref, ckv_ref, kpe_ref, q_s, k_s, v_s, s_ref, *, seq):
    _mla_project(x_ref, mod_ref, g_ref, wa_ref, qg_ref, kvg_ref, wq_ref, wkv_ref, None,
                 q_s, k_s, v_s, (ckv_ref, kpe_ref))
    _attend(q_s, [k_s], [v_s], o_ref, s_ref, nb=x_ref.shape[0] // seq, tq=seq, sks=(seq,))


def _mla_ctx_call(x, mod, g, wts, layer, rows, seq):
    tm, d = rows.tm, x.shape[1]
    assert tm % seq == 0
    wa, qg, kvg, wq, wkv = wts
    m = rows.ctx_tiles * tm
    hp, hv = MLA_HEADS * MLA_HEAD_PAD, MLA_HEADS * MLA_V
    return pl.pallas_call(
        functools.partial(_mla_ctx_kernel, seq=seq),
        grid=(rows.ctx_tiles,),
        in_specs=[
            pl.BlockSpec((tm, d), lambda i: (i, 0)),
            pl.BlockSpec((1, 6, d), lambda i: (0, 0, 0)),
            _resident(g.shape, lambda i: (0, 0)),
        ] + _layer_specs(wts, layer),
        out_specs=[pl.BlockSpec((tm, hv), lambda i: (i, 0)),
                   pl.BlockSpec((tm, MLA_KV_RANK), lambda i: (i, 0)),
                   pl.BlockSpec((tm, MLA_ROPE), lambda i: (i, 0))],
        out_shape=[jax.ShapeDtypeStruct((m, hv), BF16),
                   jax.ShapeDtypeStruct((m, MLA_KV_RANK), F32),
                   jax.ShapeDtypeStruct((m, MLA_ROPE), F32)],
        scratch_shapes=[pltpu.VMEM((tm, hp), BF16), pltpu.VMEM((tm, hp), BF16), pltpu.VMEM((tm, hv), BF16),
                        pltpu.VMEM((2, seq, seq), F32)],
        compiler_params=_cparams(("arbitrary",)),
        name="mla_ctx",
    )(x, mod, g, wa, qg, kvg, wq, wkv)


def _cache_expand(ckv_ref, kpe_ref, wkv_ref, k_ref, v_ref):
    kv = _dot(ckv_ref[...].astype(BF16), wkv_ref[0])
    kpe = kpe_ref[...].astype(BF16)
    nh = MLA_HEADS
    for hd in range(nh):
        lo = hd * MLA_HEAD_PAD
        k_ref[:, lo:lo + LANES] = kv[:, hd * LANES:(hd + 1) * LANES].astype(BF16)
        k_ref[:, lo + LANES:lo + 2 * LANES] = kpe
    v_ref[...] = kv[:, nh * LANES:].astype(BF16)


def _attend(q_ref, k_refs, v_refs, o_ref, s_ref, *, nb, tq, sks):
    problems = [(b, h) for b in range(nb) for h in range(MLA_HEADS)]
    seg_cols = [sum(sks[:i]) for i in range(len(sks))]

    def scores(idx):
        b, h = problems[idx]
        q = q_ref[b * tq:(b + 1) * tq, h * MLA_HEAD_PAD:(h + 1) * MLA_HEAD_PAD]
        m_lane = None
        for k_ref, sk, c0 in zip(k_refs, sks, seg_cols):
            s = _dot_nt(q, k_ref[b * sk:(b + 1) * sk, h * MLA_HEAD_PAD:(h + 1) * MLA_HEAD_PAD])
            s_ref[idx % 2, :, c0:c0 + sk] = s
            for c in range(sk // LANES):
                piece = s[:, c * LANES:(c + 1) * LANES]
                m_lane = piece if m_lane is None else jnp.maximum(m_lane, piece)
        return m_lane.max(axis=-1, keepdims=True)

    def weighted_values(idx, m):
        b, h = problems[idx]
        acc = None
        for v_ref, sk, c0 in zip(v_refs, sks, seg_cols):
            p = jnp.exp2(s_ref[idx % 2, :, c0:c0 + sk] - m).astype(BF16)
            v = v_ref[b * sk:(b + 1) * sk, h * MLA_V:(h + 1) * MLA_V]
            part = _dot(p, jnp.concatenate([v, jnp.ones_like(v)], axis=1))
            acc = part if acc is None else acc + part
        o = acc[:, :MLA_V] / acc[:, MLA_V:]
        o_ref[b * tq:(b + 1) * tq, h * MLA_V:(h + 1) * MLA_V] = o.astype(o_ref.dtype)

    m = scores(0)
    for idx in range(len(problems)):
        m_next = scores(idx + 1) if idx + 1 < len(problems) else None
        weighted_values(idx, m)
        m = m_next


def _attn_lat_kernel(*refs, tq, past, seq, n_cast):
    q_ref, k_ref, v_ref, ckv_ref, kpe_ref, wkv_ref = refs[:6]
    n_in = 6 + n_cast
    o_ref = refs[n_in]
    kp_s, vp_s, s_ref = refs[n_in + 1 + n_cast:]
    _run_casts(refs[6:n_in], refs[n_in + 1:n_in + 1 + n_cast])

    @pl.when(pl.program_id(1) == 0)
    def _():
        _cache_expand(ckv_ref, kpe_ref, wkv_ref, kp_s, vp_s)

    _attend(q_ref, [kp_s, k_ref], [vp_s, v_ref], o_ref, s_ref, nb=1, tq=tq, sks=(past, seq))


def _attn_lat_call(q, k, v, cache_ckv, cache_kpe, wkv, layer, *, n_batch, seq, past, tq, cast_jobs=()):
    tiles = seq // tq
    hp, hv = MLA_HEADS * MLA_HEAD_PAD, MLA_HEADS * MLA_V
    c_in, c_out, c_shape = _cast_specs(cast_jobs, n_batch * tiles, lambda b, t: b * tiles + t)
    outs = pl.pallas_call(
        functools.partial(_attn_lat_kernel, tq=tq, past=past, seq=seq, n_cast=len(cast_jobs)),
        grid=(n_batch, tiles),
        in_specs=[
            pl.BlockSpec((tq, hp), lambda b, t: (b * tiles + t, 0)),
            pl.BlockSpec((seq, hp), lambda b, t: (b, 0)),
            pl.BlockSpec((seq, hv), lambda b, t: (b, 0)),
            pl.BlockSpec((past, MLA_KV_RANK), lambda b, t: (b, 0)),
            pl.BlockSpec((past, LANES), lambda b, t: (b, 0)),
        ] + _layer_specs([wkv], layer) + c_in,
        out_specs=[pl.BlockSpec((tq, hv), lambda b, t: (b * tiles + t, 0))] + c_out,
        out_shape=[jax.ShapeDtypeStruct((n_batch * seq, hv), BF16)] + c_shape,
        scratch_shapes=[pltpu.VMEM((past, hp), BF16), pltpu.VMEM((past, hv), BF16),
                        pltpu.VMEM((2, tq, past + seq), F32)],
        compiler_params=_cparams(("arbitrary", "arbitrary")),
        name="attn_lat",
    )(q, k, v, cache_ckv, cache_kpe, wkv, *[w for w, _ in cast_jobs])
    return outs[0], list(outs[1:])


def _split_specs(rows, width):
    ct = rows.ctx_tiles
    return [pl.BlockSpec((rows.tm, width), lambda i: (jnp.minimum(i, ct - 1), 0)),
            pl.BlockSpec((rows.tm, width), lambda i: (jnp.maximum(i - ct, 0), 0))]


def _ffn_kernel(*refs, th, final, ctx_tiles, n_cast, n_x, mixed):
    it = iter(refs)
    x_refs = [next(it) for _ in range(n_x)]
    if mixed:
        yc_ref, yl_ref, wo_ref = next(it), next(it), next(it)
    mod_ref, g_ref, win_ref, wout_ref, fg_ref = (next(it) for _ in range(5))
    cast_in = [next(it) for _ in range(n_cast)]
    out_refs = [next(it) for _ in range(2 if final else 1)]
    cast_out = [next(it) for _ in range(n_cast)]
    scratch = list(it)
    act_ref = scratch[0]
    _run_casts(cast_in, cast_out)
    is_ctx = pl.program_id(0) < ctx_tiles
    if mixed:
        x1_ref = scratch[1]

        def mix(y_ref, x_ref):
            x1_ref[...] = x_ref[...] + mod_ref[0, 2:3, :] * _dot(y_ref[...], wo_ref[0])

        pl.when(is_ctx)(lambda: mix(yc_ref, x_refs[0]))
        pl.when(jnp.logical_not(is_ctx))(lambda: mix(yl_ref, x_refs[-1]))
        x = x1_ref[...]
    else:
        x = x_refs[0][...]
    h = _modulate(x, g_ref[...], mod_ref[0, 3:4, :], mod_ref[0, 4:5, :]).astype(BF16)
    hidden = wout_ref.shape[1]
    for c in range(hidden // th):
        a = _dot(h, win_ref[0, :, c * th:(c + 1) * th])
        b = _dot(h, win_ref[0, :, hidden + c * th:hidden + (c + 1) * th])
        act_ref[:, c * th:(c + 1) * th] = (_silu(a) * b).astype(BF16)
    y = x + mod_ref[0, 5:6, :] * _dot(act_ref[...], wout_ref[0])
    if not final:
        out_refs[0][...] = y
    else:
        acc_ref = scratch[-1]
        acc_ref[...] = _rms(y, fg_ref[...])

        @pl.when(is_ctx)
        def _():
            out_refs[0][...] = acc_ref[...]

        @pl.when(jnp.logical_not(is_ctx))
        def _():
            out_refs[1][...] = acc_ref[...]


def _ffn_call(x, mod, g, w_in_all, w_out_all, layer, final_g, rows, final, cast_jobs=(), mix=None):
    tm, d = rows.tm, w_in_all.shape[1]
    hidden = w_out_all.shape[1]
    split_x = isinstance(x, tuple)
    assert not split_x or mix is not None
    xs = list(x) if split_x else [x]
    x_specs = _split_specs(rows, d) if split_x else [pl.BlockSpec((tm, d), lambda i: (i, 0))]
    mix_args, mix_specs = [], []
    scratch = [pltpu.VMEM((tm, hidden), BF16)]
    if mix is not None:
        y_ctx, y_lat, w_o_all, w_o_layer = mix
        k = w_o_all.shape[1]
        mix_args = [y_ctx, y_lat, w_o_all]
        mix_specs = _split_specs(rows, k) + [_resident((1, k, d), lambda i: (w_o_layer, 0, 0))]
        scratch.append(pltpu.VMEM((tm, d), F32))
    if final:
        out_specs = _split_specs(rows, d)
        out_shape = [jax.ShapeDtypeStruct((rows.ctx_tiles * tm, d), F32),
                     jax.ShapeDtypeStruct(((rows.n_tiles - rows.ctx_tiles) * tm, d), F32)]
        scratch.append(pltpu.VMEM((tm, d), F32))
    else:
        out_specs = [pl.BlockSpec((tm, d), lambda i: (i, 0))]
        out_shape = [jax.ShapeDtypeStruct((rows.n_tiles * tm, d), F32)]
    c_in, c_out, c_shape = _cast_specs(cast_jobs, rows.n_tiles, lambda i: i)
    outs = pl.pallas_call(
        functools.partial(_ffn_kernel, th=2 * LANES, final=final, ctx_tiles=rows.ctx_tiles,
                          n_cast=len(cast_jobs), n_x=len(xs), mixed=mix is not None),
        grid=(rows.n_tiles,),
        in_specs=x_specs + mix_specs + [
            pl.BlockSpec((1, 6, d), lambda i: (rows.group(i), 0, 0)),
            _resident(g.shape, lambda i: (0, 0)),
            _resident((1, d, 2 * hidden), lambda i: (layer, 0, 0)),
            _resident((1, hidden, d), lambda i: (layer, 0, 0)),
            _resident(final_g.shape, lambda i: (0, 0)),
        ] + c_in,
        out_specs=out_specs + c_out,
        out_shape=out_shape + c_shape,
        scratch_shapes=scratch,
        compiler_params=_cparams(("arbitrary",)),
        name=("ffn_final" if final else "ffn") + ("_mix" if mix is not None else ""),
    )(*xs, *mix_args, mod, g, w_in_all, w_out_all, final_g, *[w for w, _ in cast_jobs])
    n_out = 2 if final else 1
    stream = tuple(outs[:2]) if final else outs[0]
    return stream, list(outs[n_out:])


CONV_HALO = 16


def _conv_kernel(x_ref, xp_ref, xn_ref, mod_ref, g_ref, win_ref, cw_ref, wout_ref, o_ref,
                 h_ref, z_ref, act_ref, *, tn, ctx_tiles, ctx_seq, lat_seq):
    tm, d = x_ref.shape
    hl = CONV_HALO
    shift, scale = mod_ref[0, 0:1, :], mod_ref[0, 1:2, :]
    h_ref[0:hl, :] = _modulate(xp_ref[...], g_ref[...], shift, scale).astype(BF16)
    h_ref[hl:hl + tm, :] = _modulate(x_ref[...], g_ref[...], shift, scale).astype(BF16)
    h_ref[hl + tm:, :] = _modulate(xn_ref[...], g_ref[...], shift, scale).astype(BF16)
    i = pl.program_id(0)
    is_ctx = i < ctx_tiles
    row = lax.broadcasted_iota(jnp.int32, (tm, 1), 0)
    lat_row0 = (jnp.maximum(i - ctx_tiles, 0) % (lat_seq // tm)) * tm
    pos = jnp.where(is_ctx, row & (ctx_seq - 1), row + lat_row0)
    seq = jnp.where(is_ctx, ctx_seq, lat_seq)
    has_prev = pos != 0
    has_next = pos != seq - 1
    for c in range(d // tn):
        sl = slice(c * tn, (c + 1) * tn)
        h = h_ref[...]
        cg = _dot(h, win_ref[:, d + c * tn:d + (c + 1) * tn])
        u = _dot(h, win_ref[:, 2 * d + c * tn:2 * d + (c + 1) * tn])
        z_ref[...] = cg * u
        bg = _dot(h_ref[hl:hl + tm, :], win_ref[:, sl])
        z_prev = jnp.where(has_prev, z_ref[hl - 1:hl - 1 + tm, :], 0.0)
        z_next = jnp.where(has_next, z_ref[hl + 1:hl + 1 + tm, :], 0.0)
        conv = z_prev * cw_ref[0:1, sl] + z_ref[hl:hl + tm, :] * cw_ref[1:2, sl] + z_next * cw_ref[2:3, sl]
        act_ref[:, sl] = (bg * conv).astype(BF16)
    o_ref[...] = x_ref[...] + mod_ref[0, 2:3, :] * _dot(act_ref[...], wout_ref[...])


def _conv_call(x, mod, g, w_in, conv_w, w_out, rows, ctx_seq, lat_seq):
    tm, d = rows.tm, x.shape[1]
    hl = CONV_HALO
    assert tm % ctx_seq == 0 and ctx_seq & (ctx_seq - 1) == 0 and lat_seq % tm == 0 and tm % hl == 0
    last_halo = x.shape[0] // hl - 1
    return pl.pallas_call(
        functools.partial(_conv_kernel, tn=2 * LANES, ctx_tiles=rows.ctx_tiles, ctx_seq=ctx_seq,
                          lat_seq=lat_seq),
        grid=(rows.n_tiles,),
        in_specs=[
            pl.BlockSpec((tm, d), lambda i: (i, 0)),
            pl.BlockSpec((hl, d), lambda i: (jnp.maximum(i * (tm // hl) - 1, 0), 0)),
            pl.BlockSpec((hl, d), lambda i: (jnp.minimum((i + 1) * (tm // hl), last_halo), 0)),
            pl.BlockSpec((1, 6, d), lambda i: (rows.group(i), 0, 0)),
            _resident(g.shape, lambda i: (0, 0)),
            _resident(w_in.shape, lambda i: (0, 0)),
            _resident(conv_w.shape, lambda i: (0, 0)),
            _resident(w_out.shape, lambda i: (0, 0)),
        ],
        out_specs=pl.BlockSpec((tm, d), lambda i: (i, 0)),
        out_shape=jax.ShapeDtypeStruct(x.shape, F32),
        scratch_shapes=[pltpu.VMEM((tm + 2 * hl, d), BF16), pltpu.VMEM((tm + 2 * hl, 2 * LANES), F32),
                        pltpu.VMEM((tm, d), BF16)],
        compiler_params=_cparams(("arbitrary",)),
        name="conv_mixer",
    )(x, x, x, mod, g, w_in, conv_w, w_out)


RET_PROJ_CHUNK = 4 * LANES


def _ret_project(x_ref, mod_ref, g_ref, w_ref, qkv_ref, gate_ref):
    tn = RET_PROJ_CHUNK
    h = _modulate(x_ref[...], g_ref[...], mod_ref[0, 0:1, :], mod_ref[0, 1:2, :]).astype(BF16)
    n_qkv = qkv_ref.shape[1]
    for c in range(n_qkv // tn):
        qkv_ref[:, c * tn:(c + 1) * tn] = _dot(h, w_ref[:, c * tn:(c + 1) * tn]).astype(BF16)
    for c in range(gate_ref.shape[1] // tn):
        gate = _dot(h, w_ref[:, n_qkv + c * tn:n_qkv + (c + 1) * tn])
        gate_ref[:, c * tn:(c + 1) * tn] = _silu(gate).astype(gate_ref.dtype)


def _ret_decays(lr_ref, hd, chunk, k_scale):
    row = lax.broadcasted_iota(jnp.int32, (chunk, chunk), 0).astype(F32)
    col = lax.broadcasted_iota(jnp.int32, (chunk, chunk), 1).astype(F32)
    ridx = lax.broadcasted_iota(jnp.int32, (chunk, 1), 0).astype(F32)
    dist = row - col
    log_gamma = -jnp.exp(lr_ref[hd])
    lg_f, lg_b = log_gamma[0:1, :], log_gamma[1:2, :]
    mask = jnp.where(dist > 0, jnp.exp(jnp.maximum(dist, 0.0) * lg_f),
                     jnp.where(dist < 0, jnp.exp(jnp.maximum(-dist, 0.0) * lg_b), 2.0)) * k_scale
    q_decay = (jnp.exp((ridx + 1.0) * lg_f), jnp.exp((chunk - ridx) * lg_b))
    k_decay = (jnp.exp((chunk - 1.0 - ridx) * lg_f) * k_scale, jnp.exp(ridx * lg_b) * k_scale)
    chunk_decay = (jnp.exp(chunk * lg_f), jnp.exp(chunk * lg_b))
    return mask, q_decay, k_decay, chunk_decay


def _ret_scan(decays, q, k, v, gate_ref, gn_ref, s0_ref, y_ref, sout_ref, o_ref, st_ref, *,
              row0, seq, chunk, hps, dk, dv):
    n_chunks = seq // chunk
    has_init = s0_ref is not None
    (q_ref, q0), (k_ref, k0), (v_ref, v0) = q, k, v
    for hd in range(hps):
        mask, q_decay, k_decay, chunk_decay = decays[hd]
        vs = slice(hd * dv, (hd + 1) * dv)
        for direction in range(2):
            order = range(n_chunks) if direction == 0 else range(n_chunks - 1, -1, -1)
            if has_init:
                st_ref[hd] = s0_ref[direction, hd]
            for step, c in enumerate(order):
                rs = slice(row0 + c * chunk, row0 + (c + 1) * chunk)
                ls = slice(c * chunk, (c + 1) * chunk)
                qc = q_ref[rs, q0 + hd * dk:q0 + (hd + 1) * dk]
                kc = k_ref[rs, k0 + hd * dk:k0 + (hd + 1) * dk]
                vc = v_ref[rs, v0 + hd * dv:v0 + (hd + 1) * dv]
                have_state = has_init or step > 0
                if direction == 0:
                    scores = _dot_nt(qc, kc) * mask
                    o_ref[ls, vs] = _dot(scores.astype(BF16), vc)
                if have_state:
                    o_ref[ls, vs] += _dot(qc, st_ref[hd].astype(BF16)) * q_decay[direction]
                kd = (kc.astype(F32) * k_decay[direction]).T.astype(BF16)
                update = _dot(kd, vc)
                if have_state:
                    st_ref[hd] = st_ref[hd] * chunk_decay[direction] + update
                else:
                    st_ref[hd] = update
            if sout_ref is not None:
                sout_ref[direction, hd] = st_ref[hd]

        o = o_ref[:, vs]
        mu = jnp.mean(o, axis=-1, keepdims=True)
        var = jnp.mean(jnp.square(o - mu), axis=-1, keepdims=True)
        on = (o - mu) * lax.rsqrt(var + EPS) * gn_ref[:, vs]
        ys = slice(row0, row0 + seq)
        y_ref[ys, vs] = (gate_ref[ys, vs].astype(F32) * on).astype(y_ref.dtype)


def _ret_ctx_kernel(x_ref, mod_ref, g_ref, w_ref, lr_ref, gn_ref, y_ref, sout_ref,
                    qkv_s, gate_s, o_s, st_s, *, seq, dk, dv):
    nh = RET_HEADS
    _ret_project(x_ref, mod_ref, g_ref, w_ref, qkv_s, gate_s)
    decays = [_ret_decays(lr_ref, hd, seq, dk ** -0.5) for hd in range(nh)]
    for b in range(x_ref.shape[0] // seq):
        _ret_scan(decays, (qkv_s, 0), (qkv_s, nh * dk), (qkv_s, 2 * nh * dk), gate_s, gn_ref, None,
                  y_ref, sout_ref.at[b, 0], o_s, st_s, row0=b * seq, seq=seq, chunk=seq,
                  hps=nh, dk=dk, dv=dv)


def _ret_ctx_call(x, mod, g, w, log_rate, gn_g, rows, seq, n_gate):
    tm, d = rows.tm, x.shape[1]
    assert tm % seq == 0 and seq <= 2 * LANES
    nh = RET_HEADS
    n_qkv = w.shape[1] - n_gate
    dv = n_gate // nh
    dk = (n_qkv - n_gate) // (2 * nh)
    nb = tm // seq
    m = rows.ctx_tiles * tm
    return pl.pallas_call(
        functools.partial(_ret_ctx_kernel, seq=seq, dk=dk, dv=dv),
        grid=(rows.ctx_tiles,),
        in_specs=[
            pl.BlockSpec((tm, d), lambda i: (i, 0)),
            pl.BlockSpec((1, 6, d), lambda i: (0, 0, 0)),
            _resident(g.shape, lambda i: (0, 0)),
            _resident(w.shape, lambda i: (0, 0)),
            _resident(log_rate.shape, lambda i: (0, 0, 0)),
            _resident(gn_g.shape, lambda i: (0, 0)),
        ],
        out_specs=[pl.BlockSpec((tm, n_gate), lambda i: (i, 0)),
                   pl.BlockSpec((nb, 1, 2, nh, dk, dv), lambda i: (i, 0, 0, 0, 0, 0))],
        out_shape=[jax.ShapeDtypeStruct((m, n_gate), BF16),
                   jax.ShapeDtypeStruct((m // seq, 1, 2, nh, dk, dv), F32)],
        scratch_shapes=[pltpu.VMEM((tm, n_qkv), BF16), pltpu.VMEM((tm, n_gate), BF16),
                        pltpu.VMEM((seq, n_gate), F32), pltpu.VMEM((nh, dk, dv), F32)],
        compiler_params=_cparams(("arbitrary",)),
        name="ret_ctx",
    )(x, mod, g, w, log_rate, gn_g)


RET_LAT_ROWS = 512


def _ret_lat_kernel(x_ref, mod_ref, g_ref, wq_ref, wk_ref, wv_ref, wg_ref, lr_ref, gn_ref, s0_ref,
                    y_ref, h_s, q_s, k_s, v_s, gate_s, o_s, st_s, *, chunk, dk, dv):
    seq = x_ref.shape[0]

    @pl.when(pl.program_id(1) == 0)
    def _():
        h_s[...] = _modulate(x_ref[...], g_ref[...], mod_ref[0, 0:1, :], mod_ref[0, 1:2, :]).astype(BF16)

    rc = min(RET_LAT_ROWS, seq)
    for r in range(seq // rc):
        rs = slice(r * rc, (r + 1) * rc)
        h = h_s[rs, :]
        q_s[rs, :] = _dot(h, wq_ref[...]).astype(BF16)
        k_s[rs, :] = _dot(h, wk_ref[...]).astype(BF16)
        v_s[rs, :] = _dot(h, wv_ref[...]).astype(BF16)
        gate_s[rs, :] = _silu(_dot(h, wg_ref[...])).astype(BF16)
    decays = [_ret_decays(lr_ref, 0, chunk, dk ** -0.5)]
    _ret_scan(decays, (q_s, 0), (k_s, 0), (v_s, 0), gate_s, gn_ref, s0_ref.at[0], y_ref, None,
              o_s, st_s, row0=0, seq=seq, chunk=chunk, hps=1, dk=dk, dv=dv)


def _ret_lat_call(x, mod, g, w, log_rate, gn_g, s0, *, bsz, seq, row0, n_gate):
    d = x.shape[1]
    nh = RET_HEADS
    n_qkv = w.shape[1] - n_gate
    dv = n_gate // nh
    dk = (n_qkv - n_gate) // (2 * nh)
    chunk = min(seq, 2 * LANES)
    assert row0 % seq == 0 and dv % dk == 0
    b0 = row0 // seq
    return pl.pallas_call(
        functools.partial(_ret_lat_kernel, chunk=chunk, dk=dk, dv=dv),
        grid=(bsz, nh),
        in_specs=[
            pl.BlockSpec((seq, d), lambda b, h: (b0 + b, 0)),
            pl.BlockSpec((1, 6, d), lambda b, h: (1 + b, 0, 0)),
            _resident(g.shape, lambda b, h: (0, 0)),
            pl.BlockSpec((d, dk), lambda b, h: (0, h)),
            pl.BlockSpec((d, dk), lambda b, h: (0, nh + h)),
            pl.BlockSpec((d, dv), lambda b, h: (0, (2 * nh * dk) // dv + h)),
            pl.BlockSpec((d, dv), lambda b, h: (0, n_qkv // dv + h)),
            pl.BlockSpec((1, 2, 1), lambda b, h: (h, 0, 0)),
            pl.BlockSpec((1, dv), lambda b, h: (0, h)),
            pl.BlockSpec((1, 2, 1, dk, dv), lambda b, h: (b, 0, h, 0, 0)),
        ],
        out_specs=pl.BlockSpec((seq, dv), lambda b, h: (b, h)),
        out_shape=jax.ShapeDtypeStruct((bsz * seq, n_gate), BF16),
        scratch_shapes=[pltpu.VMEM((seq, d), BF16), pltpu.VMEM((seq, dk), BF16), pltpu.VMEM((seq, dk), BF16),
                        pltpu.VMEM((seq, dv), BF16), pltpu.VMEM((seq, dv), BF16),
                        pltpu.VMEM((seq, dv), F32), pltpu.VMEM((1, dk, dv), F32)],
        compiler_params=_cparams(("arbitrary", "arbitrary")),
        name="ret_lat",
    )(x, mod, g, w, w, w, w, log_rate, gn_g, s0)


def _rope_swap_index():
    f = ROPE_AXIS_FREQS
    idx = jnp.arange(MLA_ROPE)
    return jnp.where((idx // f) % 2 == 0, idx + f, idx - f)


def _mla_weights(w_a, q_norm_g, kv_norm_g, w_q_b, w_kv_b):
    n_l = w_a.shape[0]
    swap = _rope_swap_index()
    nq = MLA_Q_RANK + MLA_KV_RANK
    w_kpe = w_a[:, :, nq:]
    w_kpe_sw = w_kpe[:, :, swap]
    wa = jnp.concatenate([w_a[:, :, :nq], w_kpe, w_kpe, w_kpe_sw, w_kpe_sw], axis=2).astype(BF16)
    wq = w_q_b.reshape(n_l, MLA_Q_RANK, MLA_HEADS, MLA_NOPE + MLA_ROPE)
    wq_nope = wq[..., :MLA_NOPE].reshape(n_l, MLA_Q_RANK, MLA_HEADS * MLA_NOPE)
    wq_pe = wq[..., MLA_NOPE:]
    wq_all = jnp.concatenate(
        [wq_nope, wq_pe.reshape(n_l, MLA_Q_RANK, -1), wq_pe[..., swap].reshape(n_l, MLA_Q_RANK, -1)],
        axis=2).astype(BF16)
    wkv = w_kv_b.reshape(n_l, MLA_KV_RANK, MLA_HEADS, MLA_NOPE + MLA_V)
    wkv_all = jnp.concatenate(
        [wkv[..., :MLA_NOPE].reshape(n_l, MLA_KV_RANK, -1), wkv[..., MLA_NOPE:].reshape(n_l, MLA_KV_RANK, -1)],
        axis=2).astype(BF16)
    return wa, q_norm_g[:, None, :], kv_norm_g[:, None, :], wq_all, wkv_all


def _rope_tables(n_tokens):
    f = ROPE_AXIS_FREQS
    f32 = np.float32
    rows = n_tokens // GRID_W
    r = np.repeat(np.arange(rows, dtype=f32), GRID_W)
    col = np.tile(np.arange(GRID_W, dtype=f32), rows)
    inv = (f32(ROPE_THETA) ** (-np.arange(f, dtype=f32) / f32(f))).astype(f32)
    ang_r, ang_c = r[:, None] * inv, col[:, None] * inv
    cos = np.concatenate([np.cos(ang_r)] * 2 + [np.cos(ang_c)] * 2, axis=1)
    sin = np.concatenate([-np.sin(ang_r), np.sin(ang_r), -np.sin(ang_c), np.sin(ang_c)], axis=1)
    reps = LANES // MLA_ROPE
    return (jnp.asarray(np.concatenate([cos] * reps, axis=1), F32),
            jnp.asarray(np.concatenate([sin] * reps, axis=1), F32))


def kernel(x_prompt, x_sample, c, c_ctx, cache_mla_ckv, cache_mla_kpe, state_ret, ada_w, ada_b, norm_mix_g, norm_ffn_g, mla_w_a, mla_q_norm_g, mla_kv_norm_g, mla_w_q_b, mla_w_kv_b, mla_w_o, conv_w_in, conv_w, conv_w_out, ret_w_in, ret_log_rate, ret_gn_g, ret_w_out, ffn_w_in, ffn_w_out, final_norm_g):
    batch, seq, d = x_prompt.shape
    dec_batch, dec_seq, _ = x_sample.shape
    depth = ada_w.shape[0]
    n_ctx = batch * seq
    n_lat = dec_batch * dec_seq
    past = cache_mla_ckv.shape[2]
    assert 1 + dec_batch <= MOD_ROWS

    cvec = jnp.zeros((MOD_ROWS, d), F32).at[0].set(c_ctx).at[1:1 + dec_batch].set(c)
    mod_all = _ada_call(cvec, ada_w, ada_b).reshape(depth, MOD_ROWS, 6, d)

    rows_s = _Rows(n_ctx, dec_batch, dec_seq, min(ROW_TILE_FUSED, dec_seq))
    rows_l = _Rows(n_ctx, dec_batch, dec_seq, min(ROW_TILE, dec_seq))
    rope_tabs = _rope_tables(dec_seq)
    final_g = final_norm_g[None, :]
    mla_w_o_bf = mla_w_o.astype(BF16)
    mla_wts = _mla_weights(mla_w_a, mla_q_norm_g, mla_kv_norm_g, mla_w_q_b, mla_w_kv_b)

    def mixer_cast_jobs(layer):
        if layer >= depth:
            return []
        kind, j = layer % N_MIXERS, layer // N_MIXERS
        if kind == 1:
            return [(conv_w_in, j), (conv_w_out, j)]
        if kind == 2:
            return [(ret_w_in, j), (ret_w_out, j)]
        return []

    x = (x_prompt.reshape(n_ctx, d), x_sample.reshape(n_lat, d))
    new_ckv, new_kpe, new_ret = [], [], []
    ffn_bf = mixer_bf = None
    for i in range(depth):
        kind, j = i % N_MIXERS, i // N_MIXERS
        mod = mod_all[i]
        g_mix = norm_mix_g[i][None, :]
        if ffn_bf is None:
            assert kind == 0
        if kind == 0:
            split = isinstance(x, tuple)
            o_c, ckv_c, kpe_c = _mla_ctx_call(x[0] if split else x, mod, g_mix, mla_wts, j, rows_l, seq)
            ql, kl, vl = _mla_proj_call(
                x[1] if split else x, 0 if split else rows_l.ctx_tiles, mod, g_mix, mla_wts, j, rows_l,
                tile0=rows_l.ctx_tiles, n_tiles=rows_l.n_tiles - rows_l.ctx_tiles, rope_tabs=rope_tabs)
            new_ckv.append(ckv_c.reshape(batch, seq, MLA_KV_RANK))
            new_kpe.append(kpe_c.reshape(batch, seq, MLA_ROPE))
            kpe_rep = jnp.concatenate([cache_mla_kpe[:, j]] * (LANES // MLA_ROPE), axis=-1)
            jobs = [] if ffn_bf is not None else [(ffn_w_in, i), (ffn_w_out, i)]
            o_l, cast = _attn_lat_call(
                ql, kl, vl, cache_mla_ckv[:, j].reshape(dec_batch * past, MLA_KV_RANK),
                kpe_rep.reshape(dec_batch * past, LANES), mla_wts[4], j, n_batch=dec_batch, seq=dec_seq,
                past=past, tq=min(ATTN_Q_TILE, dec_seq), cast_jobs=jobs)
            if jobs:
                ffn_bf = cast
            mix = (o_c, o_l, mla_w_o_bf, j)
        elif kind == 1:
            mix = None
            x = _conv_call(x, mod, g_mix, mixer_bf[0][0], conv_w[j], mixer_bf[1][0], rows_l, seq, dec_seq)
        else:
            n_gate = ret_w_out.shape[1]
            lr = ret_log_rate[j].T[:, :, None]
            gn = ret_gn_g[j][None, :]
            y_c, st = _ret_ctx_call(x, mod, g_mix, mixer_bf[0][0], lr, gn, rows_s, seq, n_gate)
            y_l = _ret_lat_call(x, mod, g_mix, mixer_bf[0][0], lr, gn, state_ret[:, j],
                                bsz=dec_batch, seq=dec_seq, row0=n_ctx, n_gate=n_gate)
            new_ret.append(st)
            mix = (y_c, y_l, mixer_bf[1], 0)
        last = i == depth - 1
        jobs = [] if last else [(ffn_w_in, i + 1), (ffn_w_out, i + 1)] + mixer_cast_jobs(i + 1)
        x, cast = _ffn_call(x, mod, norm_ffn_g[i][None, :], ffn_bf[0], ffn_bf[1], 0, final_g,
                            rows_l if mix is None else rows_s, final=last, cast_jobs=jobs, mix=mix)
        ffn_bf, mixer_bf = cast[:2], cast[2:]

    y_prompt = x[0].reshape(batch, seq, d)
    y_sample = x[1].reshape(dec_batch, dec_seq, d)
    return (y_prompt, y_sample, jnp.stack(new_ckv, axis=1), jnp.stack(new_kpe, axis=1),
            jnp.concatenate(new_ret, axis=1))
```

```python
import functools
import math

import jax
import jax.numpy as jnp
import numpy as np
from jax import lax
from jax.experimental import pallas as pl
from jax.experimental.pallas import tpu as pltpu

F32 = jnp.float32
BF16 = jnp.bfloat16

N_MIXERS = 3
MLA_HEADS = 8
MLA_NOPE = 128
MLA_ROPE = 64
MLA_V = 128
MLA_Q_RANK = 384
MLA_KV_RANK = 256
MLA_SCALE = (MLA_NOPE + MLA_ROPE) ** -0.5
MLA_Q_SCALE = MLA_SCALE * 1.4426950408889634
ROPE_THETA = 10000.0
ROPE_AXIS_FREQS = MLA_ROPE // 4
GRID_W = 64
RET_HEADS = 4
EPS = 1e-6

LANES = 128
MLA_HEAD_PAD = 2 * LANES
MOD_ROWS = 8
VMEM_LIMIT = 56 * 1024 * 1024

ROW_TILE = 1024
ROW_TILE_FUSED = 512
ATTN_Q_TILE = 512


def _cparams(sem):
    return pltpu.CompilerParams(dimension_semantics=sem, vmem_limit_bytes=VMEM_LIMIT)


def _resident(shape, index_map):
    return pl.BlockSpec(shape, index_map, pipeline_mode=pl.Buffered(1))


def _rms(x, g):
    return x * lax.rsqrt(jnp.mean(x * x, axis=-1, keepdims=True) + EPS) * g


def _modulate(x, g, shift, scale):
    return _rms(x, g) * (1.0 + scale) + shift


def _silu(x):
    return x * jax.nn.sigmoid(x)


def _dot(a, b):
    return jnp.dot(a, b, preferred_element_type=F32)


def _dot_nt(a, b):
    return lax.dot_general(a, b, (((1,), (1,)), ((), ())), preferred_element_type=F32)


MAX_CAST_SLABS = 16


def _cast_specs(jobs, n_steps, step_of):
    n_slabs = math.gcd(n_steps, MAX_CAST_SLABS)
    per = n_steps // n_slabs

    def slab_of_step(*g):
        return step_of(*g) // per

    in_specs, out_specs, out_shape = [], [], []
    for w, layer in jobs:
        _, k, n = w.shape
        blk = (1, k // n_slabs, n)
        in_specs.append(pl.BlockSpec(blk, lambda *g, layer=layer: (layer, slab_of_step(*g), 0)))
        out_specs.append(pl.BlockSpec(blk, lambda *g: (0, slab_of_step(*g), 0)))
        out_shape.append(jax.ShapeDtypeStruct((1, k, n), BF16))
    return in_specs, out_specs, out_shape


def _run_casts(in_refs, out_refs):
    for src, dst in zip(in_refs, out_refs):
        dst[...] = src[...].astype(BF16)


def _ada_kernel(c_ref, w_ref, b_ref, o_ref):
    a = _silu(c_ref[...]).astype(BF16)
    o_ref[0] = _dot(a, w_ref[0].astype(BF16)) + b_ref[0]


def _ada_call(cvec, ada_w, ada_b):
    depth, d, n = ada_w.shape
    tn = n // 2
    return pl.pallas_call(
        _ada_kernel,
        grid=(depth, n // tn),
        in_specs=[
            pl.BlockSpec((MOD_ROWS, d), lambda l, j: (0, 0)),
            pl.BlockSpec((1, d, tn), lambda l, j: (l, 0, j)),
            pl.BlockSpec((1, 1, tn), lambda l, j: (l, 0, j)),
        ],
        out_specs=pl.BlockSpec((1, MOD_ROWS, tn), lambda l, j: (l, 0, j)),
        out_shape=jax.ShapeDtypeStruct((depth, MOD_ROWS, n), F32),
        compiler_params=_cparams(("arbitrary", "arbitrary")),
        name="ada",
    )(cvec, ada_w, ada_b.reshape(depth, 1, n))


class _Rows:
    def __init__(self, n_ctx, dec_batch, dec_seq, tm):
        assert n_ctx % tm == 0 and dec_seq % tm == 0
        self.tm = tm
        self.ctx_tiles = n_ctx // tm
        self.seq_tiles = dec_seq // tm
        self.n_tiles = self.ctx_tiles + dec_batch * self.seq_tiles

    def group(self, i):
        lat = jnp.maximum(i - self.ctx_tiles, 0) // self.seq_tiles
        return jnp.where(i < self.ctx_tiles, 0, 1 + lat)


def _mla_project(x_ref, mod_ref, g_ref, wa_ref, qg_ref, kvg_ref, wq_ref, wkv_ref, rope_refs,
                 q_ref, k_ref, v_ref, cache_refs):
    h = _modulate(x_ref[...], g_ref[...], mod_ref[0, 0:1, :], mod_ref[0, 1:2, :]).astype(BF16)
    a = _dot(h, wa_ref[0])
    nq = MLA_Q_RANK + MLA_KV_RANK
    ckv_n = _rms(a[:, MLA_Q_RANK:nq], kvg_ref[0])
    kpe = a[:, nq:nq + LANES]
    if cache_refs is not None:
        cache_refs[0][...] = ckv_n
        cache_refs[1][...] = kpe[:, :MLA_ROPE]
    if rope_refs is not None:
        cos, sin = rope_refs[0][...], rope_refs[1][...]
        kpe = kpe * cos + a[:, nq + LANES:nq + 2 * LANES] * sin
    kpe = kpe.astype(BF16)

    qn = _rms(a[:, :MLA_Q_RANK], qg_ref[0]).astype(BF16)
    nh = MLA_HEADS
    n_nope, n_pe = nh * MLA_NOPE, nh * MLA_ROPE
    q_nope = _dot(qn, wq_ref[0, :, :n_nope])
    q_pe = _dot(qn, wq_ref[0, :, n_nope:n_nope + n_pe])
    if rope_refs is not None:
        q_sw = _dot(qn, wq_ref[0, :, n_nope + n_pe:n_nope + 2 * n_pe])
        cos, sin = cos * MLA_Q_SCALE, sin * MLA_Q_SCALE
    kv = _dot(ckv_n.astype(BF16), wkv_ref[0])
    low_half = lax.broadcasted_iota(jnp.int32, (1, LANES), 1) < MLA_ROPE
    for pair in range(nh // 2):
        ps = slice(pair * LANES, (pair + 1) * LANES)
        if rope_refs is not None:
            pe = q_pe[:, ps] * cos + q_sw[:, ps] * sin
        else:
            pe = q_pe[:, ps] * MLA_Q_SCALE
        for hd, keep in ((2 * pair, low_half), (2 * pair + 1, jnp.logical_not(low_half))):
            lo = hd * MLA_HEAD_PAD
            hs = slice(hd * LANES, (hd + 1) * LANES)
            q_ref[:, lo:lo + LANES] = (q_nope[:, hs] * MLA_Q_SCALE).astype(BF16)
            q_ref[:, lo + LANES:lo + 2 * LANES] = jnp.where(keep, pe, 0.0).astype(BF16)
            k_ref[:, lo:lo + LANES] = kv[:, hs].astype(BF16)
            k_ref[:, lo + LANES:lo + 2 * LANES] = kpe
    v_ref[...] = kv[:, nh * LANES:].astype(BF16)


def _mla_proj_kernel(x_ref, mod_ref, g_ref, wa_ref, qg_ref, kvg_ref, wq_ref, wkv_ref, ck_ref, sk_ref,
                     q_ref, k_ref, v_ref):
    _mla_project(x_ref, mod_ref, g_ref, wa_ref, qg_ref, kvg_ref, wq_ref, wkv_ref, (ck_ref, sk_ref),
                 q_ref, k_ref, v_ref, None)


def _layer_specs(weights, layer):
    return [_resident((1,) + w.shape[1:], lambda *_: (layer, 0, 0)) for w in weights]


def _mla_proj_call(x, x_tile0, mod, g, wts, layer, rows, *, tile0, n_tiles, rope_tabs):
    tm, d = rows.tm, x.shape[1]
    wa, qg, kvg, wq, wkv = wts
    in_specs = [
        pl.BlockSpec((tm, d), lambda i: (i + x_tile0, 0)),
        pl.BlockSpec((1, 6, d), lambda i: (rows.group(i + tile0), 0, 0)),
        _resident(g.shape, lambda i: (0, 0)),
    ] + _layer_specs(wts, layer)
    in_specs += [pl.BlockSpec((tm, LANES), lambda i: (i % rows.seq_tiles, 0)) for _ in rope_tabs]
    m = n_tiles * tm
    hp, hv = MLA_HEADS * MLA_HEAD_PAD, MLA_HEADS * MLA_V
    return pl.pallas_call(
        _mla_proj_kernel,
        grid=(n_tiles,),
        in_specs=in_specs,
        out_specs=[pl.BlockSpec((tm, hp), lambda i: (i, 0)), pl.BlockSpec((tm, hp), lambda i: (i, 0)),
                   pl.BlockSpec((tm, hv), lambda i: (i, 0))],
        out_shape=[jax.ShapeDtypeStruct((m, hp), BF16), jax.ShapeDtypeStruct((m, hp), BF16),
                   jax.ShapeDtypeStruct((m, hv), BF16)],
        compiler_params=_cparams(("arbitrary",)),
        name="mla_proj_lat",
    )(x, mod, g, wa, qg, kvg, wq, wkv, *rope_tabs)


def _mla_ctx_kernel(x_ref, mod_ref, g_ref, wa_ref, qg_ref, kvg_ref, wq_ref, wkv_ref,
                    o_ref, ckv_ref, kpe_ref, q_s, k_s, v_s, s_ref, *, seq):
    _mla_project(x_ref, mod_ref, g_ref, wa_ref, qg_ref, kvg_ref, wq_ref, wkv_ref, None,
                 q_s, k_s, v_s, (ckv_ref, kpe_ref))
    _attend(q_s, [k_s], [v_s], o_ref, s_ref, nb=x_ref.shape[0] // seq, tq=seq, sks=(seq,))


def _mla_ctx_call(x, mod, g, wts, layer, rows, seq):
    tm, d = rows.tm, x.shape[1]
    assert tm % seq == 0
    wa, qg, kvg, wq, wkv = wts
    m = rows.ctx_tiles * tm
    hp, hv = MLA_HEADS * MLA_HEAD_PAD, MLA_HEADS * MLA_V
    return pl.pallas_call(
        functools.partial(_mla_ctx_kernel, seq=seq),
        grid=(rows.ctx_tiles,),
        in_specs=[
            pl.BlockSpec((tm, d), lambda i: (i, 0)),
            pl.BlockSpec((1, 6, d), lambda i: (0, 0, 0)),
            _resident(g.shape, lambda i: (0, 0)),
        ] + _layer_specs(wts, layer),
        out_specs=[pl.BlockSpec((tm, hv), lambda i: (i, 0)),
                   pl.BlockSpec((tm, MLA_KV_RANK), lambda i: (i, 0)),
                   pl.BlockSpec((tm, MLA_ROPE), lambda i: (i, 0))],
        out_shape=[jax.ShapeDtypeStruct((m, hv), BF16),
                   jax.ShapeDtypeStruct((m, MLA_KV_RANK), F32),
                   jax.ShapeDtypeStruct((m, MLA_ROPE), F32)],
        scratch_shapes=[pltpu.VMEM((tm, hp), BF16), pltpu.VMEM((tm, hp), BF16), pltpu.VMEM((tm, hv), BF16),
                        pltpu.VMEM((2, seq, seq), F32)],
        compiler_params=_cparams(("arbitrary",)),
        name="mla_ctx",
    )(x, mod, g, wa, qg, kvg, wq, wkv)


def _cache_expand(ckv_ref, kpe_ref, wkv_ref, k_ref, v_ref):
    kv = _dot(ckv_ref[...].astype(BF16), wkv_ref[0])
    kpe = kpe_ref[...].astype(BF16)
    nh = MLA_HEADS
    for hd in range(nh):
        lo = hd * MLA_HEAD_PAD
        k_ref[:, lo:lo + LANES] = kv[:, hd * LANES:(hd + 1) * LANES].astype(BF16)
        k_ref[:, lo + LANES:lo + 2 * LANES] = kpe
    v_ref[...] = kv[:, nh * LANES:].astype(BF16)


def _attend(q_ref, k_refs, v_refs, o_ref, s_ref, *, nb, tq, sks):
    problems = [(b, h) for b in range(nb) for h in range(MLA_HEADS)]
    seg_cols = [sum(sks[:i]) for i in range(len(sks))]

    def scores(idx):
        b, h = problems[idx]
        q = q_ref[b * tq:(b + 1) * tq, h * MLA_HEAD_PAD:(h + 1) * MLA_HEAD_PAD]
        m_lane = None
        for k_ref, sk, c0 in zip(k_refs, sks, seg_cols):
            s = _dot_nt(q, k_ref[b * sk:(b + 1) * sk, h * MLA_HEAD_PAD:(h + 1) * MLA_HEAD_PAD])
            s_ref[idx % 2, :, c0:c0 + sk] = s
            for c in range(sk // LANES):
                piece = s[:, c * LANES:(c + 1) * LANES]
                m_lane = piece if m_lane is None else jnp.maximum(m_lane, piece)
        return m_lane.max(axis=-1, keepdims=True)

    def weighted_values(idx, m):
        b, h = problems[idx]
        acc = None
        for v_ref, sk, c0 in zip(v_refs, sks, seg_cols):
            p = jnp.exp2(s_ref[idx % 2, :, c0:c0 + sk] - m).astype(BF16)
            v = v_ref[b * sk:(b + 1) * sk, h * MLA_V:(h + 1) * MLA_V]
            part = _dot(p, jnp.concatenate([v, jnp.ones_like(v)], axis=1))
            acc = part if acc is None else acc + part
        o = acc[:, :MLA_V] / acc[:, MLA_V:]
        o_ref[b * tq:(b + 1) * tq, h * MLA_V:(h + 1) * MLA_V] = o.astype(o_ref.dtype)

    m = scores(0)
    for idx in range(len(problems)):
        m_next = scores(idx + 1) if idx + 1 < len(problems) else None
        weighted_values(idx, m)
        m = m_next


def _attn_lat_kernel(*refs, tq, past, seq, n_cast):
    q_ref, k_ref, v_ref, ckv_ref, kpe_ref, wkv_ref = refs[:6]
    n_in = 6 + n_cast
    o_ref = refs[n_in]
    kp_s, vp_s, s_ref = refs[n_in + 1 + n_cast:]
    _run_casts(refs[6:n_in], refs[n_in + 1:n_in + 1 + n_cast])

    @pl.when(pl.program_id(1) == 0)
    def _():
        _cache_expand(ckv_ref, kpe_ref, wkv_ref, kp_s, vp_s)

    _attend(q_ref, [kp_s, k_ref], [vp_s, v_ref], o_ref, s_ref, nb=1, tq=tq, sks=(past, seq))


def _attn_lat_call(q, k, v, cache_ckv, cache_kpe, wkv, layer, *, n_batch, seq, past, tq, cast_jobs=()):
    tiles = seq // tq
    hp, hv = MLA_HEADS * MLA_HEAD_PAD, MLA_HEADS * MLA_V
    c_in, c_out, c_shape = _cast_specs(cast_jobs, n_batch * tiles, lambda b, t: b * tiles + t)
    outs = pl.pallas_call(
        functools.partial(_attn_lat_kernel, tq=tq, past=past, seq=seq, n_cast=len(cast_jobs)),
        grid=(n_batch, tiles),
        in_specs=[
            pl.BlockSpec((tq, hp), lambda b, t: (b * tiles + t, 0)),
            pl.BlockSpec((seq, hp), lambda b, t: (b, 0)),
            pl.BlockSpec((seq, hv), lambda b, t: (b, 0)),
            pl.BlockSpec((past, MLA_KV_RANK), lambda b, t: (b, 0)),
            pl.BlockSpec((past, LANES), lambda b, t: (b, 0)),
        ] + _layer_specs([wkv], layer) + c_in,
        out_specs=[pl.BlockSpec((tq, hv), lambda b, t: (b * tiles + t, 0))] + c_out,
        out_shape=[jax.ShapeDtypeStruct((n_batch * seq, hv), BF16)] + c_shape,
        scratch_shapes=[pltpu.VMEM((past, hp), BF16), pltpu.VMEM((past, hv), BF16),
                        pltpu.VMEM((2, tq, past + seq), F32)],
        compiler_params=_cparams(("arbitrary", "arbitrary")),
        name="attn_lat",
    )(q, k, v, cache_ckv, cache_kpe, wkv, *[w for w, _ in cast_jobs])
    return outs[0], list(outs[1:])


def _split_specs(rows, width):
    ct = rows.ctx_tiles
    return [pl.BlockSpec((rows.tm, width), lambda i: (jnp.minimum(i, ct - 1), 0)),
            pl.BlockSpec((rows.tm, width), lambda i: (jnp.maximum(i - ct, 0), 0))]


def _ffn_kernel(*refs, th, final, ctx_tiles, n_cast, n_x, mixed):
    it = iter(refs)
    x_refs = [next(it) for _ in range(n_x)]
    if mixed:
        yc_ref, yl_ref, wo_ref = next(it), next(it), next(it)
    mod_ref, g_ref, win_ref, wout_ref, fg_ref = (next(it) for _ in range(5))
    cast_in = [next(it) for _ in range(n_cast)]
    out_refs = [next(it) for _ in range(2 if final else 1)]
    cast_out = [next(it) for _ in range(n_cast)]
    scratch = list(it)
    act_ref = scratch[0]
    _run_casts(cast_in, cast_out)
    is_ctx = pl.program_id(0) < ctx_tiles
    if mixed:
        x1_ref = scratch[1]

        def mix(y_ref, x_ref):
            x1_ref[...] = x_ref[...] + mod_ref[0, 2:3, :] * _dot(y_ref[...], wo_ref[0])

        pl.when(is_ctx)(lambda: mix(yc_ref, x_refs[0]))
        pl.when(jnp.logical_not(is_ctx))(lambda: mix(yl_ref, x_refs[-1]))
        x = x1_ref[...]
    else:
        x = x_refs[0][...]
    h = _modulate(x, g_ref[...], mod_ref[0, 3:4, :], mod_ref[0, 4:5, :]).astype(BF16)
    hidden = wout_ref.shape[1]
    for c in range(hidden // th):
        a = _dot(h, win_ref[0, :, c * th:(c + 1) * th])
        b = _dot(h, win_ref[0, :, hidden + c * th:hidden + (c + 1) * th])
        act_ref[:, c * th:(c + 1) * th] = (_silu(a) * b).astype(BF16)
    y = x + mod_ref[0, 5:6, :] * _dot(act_ref[...], wout_ref[0])
    if not final:
        out_refs[0][...] = y
    else:
        acc_ref = scratch[-1]
        acc_ref[...] = _rms(y, fg_ref[...])

        @pl.when(is_ctx)
        def _():
            out_refs[0][...] = acc_ref[...]

        @pl.when(jnp.logical_not(is_ctx))
        def _():
            out_refs[1][...] = acc_ref[...]


def _ffn_call(x, mod, g, w_in_all, w_out_all, layer, final_g, rows, final, cast_jobs=(), mix=None):
    tm, d = rows.tm, w_in_all.shape[1]
    hidden = w_out_all.shape[1]
    split_x = isinstance(x, tuple)
    assert not split_x or mix is not None
    xs = list(x) if split_x else [x]
    x_specs = _split_specs(rows, d) if split_x else [pl.BlockSpec((tm, d), lambda i: (i, 0))]
    mix_args, mix_specs = [], []
    scratch = [pltpu.VMEM((tm, hidden), BF16)]
    if mix is not None:
        y_ctx, y_lat, w_o_all, w_o_layer = mix
        k = w_o_all.shape[1]
        mix_args = [y_ctx, y_lat, w_o_all]
        mix_specs = _split_specs(rows, k) + [_resident((1, k, d), lambda i: (w_o_layer, 0, 0))]
        scratch.append(pltpu.VMEM((tm, d), F32))
    if final:
        out_specs = _split_specs(rows, d)
        out_shape = [jax.ShapeDtypeStruct((rows.ctx_tiles * tm, d), F32),
                     jax.ShapeDtypeStruct(((rows.n_tiles - rows.ctx_tiles) * tm, d), F32)]
        scratch.append(pltpu.VMEM((tm, d), F32))
    else:
        out_specs = [pl.BlockSpec((tm, d), lambda i: (i, 0))]
        out_shape = [jax.ShapeDtypeStruct((rows.n_tiles * tm, d), F32)]
    c_in, c_out, c_shape = _cast_specs(cast_jobs, rows.n_tiles, lambda i: i)
    outs = pl.pallas_call(
        functools.partial(_ffn_kernel, th=2 * LANES, final=final, ctx_tiles=rows.ctx_tiles,
                          n_cast=len(cast_jobs), n_x=len(xs), mixed=mix is not None),
        grid=(rows.n_tiles,),
        in_specs=x_specs + mix_specs + [
            pl.BlockSpec((1, 6, d), lambda i: (rows.group(i), 0, 0)),
            _resident(g.shape, lambda i: (0, 0)),
            _resident((1, d, 2 * hidden), lambda i: (layer, 0, 0)),
            _resident((1, hidden, d), lambda i: (layer, 0, 0)),
            _resident(final_g.shape, lambda i: (0, 0)),
        ] + c_in,
        out_specs=out_specs + c_out,
        out_shape=out_shape + c_shape,
        scratch_shapes=scratch,
        compiler_params=_cparams(("arbitrary",)),
        name=("ffn_final" if final else "ffn") + ("_mix" if mix is not None else ""),
    )(*xs, *mix_args, mod, g, w_in_all, w_out_all, final_g, *[w for w, _ in cast_jobs])
    n_out = 2 if final else 1
    stream = tuple(outs[:2]) if final else outs[0]
    return stream, list(outs[n_out:])


CONV_HALO = 16


def _conv_kernel(x_ref, xp_ref, xn_ref, mod_ref, g_ref, win_ref, cw_ref, wout_ref, o_ref,
                 h_ref, z_ref, act_ref, *, tn, ctx_tiles, ctx_seq, lat_seq):
    tm, d = x_ref.shape
    hl = CONV_HALO
    shift, scale = mod_ref[0, 0:1, :], mod_ref[0, 1:2, :]
    h_ref[0:hl, :] = _modulate(xp_ref[...], g_ref[...], shift, scale).astype(BF16)
    h_ref[hl:hl + tm, :] = _modulate(x_ref[...], g_ref[...], shift, scale).astype(BF16)
    h_ref[hl + tm:, :] = _modulate(xn_ref[...], g_ref[...], shift, scale).astype(BF16)
    i = pl.program_id(0)
    is_ctx = i < ctx_tiles
    row = lax.broadcasted_iota(jnp.int32, (tm, 1), 0)
    lat_row0 = (jnp.maximum(i - ctx_tiles, 0) % (lat_seq // tm)) * tm
    pos = jnp.where(is_ctx, row & (ctx_seq - 1), row + lat_row0)
    seq = jnp.where(is_ctx, ctx_seq, lat_seq)
    has_prev = pos != 0
    has_next = pos != seq - 1
    for c in range(d // tn):
        sl = slice(c * tn, (c + 1) * tn)
        h = h_ref[...]
        cg = _dot(h, win_ref[:, d + c * tn:d + (c + 1) * tn])
        u = _dot(h, win_ref[:, 2 * d + c * tn:2 * d + (c + 1) * tn])
        z_ref[...] = cg * u
        bg = _dot(h_ref[hl:hl + tm, :], win_ref[:, sl])
        z_prev = jnp.where(has_prev, z_ref[hl - 1:hl - 1 + tm, :], 0.0)
        z_next = jnp.where(has_next, z_ref[hl + 1:hl + 1 + tm, :], 0.0)
        conv = z_prev * cw_ref[0:1, sl] + z_ref[hl:hl + tm, :] * cw_ref[1:2, sl] + z_next * cw_ref[2:3, sl]
        act_ref[:, sl] = (bg * conv).astype(BF16)
    o_ref[...] = x_ref[...] + mod_ref[0, 2:3, :] * _dot(act_ref[...], wout_ref[...])


def _conv_call(x, mod, g, w_in, conv_w, w_out, rows, ctx_seq, lat_seq):
    tm, d = rows.tm, x.shape[1]
    hl = CONV_HALO
    assert tm % ctx_seq == 0 and ctx_seq & (ctx_seq - 1) == 0 and lat_seq % tm == 0 and tm % hl == 0
    last_halo = x.shape[0] // hl - 1
    return pl.pallas_call(
        functools.partial(_conv_kernel, tn=2 * LANES, ctx_tiles=rows.ctx_tiles, ctx_seq=ctx_seq,
                          lat_seq=lat_seq),
        grid=(rows.n_tiles,),
        in_specs=[
            pl.BlockSpec((tm, d), lambda i: (i, 0)),
            pl.BlockSpec((hl, d), lambda i: (jnp.maximum(i * (tm // hl) - 1, 0), 0)),
            pl.BlockSpec((hl, d), lambda i: (jnp.minimum((i + 1) * (tm // hl), last_halo), 0)),
            pl.BlockSpec((1, 6, d), lambda i: (rows.group(i), 0, 0)),
            _resident(g.shape, lambda i: (0, 0)),
            _resident(w_in.shape, lambda i: (0, 0)),
            _resident(conv_w.shape, lambda i: (0, 0)),
            _resident(w_out.shape, lambda i: (0, 0)),
        ],
        out_specs=pl.BlockSpec((tm, d), lambda i: (i, 0)),
        out_shape=jax.ShapeDtypeStruct(x.shape, F32),
        scratch_shapes=[pltpu.VMEM((tm + 2 * hl, d), BF16), pltpu.VMEM((tm + 2 * hl, 2 * LANES), F32),
                        pltpu.VMEM((tm, d), BF16)],
        compiler_params=_cparams(("arbitrary",)),
        name="conv_mixer",
    )(x, x, x, mod, g, w_in, conv_w, w_out)


RET_PROJ_CHUNK = 4 * LANES


def _ret_project(x_ref, mod_ref, g_ref, w_ref, qkv_ref, gate_ref):
    tn = RET_PROJ_CHUNK
    h = _modulate(x_ref[...], g_ref[...], mod_ref[0, 0:1, :], mod_ref[0, 1:2, :]).astype(BF16)
    n_qkv = qkv_ref.shape[1]
    for c in range(n_qkv // tn):
        qkv_ref[:, c * tn:(c + 1) * tn] = _dot(h, w_ref[:, c * tn:(c + 1) * tn]).astype(BF16)
    for c in range(gate_ref.shape[1] // tn):
        gate = _dot(h, w_ref[:, n_qkv + c * tn:n_qkv + (c + 1) * tn])
        gate_ref[:, c * tn:(c + 1) * tn] = _silu(gate).astype(gate_ref.dtype)


def _ret_decays(lr_ref, hd, chunk, k_scale):
    row = lax.broadcasted_iota(jnp.int32, (chunk, chunk), 0).astype(F32)
    col = lax.broadcasted_iota(jnp.int32, (chunk, chunk), 1).astype(F32)
    ridx = lax.broadcasted_iota(jnp.int32, (chunk, 1), 0).astype(F32)
    dist = row - col
    log_gamma = -jnp.exp(lr_ref[hd])
    lg_f, lg_b = log_gamma[0:1, :], log_gamma[1:2, :]
    mask = jnp.where(dist > 0, jnp.exp(jnp.maximum(dist, 0.0) * lg_f),
                     jnp.where(dist < 0, jnp.exp(jnp.maximum(-dist, 0.0) * lg_b), 2.0)) * k_scale
    q_decay = (jnp.exp((ridx + 1.0) * lg_f), jnp.exp((chunk - ridx) * lg_b))
    k_decay = (jnp.exp((chunk - 1.0 - ridx) * lg_f) * k_scale, jnp.exp(ridx * lg_b) * k_scale)
    chunk_decay = (jnp.exp(chunk * lg_f), jnp.exp(chunk * lg_b))
    return mask, q_decay, k_decay, chunk_decay


def _ret_scan(decays, q, k, v, gate_ref, gn_ref, s0_ref, y_ref, sout_ref, o_ref, st_ref, *,
              row0, seq, chunk, hps, dk, dv):
    n_chunks = seq // chunk
    has_init = s0_ref is not None
    (q_ref, q0), (k_ref, k0), (v_ref, v0) = q, k, v
    for hd in range(hps):
        mask, q_decay, k_decay, chunk_decay = decays[hd]
        vs = slice(hd * dv, (hd + 1) * dv)
        for direction in range(2):
            order = range(n_chunks) if direction == 0 else range(n_chunks - 1, -1, -1)
            if has_init:
                st_ref[hd] = s0_ref[direction, hd]
            for step, c in enumerate(order):
                rs = slice(row0 + c * chunk, row0 + (c + 1) * chunk)
                ls = slice(c * chunk, (c + 1) * chunk)
                qc = q_ref[rs, q0 + hd * dk:q0 + (hd + 1) * dk]
                kc = k_ref[rs, k0 + hd * dk:k0 + (hd + 1) * dk]
                vc = v_ref[rs, v0 + hd * dv:v0 + (hd + 1) * dv]
                have_state = has_init or step > 0
                if direction == 0:
                    scores = _dot_nt(qc, kc) * mask
                    o_ref[ls, vs] = _dot(scores.astype(BF16), vc)
                if have_state:
                    o_ref[ls, vs] += _dot(qc, st_ref[hd].astype(BF16)) * q_decay[direction]
                kd = (kc.astype(F32) * k_decay[direction]).T.astype(BF16)
                update = _dot(kd, vc)
                if have_state:
                    st_ref[hd] = st_ref[hd] * chunk_decay[direction] + update
                else:
                    st_ref[hd] = update
            if sout_ref is not None:
                sout_ref[direction, hd] = st_ref[hd]

        o = o_ref[:, vs]
        mu = jnp.mean(o, axis=-1, keepdims=True)
        var = jnp.mean(jnp.square(o - mu), axis=-1, keepdims=True)
        on = (o - mu) * lax.rsqrt(var + EPS) * gn_ref[:, vs]
        ys = slice(row0, row0 + seq)
        y_ref[ys, vs] = (gate_ref[ys, vs].astype(F32) * on).astype(y_ref.dtype)


def _ret_ctx_kernel(x_ref, mod_ref, g_ref, w_ref, lr_ref, gn_ref, y_ref, sout_ref,
                    qkv_s, gate_s, o_s, st_s, *, seq, dk, dv):
    nh = RET_HEADS
    _ret_project(x_ref, mod_ref, g_ref, w_ref, qkv_s, gate_s)
    decays = [_ret_decays(lr_ref, hd, seq, dk ** -0.5) for hd in range(nh)]
    for b in range(x_ref.shape[0] // seq):
        _ret_scan(decays, (qkv_s, 0), (qkv_s, nh * dk), (qkv_s, 2 * nh * dk), gate_s, gn_ref, None,
                  y_ref, sout_ref.at[b, 0], o_s, st_s, row0=b * seq, seq=seq, chunk=seq,
                  hps=nh, dk=dk, dv=dv)


def _ret_ctx_call(x, mod, g, w, log_rate, gn_g, rows, seq, n_gate):
    tm, d = rows.tm, x.shape[1]
    assert tm % seq == 0 and seq <= 2 * LANES
    nh = RET_HEADS
    n_qkv = w.shape[1] - n_gate
    dv = n_gate // nh
    dk = (n_qkv - n_gate) // (2 * nh)
    nb = tm // seq
    m = rows.ctx_tiles * tm
    return pl.pallas_call(
        functools.partial(_ret_ctx_kernel, seq=seq, dk=dk, dv=dv),
        grid=(rows.ctx_tiles,),
        in_specs=[
            pl.BlockSpec((tm, d), lambda i: (i, 0)),
            pl.BlockSpec((1, 6, d), lambda i: (0, 0, 0)),
            _resident(g.shape, lambda i: (0, 0)),
            _resident(w.shape, lambda i: (0, 0)),
            _resident(log_rate.shape, lambda i: (0, 0, 0)),
            _resident(gn_g.shape, lambda i: (0, 0)),
        ],
        out_specs=[pl.BlockSpec((tm, n_gate), lambda i: (i, 0)),
                   pl.BlockSpec((nb, 1, 2, nh, dk, dv), lambda i: (i, 0, 0, 0, 0, 0))],
        out_shape=[jax.ShapeDtypeStruct((m, n_gate), BF16),
                   jax.ShapeDtypeStruct((m // seq, 1, 2, nh, dk, dv), F32)],
        scratch_shapes=[pltpu.VMEM((tm, n_qkv), BF16), pltpu.VMEM((tm, n_gate), BF16),
                        pltpu.VMEM((seq, n_gate), F32), pltpu.VMEM((nh, dk, dv), F32)],
        compiler_params=_cparams(("arbitrary",)),
        name="ret_ctx",
    )(x, mod, g, w, log_rate, gn_g)


RET_LAT_ROWS = 512


def _ret_lat_kernel(x_ref, mod_ref, g_ref, wq_ref, wk_ref, wv_ref, wg_ref, lr_ref, gn_ref, s0_ref,
                    y_ref, h_s, q_s, k_s, v_s, gate_s, o_s, st_s, *, chunk, dk, dv):
    seq = x_ref.shape[0]

    @pl.when(pl.program_id(1) == 0)
    def _():
        h_s[...] = _modulate(x_ref[...], g_ref[...], mod_ref[0, 0:1, :], mod_ref[0, 1:2, :]).astype(BF16)

    rc = min(RET_LAT_ROWS, seq)
    for r in range(seq // rc):
        rs = slice(r * rc, (r + 1) * rc)
        h = h_s[rs, :]
        q_s[rs, :] = _dot(h, wq_ref[...]).astype(BF16)
        k_s[rs, :] = _dot(h, wk_ref[...]).astype(BF16)
        v_s[rs, :] = _dot(h, wv_ref[...]).astype(BF16)
        gate_s[rs, :] = _silu(_dot(h, wg_ref[...])).astype(BF16)
    decays = [_ret_decays(lr_ref, 0, chunk, dk ** -0.5)]
    _ret_scan(decays, (q_s, 0), (k_s, 0), (v_s, 0), gate_s, gn_ref, s0_ref.at[0], y_ref, None,
              o_s, st_s, row0=0, seq=seq, chunk=chunk, hps=1, dk=dk, dv=dv)


def _ret_lat_call(x, mod, g, w, log_rate, gn_g, s0, *, bsz, seq, row0, n_gate):
    d = x.shape[1]
    nh = RET_HEADS
    n_qkv = w.shape[1] - n_gate
    dv = n_gate // nh
    dk = (n_qkv - n_gate) // (2 * nh)
    chunk = min(seq, 2 * LANES)
    assert row0 % seq == 0 and dv % dk == 0
    b0 = row0 // seq
    return pl.pallas_call(
        functools.partial(_ret_lat_kernel, chunk=chunk, dk=dk, dv=dv),
        grid=(bsz, nh),
        in_specs=[
            pl.BlockSpec((seq, d), lambda b, h: (b0 + b, 0)),
            pl.BlockSpec((1, 6, d), lambda b, h: (1 + b, 0, 0)),
            _resident(g.shape, lambda b, h: (0, 0)),
            pl.BlockSpec((d, dk), lambda b, h: (0, h)),
            pl.BlockSpec((d, dk), lambda b, h: (0, nh + h)),
            pl.BlockSpec((d, dv), lambda b, h: (0, (2 * nh * dk) // dv + h)),
            pl.BlockSpec((d, dv), lambda b, h: (0, n_qkv // dv + h)),
            pl.BlockSpec((1, 2, 1), lambda b, h: (h, 0, 0)),
            pl.BlockSpec((1, dv), lambda b, h: (0, h)),
            pl.BlockSpec((1, 2, 1, dk, dv), lambda b, h: (b, 0, h, 0, 0)),
        ],
        out_specs=pl.BlockSpec((seq, dv), lambda b, h: (b, h)),
        out_shape=jax.ShapeDtypeStruct((bsz * seq, n_gate), BF16),
        scratch_shapes=[pltpu.VMEM((seq, d), BF16), pltpu.VMEM((seq, dk), BF16), pltpu.VMEM((seq, dk), BF16),
                        pltpu.VMEM((seq, dv), BF16), pltpu.VMEM((seq, dv), BF16),
                        pltpu.VMEM((seq, dv), F32), pltpu.VMEM((1, dk, dv), F32)],
        compiler_params=_cparams(("arbitrary", "arbitrary")),
        name="ret_lat",
    )(x, mod, g, w, w, w, w, log_rate, gn_g, s0)


def _rope_swap_index():
    f = ROPE_AXIS_FREQS
    idx = jnp.arange(MLA_ROPE)
    return jnp.where((idx // f) % 2 == 0, idx + f, idx - f)


def _mla_weights(w_a, q_norm_g, kv_norm_g, w_q_b, w_kv_b):
    n_l = w_a.shape[0]
    swap = _rope_swap_index()
    nq = MLA_Q_RANK + MLA_KV_RANK
    w_kpe = w_a[:, :, nq:]
    w_kpe_sw = w_kpe[:, :, swap]
    wa = jnp.concatenate([w_a[:, :, :nq], w_kpe, w_kpe, w_kpe_sw, w_kpe_sw], axis=2).astype(BF16)
    wq = w_q_b.reshape(n_l, MLA_Q_RANK, MLA_HEADS, MLA_NOPE + MLA_ROPE)
    wq_nope = wq[..., :MLA_NOPE].reshape(n_l, MLA_Q_RANK, MLA_HEADS * MLA_NOPE)
    wq_pe = wq[..., MLA_NOPE:]
    wq_all = jnp.concatenate(
        [wq_nope, wq_pe.reshape(n_l, MLA_Q_RANK, -1), wq_pe[..., swap].reshape(n_l, MLA_Q_RANK, -1)],
        axis=2).astype(BF16)
    wkv = w_kv_b.reshape(n_l, MLA_KV_RANK, MLA_HEADS, MLA_NOPE + MLA_V)
    wkv_all = jnp.concatenate(
        [wkv[..., :MLA_NOPE].reshape(n_l, MLA_KV_RANK, -1), wkv[..., MLA_NOPE:].reshape(n_l, MLA_KV_RANK, -1)],
        axis=2).astype(BF16)
    return wa, q_norm_g[:, None, :], kv_norm_g[:, None, :], wq_all, wkv_all


def _rope_tables(n_tokens):
    f = ROPE_AXIS_FREQS
    f32 = np.float32
    rows = n_tokens // GRID_W
    r = np.repeat(np.arange(rows, dtype=f32), GRID_W)
    col = np.tile(np.arange(GRID_W, dtype=f32), rows)
    inv = (f32(ROPE_THETA) ** (-np.arange(f, dtype=f32) / f32(f))).astype(f32)
    ang_r, ang_c = r[:, None] * inv, col[:, None] * inv
    cos = np.concatenate([np.cos(ang_r)] * 2 + [np.cos(ang_c)] * 2, axis=1)
    sin = np.concatenate([-np.sin(ang_r), np.sin(ang_r), -np.sin(ang_c), np.sin(ang_c)], axis=1)
    reps = LANES // MLA_ROPE
    return (jnp.asarray(np.concatenate([cos] * reps, axis=1), F32),
            jnp.asarray(np.concatenate([sin] * reps, axis=1), F32))


def kernel(x_prompt, x_sample, c, c_ctx, cache_mla_ckv, cache_mla_kpe, state_ret, ada_w, ada_b, norm_mix_g, norm_ffn_g, mla_w_a, mla_q_norm_g, mla_kv_norm_g, mla_w_q_b, mla_w_kv_b, mla_w_o, conv_w_in, conv_w, conv_w_out, ret_w_in, ret_log_rate, ret_gn_g, ret_w_out, ffn_w_in, ffn_w_out, final_norm_g):
    batch, seq, d = x_prompt.shape
    dec_batch, dec_seq, _ = x_sample.shape
    depth = ada_w.shape[0]
    n_ctx = batch * seq
    n_lat = dec_batch * dec_seq
    past = cache_mla_ckv.shape[2]
    assert 1 + dec_batch <= MOD_ROWS

    cvec = jnp.zeros((MOD_ROWS, d), F32).at[0].set(c_ctx).at[1:1 + dec_batch].set(c)
    mod_all = _ada_call(cvec, ada_w, ada_b).reshape(depth, MOD_ROWS, 6, d)

    rows_s = _Rows(n_ctx, dec_batch, dec_seq, min(ROW_TILE_FUSED, dec_seq))
    rows_l = _Rows(n_ctx, dec_batch, dec_seq, min(ROW_TILE, dec_seq))
    rope_tabs = _rope_tables(dec_seq)
    final_g = final_norm_g[None, :]
    mla_w_o_bf = mla_w_o.astype(BF16)

    def mixer_cast_jobs(layer):
        if layer >= depth:
            return []
        kind, j = layer % N_MIXERS, layer // N_MIXERS
        if kind == 1:
            return [(conv_w_in, j), (conv_w_out, j)]
        if kind == 2:
            return [(ret_w_in, j), (ret_w_out, j)]
        return []

    x = (x_prompt.reshape(n_ctx, d), x_sample.reshape(n_lat, d))
    new_ckv, new_kpe, new_ret = [], [], []
    ffn_bf = mixer_bf = None
    for i in range(depth):
        kind, j = i % N_MIXERS, i // N_MIXERS
        mod = mod_all[i]
        g_mix = norm_mix_g[i][None, :]
        if ffn_bf is None:
            assert kind == 0
        if kind == 0:
            split = isinstance(x, tuple)
            mla_wts = _mla_weights(mla_w_a[j:j + 1], mla_q_norm_g[j:j + 1], mla_kv_norm_g[j:j + 1],
                                   mla_w_q_b[j:j + 1], mla_w_kv_b[j:j + 1])
            o_c, ckv_c, kpe_c = _mla_ctx_call(x[0] if split else x, mod, g_mix, mla_wts, 0, rows_l, seq)
            ql, kl, vl = _mla_proj_call(
                x[1] if split else x, 0 if split else rows_l.ctx_tiles, mod, g_mix, mla_wts, 0, rows_l,
                tile0=rows_l.ctx_tiles, n_tiles=rows_l.n_tiles - rows_l.ctx_tiles, rope_tabs=rope_tabs)
            new_ckv.append(ckv_c.reshape(batch, seq, MLA_KV_RANK))
            new_kpe.append(kpe_c.reshape(batch, seq, MLA_ROPE))
            kpe_rep = jnp.concatenate([cache_mla_kpe[:, j]] * (LANES // MLA_ROPE), axis=-1)
            jobs = [] if ffn_bf is not None else [(ffn_w_in, i), (ffn_w_out, i)]
            o_l, cast = _attn_lat_call(
                ql, kl, vl, cache_mla_ckv[:, j].reshape(dec_batch * past, MLA_KV_RANK),
                kpe_rep.reshape(dec_batch * past, LANES), mla_wts[4], 0, n_batch=dec_batch, seq=dec_seq,
                past=past, tq=min(ATTN_Q_TILE, dec_seq), cast_jobs=jobs)
            if jobs:
                ffn_bf = cast
            mix = (o_c, o_l, mla_w_o_bf, j)
        elif kind == 1:
            mix = None
            x = _conv_call(x, mod, g_mix, mixer_bf[0][0], conv_w[j], mixer_bf[1][0], rows_l, seq, dec_seq)
        else:
            n_gate = ret_w_out.shape[1]
            lr = ret_log_rate[j].T[:, :, None]
            gn = ret_gn_g[j][None, :]
            y_c, st = _ret_ctx_call(x, mod, g_mix, mixer_bf[0][0], lr, gn, rows_s, seq, n_gate)
            y_l = _ret_lat_call(x, mod, g_mix, mixer_bf[0][0], lr, gn, state_ret[:, j],
                                bsz=dec_batch, seq=dec_seq, row0=n_ctx, n_gate=n_gate)
            new_ret.append(st)
            mix = (y_c, y_l, mixer_bf[1], 0)
        last = i == depth - 1
        jobs = [] if last else [(ffn_w_in, i + 1), (ffn_w_out, i + 1)] + mixer_cast_jobs(i + 1)
        x, cast = _ffn_call(x, mod, norm_ffn_g[i][None, :], ffn_bf[0], ffn_bf[1], 0, final_g,
                            rows_l if mix is None else rows_s, final=last, cast_jobs=jobs, mix=mix)
        ffn_bf, mixer_bf = cast[:2], cast[2:]

    y_prompt = x[0].reshape(batch, seq, d)
    y_sample = x[1].reshape(dec_batch, dec_seq, d)
    return (y_prompt, y_sample, jnp.stack(new_ckv, axis=1), jnp.stack(new_kpe, axis=1),
            jnp.concatenate(new_ret, axis=1))
```

```python
import functools
import math

import jax
import jax.numpy as jnp
import numpy as np
from jax import lax
from jax.experimental import pallas as pl
from jax.experimental.pallas import tpu as pltpu

F32 = jnp.float32
BF16 = jnp.bfloat16

N_MIXERS = 3
MLA_HEADS = 8
MLA_NOPE = 128
MLA_ROPE = 64
MLA_V = 128
MLA_Q_RANK = 384
MLA_KV_RANK = 256
MLA_SCALE = (MLA_NOPE + MLA_ROPE) ** -0.5
MLA_Q_SCALE = MLA_SCALE * 1.4426950408889634
ROPE_THETA = 10000.0
ROPE_AXIS_FREQS = MLA_ROPE // 4
GRID_W = 64
RET_HEADS = 4
EPS = 1e-6

LANES = 128
MLA_HEAD_PAD = 2 * LANES
MOD_ROWS = 8
VMEM_LIMIT = 56 * 1024 * 1024

ROW_TILE = 1024
ROW_TILE_FUSED = 512
ATTN_Q_TILE = 512


def _cparams(sem):
    return pltpu.CompilerParams(dimension_semantics=sem, vmem_limit_bytes=VMEM_LIMIT)


def _resident(shape, index_map):
    return pl.BlockSpec(shape, index_map, pipeline_mode=pl.Buffered(1))


def _rms(x, g):
    return x * lax.rsqrt(jnp.mean(x * x, axis=-1, keepdims=True) + EPS) * g


def _modulate(x, g, shift, scale):
    return _rms(x, g) * (1.0 + scale) + shift


def _silu(x):
    return x * jax.nn.sigmoid(x)


def _dot(a, b):
    return jnp.dot(a, b, preferred_element_type=F32)


def _dot_nt(a, b):
    return lax.dot_general(a, b, (((1,), (1,)), ((), ())), preferred_element_type=F32)


MAX_CAST_SLABS = 16


def _cast_specs(jobs, n_steps, step_of):
    n_slabs = math.gcd(n_steps, MAX_CAST_SLABS)
    per = n_steps // n_slabs

    def slab_of_step(*g):
        return step_of(*g) // per

    in_specs, out_specs, out_shape = [], [], []
    for w, layer in jobs:
        _, k, n = w.shape
        blk = (1, k // n_slabs, n)
        in_specs.append(pl.BlockSpec(blk, lambda *g, layer=layer: (layer, slab_of_step(*g), 0)))
        out_specs.append(pl.BlockSpec(blk, lambda *g: (0, slab_of_step(*g), 0)))
        out_shape.append(jax.ShapeDtypeStruct((1, k, n), BF16))
    return in_specs, out_specs, out_shape


def _run_casts(in_refs, out_refs):
    for src, dst in zip(in_refs, out_refs):
        dst[...] = src[...].astype(BF16)


def _ada_kernel(c_ref, w_ref, b_ref, o_ref):
    a = _silu(c_ref[...]).astype(BF16)
    o_ref[0] = _dot(a, w_ref[0].astype(BF16)) + b_ref[0]


def _ada_call(cvec, ada_w, ada_b):
    depth, d, n = ada_w.shape
    tn = n // 2
    return pl.pallas_call(
        _ada_kernel,
        grid=(depth, n // tn),
        in_specs=[
            pl.BlockSpec((MOD_ROWS, d), lambda l, j: (0, 0)),
            pl.BlockSpec((1, d, tn), lambda l, j: (l, 0, j)),
            pl.BlockSpec((1, 1, tn), lambda l, j: (l, 0, j)),
        ],
        out_specs=pl.BlockSpec((1, MOD_ROWS, tn), lambda l, j: (l, 0, j)),
        out_shape=jax.ShapeDtypeStruct((depth, MOD_ROWS, n), F32),
        compiler_params=_cparams(("arbitrary", "arbitrary")),
        name="ada",
    )(cvec, ada_w, ada_b.reshape(depth, 1, n))


class _Rows:
    def __init__(self, n_ctx, dec_batch, dec_seq, tm):
        assert n_ctx % tm == 0 and dec_seq % tm == 0
        self.tm = tm
        self.ctx_tiles = n_ctx // tm
        self.seq_tiles = dec_seq // tm
        self.n_tiles = self.ctx_tiles + dec_batch * self.seq_tiles

    def group(self, i):
        lat = jnp.maximum(i - self.ctx_tiles, 0) // self.seq_tiles
        return jnp.where(i < self.ctx_tiles, 0, 1 + lat)


def _mla_project(x_ref, mod_ref, g_ref, wa_ref, qg_ref, kvg_ref, wq_ref, wkv_ref, rope_refs,
                 q_ref, k_ref, v_ref, cache_refs):
    h = _modulate(x_ref[...], g_ref[...], mod_ref[0, 0:1, :], mod_ref[0, 1:2, :]).astype(BF16)
    a = _dot(h, wa_ref[...])
    nq = MLA_Q_RANK + MLA_KV_RANK
    ckv_n = _rms(a[:, MLA_Q_RANK:nq], kvg_ref[...])
    kpe = a[:, nq:nq + LANES]
    if cache_refs is not None:
        cache_refs[0][...] = ckv_n
        cache_refs[1][...] = kpe[:, :MLA_ROPE]
    if rope_refs is not None:
        cos, sin = rope_refs[0][...], rope_refs[1][...]
        kpe = kpe * cos + a[:, nq + LANES:nq + 2 * LANES] * sin
    kpe = kpe.astype(BF16)

    qn = _rms(a[:, :MLA_Q_RANK], qg_ref[...]).astype(BF16)
    nh = MLA_HEADS
    n_nope, n_pe = nh * MLA_NOPE, nh * MLA_ROPE
    q_nope = _dot(qn, wq_ref[:, :n_nope])
    q_pe = _dot(qn, wq_ref[:, n_nope:n_nope + n_pe])
    if rope_refs is not None:
        q_sw = _dot(qn, wq_ref[:, n_nope + n_pe:n_nope + 2 * n_pe])
        cos, sin = cos * MLA_Q_SCALE, sin * MLA_Q_SCALE
    kv = _dot(ckv_n.astype(BF16), wkv_ref[...])
    low_half = lax.broadcasted_iota(jnp.int32, (1, LANES), 1) < MLA_ROPE
    for pair in range(nh // 2):
        ps = slice(pair * LANES, (pair + 1) * LANES)
        if rope_refs is not None:
            pe = q_pe[:, ps] * cos + q_sw[:, ps] * sin
        else:
            pe = q_pe[:, ps] * MLA_Q_SCALE
        for hd, keep in ((2 * pair, low_half), (2 * pair + 1, jnp.logical_not(low_half))):
            lo = hd * MLA_HEAD_PAD
            hs = slice(hd * LANES, (hd + 1) * LANES)
            q_ref[:, lo:lo + LANES] = (q_nope[:, hs] * MLA_Q_SCALE).astype(BF16)
            q_ref[:, lo + LANES:lo + 2 * LANES] = jnp.where(keep, pe, 0.0).astype(BF16)
            k_ref[:, lo:lo + LANES] = kv[:, hs].astype(BF16)
            k_ref[:, lo + LANES:lo + 2 * LANES] = kpe
    v_ref[...] = kv[:, nh * LANES:].astype(BF16)


def _mla_proj_kernel(x_ref, mod_ref, g_ref, wa_ref, qg_ref, kvg_ref, wq_ref, wkv_ref, ck_ref, sk_ref,
                     q_ref, k_ref, v_ref):
    _mla_project(x_ref, mod_ref, g_ref, wa_ref, qg_ref, kvg_ref, wq_ref, wkv_ref, (ck_ref, sk_ref),
                 q_ref, k_ref, v_ref, None)


def _mla_proj_call(x, x_tile0, mod, g, wts, rows, *, tile0, n_tiles, rope_tabs):
    tm, d = rows.tm, x.shape[1]
    wa, qg, kvg, wq, wkv = wts
    full = lambda arr: _resident(arr.shape, lambda i: (0,) * arr.ndim)
    in_specs = [
        pl.BlockSpec((tm, d), lambda i: (i + x_tile0, 0)),
        pl.BlockSpec((1, 6, d), lambda i: (rows.group(i + tile0), 0, 0)),
        full(g), full(wa), full(qg), full(kvg), full(wq), full(wkv),
    ]
    in_specs += [pl.BlockSpec((tm, LANES), lambda i: (i % rows.seq_tiles, 0)) for _ in rope_tabs]
    m = n_tiles * tm
    hp, hv = MLA_HEADS * MLA_HEAD_PAD, MLA_HEADS * MLA_V
    return pl.pallas_call(
        _mla_proj_kernel,
        grid=(n_tiles,),
        in_specs=in_specs,
        out_specs=[pl.BlockSpec((tm, hp), lambda i: (i, 0)), pl.BlockSpec((tm, hp), lambda i: (i, 0)),
                   pl.BlockSpec((tm, hv), lambda i: (i, 0))],
        out_shape=[jax.ShapeDtypeStruct((m, hp), BF16), jax.ShapeDtypeStruct((m, hp), BF16),
                   jax.ShapeDtypeStruct((m, hv), BF16)],
        compiler_params=_cparams(("arbitrary",)),
        name="mla_proj_lat",
    )(x, mod, g, wa, qg, kvg, wq, wkv, *rope_tabs)


def _mla_ctx_kernel(x_ref, mod_ref, g_ref, wa_ref, qg_ref, kvg_ref, wq_ref, wkv_ref,
                    o_ref, ckv_ref, kpe_ref, q_s, k_s, v_s, s_ref, *, seq):
    _mla_project(x_ref, mod_ref, g_ref, wa_ref, qg_ref, kvg_ref, wq_ref, wkv_ref, None,
                 q_s, k_s, v_s, (ckv_ref, kpe_ref))
    _attend(q_s, [k_s], [v_s], o_ref, s_ref, nb=x_ref.shape[0] // seq, tq=seq, sks=(seq,))


def _mla_ctx_call(x, mod, g, wts, rows, seq):
    tm, d = rows.tm, x.shape[1]
    assert tm % seq == 0
    wa, qg, kvg, wq, wkv = wts
    full = lambda arr: _resident(arr.shape, lambda i: (0,) * arr.ndim)
    m = rows.ctx_tiles * tm
    hp, hv = MLA_HEADS * MLA_HEAD_PAD, MLA_HEADS * MLA_V
    return pl.pallas_call(
        functools.partial(_mla_ctx_kernel, seq=seq),
        grid=(rows.ctx_tiles,),
        in_specs=[
            pl.BlockSpec((tm, d), lambda i: (i, 0)),
            pl.BlockSpec((1, 6, d), lambda i: (0, 0, 0)),
            full(g), full(wa), full(qg), full(kvg), full(wq), full(wkv),
        ],
        out_specs=[pl.BlockSpec((tm, hv), lambda i: (i, 0)),
                   pl.BlockSpec((tm, MLA_KV_RANK), lambda i: (i, 0)),
                   pl.BlockSpec((tm, MLA_ROPE), lambda i: (i, 0))],
        out_shape=[jax.ShapeDtypeStruct((m, hv), BF16),
                   jax.ShapeDtypeStruct((m, MLA_KV_RANK), F32),
                   jax.ShapeDtypeStruct((m, MLA_ROPE), F32)],
        scratch_shapes=[pltpu.VMEM((tm, hp), BF16), pltpu.VMEM((tm, hp), BF16), pltpu.VMEM((tm, hv), BF16),
                        pltpu.VMEM((2, seq, seq), F32)],
        compiler_params=_cparams(("arbitrary",)),
        name="mla_ctx",
    )(x, mod, g, wa, qg, kvg, wq, wkv)


def _cache_expand_kernel(ckv_ref, kpe_ref, wkv_ref, k_ref, v_ref):
    kv = _dot(ckv_ref[...].astype(BF16), wkv_ref[...])
    kpe = kpe_ref[...].astype(BF16)
    nh = MLA_HEADS
    for hd in range(nh):
        lo = hd * MLA_HEAD_PAD
        k_ref[:, lo:lo + LANES] = kv[:, hd * LANES:(hd + 1) * LANES].astype(BF16)
        k_ref[:, lo + LANES:lo + 2 * LANES] = kpe
    v_ref[...] = kv[:, nh * LANES:].astype(BF16)


def _cache_expand_call(ckv, kpe_pad, wkv):
    m = ckv.shape[0]
    tm = min(m, 512)
    hp = MLA_HEADS * MLA_HEAD_PAD
    return pl.pallas_call(
        _cache_expand_kernel,
        grid=(m // tm,),
        in_specs=[
            pl.BlockSpec((tm, MLA_KV_RANK), lambda i: (i, 0)),
            pl.BlockSpec((tm, LANES), lambda i: (i, 0)),
            _resident(wkv.shape, lambda i: (0, 0)),
        ],
        out_specs=[pl.BlockSpec((tm, hp), lambda i: (i, 0)),
                   pl.BlockSpec((tm, MLA_HEADS * MLA_V), lambda i: (i, 0))],
        out_shape=[jax.ShapeDtypeStruct((m, hp), BF16),
                   jax.ShapeDtypeStruct((m, MLA_HEADS * MLA_V), BF16)],
        compiler_params=_cparams(("arbitrary",)),
        name="mla_cache_expand",
    )(ckv, kpe_pad, wkv)


def _attend(q_ref, k_refs, v_refs, o_ref, s_ref, *, nb, tq, sks):
    problems = [(b, h) for b in range(nb) for h in range(MLA_HEADS)]
    seg_cols = [sum(sks[:i]) for i in range(len(sks))]

    def scores(idx):
        b, h = problems[idx]
        q = q_ref[b * tq:(b + 1) * tq, h * MLA_HEAD_PAD:(h + 1) * MLA_HEAD_PAD]
        m_lane = None
        for k_ref, sk, c0 in zip(k_refs, sks, seg_cols):
            s = _dot_nt(q, k_ref[b * sk:(b + 1) * sk, h * MLA_HEAD_PAD:(h + 1) * MLA_HEAD_PAD])
            s_ref[idx % 2, :, c0:c0 + sk] = s
            for c in range(sk // LANES):
                piece = s[:, c * LANES:(c + 1) * LANES]
                m_lane = piece if m_lane is None else jnp.maximum(m_lane, piece)
        return m_lane.max(axis=-1, keepdims=True)

    def weighted_values(idx, m):
        b, h = problems[idx]
        acc = None
        for v_ref, sk, c0 in zip(v_refs, sks, seg_cols):
            p = jnp.exp2(s_ref[idx % 2, :, c0:c0 + sk] - m).astype(BF16)
            v = v_ref[b * sk:(b + 1) * sk, h * MLA_V:(h + 1) * MLA_V]
            part = _dot(p, jnp.concatenate([v, jnp.ones_like(v)], axis=1))
            acc = part if acc is None else acc + part
        o = acc[:, :MLA_V] / acc[:, MLA_V:]
        o_ref[b * tq:(b + 1) * tq, h * MLA_V:(h + 1) * MLA_V] = o.astype(o_ref.dtype)

    m = scores(0)
    for idx in range(len(problems)):
        m_next = scores(idx + 1) if idx + 1 < len(problems) else None
        weighted_values(idx, m)
        m = m_next


def _attn_kernel(*refs, n_seg, nb, tq, sks, n_cast):
    n_in = 1 + 2 * n_seg + n_cast
    _run_casts(refs[1 + 2 * n_seg:n_in], refs[n_in + 1:n_in + 1 + n_cast])
    _attend(refs[0], refs[1:1 + n_seg], refs[1 + n_seg:1 + 2 * n_seg], refs[n_in],
            refs[n_in + 1 + n_cast], nb=nb, tq=tq, sks=sks)


def _attn_call(q, ks, vs, *, n_batch, seq, sks, nb, tq, name, cast_jobs=()):
    n_seg = len(ks)
    tiles = seq // tq
    assert nb == 1 or tiles == 1
    hp, hv = MLA_HEADS * MLA_HEAD_PAD, MLA_HEADS * MLA_V
    in_specs = [pl.BlockSpec((nb * tq, hp), lambda b, t: (b * tiles + t, 0))]
    in_specs += [pl.BlockSpec((nb * sk, hp), lambda b, t: (b, 0)) for sk in sks]
    in_specs += [pl.BlockSpec((nb * sk, hv), lambda b, t: (b, 0)) for sk in sks]
    c_in, c_out, c_shape = _cast_specs(cast_jobs, (n_batch // nb) * tiles, lambda b, t: b * tiles + t)
    outs = pl.pallas_call(
        functools.partial(_attn_kernel, n_seg=n_seg, nb=nb, tq=tq, sks=tuple(sks), n_cast=len(cast_jobs)),
        grid=(n_batch // nb, tiles),
        in_specs=in_specs + c_in,
        out_specs=[pl.BlockSpec((nb * tq, hv), lambda b, t: (b * tiles + t, 0))] + c_out,
        out_shape=[jax.ShapeDtypeStruct((n_batch * seq, hv), BF16)] + c_shape,
        scratch_shapes=[pltpu.VMEM((2, tq, sum(sks)), F32)],
        compiler_params=_cparams(("arbitrary", "arbitrary")),
        name=name,
    )(q, *ks, *vs, *[w for w, _ in cast_jobs])
    return outs[0], list(outs[1:])


def _split_specs(rows, width):
    ct = rows.ctx_tiles
    return [pl.BlockSpec((rows.tm, width), lambda i: (jnp.minimum(i, ct - 1), 0)),
            pl.BlockSpec((rows.tm, width), lambda i: (jnp.maximum(i - ct, 0), 0))]


def _ffn_kernel(*refs, th, final, ctx_tiles, n_cast, n_x, mixed):
    it = iter(refs)
    x_refs = [next(it) for _ in range(n_x)]
    if mixed:
        yc_ref, yl_ref, wo_ref = next(it), next(it), next(it)
    mod_ref, g_ref, win_ref, wout_ref, fg_ref = (next(it) for _ in range(5))
    cast_in = [next(it) for _ in range(n_cast)]
    out_refs = [next(it) for _ in range(2 if final else 1)]
    cast_out = [next(it) for _ in range(n_cast)]
    scratch = list(it)
    act_ref = scratch[0]
    _run_casts(cast_in, cast_out)
    is_ctx = pl.program_id(0) < ctx_tiles
    if mixed:
        x1_ref = scratch[1]

        def mix(y_ref, x_ref):
            x1_ref[...] = x_ref[...] + mod_ref[0, 2:3, :] * _dot(y_ref[...], wo_ref[0])

        pl.when(is_ctx)(lambda: mix(yc_ref, x_refs[0]))
        pl.when(jnp.logical_not(is_ctx))(lambda: mix(yl_ref, x_refs[-1]))
        x = x1_ref[...]
    else:
        x = x_refs[0][...]
    h = _modulate(x, g_ref[...], mod_ref[0, 3:4, :], mod_ref[0, 4:5, :]).astype(BF16)
    hidden = wout_ref.shape[1]
    for c in range(hidden // th):
        a = _dot(h, win_ref[0, :, c * th:(c + 1) * th])
        b = _dot(h, win_ref[0, :, hidden + c * th:hidden + (c + 1) * th])
        act_ref[:, c * th:(c + 1) * th] = (_silu(a) * b).astype(BF16)
    y = x + mod_ref[0, 5:6, :] * _dot(act_ref[...], wout_ref[0])
    if not final:
        out_refs[0][...] = y
    else:
        acc_ref = scratch[-1]
        acc_ref[...] = _rms(y, fg_ref[...])

        @pl.when(is_ctx)
        def _():
            out_refs[0][...] = acc_ref[...]

        @pl.when(jnp.logical_not(is_ctx))
        def _():
            out_refs[1][...] = acc_ref[...]


def _ffn_call(x, mod, g, w_in_all, w_out_all, layer, final_g, rows, final, cast_jobs=(), mix=None):
    tm, d = rows.tm, w_in_all.shape[1]
    hidden = w_out_all.shape[1]
    split_x = isinstance(x, tuple)
    assert not split_x or mix is not None
    xs = list(x) if split_x else [x]
    x_specs = _split_specs(rows, d) if split_x else [pl.BlockSpec((tm, d), lambda i: (i, 0))]
    mix_args, mix_specs = [], []
    scratch = [pltpu.VMEM((tm, hidden), BF16)]
    if mix is not None:
        y_ctx, y_lat, w_o_all, w_o_layer = mix
        k = w_o_all.shape[1]
        mix_args = [y_ctx, y_lat, w_o_all]
        mix_specs = _split_specs(rows, k) + [_resident((1, k, d), lambda i: (w_o_layer, 0, 0))]
        scratch.append(pltpu.VMEM((tm, d), F32))
    if final:
        out_specs = _split_specs(rows, d)
        out_shape = [jax.ShapeDtypeStruct((rows.ctx_tiles * tm, d), F32),
                     jax.ShapeDtypeStruct(((rows.n_tiles - rows.ctx_tiles) * tm, d), F32)]
        scratch.append(pltpu.VMEM((tm, d), F32))
    else:
        out_specs = [pl.BlockSpec((tm, d), lambda i: (i, 0))]
        out_shape = [jax.ShapeDtypeStruct((rows.n_tiles * tm, d), F32)]
    c_in, c_out, c_shape = _cast_specs(cast_jobs, rows.n_tiles, lambda i: i)
    outs = pl.pallas_call(
        functools.partial(_ffn_kernel, th=2 * LANES, final=final, ctx_tiles=rows.ctx_tiles,
                          n_cast=len(cast_jobs), n_x=len(xs), mixed=mix is not None),
        grid=(rows.n_tiles,),
        in_specs=x_specs + mix_specs + [
            pl.BlockSpec((1, 6, d), lambda i: (rows.group(i), 0, 0)),
            _resident(g.shape, lambda i: (0, 0)),
            _resident((1, d, 2 * hidden), lambda i: (layer, 0, 0)),
            _resident((1, hidden, d), lambda i: (layer, 0, 0)),
            _resident(final_g.shape, lambda i: (0, 0)),
        ] + c_in,
        out_specs=out_specs + c_out,
        out_shape=out_shape + c_shape,
        scratch_shapes=scratch,
        compiler_params=_cparams(("arbitrary",)),
        name=("ffn_final" if final else "ffn") + ("_mix" if mix is not None else ""),
    )(*xs, *mix_args, mod, g, w_in_all, w_out_all, final_g, *[w for w, _ in cast_jobs])
    n_out = 2 if final else 1
    stream = tuple(outs[:2]) if final else outs[0]
    return stream, list(outs[n_out:])


CONV_HALO = 16


def _conv_kernel(x_ref, xp_ref, xn_ref, mod_ref, g_ref, win_ref, cw_ref, wout_ref, o_ref,
                 h_ref, z_ref, act_ref, *, tn, ctx_tiles, ctx_seq, lat_seq):
    tm, d = x_ref.shape
    hl = CONV_HALO
    shift, scale = mod_ref[0, 0:1, :], mod_ref[0, 1:2, :]
    h_ref[0:hl, :] = _modulate(xp_ref[...], g_ref[...], shift, scale).astype(BF16)
    h_ref[hl:hl + tm, :] = _modulate(x_ref[...], g_ref[...], shift, scale).astype(BF16)
    h_ref[hl + tm:, :] = _modulate(xn_ref[...], g_ref[...], shift, scale).astype(BF16)
    i = pl.program_id(0)
    is_ctx = i < ctx_tiles
    row = lax.broadcasted_iota(jnp.int32, (tm, 1), 0)
    lat_row0 = (jnp.maximum(i - ctx_tiles, 0) % (lat_seq // tm)) * tm
    pos = jnp.where(is_ctx, row & (ctx_seq - 1), row + lat_row0)
    seq = jnp.where(is_ctx, ctx_seq, lat_seq)
    has_prev = pos != 0
    has_next = pos != seq - 1
    for c in range(d // tn):
        sl = slice(c * tn, (c + 1) * tn)
        h = h_ref[...]
        cg = _dot(h, win_ref[:, d + c * tn:d + (c + 1) * tn])
        u = _dot(h, win_ref[:, 2 * d + c * tn:2 * d + (c + 1) * tn])
        z_ref[...] = cg * u
        bg = _dot(h_ref[hl:hl + tm, :], win_ref[:, sl])
        z_prev = jnp.where(has_prev, z_ref[hl - 1:hl - 1 + tm, :], 0.0)
        z_next = jnp.where(has_next, z_ref[hl + 1:hl + 1 + tm, :], 0.0)
        conv = z_prev * cw_ref[0:1, sl] + z_ref[hl:hl + tm, :] * cw_ref[1:2, sl] + z_next * cw_ref[2:3, sl]
        act_ref[:, sl] = (bg * conv).astype(BF16)
    o_ref[...] = x_ref[...] + mod_ref[0, 2:3, :] * _dot(act_ref[...], wout_ref[...])


def _conv_call(x, mod, g, w_in, conv_w, w_out, rows, ctx_seq, lat_seq):
    tm, d = rows.tm, x.shape[1]
    hl = CONV_HALO
    assert tm % ctx_seq == 0 and ctx_seq & (ctx_seq - 1) == 0 and lat_seq % tm == 0 and tm % hl == 0
    last_halo = x.shape[0] // hl - 1
    return pl.pallas_call(
        functools.partial(_conv_kernel, tn=2 * LANES, ctx_tiles=rows.ctx_tiles, ctx_seq=ctx_seq,
                          lat_seq=lat_seq),
        grid=(rows.n_tiles,),
        in_specs=[
            pl.BlockSpec((tm, d), lambda i: (i, 0)),
            pl.BlockSpec((hl, d), lambda i: (jnp.maximum(i * (tm // hl) - 1, 0), 0)),
            pl.BlockSpec((hl, d), lambda i: (jnp.minimum((i + 1) * (tm // hl), last_halo), 0)),
            pl.BlockSpec((1, 6, d), lambda i: (rows.group(i), 0, 0)),
            _resident(g.shape, lambda i: (0, 0)),
            _resident(w_in.shape, lambda i: (0, 0)),
            _resident(conv_w.shape, lambda i: (0, 0)),
            _resident(w_out.shape, lambda i: (0, 0)),
        ],
        out_specs=pl.BlockSpec((tm, d), lambda i: (i, 0)),
        out_shape=jax.ShapeDtypeStruct(x.shape, F32),
        scratch_shapes=[pltpu.VMEM((tm + 2 * hl, d), BF16), pltpu.VMEM((tm + 2 * hl, 2 * LANES), F32),
                        pltpu.VMEM((tm, d), BF16)],
        compiler_params=_cparams(("arbitrary",)),
        name="conv_mixer",
    )(x, x, x, mod, g, w_in, conv_w, w_out)


RET_PROJ_CHUNK = 4 * LANES


def _ret_project(x_ref, mod_ref, g_ref, w_ref, qkv_ref, gate_ref):
    tn = RET_PROJ_CHUNK
    h = _modulate(x_ref[...], g_ref[...], mod_ref[0, 0:1, :], mod_ref[0, 1:2, :]).astype(BF16)
    n_qkv = qkv_ref.shape[1]
    for c in range(n_qkv // tn):
        qkv_ref[:, c * tn:(c + 1) * tn] = _dot(h, w_ref[:, c * tn:(c + 1) * tn]).astype(BF16)
    for c in range(gate_ref.shape[1] // tn):
        gate = _dot(h, w_ref[:, n_qkv + c * tn:n_qkv + (c + 1) * tn])
        gate_ref[:, c * tn:(c + 1) * tn] = _silu(gate).astype(gate_ref.dtype)


def _ret_decays(lr_ref, hd, chunk, k_scale):
    row = lax.broadcasted_iota(jnp.int32, (chunk, chunk), 0).astype(F32)
    col = lax.broadcasted_iota(jnp.int32, (chunk, chunk), 1).astype(F32)
    ridx = lax.broadcasted_iota(jnp.int32, (chunk, 1), 0).astype(F32)
    dist = row - col
    log_gamma = -jnp.exp(lr_ref[hd])
    lg_f, lg_b = log_gamma[0:1, :], log_gamma[1:2, :]
    mask = jnp.where(dist > 0, jnp.exp(jnp.maximum(dist, 0.0) * lg_f),
                     jnp.where(dist < 0, jnp.exp(jnp.maximum(-dist, 0.0) * lg_b), 2.0)) * k_scale
    q_decay = (jnp.exp((ridx + 1.0) * lg_f), jnp.exp((chunk - ridx) * lg_b))
    k_decay = (jnp.exp((chunk - 1.0 - ridx) * lg_f) * k_scale, jnp.exp(ridx * lg_b) * k_scale)
    chunk_decay = (jnp.exp(chunk * lg_f), jnp.exp(chunk * lg_b))
    return mask, q_decay, k_decay, chunk_decay


def _ret_scan(decays, q, k, v, gate_ref, gn_ref, s0_ref, y_ref, sout_ref, o_ref, st_ref, *,
              row0, seq, chunk, hps, dk, dv):
    n_chunks = seq // chunk
    has_init = s0_ref is not None
    (q_ref, q0), (k_ref, k0), (v_ref, v0) = q, k, v
    for hd in range(hps):
        mask, q_decay, k_decay, chunk_decay = decays[hd]
        vs = slice(hd * dv, (hd + 1) * dv)
        for direction in range(2):
            order = range(n_chunks) if direction == 0 else range(n_chunks - 1, -1, -1)
            if has_init:
                st_ref[hd] = s0_ref[direction, hd]
            for step, c in enumerate(order):
                rs = slice(row0 + c * chunk, row0 + (c + 1) * chunk)
                ls = slice(c * chunk, (c + 1) * chunk)
                qc = q_ref[rs, q0 + hd * dk:q0 + (hd + 1) * dk]
                kc = k_ref[rs, k0 + hd * dk:k0 + (hd + 1) * dk]
                vc = v_ref[rs, v0 + hd * dv:v0 + (hd + 1) * dv]
                have_state = has_init or step > 0
                if direction == 0:
                    scores = _dot_nt(qc, kc) * mask
                    o_ref[ls, vs] = _dot(scores.astype(BF16), vc)
                if have_state:
                    o_ref[ls, vs] += _dot(qc, st_ref[hd].astype(BF16)) * q_decay[direction]
                kd = (kc.astype(F32) * k_decay[direction]).T.astype(BF16)
                update = _dot(kd, vc)
                if have_state:
                    st_ref[hd] = st_ref[hd] * chunk_decay[direction] + update
                else:
                    st_ref[hd] = update
            if sout_ref is not None:
                sout_ref[direction, hd] = st_ref[hd]

        o = o_ref[:, vs]
        mu = jnp.mean(o, axis=-1, keepdims=True)
        var = jnp.mean(jnp.square(o - mu), axis=-1, keepdims=True)
        on = (o - mu) * lax.rsqrt(var + EPS) * gn_ref[:, vs]
        ys = slice(row0, row0 + seq)
        y_ref[ys, vs] = (gate_ref[ys, vs].astype(F32) * on).astype(y_ref.dtype)


def _ret_ctx_kernel(x_ref, mod_ref, g_ref, w_ref, lr_ref, gn_ref, y_ref, sout_ref,
                    qkv_s, gate_s, o_s, st_s, *, seq, dk, dv):
    nh = RET_HEADS
    _ret_project(x_ref, mod_ref, g_ref, w_ref, qkv_s, gate_s)
    decays = [_ret_decays(lr_ref, hd, seq, dk ** -0.5) for hd in range(nh)]
    for b in range(x_ref.shape[0] // seq):
        _ret_scan(decays, (qkv_s, 0), (qkv_s, nh * dk), (qkv_s, 2 * nh * dk), gate_s, gn_ref, None,
                  y_ref, sout_ref.at[b, 0], o_s, st_s, row0=b * seq, seq=seq, chunk=seq,
                  hps=nh, dk=dk, dv=dv)


def _ret_ctx_call(x, mod, g, w, log_rate, gn_g, rows, seq, n_gate):
    tm, d = rows.tm, x.shape[1]
    assert tm % seq == 0 and seq <= 2 * LANES
    nh = RET_HEADS
    n_qkv = w.shape[1] - n_gate
    dv = n_gate // nh
    dk = (n_qkv - n_gate) // (2 * nh)
    nb = tm // seq
    m = rows.ctx_tiles * tm
    return pl.pallas_call(
        functools.partial(_ret_ctx_kernel, seq=seq, dk=dk, dv=dv),
        grid=(rows.ctx_tiles,),
        in_specs=[
            pl.BlockSpec((tm, d), lambda i: (i, 0)),
            pl.BlockSpec((1, 6, d), lambda i: (0, 0, 0)),
            _resident(g.shape, lambda i: (0, 0)),
            _resident(w.shape, lambda i: (0, 0)),
            _resident(log_rate.shape, lambda i: (0, 0, 0)),
            _resident(gn_g.shape, lambda i: (0, 0)),
        ],
        out_specs=[pl.BlockSpec((tm, n_gate), lambda i: (i, 0)),
                   pl.BlockSpec((nb, 1, 2, nh, dk, dv), lambda i: (i, 0, 0, 0, 0, 0))],
        out_shape=[jax.ShapeDtypeStruct((m, n_gate), BF16),
                   jax.ShapeDtypeStruct((m // seq, 1, 2, nh, dk, dv), F32)],
        scratch_shapes=[pltpu.VMEM((tm, n_qkv), BF16), pltpu.VMEM((tm, n_gate), BF16),
                        pltpu.VMEM((seq, n_gate), F32), pltpu.VMEM((nh, dk, dv), F32)],
        compiler_params=_cparams(("arbitrary",)),
        name="ret_ctx",
    )(x, mod, g, w, log_rate, gn_g)


RET_LAT_ROWS = 512


def _ret_lat_kernel(x_ref, mod_ref, g_ref, wq_ref, wk_ref, wv_ref, wg_ref, lr_ref, gn_ref, s0_ref,
                    y_ref, h_s, q_s, k_s, v_s, gate_s, o_s, st_s, *, chunk, dk, dv):
    seq = x_ref.shape[0]

    @pl.when(pl.program_id(1) == 0)
    def _():
        h_s[...] = _modulate(x_ref[...], g_ref[...], mod_ref[0, 0:1, :], mod_ref[0, 1:2, :]).astype(BF16)

    rc = min(RET_LAT_ROWS, seq)
    for r in range(seq // rc):
        rs = slice(r * rc, (r + 1) * rc)
        h = h_s[rs, :]
        q_s[rs, :] = _dot(h, wq_ref[...]).astype(BF16)
        k_s[rs, :] = _dot(h, wk_ref[...]).astype(BF16)
        v_s[rs, :] = _dot(h, wv_ref[...]).astype(BF16)
        gate_s[rs, :] = _silu(_dot(h, wg_ref[...])).astype(BF16)
    decays = [_ret_decays(lr_ref, 0, chunk, dk ** -0.5)]
    _ret_scan(decays, (q_s, 0), (k_s, 0), (v_s, 0), gate_s, gn_ref, s0_ref.at[0], y_ref, None,
              o_s, st_s, row0=0, seq=seq, chunk=chunk, hps=1, dk=dk, dv=dv)


def _ret_lat_call(x, mod, g, w, log_rate, gn_g, s0, *, bsz, seq, row0, n_gate):
    d = x.shape[1]
    nh = RET_HEADS
    n_qkv = w.shape[1] - n_gate
    dv = n_gate // nh
    dk = (n_qkv - n_gate) // (2 * nh)
    chunk = min(seq, 2 * LANES)
    assert row0 % seq == 0 and dv % dk == 0
    b0 = row0 // seq
    return pl.pallas_call(
        functools.partial(_ret_lat_kernel, chunk=chunk, dk=dk, dv=dv),
        grid=(bsz, nh),
        in_specs=[
            pl.BlockSpec((seq, d), lambda b, h: (b0 + b, 0)),
            pl.BlockSpec((1, 6, d), lambda b, h: (1 + b, 0, 0)),
            _resident(g.shape, lambda b, h: (0, 0)),
            pl.BlockSpec((d, dk), lambda b, h: (0, h)),
            pl.BlockSpec((d, dk), lambda b, h: (0, nh + h)),
            pl.BlockSpec((d, dv), lambda b, h: (0, (2 * nh * dk) // dv + h)),
            pl.BlockSpec((d, dv), lambda b, h: (0, n_qkv // dv + h)),
            pl.BlockSpec((1, 2, 1), lambda b, h: (h, 0, 0)),
            pl.BlockSpec((1, dv), lambda b, h: (0, h)),
            pl.BlockSpec((1, 2, 1, dk, dv), lambda b, h: (b, 0, h, 0, 0)),
        ],
        out_specs=pl.BlockSpec((seq, dv), lambda b, h: (b, h)),
        out_shape=jax.ShapeDtypeStruct((bsz * seq, n_gate), BF16),
        scratch_shapes=[pltpu.VMEM((seq, d), BF16), pltpu.VMEM((seq, dk), BF16), pltpu.VMEM((seq, dk), BF16),
                        pltpu.VMEM((seq, dv), BF16), pltpu.VMEM((seq, dv), BF16),
                        pltpu.VMEM((seq, dv), F32), pltpu.VMEM((1, dk, dv), F32)],
        compiler_params=_cparams(("arbitrary", "arbitrary")),
        name="ret_lat",
    )(x, mod, g, w, w, w, w, log_rate, gn_g, s0)


def _rope_swap_index():
    f = ROPE_AXIS_FREQS
    idx = jnp.arange(MLA_ROPE)
    return jnp.where((idx // f) % 2 == 0, idx + f, idx - f)


def _mla_weights(w_a, q_norm_g, kv_norm_g, w_q_b, w_kv_b):
    swap = _rope_swap_index()
    nq = MLA_Q_RANK + MLA_KV_RANK
    w_kpe = w_a[:, nq:]
    w_kpe_sw = w_kpe[:, swap]
    wa = jnp.concatenate([w_a[:, :nq], w_kpe, w_kpe, w_kpe_sw, w_kpe_sw], axis=1).astype(BF16)
    wq = w_q_b.reshape(MLA_Q_RANK, MLA_HEADS, MLA_NOPE + MLA_ROPE)
    wq_nope = wq[:, :, :MLA_NOPE].reshape(MLA_Q_RANK, MLA_HEADS * MLA_NOPE)
    wq_pe = wq[:, :, MLA_NOPE:]
    wq_all = jnp.concatenate(
        [wq_nope, wq_pe.reshape(MLA_Q_RANK, -1), wq_pe[:, :, swap].reshape(MLA_Q_RANK, -1)],
        axis=1).astype(BF16)
    wkv = w_kv_b.reshape(MLA_KV_RANK, MLA_HEADS, MLA_NOPE + MLA_V)
    wkv_all = jnp.concatenate(
        [wkv[:, :, :MLA_NOPE].reshape(MLA_KV_RANK, -1), wkv[:, :, MLA_NOPE:].reshape(MLA_KV_RANK, -1)],
        axis=1).astype(BF16)
    return wa, q_norm_g[None, :], kv_norm_g[None, :], wq_all, wkv_all


def _rope_tables(n_tokens):
    f = ROPE_AXIS_FREQS
    f32 = np.float32
    rows = n_tokens // GRID_W
    r = np.repeat(np.arange(rows, dtype=f32), GRID_W)
    col = np.tile(np.arange(GRID_W, dtype=f32), rows)
    inv = (f32(ROPE_THETA) ** (-np.arange(f, dtype=f32) / f32(f))).astype(f32)
    ang_r, ang_c = r[:, None] * inv, col[:, None] * inv
    cos = np.concatenate([np.cos(ang_r)] * 2 + [np.cos(ang_c)] * 2, axis=1)
    sin = np.concatenate([-np.sin(ang_r), np.sin(ang_r), -np.sin(ang_c), np.sin(ang_c)], axis=1)
    reps = LANES // MLA_ROPE
    return (jnp.asarray(np.concatenate([cos] * reps, axis=1), F32),
            jnp.asarray(np.concatenate([sin] * reps, axis=1), F32))


def kernel(x_prompt, x_sample, c, c_ctx, cache_mla_ckv, cache_mla_kpe, state_ret, ada_w, ada_b, norm_mix_g, norm_ffn_g, mla_w_a, mla_q_norm_g, mla_kv_norm_g, mla_w_q_b, mla_w_kv_b, mla_w_o, conv_w_in, conv_w, conv_w_out, ret_w_in, ret_log_rate, ret_gn_g, ret_w_out, ffn_w_in, ffn_w_out, final_norm_g):
    batch, seq, d = x_prompt.shape
    dec_batch, dec_seq, _ = x_sample.shape
    depth = ada_w.shape[0]
    n_ctx = batch * seq
    n_lat = dec_batch * dec_seq
    past = cache_mla_ckv.shape[2]
    assert 1 + dec_batch <= MOD_ROWS

    cvec = jnp.zeros((MOD_ROWS, d), F32).at[0].set(c_ctx).at[1:1 + dec_batch].set(c)
    mod_all = _ada_call(cvec, ada_w, ada_b).reshape(depth, MOD_ROWS, 6, d)

    rows_s = _Rows(n_ctx, dec_batch, dec_seq, min(ROW_TILE_FUSED, dec_seq))
    rows_l = _Rows(n_ctx, dec_batch, dec_seq, min(ROW_TILE, dec_seq))
    rope_tabs = _rope_tables(dec_seq)
    final_g = final_norm_g[None, :]
    mla_w_o_bf = mla_w_o.astype(BF16)

    def mixer_cast_jobs(layer):
        if layer >= depth:
            return []
        kind, j = layer % N_MIXERS, layer // N_MIXERS
        if kind == 1:
            return [(conv_w_in, j), (conv_w_out, j)]
        if kind == 2:
            return [(ret_w_in, j), (ret_w_out, j)]
        return []

    x = (x_prompt.reshape(n_ctx, d), x_sample.reshape(n_lat, d))
    new_ckv, new_kpe, new_ret = [], [], []
    ffn_bf = mixer_bf = None
    for i in range(depth):
        kind, j = i % N_MIXERS, i // N_MIXERS
        mod = mod_all[i]
        g_mix = norm_mix_g[i][None, :]
        if ffn_bf is None:
            assert kind == 0
        if kind == 0:
            wts = _mla_weights(mla_w_a[j], mla_q_norm_g[j], mla_kv_norm_g[j], mla_w_q_b[j], mla_w_kv_b[j])
            split = isinstance(x, tuple)
            o_c, ckv_c, kpe_c = _mla_ctx_call(x[0] if split else x, mod, g_mix, wts, rows_l, seq)
            ql, kl, vl = _mla_proj_call(
                x[1] if split else x, 0 if split else rows_l.ctx_tiles, mod, g_mix, wts, rows_l,
                tile0=rows_l.ctx_tiles, n_tiles=rows_l.n_tiles - rows_l.ctx_tiles, rope_tabs=rope_tabs)
            new_ckv.append(ckv_c.reshape(batch, seq, MLA_KV_RANK))
            new_kpe.append(kpe_c.reshape(batch, seq, MLA_ROPE))
            kpe_rep = jnp.concatenate([cache_mla_kpe[:, j]] * (LANES // MLA_ROPE), axis=-1)
            kp, vp = _cache_expand_call(
                cache_mla_ckv[:, j].reshape(dec_batch * past, MLA_KV_RANK),
                kpe_rep.reshape(dec_batch * past, LANES), wts[4])
            jobs = [] if ffn_bf is not None else [(ffn_w_in, i), (ffn_w_out, i)]
            o_l, cast = _attn_call(ql, [kp, kl], [vp, vl], n_batch=dec_batch, seq=dec_seq,
                                   sks=[past, dec_seq], nb=1, tq=min(ATTN_Q_TILE, dec_seq),
                                   name="attn_lat", cast_jobs=jobs)
            if jobs:
                ffn_bf = cast
            mix = (o_c, o_l, mla_w_o_bf, j)
        elif kind == 1:
            mix = None
            x = _conv_call(x, mod, g_mix, mixer_bf[0][0], conv_w[j], mixer_bf[1][0], rows_l, seq, dec_seq)
        else:
            n_gate = ret_w_out.shape[1]
            lr = ret_log_rate[j].T[:, :, None]
            gn = ret_gn_g[j][None, :]
            rows_seq = _Rows(n_ctx, dec_batch, dec_seq, seq)
            y_c, st = _ret_ctx_call(x, mod, g_mix, mixer_bf[0][0], lr, gn, rows_seq, seq, n_gate)
            y_l = _ret_lat_call(x, mod, g_mix, mixer_bf[0][0], lr, gn, state_ret[:, j],
                                bsz=dec_batch, seq=dec_seq, row0=n_ctx, n_gate=n_gate)
            new_ret.append(st)
            mix = (y_c, y_l, mixer_bf[1], 0)
        last = i == depth - 1
        jobs = [] if last else [(ffn_w_in, i + 1), (ffn_w_out, i + 1)] + mixer_cast_jobs(i + 1)
        x, cast = _ffn_call(x, mod, norm_ffn_g[i][None, :], ffn_bf[0], ffn_bf[1], 0, final_g,
                            rows_l if mix is None else rows_s, final=last, cast_jobs=jobs, mix=mix)
        ffn_bf, mixer_bf = cast[:2], cast[2:]

    y_prompt = x[0].reshape(batch, seq, d)
    y_sample = x[1].reshape(dec_batch, dec_seq, d)
    return (y_prompt, y_sample, jnp.stack(new_ckv, axis=1), jnp.stack(new_kpe, axis=1),
            jnp.concatenate(new_ret, axis=1))
```

```python
import functools
import math

import jax
import jax.numpy as jnp
import numpy as np
from jax import lax
from jax.experimental import pallas as pl
from jax.experimental.pallas import tpu as pltpu

F32 = jnp.float32
BF16 = jnp.bfloat16

N_MIXERS = 3
MLA_HEADS = 8
MLA_NOPE = 128
MLA_ROPE = 64
MLA_V = 128
MLA_Q_RANK = 384
MLA_KV_RANK = 256
MLA_SCALE = (MLA_NOPE + MLA_ROPE) ** -0.5
MLA_Q_SCALE = MLA_SCALE * 1.4426950408889634
ROPE_THETA = 10000.0
ROPE_AXIS_FREQS = MLA_ROPE // 4
GRID_W = 64
RET_HEADS = 4
EPS = 1e-6

LANES = 128
MLA_HEAD_PAD = 2 * LANES
MOD_ROWS = 8
VMEM_LIMIT = 56 * 1024 * 1024

ROW_TILE = 1024
ROW_TILE_FUSED = 512
ATTN_Q_TILE = 512


def _cparams(sem):
    return pltpu.CompilerParams(dimension_semantics=sem, vmem_limit_bytes=VMEM_LIMIT)


def _resident(shape, index_map):
    return pl.BlockSpec(shape, index_map, pipeline_mode=pl.Buffered(1))


def _rms(x, g):
    return x * lax.rsqrt(jnp.mean(x * x, axis=-1, keepdims=True) + EPS) * g


def _modulate(x, g, shift, scale):
    return _rms(x, g) * (1.0 + scale) + shift


def _silu(x):
    return x * jax.nn.sigmoid(x)


def _dot(a, b):
    return jnp.dot(a, b, preferred_element_type=F32)


def _dot_nt(a, b):
    return lax.dot_general(a, b, (((1,), (1,)), ((), ())), preferred_element_type=F32)


MAX_CAST_SLABS = 16


def _cast_specs(jobs, n_steps, step_of):
    n_slabs = math.gcd(n_steps, MAX_CAST_SLABS)
    per = n_steps // n_slabs

    def slab_of_step(*g):
        return step_of(*g) // per

    in_specs, out_specs, out_shape = [], [], []
    for w, layer in jobs:
        _, k, n = w.shape
        blk = (1, k // n_slabs, n)
        in_specs.append(pl.BlockSpec(blk, lambda *g, layer=layer: (layer, slab_of_step(*g), 0)))
        out_specs.append(pl.BlockSpec(blk, lambda *g: (0, slab_of_step(*g), 0)))
        out_shape.append(jax.ShapeDtypeStruct((1, k, n), BF16))
    return in_specs, out_specs, out_shape


def _run_casts(in_refs, out_refs):
    for src, dst in zip(in_refs, out_refs):
        dst[...] = src[...].astype(BF16)


def _ada_kernel(c_ref, w_ref, b_ref, o_ref):
    a = _silu(c_ref[...]).astype(BF16)
    o_ref[0] = _dot(a, w_ref[0].astype(BF16)) + b_ref[0]


def _ada_call(cvec, ada_w, ada_b):
    depth, d, n = ada_w.shape
    tn = n // 2
    return pl.pallas_call(
        _ada_kernel,
        grid=(depth, n // tn),
        in_specs=[
            pl.BlockSpec((MOD_ROWS, d), lambda l, j: (0, 0)),
            pl.BlockSpec((1, d, tn), lambda l, j: (l, 0, j)),
            pl.BlockSpec((1, 1, tn), lambda l, j: (l, 0, j)),
        ],
        out_specs=pl.BlockSpec((1, MOD_ROWS, tn), lambda l, j: (l, 0, j)),
        out_shape=jax.ShapeDtypeStruct((depth, MOD_ROWS, n), F32),
        compiler_params=_cparams(("arbitrary", "arbitrary")),
        name="ada",
    )(cvec, ada_w, ada_b.reshape(depth, 1, n))


class _Rows:
    def __init__(self, n_ctx, dec_batch, dec_seq, tm):
        assert n_ctx % tm == 0 and dec_seq % tm == 0
        self.tm = tm
        self.ctx_tiles = n_ctx // tm
        self.seq_tiles = dec_seq // tm
        self.n_tiles = self.ctx_tiles + dec_batch * self.seq_tiles

    def group(self, i):
        lat = jnp.maximum(i - self.ctx_tiles, 0) // self.seq_tiles
        return jnp.where(i < self.ctx_tiles, 0, 1 + lat)


def _mla_project(x_ref, mod_ref, g_ref, wa_ref, qg_ref, kvg_ref, wq_ref, wkv_ref, rope_refs,
                 q_ref, k_ref, v_ref, cache_refs):
    h = _modulate(x_ref[...], g_ref[...], mod_ref[0, 0:1, :], mod_ref[0, 1:2, :]).astype(BF16)
    a = _dot(h, wa_ref[...])
    nq = MLA_Q_RANK + MLA_KV_RANK
    ckv_n = _rms(a[:, MLA_Q_RANK:nq], kvg_ref[...])
    kpe = a[:, nq:nq + LANES]
    if cache_refs is not None:
        cache_refs[0][...] = ckv_n
        cache_refs[1][...] = kpe[:, :MLA_ROPE]
    if rope_refs is not None:
        cos, sin = rope_refs[0][...], rope_refs[1][...]
        kpe = kpe * cos + a[:, nq + LANES:nq + 2 * LANES] * sin
    kpe = kpe.astype(BF16)

    qn = _rms(a[:, :MLA_Q_RANK], qg_ref[...]).astype(BF16)
    nh = MLA_HEADS
    n_nope, n_pe = nh * MLA_NOPE, nh * MLA_ROPE
    q_nope = _dot(qn, wq_ref[:, :n_nope])
    q_pe = _dot(qn, wq_ref[:, n_nope:n_nope + n_pe])
    if rope_refs is not None:
        q_sw = _dot(qn, wq_ref[:, n_nope + n_pe:n_nope + 2 * n_pe])
        cos, sin = cos * MLA_Q_SCALE, sin * MLA_Q_SCALE
    kv = _dot(ckv_n.astype(BF16), wkv_ref[...])
    low_half = lax.broadcasted_iota(jnp.int32, (1, LANES), 1) < MLA_ROPE
    for pair in range(nh // 2):
        ps = slice(pair * LANES, (pair + 1) * LANES)
        if rope_refs is not None:
            pe = q_pe[:, ps] * cos + q_sw[:, ps] * sin
        else:
            pe = q_pe[:, ps] * MLA_Q_SCALE
        for hd, keep in ((2 * pair, low_half), (2 * pair + 1, jnp.logical_not(low_half))):
            lo = hd * MLA_HEAD_PAD
            hs = slice(hd * LANES, (hd + 1) * LANES)
            q_ref[:, lo:lo + LANES] = (q_nope[:, hs] * MLA_Q_SCALE).astype(BF16)
            q_ref[:, lo + LANES:lo + 2 * LANES] = jnp.where(keep, pe, 0.0).astype(BF16)
            k_ref[:, lo:lo + LANES] = kv[:, hs].astype(BF16)
            k_ref[:, lo + LANES:lo + 2 * LANES] = kpe
    v_ref[...] = kv[:, nh * LANES:].astype(BF16)


def _mla_proj_kernel(x_ref, mod_ref, g_ref, wa_ref, qg_ref, kvg_ref, wq_ref, wkv_ref, ck_ref, sk_ref,
                     q_ref, k_ref, v_ref):
    _mla_project(x_ref, mod_ref, g_ref, wa_ref, qg_ref, kvg_ref, wq_ref, wkv_ref, (ck_ref, sk_ref),
                 q_ref, k_ref, v_ref, None)


def _mla_proj_call(x, x_tile0, mod, g, wts, rows, *, tile0, n_tiles, rope_tabs):
    tm, d = rows.tm, x.shape[1]
    wa, qg, kvg, wq, wkv = wts
    full = lambda arr: _resident(arr.shape, lambda i: (0,) * arr.ndim)
    in_specs = [
        pl.BlockSpec((tm, d), lambda i: (i + x_tile0, 0)),
        pl.BlockSpec((1, 6, d), lambda i: (rows.group(i + tile0), 0, 0)),
        full(g), full(wa), full(qg), full(kvg), full(wq), full(wkv),
    ]
    in_specs += [pl.BlockSpec((tm, LANES), lambda i: (i % rows.seq_tiles, 0)) for _ in rope_tabs]
    m = n_tiles * tm
    hp, hv = MLA_HEADS * MLA_HEAD_PAD, MLA_HEADS * MLA_V
    return pl.pallas_call(
        _mla_proj_kernel,
        grid=(n_tiles,),
        in_specs=in_specs,
        out_specs=[pl.BlockSpec((tm, hp), lambda i: (i, 0)), pl.BlockSpec((tm, hp), lambda i: (i, 0)),
                   pl.BlockSpec((tm, hv), lambda i: (i, 0))],
        out_shape=[jax.ShapeDtypeStruct((m, hp), BF16), jax.ShapeDtypeStruct((m, hp), BF16),
                   jax.ShapeDtypeStruct((m, hv), BF16)],
        compiler_params=_cparams(("arbitrary",)),
        name="mla_proj_lat",
    )(x, mod, g, wa, qg, kvg, wq, wkv, *rope_tabs)


def _mla_ctx_kernel(x_ref, mod_ref, g_ref, wa_ref, qg_ref, kvg_ref, wq_ref, wkv_ref,
                    o_ref, ckv_ref, kpe_ref, q_s, k_s, v_s, s_ref, *, seq):
    _mla_project(x_ref, mod_ref, g_ref, wa_ref, qg_ref, kvg_ref, wq_ref, wkv_ref, None,
                 q_s, k_s, v_s, (ckv_ref, kpe_ref))
    _attend(q_s, [k_s], [v_s], o_ref, s_ref, nb=x_ref.shape[0] // seq, tq=seq, sks=(seq,))


def _mla_ctx_call(x, mod, g, wts, rows, seq):
    tm, d = rows.tm, x.shape[1]
    assert tm % seq == 0
    wa, qg, kvg, wq, wkv = wts
    full = lambda arr: _resident(arr.shape, lambda i: (0,) * arr.ndim)
    m = rows.ctx_tiles * tm
    hp, hv = MLA_HEADS * MLA_HEAD_PAD, MLA_HEADS * MLA_V
    return pl.pallas_call(
        functools.partial(_mla_ctx_kernel, seq=seq),
        grid=(rows.ctx_tiles,),
        in_specs=[
            pl.BlockSpec((tm, d), lambda i: (i, 0)),
            pl.BlockSpec((1, 6, d), lambda i: (0, 0, 0)),
            full(g), full(wa), full(qg), full(kvg), full(wq), full(wkv),
        ],
        out_specs=[pl.BlockSpec((tm, hv), lambda i: (i, 0)),
                   pl.BlockSpec((tm, MLA_KV_RANK), lambda i: (i, 0)),
                   pl.BlockSpec((tm, MLA_ROPE), lambda i: (i, 0))],
        out_shape=[jax.ShapeDtypeStruct((m, hv), BF16),
                   jax.ShapeDtypeStruct((m, MLA_KV_RANK), F32),
                   jax.ShapeDtypeStruct((m, MLA_ROPE), F32)],
        scratch_shapes=[pltpu.VMEM((tm, hp), BF16), pltpu.VMEM((tm, hp), BF16), pltpu.VMEM((tm, hv), BF16),
                        pltpu.VMEM((2, seq, seq), F32)],
        compiler_params=_cparams(("arbitrary",)),
        name="mla_ctx",
    )(x, mod, g, wa, qg, kvg, wq, wkv)


def _cache_expand_kernel(ckv_ref, kpe_ref, wkv_ref, k_ref, v_ref):
    kv = _dot(ckv_ref[...].astype(BF16), wkv_ref[...])
    kpe = kpe_ref[...].astype(BF16)
    nh = MLA_HEADS
    for hd in range(nh):
        lo = hd * MLA_HEAD_PAD
        k_ref[:, lo:lo + LANES] = kv[:, hd * LANES:(hd + 1) * LANES].astype(BF16)
        k_ref[:, lo + LANES:lo + 2 * LANES] = kpe
    v_ref[...] = kv[:, nh * LANES:].astype(BF16)


def _cache_expand_call(ckv, kpe_pad, wkv):
    m = ckv.shape[0]
    tm = min(m, 512)
    hp = MLA_HEADS * MLA_HEAD_PAD
    return pl.pallas_call(
        _cache_expand_kernel,
        grid=(m // tm,),
        in_specs=[
            pl.BlockSpec((tm, MLA_KV_RANK), lambda i: (i, 0)),
            pl.BlockSpec((tm, LANES), lambda i: (i, 0)),
            _resident(wkv.shape, lambda i: (0, 0)),
        ],
        out_specs=[pl.BlockSpec((tm, hp), lambda i: (i, 0)),
                   pl.BlockSpec((tm, MLA_HEADS * MLA_V), lambda i: (i, 0))],
        out_shape=[jax.ShapeDtypeStruct((m, hp), BF16),
                   jax.ShapeDtypeStruct((m, MLA_HEADS * MLA_V), BF16)],
        compiler_params=_cparams(("arbitrary",)),
        name="mla_cache_expand",
    )(ckv, kpe_pad, wkv)


def _attend(q_ref, k_refs, v_refs, o_ref, s_ref, *, nb, tq, sks):
    problems = [(b, h) for b in range(nb) for h in range(MLA_HEADS)]
    seg_cols = [sum(sks[:i]) for i in range(len(sks))]

    def scores(idx):
        b, h = problems[idx]
        q = q_ref[b * tq:(b + 1) * tq, h * MLA_HEAD_PAD:(h + 1) * MLA_HEAD_PAD]
        m_lane = None
        for k_ref, sk, c0 in zip(k_refs, sks, seg_cols):
            s = _dot_nt(q, k_ref[b * sk:(b + 1) * sk, h * MLA_HEAD_PAD:(h + 1) * MLA_HEAD_PAD])
            s_ref[idx % 2, :, c0:c0 + sk] = s
            for c in range(sk // LANES):
                piece = s[:, c * LANES:(c + 1) * LANES]
                m_lane = piece if m_lane is None else jnp.maximum(m_lane, piece)
        return m_lane.max(axis=-1, keepdims=True)

    def weighted_values(idx, m):
        b, h = problems[idx]
        acc = None
        for v_ref, sk, c0 in zip(v_refs, sks, seg_cols):
            p = jnp.exp2(s_ref[idx % 2, :, c0:c0 + sk] - m).astype(BF16)
            v = v_ref[b * sk:(b + 1) * sk, h * MLA_V:(h + 1) * MLA_V]
            part = _dot(p, jnp.concatenate([v, jnp.ones_like(v)], axis=1))
            acc = part if acc is None else acc + part
        o = acc[:, :MLA_V] / acc[:, MLA_V:]
        o_ref[b * tq:(b + 1) * tq, h * MLA_V:(h + 1) * MLA_V] = o.astype(o_ref.dtype)

    m = scores(0)
    for idx in range(len(problems)):
        m_next = scores(idx + 1) if idx + 1 < len(problems) else None
        weighted_values(idx, m)
        m = m_next


def _attn_kernel(*refs, n_seg, nb, tq, sks, n_cast):
    n_in = 1 + 2 * n_seg + n_cast
    _run_casts(refs[1 + 2 * n_seg:n_in], refs[n_in + 1:n_in + 1 + n_cast])
    _attend(refs[0], refs[1:1 + n_seg], refs[1 + n_seg:1 + 2 * n_seg], refs[n_in],
            refs[n_in + 1 + n_cast], nb=nb, tq=tq, sks=sks)


def _attn_call(q, ks, vs, *, n_batch, seq, sks, nb, tq, name, cast_jobs=()):
    n_seg = len(ks)
    tiles = seq // tq
    assert nb == 1 or tiles == 1
    hp, hv = MLA_HEADS * MLA_HEAD_PAD, MLA_HEADS * MLA_V
    in_specs = [pl.BlockSpec((nb * tq, hp), lambda b, t: (b * tiles + t, 0))]
    in_specs += [pl.BlockSpec((nb * sk, hp), lambda b, t: (b, 0)) for sk in sks]
    in_specs += [pl.BlockSpec((nb * sk, hv), lambda b, t: (b, 0)) for sk in sks]
    c_in, c_out, c_shape = _cast_specs(cast_jobs, (n_batch // nb) * tiles, lambda b, t: b * tiles + t)
    outs = pl.pallas_call(
        functools.partial(_attn_kernel, n_seg=n_seg, nb=nb, tq=tq, sks=tuple(sks), n_cast=len(cast_jobs)),
        grid=(n_batch // nb, tiles),
        in_specs=in_specs + c_in,
        out_specs=[pl.BlockSpec((nb * tq, hv), lambda b, t: (b * tiles + t, 0))] + c_out,
        out_shape=[jax.ShapeDtypeStruct((n_batch * seq, hv), BF16)] + c_shape,
        scratch_shapes=[pltpu.VMEM((2, tq, sum(sks)), F32)],
        compiler_params=_cparams(("arbitrary", "arbitrary")),
        name=name,
    )(q, *ks, *vs, *[w for w, _ in cast_jobs])
    return outs[0], list(outs[1:])


def _split_specs(rows, width):
    ct = rows.ctx_tiles
    return [pl.BlockSpec((rows.tm, width), lambda i: (jnp.minimum(i, ct - 1), 0)),
            pl.BlockSpec((rows.tm, width), lambda i: (jnp.maximum(i - ct, 0), 0))]


def _ffn_kernel(*refs, th, final, ctx_tiles, n_cast, n_x, mixed):
    it = iter(refs)
    x_refs = [next(it) for _ in range(n_x)]
    if mixed:
        yc_ref, yl_ref, wo_ref = next(it), next(it), next(it)
    mod_ref, g_ref, win_ref, wout_ref, fg_ref = (next(it) for _ in range(5))
    cast_in = [next(it) for _ in range(n_cast)]
    out_refs = [next(it) for _ in range(2 if final else 1)]
    cast_out = [next(it) for _ in range(n_cast)]
    scratch = list(it)
    act_ref = scratch[0]
    _run_casts(cast_in, cast_out)
    is_ctx = pl.program_id(0) < ctx_tiles
    if mixed:
        x1_ref = scratch[1]

        def mix(y_ref, x_ref):
            x1_ref[...] = x_ref[...] + mod_ref[0, 2:3, :] * _dot(y_ref[...], wo_ref[0])

        pl.when(is_ctx)(lambda: mix(yc_ref, x_refs[0]))
        pl.when(jnp.logical_not(is_ctx))(lambda: mix(yl_ref, x_refs[-1]))
        x = x1_ref[...]
    else:
        x = x_refs[0][...]
    h = _modulate(x, g_ref[...], mod_ref[0, 3:4, :], mod_ref[0, 4:5, :]).astype(BF16)
    hidden = wout_ref.shape[1]
    for c in range(hidden // th):
        a = _dot(h, win_ref[0, :, c * th:(c + 1) * th])
        b = _dot(h, win_ref[0, :, hidden + c * th:hidden + (c + 1) * th])
        act_ref[:, c * th:(c + 1) * th] = (_silu(a) * b).astype(BF16)
    y = x + mod_ref[0, 5:6, :] * _dot(act_ref[...], wout_ref[0])
    if not final:
        out_refs[0][...] = y
    else:
        acc_ref = scratch[-1]
        acc_ref[...] = _rms(y, fg_ref[...])

        @pl.when(is_ctx)
        def _():
            out_refs[0][...] = acc_ref[...]

        @pl.when(jnp.logical_not(is_ctx))
        def _():
            out_refs[1][...] = acc_ref[...]


def _ffn_call(x, mod, g, w_in_all, w_out_all, layer, final_g, rows, final, cast_jobs=(), mix=None):
    tm, d = rows.tm, w_in_all.shape[1]
    hidden = w_out_all.shape[1]
    split_x = isinstance(x, tuple)
    assert not split_x or mix is not None
    xs = list(x) if split_x else [x]
    x_specs = _split_specs(rows, d) if split_x else [pl.BlockSpec((tm, d), lambda i: (i, 0))]
    mix_args, mix_specs = [], []
    scratch = [pltpu.VMEM((tm, hidden), BF16)]
    if mix is not None:
        y_ctx, y_lat, w_o_all, w_o_layer = mix
        k = w_o_all.shape[1]
        mix_args = [y_ctx, y_lat, w_o_all]
        mix_specs = _split_specs(rows, k) + [_resident((1, k, d), lambda i: (w_o_layer, 0, 0))]
        scratch.append(pltpu.VMEM((tm, d), F32))
    if final:
        out_specs = _split_specs(rows, d)
        out_shape = [jax.ShapeDtypeStruct((rows.ctx_tiles * tm, d), F32),
                     jax.ShapeDtypeStruct(((rows.n_tiles - rows.ctx_tiles) * tm, d), F32)]
        scratch.append(pltpu.VMEM((tm, d), F32))
    else:
        out_specs = [pl.BlockSpec((tm, d), lambda i: (i, 0))]
        out_shape = [jax.ShapeDtypeStruct((rows.n_tiles * tm, d), F32)]
    c_in, c_out, c_shape = _cast_specs(cast_jobs, rows.n_tiles, lambda i: i)
    outs = pl.pallas_call(
        functools.partial(_ffn_kernel, th=2 * LANES, final=final, ctx_tiles=rows.ctx_tiles,
                          n_cast=len(cast_jobs), n_x=len(xs), mixed=mix is not None),
        grid=(rows.n_tiles,),
        in_specs=x_specs + mix_specs + [
            pl.BlockSpec((1, 6, d), lambda i: (rows.group(i), 0, 0)),
            _resident(g.shape, lambda i: (0, 0)),
            _resident((1, d, 2 * hidden), lambda i: (layer, 0, 0)),
            _resident((1, hidden, d), lambda i: (layer, 0, 0)),
            _resident(final_g.shape, lambda i: (0, 0)),
        ] + c_in,
        out_specs=out_specs + c_out,
        out_shape=out_shape + c_shape,
        scratch_shapes=scratch,
        compiler_params=_cparams(("arbitrary",)),
        name=("ffn_final" if final else "ffn") + ("_mix" if mix is not None else ""),
    )(*xs, *mix_args, mod, g, w_in_all, w_out_all, final_g, *[w for w, _ in cast_jobs])
    n_out = 2 if final else 1
    stream = tuple(outs[:2]) if final else outs[0]
    return stream, list(outs[n_out:])


CONV_HALO = 16


def _conv_kernel(x_ref, xp_ref, xn_ref, mod_ref, g_ref, win_ref, cw_ref, wout_ref, o_ref,
                 h_ref, z_ref, act_ref, *, tn, ctx_tiles, ctx_seq, lat_seq):
    tm, d = x_ref.shape
    hl = CONV_HALO
    shift, scale = mod_ref[0, 0:1, :], mod_ref[0, 1:2, :]
    h_ref[0:hl, :] = _modulate(xp_ref[...], g_ref[...], shift, scale).astype(BF16)
    h_ref[hl:hl + tm, :] = _modulate(x_ref[...], g_ref[...], shift, scale).astype(BF16)
    h_ref[hl + tm:, :] = _modulate(xn_ref[...], g_ref[...], shift, scale).astype(BF16)
    i = pl.program_id(0)
    is_ctx = i < ctx_tiles
    row = lax.broadcasted_iota(jnp.int32, (tm, 1), 0)
    lat_row0 = (jnp.maximum(i - ctx_tiles, 0) % (lat_seq // tm)) * tm
    pos = jnp.where(is_ctx, row & (ctx_seq - 1), row + lat_row0)
    seq = jnp.where(is_ctx, ctx_seq, lat_seq)
    has_prev = pos != 0
    has_next = pos != seq - 1
    for c in range(d // tn):
        sl = slice(c * tn, (c + 1) * tn)
        h = h_ref[...]
        cg = _dot(h, win_ref[:, d + c * tn:d + (c + 1) * tn])
        u = _dot(h, win_ref[:, 2 * d + c * tn:2 * d + (c + 1) * tn])
        z_ref[...] = cg * u
        bg = _dot(h_ref[hl:hl + tm, :], win_ref[:, sl])
        z_prev = jnp.where(has_prev, z_ref[hl - 1:hl - 1 + tm, :], 0.0)
        z_next = jnp.where(has_next, z_ref[hl + 1:hl + 1 + tm, :], 0.0)
        conv = z_prev * cw_ref[0:1, sl] + z_ref[hl:hl + tm, :] * cw_ref[1:2, sl] + z_next * cw_ref[2:3, sl]
        act_ref[:, sl] = (bg * conv).astype(BF16)
    o_ref[...] = x_ref[...] + mod_ref[0, 2:3, :] * _dot(act_ref[...], wout_ref[...])


def _conv_call(x, mod, g, w_in, conv_w, w_out, rows, ctx_seq, lat_seq):
    tm, d = rows.tm, x.shape[1]
    hl = CONV_HALO
    assert tm % ctx_seq == 0 and ctx_seq & (ctx_seq - 1) == 0 and lat_seq % tm == 0 and tm % hl == 0
    last_halo = x.shape[0] // hl - 1
    return pl.pallas_call(
        functools.partial(_conv_kernel, tn=2 * LANES, ctx_tiles=rows.ctx_tiles, ctx_seq=ctx_seq,
                          lat_seq=lat_seq),
        grid=(rows.n_tiles,),
        in_specs=[
            pl.BlockSpec((tm, d), lambda i: (i, 0)),
            pl.BlockSpec((hl, d), lambda i: (jnp.maximum(i * (tm // hl) - 1, 0), 0)),
            pl.BlockSpec((hl, d), lambda i: (jnp.minimum((i + 1) * (tm // hl), last_halo), 0)),
            pl.BlockSpec((1, 6, d), lambda i: (rows.group(i), 0, 0)),
            _resident(g.shape, lambda i: (0, 0)),
            _resident(w_in.shape, lambda i: (0, 0)),
            _resident(conv_w.shape, lambda i: (0, 0)),
            _resident(w_out.shape, lambda i: (0, 0)),
        ],
        out_specs=pl.BlockSpec((tm, d), lambda i: (i, 0)),
        out_shape=jax.ShapeDtypeStruct(x.shape, F32),
        scratch_shapes=[pltpu.VMEM((tm + 2 * hl, d), BF16), pltpu.VMEM((tm + 2 * hl, 2 * LANES), F32),
                        pltpu.VMEM((tm, d), BF16)],
        compiler_params=_cparams(("arbitrary",)),
        name="conv_mixer",
    )(x, x, x, mod, g, w_in, conv_w, w_out)


RET_PROJ_CHUNK = 4 * LANES
RET_COL_BLOCK = 2 * LANES
RET_ROW_BLOCK = 64


def _ret_project(x_ref, mod_ref, g_ref, w_ref, qkv_ref, gate_ref):
    tn = RET_PROJ_CHUNK
    h = _modulate(x_ref[...], g_ref[...], mod_ref[0, 0:1, :], mod_ref[0, 1:2, :]).astype(BF16)
    n_qkv = qkv_ref.shape[1]
    for c in range(n_qkv // tn):
        qkv_ref[:, c * tn:(c + 1) * tn] = _dot(h, w_ref[:, c * tn:(c + 1) * tn]).astype(BF16)
    for c in range(gate_ref.shape[1] // tn):
        gate = _dot(h, w_ref[:, n_qkv + c * tn:n_qkv + (c + 1) * tn])
        gate_ref[:, c * tn:(c + 1) * tn] = _silu(gate).astype(gate_ref.dtype)


def _ret_decays(lr_ref, hd, chunk, k_scale):
    row = lax.broadcasted_iota(jnp.int32, (chunk, chunk), 0).astype(F32)
    col = lax.broadcasted_iota(jnp.int32, (chunk, chunk), 1).astype(F32)
    ridx = lax.broadcasted_iota(jnp.int32, (chunk, 1), 0).astype(F32)
    dist = row - col
    log_gamma = -jnp.exp(lr_ref[hd])
    lg_f, lg_b = log_gamma[0:1, :], log_gamma[1:2, :]
    mask = jnp.where(dist > 0, jnp.exp(jnp.maximum(dist, 0.0) * lg_f),
                     jnp.where(dist < 0, jnp.exp(jnp.maximum(-dist, 0.0) * lg_b), 2.0)) * k_scale
    q_decay = (jnp.exp((ridx + 1.0) * lg_f), jnp.exp((chunk - ridx) * lg_b))
    k_decay = (jnp.exp((chunk - 1.0 - ridx) * lg_f) * k_scale, jnp.exp(ridx * lg_b) * k_scale)
    chunk_decay = (jnp.exp(chunk * lg_f), jnp.exp(chunk * lg_b))
    return mask, q_decay, k_decay, chunk_decay


def _ret_scan(decays, q, k, v, gate_ref, gn_ref, s0_ref, y_ref, sout_ref, o_ref, st_ref, *,
              row0, seq, chunk, hps, dk, dv):
    n_chunks = seq // chunk
    has_init = s0_ref is not None
    (q_ref, q0), (k_ref, k0), (v_ref, v0) = q, k, v
    for hd in range(hps):
        mask, q_decay, k_decay, chunk_decay = decays[hd]
        vs = slice(hd * dv, (hd + 1) * dv)
        for direction in range(2):
            order = range(n_chunks) if direction == 0 else range(n_chunks - 1, -1, -1)
            if has_init:
                st_ref[hd] = s0_ref[direction, hd]
            for step, c in enumerate(order):
                rs = slice(row0 + c * chunk, row0 + (c + 1) * chunk)
                ls = slice(c * chunk, (c + 1) * chunk)
                qc = q_ref[rs, q0 + hd * dk:q0 + (hd + 1) * dk]
                kc = k_ref[rs, k0 + hd * dk:k0 + (hd + 1) * dk]
                have_state = has_init or step > 0
                if direction == 0:
                    scores = (_dot_nt(qc, kc) * mask).astype(BF16)
                kd = (kc.astype(F32) * k_decay[direction]).T.astype(BF16)
                for c0 in range(0, dv, RET_COL_BLOCK):
                    cols = slice(hd * dv + c0, hd * dv + c0 + RET_COL_BLOCK)
                    sc = slice(c0, c0 + RET_COL_BLOCK)
                    vc = v_ref[rs, v0 + hd * dv + c0:v0 + hd * dv + c0 + RET_COL_BLOCK]
                    if have_state:
                        cross = _dot(qc, st_ref[hd, :, sc].astype(BF16)) * q_decay[direction]
                    if direction == 0:
                        inner = _dot(scores, vc)
                        o_ref[ls, cols] = inner + cross if have_state else inner
                    elif have_state:
                        o_ref[ls, cols] += cross
                    update = _dot(kd, vc)
                    if have_state:
                        st_ref[hd, :, sc] = st_ref[hd, :, sc] * chunk_decay[direction] + update
                    else:
                        st_ref[hd, :, sc] = update
            if sout_ref is not None:
                sout_ref[direction, hd] = st_ref[hd]

        for r0 in range(0, seq, RET_ROW_BLOCK):
            o = o_ref[r0:r0 + RET_ROW_BLOCK, vs]
            mu = jnp.mean(o, axis=-1, keepdims=True)
            var = jnp.mean(jnp.square(o - mu), axis=-1, keepdims=True)
            on = (o - mu) * lax.rsqrt(var + EPS) * gn_ref[:, vs]
            ys = slice(row0 + r0, row0 + r0 + RET_ROW_BLOCK)
            y_ref[ys, vs] = (gate_ref[ys, vs].astype(F32) * on).astype(y_ref.dtype)


def _ret_ctx_kernel(x_ref, mod_ref, g_ref, w_ref, lr_ref, gn_ref, y_ref, sout_ref,
                    qkv_s, gate_s, o_s, st_s, *, seq, dk, dv):
    nh = RET_HEADS
    _ret_project(x_ref, mod_ref, g_ref, w_ref, qkv_s, gate_s)
    decays = [_ret_decays(lr_ref, hd, seq, dk ** -0.5) for hd in range(nh)]
    for b in range(x_ref.shape[0] // seq):
        _ret_scan(decays, (qkv_s, 0), (qkv_s, nh * dk), (qkv_s, 2 * nh * dk), gate_s, gn_ref, None,
                  y_ref, sout_ref.at[b, 0], o_s, st_s, row0=b * seq, seq=seq, chunk=seq,
                  hps=nh, dk=dk, dv=dv)


def _ret_ctx_call(x, mod, g, w, log_rate, gn_g, rows, seq, n_gate):
    tm, d = rows.tm, x.shape[1]
    assert tm % seq == 0 and seq <= 2 * LANES
    nh = RET_HEADS
    n_qkv = w.shape[1] - n_gate
    dv = n_gate // nh
    dk = (n_qkv - n_gate) // (2 * nh)
    nb = tm // seq
    m = rows.ctx_tiles * tm
    return pl.pallas_call(
        functools.partial(_ret_ctx_kernel, seq=seq, dk=dk, dv=dv),
        grid=(rows.ctx_tiles,),
        in_specs=[
            pl.BlockSpec((tm, d), lambda i: (i, 0)),
            pl.BlockSpec((1, 6, d), lambda i: (0, 0, 0)),
            _resident(g.shape, lambda i: (0, 0)),
            _resident(w.shape, lambda i: (0, 0)),
            _resident(log_rate.shape, lambda i: (0, 0, 0)),
            _resident(gn_g.shape, lambda i: (0, 0)),
        ],
        out_specs=[pl.BlockSpec((tm, n_gate), lambda i: (i, 0)),
                   pl.BlockSpec((nb, 1, 2, nh, dk, dv), lambda i: (i, 0, 0, 0, 0, 0))],
        out_shape=[jax.ShapeDtypeStruct((m, n_gate), BF16),
                   jax.ShapeDtypeStruct((m // seq, 1, 2, nh, dk, dv), F32)],
        scratch_shapes=[pltpu.VMEM((tm, n_qkv), BF16), pltpu.VMEM((tm, n_gate), BF16),
                        pltpu.VMEM((seq, n_gate), F32), pltpu.VMEM((nh, dk, dv), F32)],
        compiler_params=_cparams(("arbitrary",)),
        name="ret_ctx",
    )(x, mod, g, w, log_rate, gn_g)


RET_LAT_ROWS = 512


def _ret_lat_kernel(x_ref, mod_ref, g_ref, wq_ref, wk_ref, wv_ref, wg_ref, lr_ref, gn_ref, s0_ref,
                    y_ref, h_s, q_s, k_s, v_s, gate_s, o_s, st_s, *, chunk, dk, dv):
    seq = x_ref.shape[0]

    @pl.when(pl.program_id(1) == 0)
    def _():
        h_s[...] = _modulate(x_ref[...], g_ref[...], mod_ref[0, 0:1, :], mod_ref[0, 1:2, :]).astype(BF16)

    rc = min(RET_LAT_ROWS, seq)
    for r in range(seq // rc):
        rs = slice(r * rc, (r + 1) * rc)
        h = h_s[rs, :]
        q_s[rs, :] = _dot(h, wq_ref[...]).astype(BF16)
        k_s[rs, :] = _dot(h, wk_ref[...]).astype(BF16)
        v_s[rs, :] = _dot(h, wv_ref[...]).astype(BF16)
        gate_s[rs, :] = _silu(_dot(h, wg_ref[...])).astype(BF16)
    decays = [_ret_decays(lr_ref, 0, chunk, dk ** -0.5)]
    _ret_scan(decays, (q_s, 0), (k_s, 0), (v_s, 0), gate_s, gn_ref, s0_ref.at[0], y_ref, None,
              o_s, st_s, row0=0, seq=seq, chunk=chunk, hps=1, dk=dk, dv=dv)


def _ret_lat_call(x, mod, g, w, log_rate, gn_g, s0, *, bsz, seq, row0, n_gate):
    d = x.shape[1]
    nh = RET_HEADS
    n_qkv = w.shape[1] - n_gate
    dv = n_gate // nh
    dk = (n_qkv - n_gate) // (2 * nh)
    chunk = min(seq, 2 * LANES)
    assert row0 % seq == 0 and dv % dk == 0
    b0 = row0 // seq
    return pl.pallas_call(
        functools.partial(_ret_lat_kernel, chunk=chunk, dk=dk, dv=dv),
        grid=(bsz, nh),
        in_specs=[
            pl.BlockSpec((seq, d), lambda b, h: (b0 + b, 0)),
            pl.BlockSpec((1, 6, d), lambda b, h: (1 + b, 0, 0)),
            _resident(g.shape, lambda b, h: (0, 0)),
            pl.BlockSpec((d, dk), lambda b, h: (0, h)),
            pl.BlockSpec((d, dk), lambda b, h: (0, nh + h)),
            pl.BlockSpec((d, dv), lambda b, h: (0, (2 * nh * dk) // dv + h)),
            pl.BlockSpec((d, dv), lambda b, h: (0, n_qkv // dv + h)),
            pl.BlockSpec((1, 2, 1), lambda b, h: (h, 0, 0)),
            pl.BlockSpec((1, dv), lambda b, h: (0, h)),
            pl.BlockSpec((1, 2, 1, dk, dv), lambda b, h: (b, 0, h, 0, 0)),
        ],
        out_specs=pl.BlockSpec((seq, dv), lambda b, h: (b, h)),
        out_shape=jax.ShapeDtypeStruct((bsz * seq, n_gate), BF16),
        scratch_shapes=[pltpu.VMEM((seq, d), BF16), pltpu.VMEM((seq, dk), BF16), pltpu.VMEM((seq, dk), BF16),
                        pltpu.VMEM((seq, dv), BF16), pltpu.VMEM((seq, dv), BF16),
                        pltpu.VMEM((seq, dv), F32), pltpu.VMEM((1, dk, dv), F32)],
        compiler_params=_cparams(("arbitrary", "arbitrary")),
        name="ret_lat",
    )(x, mod, g, w, w, w, w, log_rate, gn_g, s0)


def _rope_swap_index():
    f = ROPE_AXIS_FREQS
    idx = jnp.arange(MLA_ROPE)
    return jnp.where((idx // f) % 2 == 0, idx + f, idx - f)


def _mla_weights(w_a, q_norm_g, kv_norm_g, w_q_b, w_kv_b):
    swap = _rope_swap_index()
    nq = MLA_Q_RANK + MLA_KV_RANK
    w_kpe = w_a[:, nq:]
    w_kpe_sw = w_kpe[:, swap]
    wa = jnp.concatenate([w_a[:, :nq], w_kpe, w_kpe, w_kpe_sw, w_kpe_sw], axis=1).astype(BF16)
    wq = w_q_b.reshape(MLA_Q_RANK, MLA_HEADS, MLA_NOPE + MLA_ROPE)
    wq_nope = wq[:, :, :MLA_NOPE].reshape(MLA_Q_RANK, MLA_HEADS * MLA_NOPE)
    wq_pe = wq[:, :, MLA_NOPE:]
    wq_all = jnp.concatenate(
        [wq_nope, wq_pe.reshape(MLA_Q_RANK, -1), wq_pe[:, :, swap].reshape(MLA_Q_RANK, -1)],
        axis=1).astype(BF16)
    wkv = w_kv_b.reshape(MLA_KV_RANK, MLA_HEADS, MLA_NOPE + MLA_V)
    wkv_all = jnp.concatenate(
        [wkv[:, :, :MLA_NOPE].reshape(MLA_KV_RANK, -1), wkv[:, :, MLA_NOPE:].reshape(MLA_KV_RANK, -1)],
        axis=1).astype(BF16)
    return wa, q_norm_g[None, :], kv_norm_g[None, :], wq_all, wkv_all


def _rope_tables(n_tokens):
    f = ROPE_AXIS_FREQS
    f32 = np.float32
    rows = n_tokens // GRID_W
    r = np.repeat(np.arange(rows, dtype=f32), GRID_W)
    col = np.tile(np.arange(GRID_W, dtype=f32), rows)
    inv = (f32(ROPE_THETA) ** (-np.arange(f, dtype=f32) / f32(f))).astype(f32)
    ang_r, ang_c = r[:, None] * inv, col[:, None] * inv
    cos = np.concatenate([np.cos(ang_r)] * 2 + [np.cos(ang_c)] * 2, axis=1)
    sin = np.concatenate([-np.sin(ang_r), np.sin(ang_r), -np.sin(ang_c), np.sin(ang_c)], axis=1)
    reps = LANES // MLA_ROPE
    return (jnp.asarray(np.concatenate([cos] * reps, axis=1), F32),
            jnp.asarray(np.concatenate([sin] * reps, axis=1), F32))


def kernel(x_prompt, x_sample, c, c_ctx, cache_mla_ckv, cache_mla_kpe, state_ret, ada_w, ada_b, norm_mix_g, norm_ffn_g, mla_w_a, mla_q_norm_g, mla_kv_norm_g, mla_w_q_b, mla_w_kv_b, mla_w_o, conv_w_in, conv_w, conv_w_out, ret_w_in, ret_log_rate, ret_gn_g, ret_w_out, ffn_w_in, ffn_w_out, final_norm_g):
    batch, seq, d = x_prompt.shape
    dec_batch, dec_seq, _ = x_sample.shape
    depth = ada_w.shape[0]
    n_ctx = batch * seq
    n_lat = dec_batch * dec_seq
    past = cache_mla_ckv.shape[2]
    assert 1 + dec_batch <= MOD_ROWS

    cvec = jnp.zeros((MOD_ROWS, d), F32).at[0].set(c_ctx).at[1:1 + dec_batch].set(c)
    mod_all = _ada_call(cvec, ada_w, ada_b).reshape(depth, MOD_ROWS, 6, d)

    rows_s = _Rows(n_ctx, dec_batch, dec_seq, min(ROW_TILE_FUSED, dec_seq))
    rows_l = _Rows(n_ctx, dec_batch, dec_seq, min(ROW_TILE, dec_seq))
    rope_tabs = _rope_tables(dec_seq)
    final_g = final_norm_g[None, :]
    mla_w_o_bf = mla_w_o.astype(BF16)

    def mixer_cast_jobs(layer):
        if layer >= depth:
            return []
        kind, j = layer % N_MIXERS, layer // N_MIXERS
        if kind == 1:
            return [(conv_w_in, j), (conv_w_out, j)]
        if kind == 2:
            return [(ret_w_in, j), (ret_w_out, j)]
        return []

    x = (x_prompt.reshape(n_ctx, d), x_sample.reshape(n_lat, d))
    new_ckv, new_kpe, new_ret = [], [], []
    ffn_bf = mixer_bf = None
    for i in range(depth):
        kind, j = i % N_MIXERS, i // N_MIXERS
        mod = mod_all[i]
        g_mix = norm_mix_g[i][None, :]
        if ffn_bf is None:
            assert kind == 0
        if kind == 0:
            wts = _mla_weights(mla_w_a[j], mla_q_norm_g[j], mla_kv_norm_g[j], mla_w_q_b[j], mla_w_kv_b[j])
            split = isinstance(x, tuple)
            o_c, ckv_c, kpe_c = _mla_ctx_call(x[0] if split else x, mod, g_mix, wts, rows_l, seq)
            ql, kl, vl = _mla_proj_call(
                x[1] if split else x, 0 if split else rows_l.ctx_tiles, mod, g_mix, wts, rows_l,
                tile0=rows_l.ctx_tiles, n_tiles=rows_l.n_tiles - rows_l.ctx_tiles, rope_tabs=rope_tabs)
            new_ckv.append(ckv_c.reshape(batch, seq, MLA_KV_RANK))
            new_kpe.append(kpe_c.reshape(batch, seq, MLA_ROPE))
            kpe_rep = jnp.concatenate([cache_mla_kpe[:, j]] * (LANES // MLA_ROPE), axis=-1)
            kp, vp = _cache_expand_call(
                cache_mla_ckv[:, j].reshape(dec_batch * past, MLA_KV_RANK),
                kpe_rep.reshape(dec_batch * past, LANES), wts[4])
            jobs = [] if ffn_bf is not None else [(ffn_w_in, i), (ffn_w_out, i)]
            o_l, cast = _attn_call(ql, [kp, kl], [vp, vl], n_batch=dec_batch, seq=dec_seq,
                                   sks=[past, dec_seq], nb=1, tq=min(ATTN_Q_TILE, dec_seq),
                                   name="attn_lat", cast_jobs=jobs)
            if jobs:
                ffn_bf = cast
            mix = (o_c, o_l, mla_w_o_bf, j)
        elif kind == 1:
            mix = None
            x = _conv_call(x, mod, g_mix, mixer_bf[0][0], conv_w[j], mixer_bf[1][0], rows_l, seq, dec_seq)
        else:
            n_gate = ret_w_out.shape[1]
            lr = ret_log_rate[j].T[:, :, None]
            gn = ret_gn_g[j][None, :]
            y_c, st = _ret_ctx_call(x, mod, g_mix, mixer_bf[0][0], lr, gn, rows_s, seq, n_gate)
            y_l = _ret_lat_call(x, mod, g_mix, mixer_bf[0][0], lr, gn, state_ret[:, j],
                                bsz=dec_batch, seq=dec_seq, row0=n_ctx, n_gate=n_gate)
            new_ret.append(st)
            mix = (y_c, y_l, mixer_bf[1], 0)
        last = i == depth - 1
        jobs = [] if last else [(ffn_w_in, i + 1), (ffn_w_out, i + 1)] + mixer_cast_jobs(i + 1)
        x, cast = _ffn_call(x, mod, norm_ffn_g[i][None, :], ffn_bf[0], ffn_bf[1], 0, final_g,
                            rows_l if mix is None else rows_s, final=last, cast_jobs=jobs, mix=mix)
        ffn_bf, mixer_bf = cast[:2], cast[2:]

    y_prompt = x[0].reshape(batch, seq, d)
    y_sample = x[1].reshape(dec_batch, dec_seq, d)
    return (y_prompt, y_sample, jnp.stack(new_ckv, axis=1), jnp.stack(new_kpe, axis=1),
            jnp.concatenate(new_ret, axis=1))
```

```python
import functools
import math

import jax
import jax.numpy as jnp
import numpy as np
from jax import lax
from jax.experimental import pallas as pl
from jax.experimental.pallas import tpu as pltpu

F32 = jnp.float32
BF16 = jnp.bfloat16

N_MIXERS = 3
MLA_HEADS = 8
MLA_NOPE = 128
MLA_ROPE = 64
MLA_V = 128
MLA_Q_RANK = 384
MLA_KV_RANK = 256
MLA_SCALE = (MLA_NOPE + MLA_ROPE) ** -0.5
MLA_Q_SCALE = MLA_SCALE * 1.4426950408889634
ROPE_THETA = 10000.0
ROPE_AXIS_FREQS = MLA_ROPE // 4
GRID_W = 64
RET_HEADS = 4
EPS = 1e-6

LANES = 128
MLA_HEAD_PAD = 2 * LANES
MOD_ROWS = 8
VMEM_LIMIT = 56 * 1024 * 1024

ROW_TILE = 1024
ROW_TILE_FUSED = 512
ATTN_Q_TILE = 512
FFN_ROW_BLOCK = 256


def _cparams(sem):
    return pltpu.CompilerParams(dimension_semantics=sem, vmem_limit_bytes=VMEM_LIMIT)


def _resident(shape, index_map):
    return pl.BlockSpec(shape, index_map, pipeline_mode=pl.Buffered(1))


def _rms(x, g):
    return x * lax.rsqrt(jnp.mean(x * x, axis=-1, keepdims=True) + EPS) * g


def _modulate(x, g, shift, scale):
    return _rms(x, g) * (1.0 + scale) + shift


def _silu(x):
    return x * jax.nn.sigmoid(x)


def _dot(a, b):
    return jnp.dot(a, b, preferred_element_type=F32)


def _dot_nt(a, b):
    return lax.dot_general(a, b, (((1,), (1,)), ((), ())), preferred_element_type=F32)


MAX_CAST_SLABS = 16


def _cast_specs(jobs, n_steps, step_of):
    n_slabs = math.gcd(n_steps, MAX_CAST_SLABS)
    per = n_steps // n_slabs

    def slab_of_step(*g):
        return step_of(*g) // per

    in_specs, out_specs, out_shape = [], [], []
    for w, layer in jobs:
        _, k, n = w.shape
        blk = (1, k // n_slabs, n)
        in_specs.append(pl.BlockSpec(blk, lambda *g, layer=layer: (layer, slab_of_step(*g), 0)))
        out_specs.append(pl.BlockSpec(blk, lambda *g: (0, slab_of_step(*g), 0)))
        out_shape.append(jax.ShapeDtypeStruct((1, k, n), BF16))
    return in_specs, out_specs, out_shape


def _run_casts(in_refs, out_refs):
    for src, dst in zip(in_refs, out_refs):
        dst[...] = src[...].astype(BF16)


def _ada_kernel(c_ref, w_ref, b_ref, o_ref):
    a = _silu(c_ref[...]).astype(BF16)
    o_ref[0] = _dot(a, w_ref[0].astype(BF16)) + b_ref[0]


def _ada_call(cvec, ada_w, ada_b):
    depth, d, n = ada_w.shape
    tn = n // 2
    return pl.pallas_call(
        _ada_kernel,
        grid=(depth, n // tn),
        in_specs=[
            pl.BlockSpec((MOD_ROWS, d), lambda l, j: (0, 0)),
            pl.BlockSpec((1, d, tn), lambda l, j: (l, 0, j)),
            pl.BlockSpec((1, 1, tn), lambda l, j: (l, 0, j)),
        ],
        out_specs=pl.BlockSpec((1, MOD_ROWS, tn), lambda l, j: (l, 0, j)),
        out_shape=jax.ShapeDtypeStruct((depth, MOD_ROWS, n), F32),
        compiler_params=_cparams(("arbitrary", "arbitrary")),
        name="ada",
    )(cvec, ada_w, ada_b.reshape(depth, 1, n))


class _Rows:
    def __init__(self, n_ctx, dec_batch, dec_seq, tm):
        assert n_ctx % tm == 0 and dec_seq % tm == 0
        self.tm = tm
        self.ctx_tiles = n_ctx // tm
        self.seq_tiles = dec_seq // tm
        self.n_tiles = self.ctx_tiles + dec_batch * self.seq_tiles

    def group(self, i):
        lat = jnp.maximum(i - self.ctx_tiles, 0) // self.seq_tiles
        return jnp.where(i < self.ctx_tiles, 0, 1 + lat)


def _mla_project(x_ref, mod_ref, g_ref, wa_ref, qg_ref, kvg_ref, wq_ref, wkv_ref, rope_refs,
                 q_ref, k_ref, v_ref, cache_refs):
    h = _modulate(x_ref[...], g_ref[...], mod_ref[0, 0:1, :], mod_ref[0, 1:2, :]).astype(BF16)
    a = _dot(h, wa_ref[...])
    nq = MLA_Q_RANK + MLA_KV_RANK
    ckv_n = _rms(a[:, MLA_Q_RANK:nq], kvg_ref[...])
    kpe = a[:, nq:nq + LANES]
    if cache_refs is not None:
        cache_refs[0][...] = ckv_n
        cache_refs[1][...] = kpe[:, :MLA_ROPE]
    if rope_refs is not None:
        cos, sin = rope_refs[0][...], rope_refs[1][...]
        kpe = kpe * cos + a[:, nq + LANES:nq + 2 * LANES] * sin
    kpe = kpe.astype(BF16)

    qn = _rms(a[:, :MLA_Q_RANK], qg_ref[...]).astype(BF16)
    nh = MLA_HEADS
    n_nope, n_pe = nh * MLA_NOPE, nh * MLA_ROPE
    q_nope = _dot(qn, wq_ref[:, :n_nope])
    q_pe = _dot(qn, wq_ref[:, n_nope:n_nope + n_pe])
    if rope_refs is not None:
        q_sw = _dot(qn, wq_ref[:, n_nope + n_pe:n_nope + 2 * n_pe])
        cos, sin = cos * MLA_Q_SCALE, sin * MLA_Q_SCALE
    kv = _dot(ckv_n.astype(BF16), wkv_ref[...])
    low_half = lax.broadcasted_iota(jnp.int32, (1, LANES), 1) < MLA_ROPE
    for pair in range(nh // 2):
        ps = slice(pair * LANES, (pair + 1) * LANES)
        if rope_refs is not None:
            pe = q_pe[:, ps] * cos + q_sw[:, ps] * sin
        else:
            pe = q_pe[:, ps] * MLA_Q_SCALE
        for hd, keep in ((2 * pair, low_half), (2 * pair + 1, jnp.logical_not(low_half))):
            lo = hd * MLA_HEAD_PAD
            hs = slice(hd * LANES, (hd + 1) * LANES)
            q_ref[:, lo:lo + LANES] = (q_nope[:, hs] * MLA_Q_SCALE).astype(BF16)
            q_ref[:, lo + LANES:lo + 2 * LANES] = jnp.where(keep, pe, 0.0).astype(BF16)
            k_ref[:, lo:lo + LANES] = kv[:, hs].astype(BF16)
            k_ref[:, lo + LANES:lo + 2 * LANES] = kpe
    v_ref[...] = kv[:, nh * LANES:].astype(BF16)


def _mla_proj_kernel(x_ref, mod_ref, g_ref, wa_ref, qg_ref, kvg_ref, wq_ref, wkv_ref, ck_ref, sk_ref,
                     q_ref, k_ref, v_ref):
    _mla_project(x_ref, mod_ref, g_ref, wa_ref, qg_ref, kvg_ref, wq_ref, wkv_ref, (ck_ref, sk_ref),
                 q_ref, k_ref, v_ref, None)


def _mla_proj_call(x, x_tile0, mod, g, wts, rows, *, tile0, n_tiles, rope_tabs):
    tm, d = rows.tm, x.shape[1]
    wa, qg, kvg, wq, wkv = wts
    full = lambda arr: _resident(arr.shape, lambda i: (0,) * arr.ndim)
    in_specs = [
        pl.BlockSpec((tm, d), lambda i: (i + x_tile0, 0)),
        pl.BlockSpec((1, 6, d), lambda i: (rows.group(i + tile0), 0, 0)),
        full(g), full(wa), full(qg), full(kvg), full(wq), full(wkv),
    ]
    in_specs += [pl.BlockSpec((tm, LANES), lambda i: (i % rows.seq_tiles, 0)) for _ in rope_tabs]
    m = n_tiles * tm
    hp, hv = MLA_HEADS * MLA_HEAD_PAD, MLA_HEADS * MLA_V
    return pl.pallas_call(
        _mla_proj_kernel,
        grid=(n_tiles,),
        in_specs=in_specs,
        out_specs=[pl.BlockSpec((tm, hp), lambda i: (i, 0)), pl.BlockSpec((tm, hp), lambda i: (i, 0)),
                   pl.BlockSpec((tm, hv), lambda i: (i, 0))],
        out_shape=[jax.ShapeDtypeStruct((m, hp), BF16), jax.ShapeDtypeStruct((m, hp), BF16),
                   jax.ShapeDtypeStruct((m, hv), BF16)],
        compiler_params=_cparams(("arbitrary",)),
        name="mla_proj_lat",
    )(x, mod, g, wa, qg, kvg, wq, wkv, *rope_tabs)


def _mla_ctx_kernel(x_ref, mod_ref, g_ref, wa_ref, qg_ref, kvg_ref, wq_ref, wkv_ref,
                    o_ref, ckv_ref, kpe_ref, q_s, k_s, v_s, s_ref, *, seq):
    _mla_project(x_ref, mod_ref, g_ref, wa_ref, qg_ref, kvg_ref, wq_ref, wkv_ref, None,
                 q_s, k_s, v_s, (ckv_ref, kpe_ref))
    _attend(q_s, [k_s], [v_s], o_ref, s_ref, nb=x_ref.shape[0] // seq, tq=seq, sks=(seq,))


def _mla_ctx_call(x, mod, g, wts, rows, seq):
    tm, d = rows.tm, x.shape[1]
    assert tm % seq == 0
    wa, qg, kvg, wq, wkv = wts
    full = lambda arr: _resident(arr.shape, lambda i: (0,) * arr.ndim)
    m = rows.ctx_tiles * tm
    hp, hv = MLA_HEADS * MLA_HEAD_PAD, MLA_HEADS * MLA_V
    return pl.pallas_call(
        functools.partial(_mla_ctx_kernel, seq=seq),
        grid=(rows.ctx_tiles,),
        in_specs=[
            pl.BlockSpec((tm, d), lambda i: (i, 0)),
            pl.BlockSpec((1, 6, d), lambda i: (0, 0, 0)),
            full(g), full(wa), full(qg), full(kvg), full(wq), full(wkv),
        ],
        out_specs=[pl.BlockSpec((tm, hv), lambda i: (i, 0)),
                   pl.BlockSpec((tm, MLA_KV_RANK), lambda i: (i, 0)),
                   pl.BlockSpec((tm, MLA_ROPE), lambda i: (i, 0))],
        out_shape=[jax.ShapeDtypeStruct((m, hv), BF16),
                   jax.ShapeDtypeStruct((m, MLA_KV_RANK), F32),
                   jax.ShapeDtypeStruct((m, MLA_ROPE), F32)],
        scratch_shapes=[pltpu.VMEM((tm, hp), BF16), pltpu.VMEM((tm, hp), BF16), pltpu.VMEM((tm, hv), BF16),
                        pltpu.VMEM((2, seq, seq), F32)],
        compiler_params=_cparams(("arbitrary",)),
        name="mla_ctx",
    )(x, mod, g, wa, qg, kvg, wq, wkv)


def _cache_expand_kernel(ckv_ref, kpe_ref, wkv_ref, k_ref, v_ref):
    kv = _dot(ckv_ref[...].astype(BF16), wkv_ref[...])
    kpe = kpe_ref[...].astype(BF16)
    nh = MLA_HEADS
    for hd in range(nh):
        lo = hd * MLA_HEAD_PAD
        k_ref[:, lo:lo + LANES] = kv[:, hd * LANES:(hd + 1) * LANES].astype(BF16)
        k_ref[:, lo + LANES:lo + 2 * LANES] = kpe
    v_ref[...] = kv[:, nh * LANES:].astype(BF16)


def _cache_expand_call(ckv, kpe_pad, wkv):
    m = ckv.shape[0]
    tm = min(m, 512)
    hp = MLA_HEADS * MLA_HEAD_PAD
    return pl.pallas_call(
        _cache_expand_kernel,
        grid=(m // tm,),
        in_specs=[
            pl.BlockSpec((tm, MLA_KV_RANK), lambda i: (i, 0)),
            pl.BlockSpec((tm, LANES), lambda i: (i, 0)),
            _resident(wkv.shape, lambda i: (0, 0)),
        ],
        out_specs=[pl.BlockSpec((tm, hp), lambda i: (i, 0)),
                   pl.BlockSpec((tm, MLA_HEADS * MLA_V), lambda i: (i, 0))],
        out_shape=[jax.ShapeDtypeStruct((m, hp), BF16),
                   jax.ShapeDtypeStruct((m, MLA_HEADS * MLA_V), BF16)],
        compiler_params=_cparams(("arbitrary",)),
        name="mla_cache_expand",
    )(ckv, kpe_pad, wkv)


def _attend(q_ref, k_refs, v_refs, o_ref, s_ref, *, nb, tq, sks):
    problems = [(b, h) for b in range(nb) for h in range(MLA_HEADS)]
    seg_cols = [sum(sks[:i]) for i in range(len(sks))]

    def scores(idx):
        b, h = problems[idx]
        q = q_ref[b * tq:(b + 1) * tq, h * MLA_HEAD_PAD:(h + 1) * MLA_HEAD_PAD]
        m_lane = None
        for k_ref, sk, c0 in zip(k_refs, sks, seg_cols):
            s = _dot_nt(q, k_ref[b * sk:(b + 1) * sk, h * MLA_HEAD_PAD:(h + 1) * MLA_HEAD_PAD])
            s_ref[idx % 2, :, c0:c0 + sk] = s
            for c in range(sk // LANES):
                piece = s[:, c * LANES:(c + 1) * LANES]
                m_lane = piece if m_lane is None else jnp.maximum(m_lane, piece)
        return m_lane.max(axis=-1, keepdims=True)

    def weighted_values(idx, m):
        b, h = problems[idx]
        acc = None
        for v_ref, sk, c0 in zip(v_refs, sks, seg_cols):
            p = jnp.exp2(s_ref[idx % 2, :, c0:c0 + sk] - m).astype(BF16)
            v = v_ref[b * sk:(b + 1) * sk, h * MLA_V:(h + 1) * MLA_V]
            part = _dot(p, jnp.concatenate([v, jnp.ones_like(v)], axis=1))
            acc = part if acc is None else acc + part
        o = acc[:, :MLA_V] / acc[:, MLA_V:]
        o_ref[b * tq:(b + 1) * tq, h * MLA_V:(h + 1) * MLA_V] = o.astype(o_ref.dtype)

    m = scores(0)
    for idx in range(len(problems)):
        m_next = scores(idx + 1) if idx + 1 < len(problems) else None
        weighted_values(idx, m)
        m = m_next


def _attn_kernel(*refs, n_seg, nb, tq, sks, n_cast):
    n_in = 1 + 2 * n_seg + n_cast
    _run_casts(refs[1 + 2 * n_seg:n_in], refs[n_in + 1:n_in + 1 + n_cast])
    _attend(refs[0], refs[1:1 + n_seg], refs[1 + n_seg:1 + 2 * n_seg], refs[n_in],
            refs[n_in + 1 + n_cast], nb=nb, tq=tq, sks=sks)


def _attn_call(q, ks, vs, *, n_batch, seq, sks, nb, tq, name, cast_jobs=()):
    n_seg = len(ks)
    tiles = seq // tq
    assert nb == 1 or tiles == 1
    hp, hv = MLA_HEADS * MLA_HEAD_PAD, MLA_HEADS * MLA_V
    in_specs = [pl.BlockSpec((nb * tq, hp), lambda b, t: (b * tiles + t, 0))]
    in_specs += [pl.BlockSpec((nb * sk, hp), lambda b, t: (b, 0)) for sk in sks]
    in_specs += [pl.BlockSpec((nb * sk, hv), lambda b, t: (b, 0)) for sk in sks]
    c_in, c_out, c_shape = _cast_specs(cast_jobs, (n_batch // nb) * tiles, lambda b, t: b * tiles + t)
    outs = pl.pallas_call(
        functools.partial(_attn_kernel, n_seg=n_seg, nb=nb, tq=tq, sks=tuple(sks), n_cast=len(cast_jobs)),
        grid=(n_batch // nb, tiles),
        in_specs=in_specs + c_in,
        out_specs=[pl.BlockSpec((nb * tq, hv), lambda b, t: (b * tiles + t, 0))] + c_out,
        out_shape=[jax.ShapeDtypeStruct((n_batch * seq, hv), BF16)] + c_shape,
        scratch_shapes=[pltpu.VMEM((2, tq, sum(sks)), F32)],
        compiler_params=_cparams(("arbitrary", "arbitrary")),
        name=name,
    )(q, *ks, *vs, *[w for w, _ in cast_jobs])
    return outs[0], list(outs[1:])


def _split_specs(rows, width):
    ct = rows.ctx_tiles
    return [pl.BlockSpec((rows.tm, width), lambda i: (jnp.minimum(i, ct - 1), 0)),
            pl.BlockSpec((rows.tm, width), lambda i: (jnp.maximum(i - ct, 0), 0))]


def _ffn_kernel(*refs, th, final, ctx_tiles, n_cast, n_x, mixed):
    it = iter(refs)
    x_refs = [next(it) for _ in range(n_x)]
    if mixed:
        yc_ref, yl_ref, wo_ref = next(it), next(it), next(it)
    mod_ref, g_ref, win_ref, wout_ref, fg_ref = (next(it) for _ in range(5))
    cast_in = [next(it) for _ in range(n_cast)]
    out_refs = [next(it) for _ in range(2 if final else 1)]
    cast_out = [next(it) for _ in range(n_cast)]
    scratch = list(it)
    act_ref = scratch[0]
    _run_casts(cast_in, cast_out)
    is_ctx = pl.program_id(0) < ctx_tiles
    if mixed:
        x1_ref = scratch[1]

        def mix(y_ref, x_ref):
            x1_ref[...] = x_ref[...] + mod_ref[0, 2:3, :] * _dot(y_ref[...], wo_ref[0])

        pl.when(is_ctx)(lambda: mix(yc_ref, x_refs[0]))
        pl.when(jnp.logical_not(is_ctx))(lambda: mix(yl_ref, x_refs[-1]))
        x = x1_ref[...]
    else:
        x = x_refs[0][...]
    h = _modulate(x, g_ref[...], mod_ref[0, 3:4, :], mod_ref[0, 4:5, :]).astype(BF16)
    hidden = wout_ref.shape[1]
    tm = h.shape[0]
    for c in range(hidden // th):
        for r0 in range(0, tm, FFN_ROW_BLOCK):
            hr = h[r0:r0 + FFN_ROW_BLOCK, :]
            a = _dot(hr, win_ref[0, :, c * th:(c + 1) * th])
            b = _dot(hr, win_ref[0, :, hidden + c * th:hidden + (c + 1) * th])
            act_ref[r0:r0 + FFN_ROW_BLOCK, c * th:(c + 1) * th] = (_silu(a) * b).astype(BF16)
    y = x + mod_ref[0, 5:6, :] * _dot(act_ref[...], wout_ref[0])
    if not final:
        out_refs[0][...] = y
    else:
        acc_ref = scratch[-1]
        acc_ref[...] = _rms(y, fg_ref[...])

        @pl.when(is_ctx)
        def _():
            out_refs[0][...] = acc_ref[...]

        @pl.when(jnp.logical_not(is_ctx))
        def _():
            out_refs[1][...] = acc_ref[...]


def _ffn_call(x, mod, g, w_in_all, w_out_all, layer, final_g, rows, final, cast_jobs=(), mix=None):
    tm, d = rows.tm, w_in_all.shape[1]
    hidden = w_out_all.shape[1]
    split_x = isinstance(x, tuple)
    assert not split_x or mix is not None
    xs = list(x) if split_x else [x]
    x_specs = _split_specs(rows, d) if split_x else [pl.BlockSpec((tm, d), lambda i: (i, 0))]
    mix_args, mix_specs = [], []
    scratch = [pltpu.VMEM((tm, hidden), BF16)]
    if mix is not None:
        y_ctx, y_lat, w_o_all, w_o_layer = mix
        k = w_o_all.shape[1]
        mix_args = [y_ctx, y_lat, w_o_all]
        mix_specs = _split_specs(rows, k) + [_resident((1, k, d), lambda i: (w_o_layer, 0, 0))]
        scratch.append(pltpu.VMEM((tm, d), F32))
    if final:
        out_specs = _split_specs(rows, d)
        out_shape = [jax.ShapeDtypeStruct((rows.ctx_tiles * tm, d), F32),
                     jax.ShapeDtypeStruct(((rows.n_tiles - rows.ctx_tiles) * tm, d), F32)]
        scratch.append(pltpu.VMEM((tm, d), F32))
    else:
        out_specs = [pl.BlockSpec((tm, d), lambda i: (i, 0))]
        out_shape = [jax.ShapeDtypeStruct((rows.n_tiles * tm, d), F32)]
    c_in, c_out, c_shape = _cast_specs(cast_jobs, rows.n_tiles, lambda i: i)
    outs = pl.pallas_call(
        functools.partial(_ffn_kernel, th=2 * LANES, final=final, ctx_tiles=rows.ctx_tiles,
                          n_cast=len(cast_jobs), n_x=len(xs), mixed=mix is not None),
        grid=(rows.n_tiles,),
        in_specs=x_specs + mix_specs + [
            pl.BlockSpec((1, 6, d), lambda i: (rows.group(i), 0, 0)),
            _resident(g.shape, lambda i: (0, 0)),
            _resident((1, d, 2 * hidden), lambda i: (layer, 0, 0)),
            _resident((1, hidden, d), lambda i: (layer, 0, 0)),
            _resident(final_g.shape, lambda i: (0, 0)),
        ] + c_in,
        out_specs=out_specs + c_out,
        out_shape=out_shape + c_shape,
        scratch_shapes=scratch,
        compiler_params=_cparams(("arbitrary",)),
        name=("ffn_final" if final else "ffn") + ("_mix" if mix is not None else ""),
    )(*xs, *mix_args, mod, g, w_in_all, w_out_all, final_g, *[w for w, _ in cast_jobs])
    n_out = 2 if final else 1
    stream = tuple(outs[:2]) if final else outs[0]
    return stream, list(outs[n_out:])


CONV_HALO = 16


def _conv_kernel(x_ref, xp_ref, xn_ref, mod_ref, g_ref, win_ref, cw_ref, wout_ref, o_ref,
                 h_ref, z_ref, act_ref, *, tn, ctx_tiles, ctx_seq, lat_seq):
    tm, d = x_ref.shape
    hl = CONV_HALO
    shift, scale = mod_ref[0, 0:1, :], mod_ref[0, 1:2, :]
    h_ref[0:hl, :] = _modulate(xp_ref[...], g_ref[...], shift, scale).astype(BF16)
    h_ref[hl:hl + tm, :] = _modulate(x_ref[...], g_ref[...], shift, scale).astype(BF16)
    h_ref[hl + tm:, :] = _modulate(xn_ref[...], g_ref[...], shift, scale).astype(BF16)
    i = pl.program_id(0)
    is_ctx = i < ctx_tiles
    row = lax.broadcasted_iota(jnp.int32, (tm, 1), 0)
    lat_row0 = (jnp.maximum(i - ctx_tiles, 0) % (lat_seq // tm)) * tm
    pos = jnp.where(is_ctx, row & (ctx_seq - 1), row + lat_row0)
    seq = jnp.where(is_ctx, ctx_seq, lat_seq)
    has_prev = pos != 0
    has_next = pos != seq - 1
    for c in range(d // tn):
        sl = slice(c * tn, (c + 1) * tn)
        h = h_ref[...]
        cg = _dot(h, win_ref[:, d + c * tn:d + (c + 1) * tn])
        u = _dot(h, win_ref[:, 2 * d + c * tn:2 * d + (c + 1) * tn])
        z_ref[...] = cg * u
        bg = _dot(h_ref[hl:hl + tm, :], win_ref[:, sl])
        z_prev = jnp.where(has_prev, z_ref[hl - 1:hl - 1 + tm, :], 0.0)
        z_next = jnp.where(has_next, z_ref[hl + 1:hl + 1 + tm, :], 0.0)
        conv = z_prev * cw_ref[0:1, sl] + z_ref[hl:hl + tm, :] * cw_ref[1:2, sl] + z_next * cw_ref[2:3, sl]
        act_ref[:, sl] = (bg * conv).astype(BF16)
    o_ref[...] = x_ref[...] + mod_ref[0, 2:3, :] * _dot(act_ref[...], wout_ref[...])


def _conv_call(x, mod, g, w_in, conv_w, w_out, rows, ctx_seq, lat_seq):
    tm, d = rows.tm, x.shape[1]
    hl = CONV_HALO
    assert tm % ctx_seq == 0 and ctx_seq & (ctx_seq - 1) == 0 and lat_seq % tm == 0 and tm % hl == 0
    last_halo = x.shape[0] // hl - 1
    return pl.pallas_call(
        functools.partial(_conv_kernel, tn=2 * LANES, ctx_tiles=rows.ctx_tiles, ctx_seq=ctx_seq,
                          lat_seq=lat_seq),
        grid=(rows.n_tiles,),
        in_specs=[
            pl.BlockSpec((tm, d), lambda i: (i, 0)),
            pl.BlockSpec((hl, d), lambda i: (jnp.maximum(i * (tm // hl) - 1, 0), 0)),
            pl.BlockSpec((hl, d), lambda i: (jnp.minimum((i + 1) * (tm // hl), last_halo), 0)),
            pl.BlockSpec((1, 6, d), lambda i: (rows.group(i), 0, 0)),
            _resident(g.shape, lambda i: (0, 0)),
            _resident(w_in.shape, lambda i: (0, 0)),
            _resident(conv_w.shape, lambda i: (0, 0)),
            _resident(w_out.shape, lambda i: (0, 0)),
        ],
        out_specs=pl.BlockSpec((tm, d), lambda i: (i, 0)),
        out_shape=jax.ShapeDtypeStruct(x.shape, F32),
        scratch_shapes=[pltpu.VMEM((tm + 2 * hl, d), BF16), pltpu.VMEM((tm + 2 * hl, 2 * LANES), F32),
                        pltpu.VMEM((tm, d), BF16)],
        compiler_params=_cparams(("arbitrary",)),
        name="conv_mixer",
    )(x, x, x, mod, g, w_in, conv_w, w_out)


RET_PROJ_CHUNK = 4 * LANES
RET_COL_BLOCK = 2 * LANES
RET_ROW_BLOCK = 64


def _ret_project(x_ref, mod_ref, g_ref, w_ref, qkv_ref, gate_ref):
    tn = RET_PROJ_CHUNK
    h = _modulate(x_ref[...], g_ref[...], mod_ref[0, 0:1, :], mod_ref[0, 1:2, :]).astype(BF16)
    n_qkv = qkv_ref.shape[1]
    for c in range(n_qkv // tn):
        qkv_ref[:, c * tn:(c + 1) * tn] = _dot(h, w_ref[:, c * tn:(c + 1) * tn]).astype(BF16)
    for c in range(gate_ref.shape[1] // tn):
        gate = _dot(h, w_ref[:, n_qkv + c * tn:n_qkv + (c + 1) * tn])
        gate_ref[:, c * tn:(c + 1) * tn] = _silu(gate).astype(gate_ref.dtype)


def _ret_decays(lr_ref, hd, chunk, k_scale):
    row = lax.broadcasted_iota(jnp.int32, (chunk, chunk), 0).astype(F32)
    col = lax.broadcasted_iota(jnp.int32, (chunk, chunk), 1).astype(F32)
    ridx = lax.broadcasted_iota(jnp.int32, (chunk, 1), 0).astype(F32)
    dist = row - col
    log_gamma = -jnp.exp(lr_ref[hd])
    lg_f, lg_b = log_gamma[0:1, :], log_gamma[1:2, :]
    mask = jnp.where(dist > 0, jnp.exp(jnp.maximum(dist, 0.0) * lg_f),
                     jnp.where(dist < 0, jnp.exp(jnp.maximum(-dist, 0.0) * lg_b), 2.0)) * k_scale
    q_decay = (jnp.exp((ridx + 1.0) * lg_f), jnp.exp((chunk - ridx) * lg_b))
    k_decay = (jnp.exp((chunk - 1.0 - ridx) * lg_f) * k_scale, jnp.exp(ridx * lg_b) * k_scale)
    chunk_decay = (jnp.exp(chunk * lg_f), jnp.exp(chunk * lg_b))
    return mask, q_decay, k_decay, chunk_decay


def _ret_scan(decays, q, k, v, gate_ref, gn_ref, s0_ref, y_ref, sout_ref, o_ref, st_ref, *,
              row0, seq, chunk, hps, dk, dv):
    n_chunks = seq // chunk
    has_init = s0_ref is not None
    (q_ref, q0), (k_ref, k0), (v_ref, v0) = q, k, v
    for hd in range(hps):
        mask, q_decay, k_decay, chunk_decay = decays[hd]
        vs = slice(hd * dv, (hd + 1) * dv)
        for direction in range(2):
            order = range(n_chunks) if direction == 0 else range(n_chunks - 1, -1, -1)
            if has_init:
                st_ref[hd] = s0_ref[direction, hd]
            for step, c in enumerate(order):
                rs = slice(row0 + c * chunk, row0 + (c + 1) * chunk)
                ls = slice(c * chunk, (c + 1) * chunk)
                qc = q_ref[rs, q0 + hd * dk:q0 + (hd + 1) * dk]
                kc = k_ref[rs, k0 + hd * dk:k0 + (hd + 1) * dk]
                have_state = has_init or step > 0
                if direction == 0:
                    scores = (_dot_nt(qc, kc) * mask).astype(BF16)
                kd = (kc.astype(F32) * k_decay[direction]).T.astype(BF16)
                for c0 in range(0, dv, RET_COL_BLOCK):
                    cols = slice(hd * dv + c0, hd * dv + c0 + RET_COL_BLOCK)
                    sc = slice(c0, c0 + RET_COL_BLOCK)
                    vc = v_ref[rs, v0 + hd * dv + c0:v0 + hd * dv + c0 + RET_COL_BLOCK]
                    if have_state:
                        cross = _dot(qc, st_ref[hd, :, sc].astype(BF16)) * q_decay[direction]
                    if direction == 0:
                        inner = _dot(scores, vc)
                        o_ref[ls, cols] = inner + cross if have_state else inner
                    elif have_state:
                        o_ref[ls, cols] += cross
                    update = _dot(kd, vc)
                    if have_state:
                        st_ref[hd, :, sc] = st_ref[hd, :, sc] * chunk_decay[direction] + update
                    else:
                        st_ref[hd, :, sc] = update
            if sout_ref is not None:
                sout_ref[direction, hd] = st_ref[hd]

        for r0 in range(0, seq, RET_ROW_BLOCK):
            o = o_ref[r0:r0 + RET_ROW_BLOCK, vs]
            mu = jnp.mean(o, axis=-1, keepdims=True)
            var = jnp.mean(jnp.square(o - mu), axis=-1, keepdims=True)
            on = (o - mu) * lax.rsqrt(var + EPS) * gn_ref[:, vs]
            ys = slice(row0 + r0, row0 + r0 + RET_ROW_BLOCK)
            y_ref[ys, vs] = (gate_ref[ys, vs].astype(F32) * on).astype(y_ref.dtype)


def _ret_ctx_kernel(x_ref, mod_ref, g_ref, w_ref, lr_ref, gn_ref, y_ref, sout_ref,
                    qkv_s, gate_s, o_s, st_s, *, seq, dk, dv):
    nh = RET_HEADS
    _ret_project(x_ref, mod_ref, g_ref, w_ref, qkv_s, gate_s)
    decays = [_ret_decays(lr_ref, hd, seq, dk ** -0.5) for hd in range(nh)]
    for b in range(x_ref.shape[0] // seq):
        _ret_scan(decays, (qkv_s, 0), (qkv_s, nh * dk), (qkv_s, 2 * nh * dk), gate_s, gn_ref, None,
                  y_ref, sout_ref.at[b, 0], o_s, st_s, row0=b * seq, seq=seq, chunk=seq,
                  hps=nh, dk=dk, dv=dv)


def _ret_ctx_call(x, mod, g, w, log_rate, gn_g, rows, seq, n_gate):
    tm, d = rows.tm, x.shape[1]
    assert tm % seq == 0 and seq <= 2 * LANES
    nh = RET_HEADS
    n_qkv = w.shape[1] - n_gate
    dv = n_gate // nh
    dk = (n_qkv - n_gate) // (2 * nh)
    nb = tm // seq
    m = rows.ctx_tiles * tm
    return pl.pallas_call(
        functools.partial(_ret_ctx_kernel, seq=seq, dk=dk, dv=dv),
        grid=(rows.ctx_tiles,),
        in_specs=[
            pl.BlockSpec((tm, d), lambda i: (i, 0)),
            pl.BlockSpec((1, 6, d), lambda i: (0, 0, 0)),
            _resident(g.shape, lambda i: (0, 0)),
            _resident(w.shape, lambda i: (0, 0)),
            _resident(log_rate.shape, lambda i: (0, 0, 0)),
            _resident(gn_g.shape, lambda i: (0, 0)),
        ],
        out_specs=[pl.BlockSpec((tm, n_gate), lambda i: (i, 0)),
                   pl.BlockSpec((nb, 1, 2, nh, dk, dv), lambda i: (i, 0, 0, 0, 0, 0))],
        out_shape=[jax.ShapeDtypeStruct((m, n_gate), BF16),
                   jax.ShapeDtypeStruct((m // seq, 1, 2, nh, dk, dv), F32)],
        scratch_shapes=[pltpu.VMEM((tm, n_qkv), BF16), pltpu.VMEM((tm, n_gate), BF16),
                        pltpu.VMEM((seq, n_gate), F32), pltpu.VMEM((nh, dk, dv), F32)],
        compiler_params=_cparams(("arbitrary",)),
        name="ret_ctx",
    )(x, mod, g, w, log_rate, gn_g)


RET_LAT_ROWS = 512


def _ret_lat_kernel(x_ref, mod_ref, g_ref, wq_ref, wk_ref, wv_ref, wg_ref, lr_ref, gn_ref, s0_ref,
                    y_ref, h_s, q_s, k_s, v_s, gate_s, o_s, st_s, *, chunk, dk, dv):
    seq = x_ref.shape[0]

    @pl.when(pl.program_id(1) == 0)
    def _():
        h_s[...] = _modulate(x_ref[...], g_ref[...], mod_ref[0, 0:1, :], mod_ref[0, 1:2, :]).astype(BF16)

    rc = min(RET_LAT_ROWS, seq)
    for r in range(seq // rc):
        rs = slice(r * rc, (r + 1) * rc)
        h = h_s[rs, :]
        q_s[rs, :] = _dot(h, wq_ref[...]).astype(BF16)
        k_s[rs, :] = _dot(h, wk_ref[...]).astype(BF16)
        v_s[rs, :] = _dot(h, wv_ref[...]).astype(BF16)
        gate_s[rs, :] = _silu(_dot(h, wg_ref[...])).astype(BF16)
    decays = [_ret_decays(lr_ref, 0, chunk, dk ** -0.5)]
    _ret_scan(decays, (q_s, 0), (k_s, 0), (v_s, 0), gate_s, gn_ref, s0_ref.at[0], y_ref, None,
              o_s, st_s, row0=0, seq=seq, chunk=chunk, hps=1, dk=dk, dv=dv)


def _ret_lat_call(x, mod, g, w, log_rate, gn_g, s0, *, bsz, seq, row0, n_gate):
    d = x.shape[1]
    nh = RET_HEADS
    n_qkv = w.shape[1] - n_gate
    dv = n_gate // nh
    dk = (n_qkv - n_gate) // (2 * nh)
    chunk = min(seq, 2 * LANES)
    assert row0 % seq == 0 and dv % dk == 0
    b0 = row0 // seq
    return pl.pallas_call(
        functools.partial(_ret_lat_kernel, chunk=chunk, dk=dk, dv=dv),
        grid=(bsz, nh),
        in_specs=[
            pl.BlockSpec((seq, d), lambda b, h: (b0 + b, 0)),
            pl.BlockSpec((1, 6, d), lambda b, h: (1 + b, 0, 0)),
            _resident(g.shape, lambda b, h: (0, 0)),
            pl.BlockSpec((d, dk), lambda b, h: (0, h)),
            pl.BlockSpec((d, dk), lambda b, h: (0, nh + h)),
            pl.BlockSpec((d, dv), lambda b, h: (0, (2 * nh * dk) // dv + h)),
            pl.BlockSpec((d, dv), lambda b, h: (0, n_qkv // dv + h)),
            pl.BlockSpec((1, 2, 1), lambda b, h: (h, 0, 0)),
            pl.BlockSpec((1, dv), lambda b, h: (0, h)),
            pl.BlockSpec((1, 2, 1, dk, dv), lambda b, h: (b, 0, h, 0, 0)),
        ],
        out_specs=pl.BlockSpec((seq, dv), lambda b, h: (b, h)),
        out_shape=jax.ShapeDtypeStruct((bsz * seq, n_gate), BF16),
        scratch_shapes=[pltpu.VMEM((seq, d), BF16), pltpu.VMEM((seq, dk), BF16), pltpu.VMEM((seq, dk), BF16),
                        pltpu.VMEM((seq, dv), BF16), pltpu.VMEM((seq, dv), BF16),
                        pltpu.VMEM((seq, dv), F32), pltpu.VMEM((1, dk, dv), F32)],
        compiler_params=_cparams(("arbitrary", "arbitrary")),
        name="ret_lat",
    )(x, mod, g, w, w, w, w, log_rate, gn_g, s0)


def _rope_swap_index():
    f = ROPE_AXIS_FREQS
    idx = jnp.arange(MLA_ROPE)
    return jnp.where((idx // f) % 2 == 0, idx + f, idx - f)


def _mla_weights(w_a, q_norm_g, kv_norm_g, w_q_b, w_kv_b):
    swap = _rope_swap_index()
    nq = MLA_Q_RANK + MLA_KV_RANK
    w_kpe = w_a[:, nq:]
    w_kpe_sw = w_kpe[:, swap]
    wa = jnp.concatenate([w_a[:, :nq], w_kpe, w_kpe, w_kpe_sw, w_kpe_sw], axis=1).astype(BF16)
    wq = w_q_b.reshape(MLA_Q_RANK, MLA_HEADS, MLA_NOPE + MLA_ROPE)
    wq_nope = wq[:, :, :MLA_NOPE].reshape(MLA_Q_RANK, MLA_HEADS * MLA_NOPE)
    wq_pe = wq[:, :, MLA_NOPE:]
    wq_all = jnp.concatenate(
        [wq_nope, wq_pe.reshape(MLA_Q_RANK, -1), wq_pe[:, :, swap].reshape(MLA_Q_RANK, -1)],
        axis=1).astype(BF16)
    wkv = w_kv_b.reshape(MLA_KV_RANK, MLA_HEADS, MLA_NOPE + MLA_V)
    wkv_all = jnp.concatenate(
        [wkv[:, :, :MLA_NOPE].reshape(MLA_KV_RANK, -1), wkv[:, :, MLA_NOPE:].reshape(MLA_KV_RANK, -1)],
        axis=1).astype(BF16)
    return wa, q_norm_g[None, :], kv_norm_g[None, :], wq_all, wkv_all


def _rope_tables(n_tokens):
    f = ROPE_AXIS_FREQS
    f32 = np.float32
    rows = n_tokens // GRID_W
    r = np.repeat(np.arange(rows, dtype=f32), GRID_W)
    col = np.tile(np.arange(GRID_W, dtype=f32), rows)
    inv = (f32(ROPE_THETA) ** (-np.arange(f, dtype=f32) / f32(f))).astype(f32)
    ang_r, ang_c = r[:, None] * inv, col[:, None] * inv
    cos = np.concatenate([np.cos(ang_r)] * 2 + [np.cos(ang_c)] * 2, axis=1)
    sin = np.concatenate([-np.sin(ang_r), np.sin(ang_r), -np.sin(ang_c), np.sin(ang_c)], axis=1)
    reps = LANES // MLA_ROPE
    return (jnp.asarray(np.concatenate([cos] * reps, axis=1), F32),
            jnp.asarray(np.concatenate([sin] * reps, axis=1), F32))


def kernel(x_prompt, x_sample, c, c_ctx, cache_mla_ckv, cache_mla_kpe, state_ret, ada_w, ada_b, norm_mix_g, norm_ffn_g, mla_w_a, mla_q_norm_g, mla_kv_norm_g, mla_w_q_b, mla_w_kv_b, mla_w_o, conv_w_in, conv_w, conv_w_out, ret_w_in, ret_log_rate, ret_gn_g, ret_w_out, ffn_w_in, ffn_w_out, final_norm_g):
    batch, seq, d = x_prompt.shape
    dec_batch, dec_seq, _ = x_sample.shape
    depth = ada_w.shape[0]
    n_ctx = batch * seq
    n_lat = dec_batch * dec_seq
    past = cache_mla_ckv.shape[2]
    assert 1 + dec_batch <= MOD_ROWS

    cvec = jnp.zeros((MOD_ROWS, d), F32).at[0].set(c_ctx).at[1:1 + dec_batch].set(c)
    mod_all = _ada_call(cvec, ada_w, ada_b).reshape(depth, MOD_ROWS, 6, d)

    rows_s = _Rows(n_ctx, dec_batch, dec_seq, min(ROW_TILE_FUSED, dec_seq))
    rows_l = _Rows(n_ctx, dec_batch, dec_seq, min(ROW_TILE, dec_seq))
    rope_tabs = _rope_tables(dec_seq)
    final_g = final_norm_g[None, :]
    mla_w_o_bf = mla_w_o.astype(BF16)

    def mixer_cast_jobs(layer):
        if layer >= depth:
            return []
        kind, j = layer % N_MIXERS, layer // N_MIXERS
        if kind == 1:
            return [(conv_w_in, j), (conv_w_out, j)]
        if kind == 2:
            return [(ret_w_in, j), (ret_w_out, j)]
        return []

    x = (x_prompt.reshape(n_ctx, d), x_sample.reshape(n_lat, d))
    new_ckv, new_kpe, new_ret = [], [], []
    ffn_bf = mixer_bf = None
    for i in range(depth):
        kind, j = i % N_MIXERS, i // N_MIXERS
        mod = mod_all[i]
        g_mix = norm_mix_g[i][None, :]
        if ffn_bf is None:
            assert kind == 0
        if kind == 0:
            wts = _mla_weights(mla_w_a[j], mla_q_norm_g[j], mla_kv_norm_g[j], mla_w_q_b[j], mla_w_kv_b[j])
            split = isinstance(x, tuple)
            o_c, ckv_c, kpe_c = _mla_ctx_call(x[0] if split else x, mod, g_mix, wts, rows_l, seq)
            ql, kl, vl = _mla_proj_call(
                x[1] if split else x, 0 if split else rows_l.ctx_tiles, mod, g_mix, wts, rows_l,
                tile0=rows_l.ctx_tiles, n_tiles=rows_l.n_tiles - rows_l.ctx_tiles, rope_tabs=rope_tabs)
            new_ckv.append(ckv_c.reshape(batch, seq, MLA_KV_RANK))
            new_kpe.append(kpe_c.reshape(batch, seq, MLA_ROPE))
            kpe_rep = jnp.concatenate([cache_mla_kpe[:, j]] * (LANES // MLA_ROPE), axis=-1)
            kp, vp = _cache_expand_call(
                cache_mla_ckv[:, j].reshape(dec_batch * past, MLA_KV_RANK),
                kpe_rep.reshape(dec_batch * past, LANES), wts[4])
            jobs = [] if ffn_bf is not None else [(ffn_w_in, i), (ffn_w_out, i)]
            o_l, cast = _attn_call(ql, [kp, kl], [vp, vl], n_batch=dec_batch, seq=dec_seq,
                                   sks=[past, dec_seq], nb=1, tq=min(ATTN_Q_TILE, dec_seq),
                                   name="attn_lat", cast_jobs=jobs)
            if jobs:
                ffn_bf = cast
            mix = (o_c, o_l, mla_w_o_bf, j)
        elif kind == 1:
            mix = None
            x = _conv_call(x, mod, g_mix, mixer_bf[0][0], conv_w[j], mixer_bf[1][0], rows_l, seq, dec_seq)
        else:
            n_gate = ret_w_out.shape[1]
            lr = ret_log_rate[j].T[:, :, None]
            gn = ret_gn_g[j][None, :]
            y_c, st = _ret_ctx_call(x, mod, g_mix, mixer_bf[0][0], lr, gn, rows_s, seq, n_gate)
            y_l = _ret_lat_call(x, mod, g_mix, mixer_bf[0][0], lr, gn, state_ret[:, j],
                                bsz=dec_batch, seq=dec_seq, row0=n_ctx, n_gate=n_gate)
            new_ret.append(st)
            mix = (y_c, y_l, mixer_bf[1], 0)
        last = i == depth - 1
        jobs = [] if last else [(ffn_w_in, i + 1), (ffn_w_out, i + 1)] + mixer_cast_jobs(i + 1)
        x, cast = _ffn_call(x, mod, norm_ffn_g[i][None, :], ffn_bf[0], ffn_bf[1], 0, final_g,
                            rows_l if mix is None else rows_s, final=last, cast_jobs=jobs, mix=mix)
        ffn_bf, mixer_bf = cast[:2], cast[2:]

    y_prompt = x[0].reshape(batch, seq, d)
    y_sample = x[1].reshape(dec_batch, dec_seq, d)
    return (y_prompt, y_sample, jnp.stack(new_ckv, axis=1), jnp.stack(new_kpe, axis=1),
            jnp.concatenate(new_ret, axis=1))
```

```python
import functools
import math

import jax
import jax.numpy as jnp
import numpy as np
from jax import lax
from jax.experimental import pallas as pl
from jax.experimental.pallas import tpu as pltpu

F32 = jnp.float32
BF16 = jnp.bfloat16

N_MIXERS = 3
MLA_HEADS = 8
MLA_NOPE = 128
MLA_ROPE = 64
MLA_V = 128
MLA_Q_RANK = 384
MLA_KV_RANK = 256
MLA_SCALE = (MLA_NOPE + MLA_ROPE) ** -0.5
MLA_Q_SCALE = MLA_SCALE * 1.4426950408889634
ROPE_THETA = 10000.0
ROPE_AXIS_FREQS = MLA_ROPE // 4
GRID_W = 64
RET_HEADS = 4
EPS = 1e-6

LANES = 128
MLA_HEAD_PAD = 2 * LANES
MOD_ROWS = 8
VMEM_LIMIT = 56 * 1024 * 1024

ROW_TILE = 1024
ROW_TILE_FUSED = 512
ATTN_Q_TILE = 512
FFN_ROW_BLOCK = 256


def _cparams(sem):
    return pltpu.CompilerParams(dimension_semantics=sem, vmem_limit_bytes=VMEM_LIMIT)


def _resident(shape, index_map):
    return pl.BlockSpec(shape, index_map, pipeline_mode=pl.Buffered(1))


def _rms(x, g):
    return x * lax.rsqrt(jnp.mean(x * x, axis=-1, keepdims=True) + EPS) * g


def _modulate(x, g, shift, scale):
    return _rms(x, g) * (1.0 + scale) + shift


def _silu(x):
    return x * jax.nn.sigmoid(x)


def _dot(a, b):
    return jnp.dot(a, b, preferred_element_type=F32)


def _dot_nt(a, b):
    return lax.dot_general(a, b, (((1,), (1,)), ((), ())), preferred_element_type=F32)


MAX_CAST_SLABS = 16


def _cast_specs(jobs, n_steps, step_of):
    n_slabs = math.gcd(n_steps, MAX_CAST_SLABS)
    per = n_steps // n_slabs

    def slab_of_step(*g):
        return step_of(*g) // per

    in_specs, out_specs, out_shape = [], [], []
    for w, layer in jobs:
        _, k, n = w.shape
        blk = (1, k // n_slabs, n)
        in_specs.append(pl.BlockSpec(blk, lambda *g, layer=layer: (layer, slab_of_step(*g), 0)))
        out_specs.append(pl.BlockSpec(blk, lambda *g: (0, slab_of_step(*g), 0)))
        out_shape.append(jax.ShapeDtypeStruct((1, k, n), BF16))
    return in_specs, out_specs, out_shape


def _run_casts(in_refs, out_refs):
    for src, dst in zip(in_refs, out_refs):
        dst[...] = src[...].astype(BF16)


def _ada_kernel(c_ref, w_ref, b_ref, o_ref):
    a = _silu(c_ref[...]).astype(BF16)
    o_ref[0] = _dot(a, w_ref[0].astype(BF16)) + b_ref[0]


def _ada_call(cvec, ada_w, ada_b):
    depth, d, n = ada_w.shape
    tn = n // 2
    return pl.pallas_call(
        _ada_kernel,
        grid=(depth, n // tn),
        in_specs=[
            pl.BlockSpec((MOD_ROWS, d), lambda l, j: (0, 0)),
            pl.BlockSpec((1, d, tn), lambda l, j: (l, 0, j)),
            pl.BlockSpec((1, 1, tn), lambda l, j: (l, 0, j)),
        ],
        out_specs=pl.BlockSpec((1, MOD_ROWS, tn), lambda l, j: (l, 0, j)),
        out_shape=jax.ShapeDtypeStruct((depth, MOD_ROWS, n), F32),
        compiler_params=_cparams(("arbitrary", "arbitrary")),
        name="ada",
    )(cvec, ada_w, ada_b.reshape(depth, 1, n))


class _Rows:
    def __init__(self, n_ctx, dec_batch, dec_seq, tm):
        assert n_ctx % tm == 0 and dec_seq % tm == 0
        self.tm = tm
        self.ctx_tiles = n_ctx // tm
        self.seq_tiles = dec_seq // tm
        self.n_tiles = self.ctx_tiles + dec_batch * self.seq_tiles

    def group(self, i):
        lat = jnp.maximum(i - self.ctx_tiles, 0) // self.seq_tiles
        return jnp.where(i < self.ctx_tiles, 0, 1 + lat)


def _mla_project(x_ref, mod_ref, g_ref, wa_ref, qg_ref, kvg_ref, wq_ref, wkv_ref, rope_refs,
                 q_ref, k_ref, v_ref, cache_refs):
    h = _modulate(x_ref[...], g_ref[...], mod_ref[0, 0:1, :], mod_ref[0, 1:2, :]).astype(BF16)
    a = _dot(h, wa_ref[...])
    nq = MLA_Q_RANK + MLA_KV_RANK
    ckv_n = _rms(a[:, MLA_Q_RANK:nq], kvg_ref[...])
    kpe = a[:, nq:nq + LANES]
    if cache_refs is not None:
        cache_refs[0][...] = ckv_n
        cache_refs[1][...] = kpe[:, :MLA_ROPE]
    if rope_refs is not None:
        cos, sin = rope_refs[0][...], rope_refs[1][...]
        kpe = kpe * cos + a[:, nq + LANES:nq + 2 * LANES] * sin
    kpe = kpe.astype(BF16)

    qn = _rms(a[:, :MLA_Q_RANK], qg_ref[...]).astype(BF16)
    nh = MLA_HEADS
    n_nope, n_pe = nh * MLA_NOPE, nh * MLA_ROPE
    q_nope = _dot(qn, wq_ref[:, :n_nope])
    q_pe = _dot(qn, wq_ref[:, n_nope:n_nope + n_pe])
    if rope_refs is not None:
        q_sw = _dot(qn, wq_ref[:, n_nope + n_pe:n_nope + 2 * n_pe])
        cos, sin = cos * MLA_Q_SCALE, sin * MLA_Q_SCALE
    kv = _dot(ckv_n.astype(BF16), wkv_ref[...])
    low_half = lax.broadcasted_iota(jnp.int32, (1, LANES), 1) < MLA_ROPE
    for pair in range(nh // 2):
        ps = slice(pair * LANES, (pair + 1) * LANES)
        if rope_refs is not None:
            pe = q_pe[:, ps] * cos + q_sw[:, ps] * sin
        else:
            pe = q_pe[:, ps] * MLA_Q_SCALE
        for hd, keep in ((2 * pair, low_half), (2 * pair + 1, jnp.logical_not(low_half))):
            lo = hd * MLA_HEAD_PAD
            hs = slice(hd * LANES, (hd + 1) * LANES)
            q_ref[:, lo:lo + LANES] = (q_nope[:, hs] * MLA_Q_SCALE).astype(BF16)
            q_ref[:, lo + LANES:lo + 2 * LANES] = jnp.where(keep, pe, 0.0).astype(BF16)
            k_ref[:, lo:lo + LANES] = kv[:, hs].astype(BF16)
            k_ref[:, lo + LANES:lo + 2 * LANES] = kpe
    v_ref[...] = kv[:, nh * LANES:].astype(BF16)


def _mla_proj_kernel(x_ref, mod_ref, g_ref, wa_ref, qg_ref, kvg_ref, wq_ref, wkv_ref, ck_ref, sk_ref,
                     q_ref, k_ref, v_ref):
    _mla_project(x_ref, mod_ref, g_ref, wa_ref, qg_ref, kvg_ref, wq_ref, wkv_ref, (ck_ref, sk_ref),
                 q_ref, k_ref, v_ref, None)


def _mla_proj_call(x, x_tile0, mod, g, wts, rows, *, tile0, n_tiles, rope_tabs):
    tm, d = rows.tm, x.shape[1]
    wa, qg, kvg, wq, wkv = wts
    full = lambda arr: _resident(arr.shape, lambda i: (0,) * arr.ndim)
    in_specs = [
        pl.BlockSpec((tm, d), lambda i: (i + x_tile0, 0)),
        pl.BlockSpec((1, 6, d), lambda i: (rows.group(i + tile0), 0, 0)),
        full(g), full(wa), full(qg), full(kvg), full(wq), full(wkv),
    ]
    in_specs += [pl.BlockSpec((tm, LANES), lambda i: (i % rows.seq_tiles, 0)) for _ in rope_tabs]
    m = n_tiles * tm
    hp, hv = MLA_HEADS * MLA_HEAD_PAD, MLA_HEADS * MLA_V
    return pl.pallas_call(
        _mla_proj_kernel,
        grid=(n_tiles,),
        in_specs=in_specs,
        out_specs=[pl.BlockSpec((tm, hp), lambda i: (i, 0)), pl.BlockSpec((tm, hp), lambda i: (i, 0)),
                   pl.BlockSpec((tm, hv), lambda i: (i, 0))],
        out_shape=[jax.ShapeDtypeStruct((m, hp), BF16), jax.ShapeDtypeStruct((m, hp), BF16),
                   jax.ShapeDtypeStruct((m, hv), BF16)],
        compiler_params=_cparams(("arbitrary",)),
        name="mla_proj_lat",
    )(x, mod, g, wa, qg, kvg, wq, wkv, *rope_tabs)


def _mla_ctx_kernel(x_ref, mod_ref, g_ref, wa_ref, qg_ref, kvg_ref, wq_ref, wkv_ref,
                    o_ref, ckv_ref, kpe_ref, q_s, k_s, v_s, s_ref, *, seq):
    _mla_project(x_ref, mod_ref, g_ref, wa_ref, qg_ref, kvg_ref, wq_ref, wkv_ref, None,
                 q_s, k_s, v_s, (ckv_ref, kpe_ref))
    _attend(q_s, [k_s], [v_s], o_ref, s_ref, nb=x_ref.shape[0] // seq, tq=seq, sks=(seq,))


def _mla_ctx_call(x, mod, g, wts, rows, seq):
    tm, d = rows.tm, x.shape[1]
    assert tm % seq == 0
    wa, qg, kvg, wq, wkv = wts
    full = lambda arr: _resident(arr.shape, lambda i: (0,) * arr.ndim)
    m = rows.ctx_tiles * tm
    hp, hv = MLA_HEADS * MLA_HEAD_PAD, MLA_HEADS * MLA_V
    return pl.pallas_call(
        functools.partial(_mla_ctx_kernel, seq=seq),
        grid=(rows.ctx_tiles,),
        in_specs=[
            pl.BlockSpec((tm, d), lambda i: (i, 0)),
            pl.BlockSpec((1, 6, d), lambda i: (0, 0, 0)),
            full(g), full(wa), full(qg), full(kvg), full(wq), full(wkv),
        ],
        out_specs=[pl.BlockSpec((tm, hv), lambda i: (i, 0)),
                   pl.BlockSpec((tm, MLA_KV_RANK), lambda i: (i, 0)),
                   pl.BlockSpec((tm, MLA_ROPE), lambda i: (i, 0))],
        out_shape=[jax.ShapeDtypeStruct((m, hv), BF16),
                   jax.ShapeDtypeStruct((m, MLA_KV_RANK), F32),
                   jax.ShapeDtypeStruct((m, MLA_ROPE), F32)],
        scratch_shapes=[pltpu.VMEM((tm, hp), BF16), pltpu.VMEM((tm, hp), BF16), pltpu.VMEM((tm, hv), BF16),
                        pltpu.VMEM((2, seq, seq), F32)],
        compiler_params=_cparams(("arbitrary",)),
        name="mla_ctx",
    )(x, mod, g, wa, qg, kvg, wq, wkv)


def _cache_expand_kernel(ckv_ref, kpe_ref, wkv_ref, k_ref, v_ref):
    kv = _dot(ckv_ref[...].astype(BF16), wkv_ref[...])
    kpe = kpe_ref[...].astype(BF16)
    nh = MLA_HEADS
    for hd in range(nh):
        lo = hd * MLA_HEAD_PAD
        k_ref[:, lo:lo + LANES] = kv[:, hd * LANES:(hd + 1) * LANES].astype(BF16)
        k_ref[:, lo + LANES:lo + 2 * LANES] = kpe
    v_ref[...] = kv[:, nh * LANES:].astype(BF16)


def _cache_expand_call(ckv, kpe_pad, wkv):
    m = ckv.shape[0]
    tm = min(m, 512)
    hp = MLA_HEADS * MLA_HEAD_PAD
    return pl.pallas_call(
        _cache_expand_kernel,
        grid=(m // tm,),
        in_specs=[
            pl.BlockSpec((tm, MLA_KV_RANK), lambda i: (i, 0)),
            pl.BlockSpec((tm, LANES), lambda i: (i, 0)),
            _resident(wkv.shape, lambda i: (0, 0)),
        ],
        out_specs=[pl.BlockSpec((tm, hp), lambda i: (i, 0)),
                   pl.BlockSpec((tm, MLA_HEADS * MLA_V), lambda i: (i, 0))],
        out_shape=[jax.ShapeDtypeStruct((m, hp), BF16),
                   jax.ShapeDtypeStruct((m, MLA_HEADS * MLA_V), BF16)],
        compiler_params=_cparams(("arbitrary",)),
        name="mla_cache_expand",
    )(ckv, kpe_pad, wkv)


def _attend(q_ref, k_refs, v_refs, o_ref, s_ref, *, nb, tq, sks):
    problems = [(b, h) for b in range(nb) for h in range(MLA_HEADS)]
    seg_cols = [sum(sks[:i]) for i in range(len(sks))]

    def scores(idx):
        b, h = problems[idx]
        q = q_ref[b * tq:(b + 1) * tq, h * MLA_HEAD_PAD:(h + 1) * MLA_HEAD_PAD]
        m_lane = None
        for k_ref, sk, c0 in zip(k_refs, sks, seg_cols):
            s = _dot_nt(q, k_ref[b * sk:(b + 1) * sk, h * MLA_HEAD_PAD:(h + 1) * MLA_HEAD_PAD])
            s_ref[idx % 2, :, c0:c0 + sk] = s
            for c in range(sk // LANES):
                piece = s[:, c * LANES:(c + 1) * LANES]
                m_lane = piece if m_lane is None else jnp.maximum(m_lane, piece)
        return m_lane.max(axis=-1, keepdims=True)

    def weighted_values(idx, m):
        b, h = problems[idx]
        acc = None
        for v_ref, sk, c0 in zip(v_refs, sks, seg_cols):
            p = jnp.exp2(s_ref[idx % 2, :, c0:c0 + sk] - m).astype(BF16)
            v = v_ref[b * sk:(b + 1) * sk, h * MLA_V:(h + 1) * MLA_V]
            part = _dot(p, jnp.concatenate([v, jnp.ones_like(v)], axis=1))
            acc = part if acc is None else acc + part
        o = acc[:, :MLA_V] / acc[:, MLA_V:]
        o_ref[b * tq:(b + 1) * tq, h * MLA_V:(h + 1) * MLA_V] = o.astype(o_ref.dtype)

    m = scores(0)
    for idx in range(len(problems)):
        m_next = scores(idx + 1) if idx + 1 < len(problems) else None
        weighted_values(idx, m)
        m = m_next


def _attn_kernel(*refs, n_seg, nb, tq, sks, n_cast):
    n_in = 1 + 2 * n_seg + n_cast
    _run_casts(refs[1 + 2 * n_seg:n_in], refs[n_in + 1:n_in + 1 + n_cast])
    _attend(refs[0], refs[1:1 + n_seg], refs[1 + n_seg:1 + 2 * n_seg], refs[n_in],
            refs[n_in + 1 + n_cast], nb=nb, tq=tq, sks=sks)


def _attn_call(q, ks, vs, *, n_batch, seq, sks, nb, tq, name, cast_jobs=()):
    n_seg = len(ks)
    tiles = seq // tq
    assert nb == 1 or tiles == 1
    hp, hv = MLA_HEADS * MLA_HEAD_PAD, MLA_HEADS * MLA_V
    in_specs = [pl.BlockSpec((nb * tq, hp), lambda b, t: (b * tiles + t, 0))]
    in_specs += [pl.BlockSpec((nb * sk, hp), lambda b, t: (b, 0)) for sk in sks]
    in_specs += [pl.BlockSpec((nb * sk, hv), lambda b, t: (b, 0)) for sk in sks]
    c_in, c_out, c_shape = _cast_specs(cast_jobs, (n_batch // nb) * tiles, lambda b, t: b * tiles + t)
    outs = pl.pallas_call(
        functools.partial(_attn_kernel, n_seg=n_seg, nb=nb, tq=tq, sks=tuple(sks), n_cast=len(cast_jobs)),
        grid=(n_batch // nb, tiles),
        in_specs=in_specs + c_in,
        out_specs=[pl.BlockSpec((nb * tq, hv), lambda b, t: (b * tiles + t, 0))] + c_out,
        out_shape=[jax.ShapeDtypeStruct((n_batch * seq, hv), BF16)] + c_shape,
        scratch_shapes=[pltpu.VMEM((2, tq, sum(sks)), F32)],
        compiler_params=_cparams(("arbitrary", "arbitrary")),
        name=name,
    )(q, *ks, *vs, *[w for w, _ in cast_jobs])
    return outs[0], list(outs[1:])


def _split_specs(rows, width):
    ct = rows.ctx_tiles
    return [pl.BlockSpec((rows.tm, width), lambda i: (jnp.minimum(i, ct - 1), 0)),
            pl.BlockSpec((rows.tm, width), lambda i: (jnp.maximum(i - ct, 0), 0))]


def _ffn_kernel(*refs, th, final, ctx_tiles, n_cast, n_x, mixed):
    it = iter(refs)
    x_refs = [next(it) for _ in range(n_x)]
    if mixed:
        yc_ref, yl_ref, wo_ref = next(it), next(it), next(it)
    mod_ref, g_ref, win_ref, wout_ref, fg_ref = (next(it) for _ in range(5))
    cast_in = [next(it) for _ in range(n_cast)]
    out_refs = [next(it) for _ in range(2 if final else 1)]
    cast_out = [next(it) for _ in range(n_cast)]
    scratch = list(it)
    act_ref = scratch[0]
    _run_casts(cast_in, cast_out)
    is_ctx = pl.program_id(0) < ctx_tiles
    if mixed:
        x1_ref = scratch[1]

        def mix(y_ref, x_ref):
            x1_ref[...] = x_ref[...] + mod_ref[0, 2:3, :] * _dot(y_ref[...], wo_ref[0])

        pl.when(is_ctx)(lambda: mix(yc_ref, x_refs[0]))
        pl.when(jnp.logical_not(is_ctx))(lambda: mix(yl_ref, x_refs[-1]))
        x = x1_ref[...]
    else:
        x = x_refs[0][...]
    h = _modulate(x, g_ref[...], mod_ref[0, 3:4, :], mod_ref[0, 4:5, :]).astype(BF16)
    hidden = wout_ref.shape[1]
    tm = h.shape[0]
    for c in range(hidden // th):
        for r0 in range(0, tm, FFN_ROW_BLOCK):
            hr = h[r0:r0 + FFN_ROW_BLOCK, :]
            a = _dot(hr, win_ref[0, :, c * th:(c + 1) * th])
            b = _dot(hr, win_ref[0, :, hidden + c * th:hidden + (c + 1) * th])
            act_ref[r0:r0 + FFN_ROW_BLOCK, c * th:(c + 1) * th] = (_silu(a) * b).astype(BF16)
    y = x + mod_ref[0, 5:6, :] * _dot(act_ref[...], wout_ref[0])
    if not final:
        out_refs[0][...] = y
    else:
        acc_ref = scratch[-1]
        acc_ref[...] = _rms(y, fg_ref[...])

        @pl.when(is_ctx)
        def _():
            out_refs[0][...] = acc_ref[...]

        @pl.when(jnp.logical_not(is_ctx))
        def _():
            out_refs[1][...] = acc_ref[...]


def _ffn_call(x, mod, g, w_in_all, w_out_all, layer, final_g, rows, final, cast_jobs=(), mix=None):
    tm, d = rows.tm, w_in_all.shape[1]
    hidden = w_out_all.shape[1]
    split_x = isinstance(x, tuple)
    assert not split_x or mix is not None
    xs = list(x) if split_x else [x]
    x_specs = _split_specs(rows, d) if split_x else [pl.BlockSpec((tm, d), lambda i: (i, 0))]
    mix_args, mix_specs = [], []
    scratch = [pltpu.VMEM((tm, hidden), BF16)]
    if mix is not None:
        y_ctx, y_lat, w_o_all, w_o_layer = mix
        k = w_o_all.shape[1]
        mix_args = [y_ctx, y_lat, w_o_all]
        mix_specs = _split_specs(rows, k) + [_resident((1, k, d), lambda i: (w_o_layer, 0, 0))]
        scratch.append(pltpu.VMEM((tm, d), F32))
    if final:
        out_specs = _split_specs(rows, d)
        out_shape = [jax.ShapeDtypeStruct((rows.ctx_tiles * tm, d), F32),
                     jax.ShapeDtypeStruct(((rows.n_tiles - rows.ctx_tiles) * tm, d), F32)]
        scratch.append(pltpu.VMEM((tm, d), F32))
    else:
        out_specs = [pl.BlockSpec((tm, d), lambda i: (i, 0))]
        out_shape = [jax.ShapeDtypeStruct((rows.n_tiles * tm, d), F32)]
    c_in, c_out, c_shape = _cast_specs(cast_jobs, rows.n_tiles, lambda i: i)
    outs = pl.pallas_call(
        functools.partial(_ffn_kernel, th=2 * LANES, final=final, ctx_tiles=rows.ctx_tiles,
                          n_cast=len(cast_jobs), n_x=len(xs), mixed=mix is not None),
        grid=(rows.n_tiles,),
        in_specs=x_specs + mix_specs + [
            pl.BlockSpec((1, 6, d), lambda i: (rows.group(i), 0, 0)),
            _resident(g.shape, lambda i: (0, 0)),
            _resident((1, d, 2 * hidden), lambda i: (layer, 0, 0)),
            _resident((1, hidden, d), lambda i: (layer, 0, 0)),
            _resident(final_g.shape, lambda i: (0, 0)),
        ] + c_in,
        out_specs=out_specs + c_out,
        out_shape=out_shape + c_shape,
        scratch_shapes=scratch,
        compiler_params=_cparams(("arbitrary",)),
        name=("ffn_final" if final else "ffn") + ("_mix" if mix is not None else ""),
    )(*xs, *mix_args, mod, g, w_in_all, w_out_all, final_g, *[w for w, _ in cast_jobs])
    n_out = 2 if final else 1
    stream = tuple(outs[:2]) if final else outs[0]
    return stream, list(outs[n_out:])


CONV_HALO = 16


def _conv_kernel(x_ref, xp_ref, xn_ref, mod_ref, g_ref, win_ref, cw_ref, wout_ref, o_ref,
                 h_ref, z_ref, act_ref, *, tn, ctx_tiles, ctx_seq, lat_seq):
    tm, d = x_ref.shape
    hl = CONV_HALO
    shift, scale = mod_ref[0, 0:1, :], mod_ref[0, 1:2, :]
    h_ref[0:hl, :] = _modulate(xp_ref[...], g_ref[...], shift, scale).astype(BF16)
    h_ref[hl:hl + tm, :] = _modulate(x_ref[...], g_ref[...], shift, scale).astype(BF16)
    h_ref[hl + tm:, :] = _modulate(xn_ref[...], g_ref[...], shift, scale).astype(BF16)
    i = pl.program_id(0)
    is_ctx = i < ctx_tiles
    row = lax.broadcasted_iota(jnp.int32, (tm, 1), 0)
    lat_row0 = (jnp.maximum(i - ctx_tiles, 0) % (lat_seq // tm)) * tm
    pos = jnp.where(is_ctx, row & (ctx_seq - 1), row + lat_row0)
    seq = jnp.where(is_ctx, ctx_seq, lat_seq)
    has_prev = pos != 0
    has_next = pos != seq - 1
    for c in range(d // tn):
        sl = slice(c * tn, (c + 1) * tn)
        h = h_ref[...]
        cg = _dot(h, win_ref[:, d + c * tn:d + (c + 1) * tn])
        u = _dot(h, win_ref[:, 2 * d + c * tn:2 * d + (c + 1) * tn])
        z_ref[...] = cg * u
        bg = _dot(h_ref[hl:hl + tm, :], win_ref[:, sl])
        z_prev = jnp.where(has_prev, z_ref[hl - 1:hl - 1 + tm, :], 0.0)
        z_next = jnp.where(has_next, z_ref[hl + 1:hl + 1 + tm, :], 0.0)
        conv = z_prev * cw_ref[0:1, sl] + z_ref[hl:hl + tm, :] * cw_ref[1:2, sl] + z_next * cw_ref[2:3, sl]
        act_ref[:, sl] = (bg * conv).astype(BF16)
    o_ref[...] = x_ref[...] + mod_ref[0, 2:3, :] * _dot(act_ref[...], wout_ref[...])


def _conv_call(x, mod, g, w_in, conv_w, w_out, rows, ctx_seq, lat_seq):
    tm, d = rows.tm, x.shape[1]
    hl = CONV_HALO
    assert tm % ctx_seq == 0 and ctx_seq & (ctx_seq - 1) == 0 and lat_seq % tm == 0 and tm % hl == 0
    last_halo = x.shape[0] // hl - 1
    return pl.pallas_call(
        functools.partial(_conv_kernel, tn=2 * LANES, ctx_tiles=rows.ctx_tiles, ctx_seq=ctx_seq,
                          lat_seq=lat_seq),
        grid=(rows.n_tiles,),
        in_specs=[
            pl.BlockSpec((tm, d), lambda i: (i, 0)),
            pl.BlockSpec((hl, d), lambda i: (jnp.maximum(i * (tm // hl) - 1, 0), 0)),
            pl.BlockSpec((hl, d), lambda i: (jnp.minimum((i + 1) * (tm // hl), last_halo), 0)),
            pl.BlockSpec((1, 6, d), lambda i: (rows.group(i), 0, 0)),
            _resident(g.shape, lambda i: (0, 0)),
            _resident(w_in.shape, lambda i: (0, 0)),
            _resident(conv_w.shape, lambda i: (0, 0)),
            _resident(w_out.shape, lambda i: (0, 0)),
        ],
        out_specs=pl.BlockSpec((tm, d), lambda i: (i, 0)),
        out_shape=jax.ShapeDtypeStruct(x.shape, F32),
        scratch_shapes=[pltpu.VMEM((tm + 2 * hl, d), BF16), pltpu.VMEM((tm + 2 * hl, 2 * LANES), F32),
                        pltpu.VMEM((tm, d), BF16)],
        compiler_params=_cparams(("arbitrary",)),
        name="conv_mixer",
    )(x, x, x, mod, g, w_in, conv_w, w_out)


RET_PROJ_CHUNK = 4 * LANES
RET_COL_BLOCK = 2 * LANES
RET_ROW_BLOCK = 64


def _ret_project(x_ref, mod_ref, g_ref, w_ref, qkv_ref, gate_ref):
    tn = RET_PROJ_CHUNK
    h = _modulate(x_ref[...], g_ref[...], mod_ref[0, 0:1, :], mod_ref[0, 1:2, :]).astype(BF16)
    n_qkv = qkv_ref.shape[1]
    for r0 in range(0, h.shape[0], FFN_ROW_BLOCK):
        rs = slice(r0, r0 + FFN_ROW_BLOCK)
        hr = h[rs, :]
        for c in range(n_qkv // tn):
            qkv_ref[rs, c * tn:(c + 1) * tn] = _dot(hr, w_ref[:, c * tn:(c + 1) * tn]).astype(BF16)
        for c in range(gate_ref.shape[1] // tn):
            gate = _dot(hr, w_ref[:, n_qkv + c * tn:n_qkv + (c + 1) * tn])
            gate_ref[rs, c * tn:(c + 1) * tn] = _silu(gate).astype(gate_ref.dtype)


def _ret_decays(lr_ref, hd, chunk, k_scale):
    row = lax.broadcasted_iota(jnp.int32, (chunk, chunk), 0).astype(F32)
    col = lax.broadcasted_iota(jnp.int32, (chunk, chunk), 1).astype(F32)
    ridx = lax.broadcasted_iota(jnp.int32, (chunk, 1), 0).astype(F32)
    dist = row - col
    log_gamma = -jnp.exp(lr_ref[hd])
    lg_f, lg_b = log_gamma[0:1, :], log_gamma[1:2, :]
    mask = jnp.where(dist > 0, jnp.exp(jnp.maximum(dist, 0.0) * lg_f),
                     jnp.where(dist < 0, jnp.exp(jnp.maximum(-dist, 0.0) * lg_b), 2.0)) * k_scale
    q_decay = (jnp.exp((ridx + 1.0) * lg_f), jnp.exp((chunk - ridx) * lg_b))
    k_decay = (jnp.exp((chunk - 1.0 - ridx) * lg_f) * k_scale, jnp.exp(ridx * lg_b) * k_scale)
    chunk_decay = (jnp.exp(chunk * lg_f), jnp.exp(chunk * lg_b))
    return mask, q_decay, k_decay, chunk_decay


def _ret_scan(decays, q, k, v, gate_ref, gn_ref, s0_ref, y_ref, sout_ref, o_ref, st_ref, *,
              row0, seq, chunk, hps, dk, dv):
    n_chunks = seq // chunk
    has_init = s0_ref is not None
    (q_ref, q0), (k_ref, k0), (v_ref, v0) = q, k, v
    for hd in range(hps):
        mask, q_decay, k_decay, chunk_decay = decays[hd]
        vs = slice(hd * dv, (hd + 1) * dv)
        for direction in range(2):
            order = range(n_chunks) if direction == 0 else range(n_chunks - 1, -1, -1)
            if has_init:
                st_ref[hd] = s0_ref[direction, hd]
            for step, c in enumerate(order):
                rs = slice(row0 + c * chunk, row0 + (c + 1) * chunk)
                ls = slice(c * chunk, (c + 1) * chunk)
                qc = q_ref[rs, q0 + hd * dk:q0 + (hd + 1) * dk]
                kc = k_ref[rs, k0 + hd * dk:k0 + (hd + 1) * dk]
                have_state = has_init or step > 0
                if direction == 0:
                    scores = (_dot_nt(qc, kc) * mask).astype(BF16)
                kd = (kc.astype(F32) * k_decay[direction]).T.astype(BF16)
                for c0 in range(0, dv, RET_COL_BLOCK):
                    cols = slice(hd * dv + c0, hd * dv + c0 + RET_COL_BLOCK)
                    sc = slice(c0, c0 + RET_COL_BLOCK)
                    vc = v_ref[rs, v0 + hd * dv + c0:v0 + hd * dv + c0 + RET_COL_BLOCK]
                    if have_state:
                        cross = _dot(qc, st_ref[hd, :, sc].astype(BF16)) * q_decay[direction]
                    if direction == 0:
                        inner = _dot(scores, vc)
                        o_ref[ls, cols] = inner + cross if have_state else inner
                    elif have_state:
                        o_ref[ls, cols] += cross
                    update = _dot(kd, vc)
                    if have_state:
                        st_ref[hd, :, sc] = st_ref[hd, :, sc] * chunk_decay[direction] + update
                    else:
                        st_ref[hd, :, sc] = update
            if sout_ref is not None:
                sout_ref[direction, hd] = st_ref[hd]

        for r0 in range(0, seq, RET_ROW_BLOCK):
            o = o_ref[r0:r0 + RET_ROW_BLOCK, vs]
            mu = jnp.mean(o, axis=-1, keepdims=True)
            var = jnp.mean(jnp.square(o - mu), axis=-1, keepdims=True)
            on = (o - mu) * lax.rsqrt(var + EPS) * gn_ref[:, vs]
            ys = slice(row0 + r0, row0 + r0 + RET_ROW_BLOCK)
            y_ref[ys, vs] = (gate_ref[ys, vs].astype(F32) * on).astype(y_ref.dtype)


def _ret_ctx_kernel(x_ref, mod_ref, g_ref, w_ref, lr_ref, gn_ref, y_ref, sout_ref,
                    qkv_s, gate_s, o_s, st_s, *, seq, dk, dv):
    nh = RET_HEADS
    _ret_project(x_ref, mod_ref, g_ref, w_ref, qkv_s, gate_s)
    decays = [_ret_decays(lr_ref, hd, seq, dk ** -0.5) for hd in range(nh)]
    for b in range(x_ref.shape[0] // seq):
        _ret_scan(decays, (qkv_s, 0), (qkv_s, nh * dk), (qkv_s, 2 * nh * dk), gate_s, gn_ref, None,
                  y_ref, sout_ref.at[b, 0], o_s, st_s, row0=b * seq, seq=seq, chunk=seq,
                  hps=nh, dk=dk, dv=dv)


def _ret_ctx_call(x, mod, g, w, log_rate, gn_g, rows, seq, n_gate):
    tm, d = rows.tm, x.shape[1]
    assert tm % seq == 0 and seq <= 2 * LANES
    nh = RET_HEADS
    n_qkv = w.shape[1] - n_gate
    dv = n_gate // nh
    dk = (n_qkv - n_gate) // (2 * nh)
    nb = tm // seq
    m = rows.ctx_tiles * tm
    return pl.pallas_call(
        functools.partial(_ret_ctx_kernel, seq=seq, dk=dk, dv=dv),
        grid=(rows.ctx_tiles,),
        in_specs=[
            pl.BlockSpec((tm, d), lambda i: (i, 0)),
            pl.BlockSpec((1, 6, d), lambda i: (0, 0, 0)),
            _resident(g.shape, lambda i: (0, 0)),
            _resident(w.shape, lambda i: (0, 0)),
            _resident(log_rate.shape, lambda i: (0, 0, 0)),
            _resident(gn_g.shape, lambda i: (0, 0)),
        ],
        out_specs=[pl.BlockSpec((tm, n_gate), lambda i: (i, 0)),
                   pl.BlockSpec((nb, 1, 2, nh, dk, dv), lambda i: (i, 0, 0, 0, 0, 0))],
        out_shape=[jax.ShapeDtypeStruct((m, n_gate), BF16),
                   jax.ShapeDtypeStruct((m // seq, 1, 2, nh, dk, dv), F32)],
        scratch_shapes=[pltpu.VMEM((tm, n_qkv), BF16), pltpu.VMEM((tm, n_gate), BF16),
                        pltpu.VMEM((seq, n_gate), F32), pltpu.VMEM((nh, dk, dv), F32)],
        compiler_params=_cparams(("arbitrary",)),
        name="ret_ctx",
    )(x, mod, g, w, log_rate, gn_g)


RET_LAT_ROWS = 512


def _ret_lat_kernel(x_ref, mod_ref, g_ref, wq_ref, wk_ref, wv_ref, wg_ref, lr_ref, gn_ref, s0_ref,
                    y_ref, h_s, q_s, k_s, v_s, gate_s, o_s, st_s, *, chunk, dk, dv):
    seq = x_ref.shape[0]

    @pl.when(pl.program_id(1) == 0)
    def _():
        h_s[...] = _modulate(x_ref[...], g_ref[...], mod_ref[0, 0:1, :], mod_ref[0, 1:2, :]).astype(BF16)

    rc = min(RET_LAT_ROWS, seq)
    for r in range(seq // rc):
        rs = slice(r * rc, (r + 1) * rc)
        h = h_s[rs, :]
        q_s[rs, :] = _dot(h, wq_ref[...]).astype(BF16)
        k_s[rs, :] = _dot(h, wk_ref[...]).astype(BF16)
        v_s[rs, :] = _dot(h, wv_ref[...]).astype(BF16)
        gate_s[rs, :] = _silu(_dot(h, wg_ref[...])).astype(BF16)
    decays = [_ret_decays(lr_ref, 0, chunk, dk ** -0.5)]
    _ret_scan(decays, (q_s, 0), (k_s, 0), (v_s, 0), gate_s, gn_ref, s0_ref.at[0], y_ref, None,
              o_s, st_s, row0=0, seq=seq, chunk=chunk, hps=1, dk=dk, dv=dv)


def _ret_lat_call(x, mod, g, w, log_rate, gn_g, s0, *, bsz, seq, row0, n_gate):
    d = x.shape[1]
    nh = RET_HEADS
    n_qkv = w.shape[1] - n_gate
    dv = n_gate // nh
    dk = (n_qkv - n_gate) // (2 * nh)
    chunk = min(seq, 2 * LANES)
    assert row0 % seq == 0 and dv % dk == 0
    b0 = row0 // seq
    return pl.pallas_call(
        functools.partial(_ret_lat_kernel, chunk=chunk, dk=dk, dv=dv),
        grid=(bsz, nh),
        in_specs=[
            pl.BlockSpec((seq, d), lambda b, h: (b0 + b, 0)),
            pl.BlockSpec((1, 6, d), lambda b, h: (1 + b, 0, 0)),
            _resident(g.shape, lambda b, h: (0, 0)),
            pl.BlockSpec((d, dk), lambda b, h: (0, h)),
            pl.BlockSpec((d, dk), lambda b, h: (0, nh + h)),
            pl.BlockSpec((d, dv), lambda b, h: (0, (2 * nh * dk) // dv + h)),
            pl.BlockSpec((d, dv), lambda b, h: (0, n_qkv // dv + h)),
            pl.BlockSpec((1, 2, 1), lambda b, h: (h, 0, 0)),
            pl.BlockSpec((1, dv), lambda b, h: (0, h)),
            pl.BlockSpec((1, 2, 1, dk, dv), lambda b, h: (b, 0, h, 0, 0)),
        ],
        out_specs=pl.BlockSpec((seq, dv), lambda b, h: (b, h)),
        out_shape=jax.ShapeDtypeStruct((bsz * seq, n_gate), BF16),
        scratch_shapes=[pltpu.VMEM((seq, d), BF16), pltpu.VMEM((seq, dk), BF16), pltpu.VMEM((seq, dk), BF16),
                        pltpu.VMEM((seq, dv), BF16), pltpu.VMEM((seq, dv), BF16),
                        pltpu.VMEM((seq, dv), F32), pltpu.VMEM((1, dk, dv), F32)],
        compiler_params=_cparams(("arbitrary", "arbitrary")),
        name="ret_lat",
    )(x, mod, g, w, w, w, w, log_rate, gn_g, s0)


def _rope_swap_index():
    f = ROPE_AXIS_FREQS
    idx = jnp.arange(MLA_ROPE)
    return jnp.where((idx // f) % 2 == 0, idx + f, idx - f)


def _mla_weights(w_a, q_norm_g, kv_norm_g, w_q_b, w_kv_b):
    swap = _rope_swap_index()
    nq = MLA_Q_RANK + MLA_KV_RANK
    w_kpe = w_a[:, nq:]
    w_kpe_sw = w_kpe[:, swap]
    wa = jnp.concatenate([w_a[:, :nq], w_kpe, w_kpe, w_kpe_sw, w_kpe_sw], axis=1).astype(BF16)
    wq = w_q_b.reshape(MLA_Q_RANK, MLA_HEADS, MLA_NOPE + MLA_ROPE)
    wq_nope = wq[:, :, :MLA_NOPE].reshape(MLA_Q_RANK, MLA_HEADS * MLA_NOPE)
    wq_pe = wq[:, :, MLA_NOPE:]
    wq_all = jnp.concatenate(
        [wq_nope, wq_pe.reshape(MLA_Q_RANK, -1), wq_pe[:, :, swap].reshape(MLA_Q_RANK, -1)],
        axis=1).astype(BF16)
    wkv = w_kv_b.reshape(MLA_KV_RANK, MLA_HEADS, MLA_NOPE + MLA_V)
    wkv_all = jnp.concatenate(
        [wkv[:, :, :MLA_NOPE].reshape(MLA_KV_RANK, -1), wkv[:, :, MLA_NOPE:].reshape(MLA_KV_RANK, -1)],
        axis=1).astype(BF16)
    return wa, q_norm_g[None, :], kv_norm_g[None, :], wq_all, wkv_all


def _rope_tables(n_tokens):
    f = ROPE_AXIS_FREQS
    f32 = np.float32
    rows = n_tokens // GRID_W
    r = np.repeat(np.arange(rows, dtype=f32), GRID_W)
    col = np.tile(np.arange(GRID_W, dtype=f32), rows)
    inv = (f32(ROPE_THETA) ** (-np.arange(f, dtype=f32) / f32(f))).astype(f32)
    ang_r, ang_c = r[:, None] * inv, col[:, None] * inv
    cos = np.concatenate([np.cos(ang_r)] * 2 + [np.cos(ang_c)] * 2, axis=1)
    sin = np.concatenate([-np.sin(ang_r), np.sin(ang_r), -np.sin(ang_c), np.sin(ang_c)], axis=1)
    reps = LANES // MLA_ROPE
    return (jnp.asarray(np.concatenate([cos] * reps, axis=1), F32),
            jnp.asarray(np.concatenate([sin] * reps, axis=1), F32))


def kernel(x_prompt, x_sample, c, c_ctx, cache_mla_ckv, cache_mla_kpe, state_ret, ada_w, ada_b, norm_mix_g, norm_ffn_g, mla_w_a, mla_q_norm_g, mla_kv_norm_g, mla_w_q_b, mla_w_kv_b, mla_w_o, conv_w_in, conv_w, conv_w_out, ret_w_in, ret_log_rate, ret_gn_g, ret_w_out, ffn_w_in, ffn_w_out, final_norm_g):
    batch, seq, d = x_prompt.shape
    dec_batch, dec_seq, _ = x_sample.shape
    depth = ada_w.shape[0]
    n_ctx = batch * seq
    n_lat = dec_batch * dec_seq
    past = cache_mla_ckv.shape[2]
    assert 1 + dec_batch <= MOD_ROWS

    cvec = jnp.zeros((MOD_ROWS, d), F32).at[0].set(c_ctx).at[1:1 + dec_batch].set(c)
    mod_all = _ada_call(cvec, ada_w, ada_b).reshape(depth, MOD_ROWS, 6, d)

    rows_s = _Rows(n_ctx, dec_batch, dec_seq, min(ROW_TILE_FUSED, dec_seq))
    rows_l = _Rows(n_ctx, dec_batch, dec_seq, min(ROW_TILE, dec_seq))
    rope_tabs = _rope_tables(dec_seq)
    final_g = final_norm_g[None, :]
    mla_w_o_bf = mla_w_o.astype(BF16)

    def mixer_cast_jobs(layer):
        if layer >= depth:
            return []
        kind, j = layer % N_MIXERS, layer // N_MIXERS
        if kind == 1:
            return [(conv_w_in, j), (conv_w_out, j)]
        if kind == 2:
            return [(ret_w_in, j), (ret_w_out, j)]
        return []

    x = (x_prompt.reshape(n_ctx, d), x_sample.reshape(n_lat, d))
    new_ckv, new_kpe, new_ret = [], [], []
    ffn_bf = mixer_bf = None
    for i in range(depth):
        kind, j = i % N_MIXERS, i // N_MIXERS
        mod = mod_all[i]
        g_mix = norm_mix_g[i][None, :]
        if ffn_bf is None:
            assert kind == 0
        if kind == 0:
            wts = _mla_weights(mla_w_a[j], mla_q_norm_g[j], mla_kv_norm_g[j], mla_w_q_b[j], mla_w_kv_b[j])
            split = isinstance(x, tuple)
            o_c, ckv_c, kpe_c = _mla_ctx_call(x[0] if split else x, mod, g_mix, wts, rows_l, seq)
            ql, kl, vl = _mla_proj_call(
                x[1] if split else x, 0 if split else rows_l.ctx_tiles, mod, g_mix, wts, rows_l,
                tile0=rows_l.ctx_tiles, n_tiles=rows_l.n_tiles - rows_l.ctx_tiles, rope_tabs=rope_tabs)
            new_ckv.append(ckv_c.reshape(batch, seq, MLA_KV_RANK))
            new_kpe.append(kpe_c.reshape(batch, seq, MLA_ROPE))
            kpe_rep = jnp.concatenate([cache_mla_kpe[:, j]] * (LANES // MLA_ROPE), axis=-1)
            kp, vp = _cache_expand_call(
                cache_mla_ckv[:, j].reshape(dec_batch * past, MLA_KV_RANK),
                kpe_rep.reshape(dec_batch * past, LANES), wts[4])
            jobs = [] if ffn_bf is not None else [(ffn_w_in, i), (ffn_w_out, i)]
            o_l, cast = _attn_call(ql, [kp, kl], [vp, vl], n_batch=dec_batch, seq=dec_seq,
                                   sks=[past, dec_seq], nb=1, tq=min(ATTN_Q_TILE, dec_seq),
                                   name="attn_lat", cast_jobs=jobs)
            if jobs:
                ffn_bf = cast
            mix = (o_c, o_l, mla_w_o_bf, j)
        elif kind == 1:
            mix = None
            x = _conv_call(x, mod, g_mix, mixer_bf[0][0], conv_w[j], mixer_bf[1][0], rows_l, seq, dec_seq)
        else:
            n_gate = ret_w_out.shape[1]
            lr = ret_log_rate[j].T[:, :, None]
            gn = ret_gn_g[j][None, :]
            y_c, st = _ret_ctx_call(x, mod, g_mix, mixer_bf[0][0], lr, gn, rows_s, seq, n_gate)
            y_l = _ret_lat_call(x, mod, g_mix, mixer_bf[0][0], lr, gn, state_ret[:, j],
                                bsz=dec_batch, seq=dec_seq, row0=n_ctx, n_gate=n_gate)
            new_ret.append(st)
            mix = (y_c, y_l, mixer_bf[1], 0)
        last = i == depth - 1
        jobs = [] if last else [(ffn_w_in, i + 1), (ffn_w_out, i + 1)] + mixer_cast_jobs(i + 1)
        x, cast = _ffn_call(x, mod, norm_ffn_g[i][None, :], ffn_bf[0], ffn_bf[1], 0, final_g,
                            rows_l if mix is None else rows_s, final=last, cast_jobs=jobs, mix=mix)
        ffn_bf, mixer_bf = cast[:2], cast[2:]

    y_prompt = x[0].reshape(batch, seq, d)
    y_sample = x[1].reshape(dec_batch, dec_seq, d)
    return (y_prompt, y_sample, jnp.stack(new_ckv, axis=1), jnp.stack(new_kpe, axis=1),
            jnp.concatenate(new_ret, axis=1))
```
